```python
import jax, jax.numpy as jnp
from jax import lax
import numpy as np

D_MODEL = 1024
BATCH = 8
SEQ = 4096
DEPTH = 2

HEAD_DIM = 64
C_ATT = D_MODEL // 4
C_RWKV = (D_MODEL - C_ATT) // 2
C_MLSTM = D_MODEL - C_ATT - C_RWKV
D_MIX = C_ATT + C_RWKV + C_MLSTM
H_ATT = C_ATT // HEAD_DIM
H_RWKV = C_RWKV // HEAD_DIM
H_MLSTM = C_MLSTM // HEAD_DIM
CMP_BLOCK = 32
CMP_STRIDE = 16
SLC_BLOCK = 64
N_SELECT = 16
WINDOW = 512
Q_BLOCK = 128
NEG = -1e30
FORCE = 1e9
RANK_W = 64
RANK_A = 64
RANK_G = 128
RWKV_GN_EPS = 64e-5
MLSTM_CHUNK = 64
CONV_WIDTH = 4
D_FF = -(-8 * D_MODEL // (3 * 256)) * 256
NORM_EPS = 1e-6
NSA_SIZES = (C_ATT, HEAD_DIM, HEAD_DIM, HEAD_DIM, HEAD_DIM, HEAD_DIM, HEAD_DIM, 3 * H_ATT)
RWKV_SIZES = (C_RWKV, C_RWKV, C_RWKV, RANK_W, RANK_A, RANK_G)
MLSTM_SIZES = (2 * C_MLSTM, C_MLSTM, C_MLSTM, H_MLSTM, H_MLSTM)
NSA_IN = sum(NSA_SIZES)
RWKV_IN = sum(RWKV_SIZES)
MLSTM_IN = sum(MLSTM_SIZES)
D_IN = NSA_IN + RWKV_IN + MLSTM_IN

kernel_name = "hybrid_nsa_rwkv7_mlstm_block"


def split_cols(t, sizes):
    points = np.cumsum(np.array(sizes))[:-1].tolist()
    return jnp.split(t, points, axis=-1)


def rms_norm(x, g, eps=NORM_EPS):
    x32 = x.astype(jnp.float32)
    y = x32 * lax.rsqrt(jnp.mean(x32 * x32, axis=-1, keepdims=True) + eps)
    return (y * g.astype(jnp.float32)).astype(x.dtype)


def shift_right(t):
    return jnp.pad(t, ((0, 0), (1, 0), (0, 0)))[:, :-1]


def causal_dwconv(t, w, b):
    k = w.shape[0]
    out = lax.conv_general_dilated(t, w[:, None, :], window_strides=(1,), padding=[(k - 1, 0)],
                                   dimension_numbers=("NWC", "WIO", "NWC"),
                                   feature_group_count=t.shape[-1])
    return out + b


def nsa_mixer(q, kc, vc, ks, vs, kw, vw, gts, gate_b, pe_k, pe_v, ck_w1, ck_w2, cv_w1, cv_w2):
    B, S, _ = q.shape
    f32 = jnp.float32
    q = q.reshape(B, S, H_ATT, HEAD_DIM)
    scale = HEAD_DIM ** -0.5
    pos = jnp.arange(S)
    n_cmp = (S - CMP_BLOCK) // CMP_STRIDE + 1
    cmp_start = jnp.arange(n_cmp) * CMP_STRIDE
    blk_idx = cmp_start[:, None] + jnp.arange(CMP_BLOCK)[None, :]

    def compress(t, pe, w1, w2):
        tb = t[:, blk_idx] + pe
        return jax.nn.gelu(tb.reshape(B, n_cmp, CMP_BLOCK * HEAD_DIM) @ w1) @ w2

    k_cmp = compress(kc, pe_k, ck_w1, ck_w2)
    v_cmp = compress(vc, pe_v, cv_w1, cv_w2)
    cmp_mask = (cmp_start + CMP_BLOCK - 1)[None, :] <= pos[:, None]
    s_cmp = jnp.einsum('bshd,bnd->bhsn', q, k_cmp).astype(f32) * scale
    p_cmp = jax.nn.softmax(jnp.where(cmp_mask, s_cmp, NEG), axis=-1) * cmp_mask
    o_cmp = jnp.einsum('bhsn,bnd->bshd', p_cmp.astype(q.dtype), v_cmp)
    n_slc = S // SLC_BLOCK
    n_sel = min(N_SELECT, n_slc)
    slc_start = jnp.arange(n_slc) * SLC_BLOCK
    overlap = ((cmp_start[:, None] < slc_start[None, :] + SLC_BLOCK)
               & (cmp_start[:, None] + CMP_BLOCK > slc_start[None, :])).astype(f32)
    imp = jnp.einsum('bhsn,nj->bsj', p_cmp, overlap)
    cur = pos // SLC_BLOCK
    j = jnp.arange(n_slc)
    forced = (j[None, :] == 0) | (j[None, :] == cur[:, None]) | (j[None, :] == cur[:, None] - 1)
    valid = j[None, :] <= cur[:, None]
    score = jnp.where(forced, FORCE, jnp.where(valid, imp, NEG))
    _, sel_idx = lax.top_k(score, n_sel)
    ks_blk = ks.reshape(B, n_slc, SLC_BLOCK, HEAD_DIM)
    vs_blk = vs.reshape(B, n_slc, SLC_BLOCK, HEAD_DIM)
    kw_pad = jnp.pad(kw, ((0, 0), (WINDOW, 0), (0, 0)))
    vw_pad = jnp.pad(vw, ((0, 0), (WINDOW, 0), (0, 0)))
    span = WINDOW + Q_BLOCK
    b_ix = jnp.arange(B)[:, None, None]

    def block(i):
        q0 = i * Q_BLOCK
        qb = lax.dynamic_slice_in_dim(q, q0, Q_BLOCK, axis=1)
        idx = lax.dynamic_slice_in_dim(sel_idx, q0, Q_BLOCK, axis=1)
        qpos = q0 + jnp.arange(Q_BLOCK)
        kg = ks_blk[b_ix, idx]
        vg = vs_blk[b_ix, idx]
        kpos = idx[..., None] * SLC_BLOCK + jnp.arange(SLC_BLOCK)
        m_s = kpos <= qpos[None, :, None, None]
        s_s = jnp.einsum('bqhd,bqnkd->bhqnk', qb, kg).astype(f32) * scale
        s_s = jnp.where(m_s[:, None], s_s, NEG).reshape(B, H_ATT, Q_BLOCK, n_sel * SLC_BLOCK)
        p_s = jax.nn.softmax(s_s, axis=-1).reshape(B, H_ATT, Q_BLOCK, n_sel, SLC_BLOCK)
        o_s = jnp.einsum('bhqnk,bqnkd->bqhd', p_s.astype(qb.dtype), vg)
        kwb = lax.dynamic_slice_in_dim(kw_pad, q0, span, axis=1)
        vwb = lax.dynamic_slice_in_dim(vw_pad, q0, span, axis=1)
        wpos = q0 - WINDOW + jnp.arange(span)
        m_w = ((wpos[None, :] <= qpos[:, None]) & (wpos[None, :] > qpos[:, None] - WINDOW)
               & (wpos[None, :] >= 0))
        s_w = jnp.einsum('bqhd,bkd->bhqk', qb, kwb).astype(f32) * scale
        p_w = jax.nn.softmax(jnp.where(m_w, s_w, NEG), axis=-1)
        o_w = jnp.einsum('bhqk,bkd->bqhd', p_w.astype(qb.dtype), vwb)
        return o_s, o_w

    o_s, o_w = lax.map(block, jnp.arange(S // Q_BLOCK))
    o_s = o_s.transpose(1, 0, 2, 3, 4).reshape(B, S, H_ATT, HEAD_DIM)
    o_w = o_w.transpose(1, 0, 2, 3, 4).reshape(B, S, H_ATT, HEAD_DIM)
    g = jax.nn.sigmoid(gts + gate_b).reshape(B, S, 3, H_ATT)[..., None]
    o = g[:, :, 0] * o_cmp + g[:, :, 1] * o_s + g[:, :, 2] * o_w
    return o.reshape(B, S, C_ATT)


def rwkv7_mixer(r, k, v, xw, xa, xg, w0, w2, a0, a2, g2, k_k, k_a, r_k, ln_w, ln_b):
    B, S, _ = r.shape
    H, N = H_RWKV, HEAD_DIM
    f32 = jnp.float32
    w = -jax.nn.softplus(-(w0 + jnp.tanh(xw) @ w2)) - 0.5
    decay = jnp.exp(-jnp.exp(w.astype(f32)))
    a = jax.nn.sigmoid(a0 + xa @ a2)
    g = jax.nn.sigmoid(xg) @ g2
    heads = lambda t: t.reshape(B, S, H, N).astype(f32)
    kk = heads(k * k_k)
    kk = kk / jnp.maximum(jnp.sqrt(jnp.sum(kk * kk, axis=-1, keepdims=True)), 1e-12)
    k = k * (1 + (a - 1) * k_a)
    rh, kh, vh, ah, wh = heads(r), heads(k), heads(v), heads(a), heads(decay)

    def step(state, inp):
        r_t, w_t, k_t, v_t, kk_t, a_t = inp
        sa = jnp.einsum('bhij,bhj->bhi', state, -kk_t)
        state = (state * w_t[:, :, None, :] + sa[..., None] * (kk_t * a_t)[:, :, None, :]
                 + v_t[..., None] * k_t[:, :, None, :])
        return state, jnp.einsum('bhij,bhj->bhi', state, r_t)

    xs = tuple(t.transpose(1, 0, 2, 3) for t in (rh, wh, kh, vh, kk, ah))
    _, y = lax.scan(step, jnp.zeros((B, H, N, N), f32), xs)
    y = y.transpose(1, 0, 2, 3)
    mu = jnp.mean(y, axis=-1, keepdims=True)
    var = jnp.mean((y - mu) ** 2, axis=-1, keepdims=True)
    y = ((y - mu) * lax.rsqrt(var + RWKV_GN_EPS)).reshape(B, S, C_RWKV) * ln_w + ln_b
    bonus = jnp.sum(rh * kh * r_k.astype(f32), axis=-1, keepdims=True) * vh
    y = y + bonus.reshape(B, S, C_RWKV)
    return (y * g).astype(r.dtype)


def mlstm_mixer(q, k, v, o, ig, fg, ig_b, fg_b, norm_g):
    B, S, _ = q.shape
    H, Dh, L = H_MLSTM, HEAD_DIM, MLSTM_CHUNK
    nc = S // L
    f32 = jnp.float32
    chunks = lambda t: t.reshape(B, nc, L, H, Dh).transpose(0, 3, 1, 2, 4).astype(f32)
    gate = lambda t: t.astype(f32).reshape(B, nc, L, H).transpose(0, 3, 1, 2)
    qc, kc, vc = chunks(q), chunks(k) * (Dh ** -0.5), chunks(v)
    log_i = gate(ig + ig_b)
    log_f = jax.nn.log_sigmoid(gate(fg + fg_b))
    b = jnp.cumsum(log_f, axis=-1)
    g_tot = b[..., -1]
    u = g_tot[..., None] - b + log_i

    def step(carry, inp):
        C, n, m = carry
        g_c, u_c, k_c, v_c = inp
        m_new = jnp.maximum(g_c + m, jnp.max(u_c, axis=-1))
        dec = jnp.exp(g_c + m - m_new)
        wgt = jnp.exp(u_c - m_new[..., None])
        C_new = dec[..., None, None] * C + jnp.einsum('bhs,bhsd,bhse->bhde', wgt, v_c, k_c)
        n_new = dec[..., None] * n + jnp.einsum('bhs,bhse->bhe', wgt, k_c)
        return (C_new, n_new, m_new), (C, n, m)

    front = lambda t: jnp.moveaxis(t, 2, 0)
    init = (jnp.zeros((B, H, Dh, Dh), f32), jnp.zeros((B, H, Dh), f32), jnp.zeros((B, H), f32))
    _, (C0, n0, m0) = lax.scan(step, init, (front(g_tot), front(u), front(kc), front(vc)))
    C0, n0, m0 = jnp.moveaxis(C0, 0, 2), jnp.moveaxis(n0, 0, 2), jnp.moveaxis(m0, 0, 2)
    a_inter = b + m0[..., None]
    causal = jnp.tril(jnp.ones((L, L), dtype=bool))
    D = jnp.where(causal, b[..., :, None] - b[..., None, :] + log_i[..., None, :], -jnp.inf)
    m_t = jnp.maximum(a_inter, jnp.max(D, axis=-1))
    w_inter = jnp.exp(a_inter - m_t)
    s = jnp.exp(D - m_t[..., None]) * jnp.einsum('bhcjd,bhcsd->bhcjs', qc, kc)
    num = (w_inter[..., None] * jnp.einsum('bhcde,bhcje->bhcjd', C0, qc)
           + jnp.einsum('bhcjs,bhcsd->bhcjd', s, vc))
    den = w_inter * jnp.einsum('bhce,bhcje->bhcj', n0, qc) + jnp.sum(s, axis=-1)
    h = num / jnp.maximum(jnp.abs(den), jnp.exp(-m_t))[..., None]
    h = h * lax.rsqrt(jnp.mean(h * h, axis=-1, keepdims=True) + NORM_EPS)
    h = h.transpose(0, 2, 3, 1, 4).reshape(B, S, C_MLSTM) * norm_g.astype(f32)
    return (jax.nn.sigmoid(o.astype(f32)) * h).astype(q.dtype)


def hybrid_mixer(h, w_in, w_out, nsa_pe_k, nsa_pe_v, nsa_ck_w1, nsa_ck_w2, nsa_cv_w1, nsa_cv_w2,
                 nsa_gate_b, nsa_out_g, rw_mu, rw_w0, rw_w2, rw_a0, rw_a2, rw_g2, rw_kk, rw_ka,
                 rw_rk, rw_ln_w, rw_ln_b, ml_conv_w, ml_conv_b, ml_ig_b, ml_fg_b, ml_norm_g):
    proj = h @ w_in
    p_nsa, p_rw, p_ml = split_cols(proj, (NSA_IN, RWKV_IN, MLSTM_IN))
    q, kc, vc, ks, vs, kw, vw, gts = split_cols(p_nsa, NSA_SIZES)
    o_nsa = nsa_mixer(q, kc, vc, ks, vs, kw, vw, gts, nsa_gate_b, nsa_pe_k, nsa_pe_v,
                      nsa_ck_w1, nsa_ck_w2, nsa_cv_w1, nsa_cv_w2)
    o_nsa = rms_norm(o_nsa, nsa_out_g)
    p_rw = p_rw + rw_mu * (shift_right(p_rw) - p_rw)
    r, k, v, xw, xa, xg = split_cols(p_rw, RWKV_SIZES)
    o_rw = rwkv7_mixer(r, k, v, xw, xa, xg, rw_w0, rw_w2, rw_a0, rw_a2, rw_g2, rw_kk, rw_ka,
                       rw_rk, rw_ln_w, rw_ln_b)
    qk, mv, mo, ig, fg = split_cols(p_ml, MLSTM_SIZES)
    qk = jax.nn.silu(causal_dwconv(qk, ml_conv_w, ml_conv_b))
    mq, mk = jnp.split(qk, 2, axis=-1)
    o_ml = mlstm_mixer(mq, mk, mv, mo, ig, fg, ml_ig_b, ml_fg_b, ml_norm_g)
    return jnp.concatenate([o_nsa, o_rw, o_ml], axis=-1) @ w_out


def swiglu(h, w_gate, w_up, w_down):
    return (jax.nn.silu(h @ w_gate) * (h @ w_up)) @ w_down


def setup_inputs(seed: int = 0) -> dict:
    key = jax.random.key(seed)
    keys = jax.random.split(key, 40)
    counter = [0]

    def nxt():
        kk = keys[counter[0]]
        counter[0] += 1
        return kk

    f32 = jnp.float32
    L = DEPTH
    nrm = lambda shape, s: s * jax.random.normal(nxt(), shape, f32)
    gain = lambda shape: 1.0 + 0.05 * jax.random.normal(nxt(), shape, f32)
    unif = lambda shape, lo, hi: jax.random.uniform(nxt(), shape, f32, lo, hi)
    return {
        "x": nrm((BATCH, SEQ, D_MODEL), 1.0),
        "c": nrm((BATCH, D_MODEL), 1.0),
        "w_mod": nrm((L, D_MODEL, 6 * D_MODEL), 0.3 * D_MODEL ** -0.5),
        "b_mod": nrm((L, 6 * D_MODEL), 0.02),
        "g_pre_mix": gain((L, D_MODEL)),
        "g_post_mix": gain((L, D_MODEL)),
        "g_pre_ffn": gain((L, D_MODEL)),
        "g_post_ffn": gain((L, D_MODEL)),
        "w_in": nrm((L, D_MODEL, D_IN), D_MODEL ** -0.5),
        "w_out": nrm((L, D_MIX, D_MODEL), D_MIX ** -0.5),
        "nsa_pe_k": nrm((L, CMP_BLOCK, HEAD_DIM), 0.1),
        "nsa_pe_v": nrm((L, CMP_BLOCK, HEAD_DIM), 0.1),
        "nsa_ck_w1": nrm((L, CMP_BLOCK * HEAD_DIM, HEAD_DIM), (CMP_BLOCK * HEAD_DIM) ** -0.5),
        "nsa_ck_w2": nrm((L, HEAD_DIM, HEAD_DIM), HEAD_DIM ** -0.5),
        "nsa_cv_w1": nrm((L, CMP_BLOCK * HEAD_DIM, HEAD_DIM), (CMP_BLOCK * HEAD_DIM) ** -0.5),
        "nsa_cv_w2": nrm((L, HEAD_DIM, HEAD_DIM), HEAD_DIM ** -0.5),
        "nsa_gate_b": nrm((L, 3 * H_ATT), 0.1),
        "nsa_out_g": gain((L, C_ATT)),
        "rw_mu": unif((L, RWKV_IN), 0.0, 1.0),
        "rw_w0": unif((L, C_RWKV), -6.5, -1.0),
        "rw_w2": nrm((L, RANK_W, C_RWKV), 0.1 * RANK_W ** -0.5),
        "rw_a0": nrm((L, C_RWKV), 0.1),
        "rw_a2": nrm((L, RANK_A, C_RWKV), 0.1 * RANK_A ** -0.5),
        "rw_g2": nrm((L, RANK_G, C_RWKV), RANK_G ** -0.5),
        "rw_kk": 0.85 + 0.05 * jax.random.normal(nxt(), (L, C_RWKV), f32),
        "rw_ka": gain((L, C_RWKV)),
        "rw_rk": nrm((L, H_RWKV, HEAD_DIM), 0.1),
        "rw_ln_w": gain((L, C_RWKV)),
        "rw_ln_b": nrm((L, C_RWKV), 0.02),
        "ml_conv_w": nrm((L, CONV_WIDTH, 2 * C_MLSTM), CONV_WIDTH ** -0.5),
        "ml_conv_b": nrm((L, 2 * C_MLSTM), 0.02),
        "ml_ig_b": nrm((L, H_MLSTM), 0.1),
        "ml_fg_b": unif((L, H_MLSTM), 3.0, 6.0),
        "ml_norm_g": gain((L, C_MLSTM)),
        "ffn_w_gate": nrm((L, D_MODEL, D_FF), D_MODEL ** -0.5),
        "ffn_w_up": nrm((L, D_MODEL, D_FF), D_MODEL ** -0.5),
        "ffn_w_down": nrm((L, D_FF, D_MODEL), D_FF ** -0.5),
    }


def reference(x, c, w_mod, b_mod, g_pre_mix, g_post_mix, g_pre_ffn, g_post_ffn, w_in, w_out,
              nsa_pe_k, nsa_pe_v, nsa_ck_w1, nsa_ck_w2, nsa_cv_w1, nsa_cv_w2, nsa_gate_b, nsa_out_g,
              rw_mu, rw_w0, rw_w2, rw_a0, rw_a2, rw_g2, rw_kk, rw_ka, rw_rk, rw_ln_w, rw_ln_b,
              ml_conv_w, ml_conv_b, ml_ig_b, ml_fg_b, ml_norm_g, ffn_w_gate, ffn_w_up, ffn_w_down):
    cs = jax.nn.silu(c)
    for l in range(DEPTH):
        mod = cs @ w_mod[l] + b_mod[l]
        sh1, sc1, gt1, sh2, sc2, gt2 = [m[:, None, :] for m in jnp.split(mod, 6, axis=-1)]
        h = rms_norm(x, g_pre_mix[l]) * (1 + sc1) + sh1
        y = hybrid_mixer(h, w_in[l], w_out[l], nsa_pe_k[l], nsa_pe_v[l], nsa_ck_w1[l], nsa_ck_w2[l],
                         nsa_cv_w1[l], nsa_cv_w2[l], nsa_gate_b[l], nsa_out_g[l], rw_mu[l], rw_w0[l],
                         rw_w2[l], rw_a0[l], rw_a2[l], rw_g2[l], rw_kk[l], rw_ka[l], rw_rk[l],
                         rw_ln_w[l], rw_ln_b[l], ml_conv_w[l], ml_conv_b[l], ml_ig_b[l], ml_fg_b[l],
                         ml_norm_g[l])
        x = x + gt1 * rms_norm(y, g_post_mix[l])
        h = rms_norm(x, g_pre_ffn[l]) * (1 + sc2) + sh2
        y = swiglu(h, ffn_w_gate[l], ffn_w_up[l], ffn_w_down[l])
        x = x + gt2 * rms_norm(y, g_post_ffn[l])
    return x
```

```python
import functools

import jax
import jax.numpy as jnp
from jax import lax
from jax.experimental import pallas as pl
from jax.experimental.pallas import tpu as pltpu

F32 = jnp.float32
BF16 = jnp.bfloat16
HIGHEST = lax.Precision.HIGHEST

HEAD_DIM = 64
H_ATT = 4
C_ATT = H_ATT * HEAD_DIM
H_RWKV = 6
C_RWKV = H_RWKV * HEAD_DIM
H_MLSTM = 6
C_MLSTM = H_MLSTM * HEAD_DIM
CMP_BLOCK = 32
CMP_STRIDE = 16
SLC_BLOCK = 64
SLC_SHIFT = 6
N_SELECT = 16
WINDOW = 512
NEG = -1e30
FORCE = 1e9
RANK_W = 64
RANK_A = 64
RANK_G = 128
RWKV_GN_EPS = 64e-5
CHUNK = 64
CONV_WIDTH = 4
NORM_EPS = 1e-6

NSA_W = 768
RWKV_W = 3 * C_RWKV + RANK_W + RANK_A + RANK_G
ML_W = 4 * C_MLSTM + 128
NSA_IN = C_ATT + 6 * HEAD_DIM + 3 * H_ATT
ML_IN = 4 * C_MLSTM + 2 * H_MLSTM

VMEM_LIMIT = 56 * 1024 * 1024

TQ = 256
TK = 256


def _params(*sem):
    return pltpu.CompilerParams(dimension_semantics=sem, vmem_limit_bytes=VMEM_LIMIT)


def _dot(a, b, precise=False):
    if precise:
        return jnp.dot(a.astype(F32), b.astype(F32), preferred_element_type=F32, precision=HIGHEST)
    return jnp.dot(a.astype(BF16), b.astype(BF16), preferred_element_type=F32)


def _dot_nt(a, b, precise=False):
    dn = (((1,), (1,)), ((), ()))
    if precise:
        return lax.dot_general(a.astype(F32), b.astype(F32), dn, preferred_element_type=F32, precision=HIGHEST)
    return lax.dot_general(a.astype(BF16), b.astype(BF16), dn, preferred_element_type=F32)


def _dot_tn(a, b, precise=False):
    dn = (((0,), (0,)), ((), ()))
    if precise:
        return lax.dot_general(a.astype(F32), b.astype(F32), dn, preferred_element_type=F32, precision=HIGHEST)
    return lax.dot_general(a.astype(BF16), b.astype(BF16), dn, preferred_element_type=F32)


def _rms(x, g):
    return x * lax.rsqrt(jnp.mean(x * x, axis=-1, keepdims=True) + NORM_EPS) * g


def _mod_body(c_ref, w_ref, b_ref, o_ref):
    c = c_ref[...]
    cs = c * jax.nn.sigmoid(c)
    o_ref[...] = _dot(cs, w_ref[...], precise=True) + b_ref[...]


def _mod(c, w_mod, b_mod):
    depth, d, n = w_mod.shape
    bsz = c.shape[0]
    tn = 1536
    return pl.pallas_call(
        _mod_body,
        out_shape=jax.ShapeDtypeStruct((depth, bsz, n), F32),
        grid=(depth, n // tn),
        in_specs=[
            pl.BlockSpec((bsz, d), lambda l, j: (0, 0)),
            pl.BlockSpec((None, d, tn), lambda l, j: (l, 0, j)),
            pl.BlockSpec((None, 1, tn), lambda l, j: (l, 0, j)),
        ],
        out_specs=pl.BlockSpec((None, bsz, tn), lambda l, j: (l, 0, j)),
        compiler_params=_params("parallel", "parallel"),
        name="mod",
    )(c, w_mod, b_mod.reshape(depth, 1, n))


def _in_proj_body(x_ref, sc_ref, sh_ref, g_ref, w_ref, nsa_ref, rw_ref, ml_ref):
    h = _rms(x_ref[...], g_ref[...]) * (1.0 + sc_ref[...]) + sh_ref[...]
    hb = h.astype(BF16)
    nsa_ref[...] = jnp.dot(hb, w_ref[:, 0:NSA_W], preferred_element_type=F32)
    rw_ref[...] = jnp.dot(hb, w_ref[:, NSA_W:NSA_W + RWKV_W], preferred_element_type=F32)
    ml_ref[...] = jnp.dot(hb, w_ref[:, NSA_W + RWKV_W:], preferred_element_type=F32)


def _in_proj(x2, sc, sh, g, w_p, seq):
    t, d = x2.shape
    tm = 512
    per_b = seq // tm
    row = lambda i: (i, 0)
    bat = lambda i: (i // per_b, 0, 0)
    fix = lambda i: (0, 0)
    return pl.pallas_call(
        _in_proj_body,
        out_shape=(jax.ShapeDtypeStruct((t, NSA_W), F32),
                   jax.ShapeDtypeStruct((t, RWKV_W), F32),
                   jax.ShapeDtypeStruct((t, ML_W), F32)),
        grid=(t // tm,),
        in_specs=[
            pl.BlockSpec((tm, d), row),
            pl.BlockSpec((None, 1, d), bat),
            pl.BlockSpec((None, 1, d), bat),
            pl.BlockSpec((1, d), fix),
            pl.BlockSpec(w_p.shape, fix),
        ],
        out_specs=(pl.BlockSpec((tm, NSA_W), row), pl.BlockSpec((tm, RWKV_W), row),
                   pl.BlockSpec((tm, ML_W), row)),
        compiler_params=_params("parallel"),
        name="in_proj",
    )(x2, sc, sh, g, w_p)


def _nsa_cmp_body(x_ref, wc_ref, pe_ref, w2_ref, o_ref):
    wc = wc_ref[...]
    f = _dot(x_ref[...], wc, precise=True)
    c = _dot(pe_ref[...], wc, precise=True)
    g = f.shape[0]
    second = pltpu.roll(f[:, 128:256], g - 1, 0)
    pre = f[:, 0:128] + second + c[0:1, 0:128] + c[1:2, 128:256]
    act = jax.nn.gelu(pre, approximate=True)
    out = _dot(act, w2_ref[...], precise=True)
    rows = lax.broadcasted_iota(jnp.int32, out.shape, 0)
    o_ref[...] = jnp.where(rows < g - 1, out, 0.0)


def _nsa_cmp(xg, wc, pe8, w2):
    bsz, g, k = xg.shape
    return pl.pallas_call(
        _nsa_cmp_body,
        out_shape=jax.ShapeDtypeStruct((bsz, g, 128), F32),
        grid=(bsz,),
        in_specs=[
            pl.BlockSpec((None, g, k), lambda b: (b, 0, 0)),
            pl.BlockSpec(wc.shape, lambda b: (0, 0)),
            pl.BlockSpec(pe8.shape, lambda b: (0, 0)),
            pl.BlockSpec(w2.shape, lambda b: (0, 0)),
        ],
        out_specs=pl.BlockSpec((None, g, 128), lambda b: (b, 0, 0)),
        compiler_params=_params("parallel"),
        name="nsa_cmp",
    )(xg, wc, pe8, w2)


def _nsa_sel_body(q_ref, kv_ref, ocmp_ref, sel_ref, *, n_slc):
    tq = q_ref.shape[0]
    n_cmp_pad = kv_ref.shape[0]
    q0 = pl.program_id(1) * tq
    scale = HEAD_DIM ** -0.5
    q = q_ref[...]
    kc = kv_ref[:, 0:HEAD_DIM]
    vc = kv_ref[:, HEAD_DIM:2 * HEAD_DIM]
    pos = q0 + lax.broadcasted_iota(jnp.int32, (tq, n_cmp_pad), 0)
    nidx = lax.broadcasted_iota(jnp.int32, (tq, n_cmp_pad), 1)
    cmask = (nidx * CMP_STRIDE + (CMP_BLOCK - 1)) <= pos
    psum = jnp.zeros((tq, n_cmp_pad), F32)
    outs = []
    for h in range(H_ATT):
        s = _dot_nt(q[:, h * HEAD_DIM:(h + 1) * HEAD_DIM], kc, precise=True) * scale
        s = jnp.where(cmask, s, NEG)
        m = jnp.max(s, axis=-1, keepdims=True)
        e = jnp.where(cmask, jnp.exp(s - m), 0.0)
        den = jnp.sum(e, axis=-1, keepdims=True)
        p = e / jnp.maximum(den, 1e-30)
        psum = psum + p
        outs.append(_dot(p, vc))
    ocmp_ref[...] = jnp.concatenate(outs, axis=-1)
    jrow = lax.broadcasted_iota(jnp.int32, (n_slc, n_cmp_pad), 0) * SLC_BLOCK
    ncol = lax.broadcasted_iota(jnp.int32, (n_slc, n_cmp_pad), 1) * CMP_STRIDE
    overlap_t = jnp.where((ncol < jrow + SLC_BLOCK) & (ncol + CMP_BLOCK > jrow), 1.0, 0.0)
    imp_t = _dot_nt(overlap_t, psum, precise=True)
    jj = lax.broadcasted_iota(jnp.int32, (n_slc, tq), 0)
    cur = jnp.right_shift(q0 + lax.broadcasted_iota(jnp.int32, (n_slc, tq), 1), SLC_SHIFT)
    forced = (jj == 0) | (jj == cur) | (jj == cur - 1)
    score = jnp.where(forced, FORCE, jnp.where(jj <= cur, imp_t, NEG))
    cnt = jnp.zeros((n_slc, tq), F32)
    for j2 in range(n_slc):
        row = score[j2:j2 + 1, :]
        ge = jnp.where(row >= score, 1.0, 0.0)
        gt = jnp.where(row > score, 1.0, 0.0)
        cnt = cnt + jnp.where(jj > j2, ge, gt)
    n_sel = min(N_SELECT, n_slc)
    sel_t = jnp.where(cnt < n_sel, 1.0, 0.0)
    sel_ref[...] = sel_t.T.astype(sel_ref.dtype)


def _nsa_sel(p_nsa, kv_cmp, bsz, seq):
    t = p_nsa.shape[0]
    n_slc = seq // SLC_BLOCK
    nq = seq // TQ
    g = kv_cmp.shape[1]
    return pl.pallas_call(
        functools.partial(_nsa_sel_body, n_slc=n_slc),
        out_shape=(jax.ShapeDtypeStruct((t, C_ATT), F32), jax.ShapeDtypeStruct((t, n_slc), BF16)),
        grid=(bsz, nq),
        in_specs=[
            pl.BlockSpec((TQ, C_ATT), lambda b, i: (b * nq + i, 0)),
            pl.BlockSpec((None, g, 128), lambda b, i: (b, 0, 0)),
        ],
        out_specs=(pl.BlockSpec((TQ, C_ATT), lambda b, i: (b * nq + i, 0)),
                   pl.BlockSpec((TQ, n_slc), lambda b, i: (b * nq + i, 0))),
        compiler_params=_params("parallel", "parallel"),
        name="nsa_sel",
    )(p_nsa, kv_cmp)


def _attend(qs, k_ref, v_ref, j, bias, m_ref, l_ref, acc_ref, first):
    tq4 = qs.shape[0]
    start = pl.multiple_of(j * TK, TK)
    k = k_ref[pl.ds(start, TK), :]
    v = v_ref[pl.ds(start, TK), :]
    s = _dot_nt(qs, k)
    if bias is not None:
        s = (s.reshape(H_ATT, tq4 // H_ATT, TK) + bias[None]).reshape(tq4, TK)
    if first:
        m_new = jnp.max(s, axis=-1, keepdims=True)
        p = jnp.exp(s - m_new)
        l_ref[...] = jnp.sum(p, axis=-1, keepdims=True)
        acc_ref[...] = _dot(p, v)
    else:
        m_old = m_ref[...]
        m_new = jnp.maximum(m_old, jnp.max(s, axis=-1, keepdims=True))
        alpha = jnp.exp(m_old - m_new)
        p = jnp.exp(s - m_new)
        l_ref[...] = alpha * l_ref[...] + jnp.sum(p, axis=-1, keepdims=True)
        acc_ref[...] = alpha * acc_ref[...] + _dot(p, v)
    m_ref[...] = m_new


def _nsa_att_body(q_ref, gts_ref, ocmp_ref, sel_ref, ks_ref, vs_ref, kw_ref, vw_ref, gb_ref, g_ref,
                  o_ref, m_s, l_s, acc_s, m_w, l_w, acc_w):
    tq = q_ref.shape[0]
    n_slc = sel_ref.shape[1]
    qi = pl.program_id(1)
    scale = HEAD_DIM ** -0.5
    q = q_ref[...] * scale
    qs = jnp.concatenate([q[:, h * HEAD_DIM:(h + 1) * HEAD_DIM] for h in range(H_ATT)], axis=0).astype(BF16)
    sel = sel_ref[...]
    qpos = lax.broadcasted_iota(jnp.int32, (tq, TK), 0)
    kpos = lax.broadcasted_iota(jnp.int32, (tq, TK), 1)
    blk_row = lax.broadcasted_iota(jnp.int32, (n_slc, TK), 0)
    blk_col = jnp.right_shift(lax.broadcasted_iota(jnp.int32, (n_slc, TK), 1), SLC_SHIFT)

    def sel_bias(j, extra=None):
        expand = jnp.where(blk_row == j * (TK // SLC_BLOCK) + blk_col, 1.0, 0.0).astype(BF16)
        picked = jnp.dot(sel, expand, preferred_element_type=F32) > 0.5
        if extra is not None:
            picked = picked & extra
        return jnp.where(picked, 0.0, NEG)

    causal = kpos <= qpos
    _attend(qs, ks_ref, vs_ref, qi, sel_bias(qi, causal), m_s, l_s, acc_s, first=True)

    def body(j, carry):
        _attend(qs, ks_ref, vs_ref, j, sel_bias(j), m_s, l_s, acc_s, first=False)
        return carry

    lax.fori_loop(0, qi, body, 0)
    _attend(qs, kw_ref, vw_ref, qi, jnp.where(causal, 0.0, NEG), m_w, l_w, acc_w, first=True)
    n_back = WINDOW // TK
    for d in range(1, n_back + 1):
        @pl.when(qi >= d)
        def _():
            if d < n_back:
                bias = None
            else:
                bias = jnp.where(kpos > qpos, 0.0, NEG)
            _attend(qs, kw_ref, vw_ref, qi - d, bias, m_w, l_w, acc_w, first=False)

    o_s = acc_s[...] / l_s[...]
    o_w = acc_w[...] / l_w[...]
    gate = jax.nn.sigmoid(gts_ref[...] + gb_ref[...])
    ocmp = ocmp_ref[...]
    outs = []
    for h in range(H_ATT):
        oc = ocmp[:, h * HEAD_DIM:(h + 1) * HEAD_DIM]
        outs.append(gate[:, h:h + 1] * oc
                    + gate[:, H_ATT + h:H_ATT + h + 1] * o_s[h * tq:(h + 1) * tq]
                    + gate[:, 2 * H_ATT + h:2 * H_ATT + h + 1] * o_w[h * tq:(h + 1) * tq])
    o = jnp.concatenate(outs, axis=-1)
    o_ref[...] = _rms(o, g_ref[...])


def _nsa_att(p_nsa, o_cmp, sel, ks, vs, kw, vw, gate_b, out_g, bsz, seq):
    t = p_nsa.shape[0]
    nq = seq // TQ
    n_slc = seq // SLC_BLOCK
    row = lambda b, i: (b * nq + i, 0)
    kv_spec = pl.BlockSpec((None, seq, HEAD_DIM), lambda b, i: (b, 0, 0))
    stat = pltpu.VMEM((H_ATT * TQ, 1), F32)
    acc = pltpu.VMEM((H_ATT * TQ, HEAD_DIM), F32)
    return pl.pallas_call(
        _nsa_att_body,
        out_shape=jax.ShapeDtypeStruct((t, C_ATT), F32),
        grid=(bsz, nq),
        in_specs=[
            pl.BlockSpec((TQ, C_ATT), row),
            pl.BlockSpec((TQ, 128), lambda b, i: (b * nq + i, NSA_W // 128 - 1)),
            pl.BlockSpec((TQ, C_ATT), row),
            pl.BlockSpec((TQ, n_slc), row),
            kv_spec, kv_spec, kv_spec, kv_spec,
            pl.BlockSpec((1, 128), lambda b, i: (0, 0)),
            pl.BlockSpec((1, C_ATT), lambda b, i: (0, 0)),
        ],
        out_specs=pl.BlockSpec((TQ, C_ATT), row),
        scratch_shapes=[stat, stat, acc, stat, stat, acc],
        compiler_params=_params("parallel", "arbitrary"),
        name="nsa_att",
    )(p_nsa, p_nsa, o_cmp, sel, ks, vs, kw, vw, gate_b, out_g)


def _tri(n, strict):
    r = lax.broadcasted_iota(jnp.int32, (n, n), 0)
    c = lax.broadcasted_iota(jnp.int32, (n, n), 1)
    return (c < r) if strict else (c <= r)


def _rwkv_body(p_ref, mu_ref, w0_ref, w2_ref, a0_ref, a2_ref, g2_ref, kk_ref, ka_ref, rk_ref, lnw_ref,
               lnb_ref, o_ref, state, prev):
    c = pl.program_id(1)
    L = CHUNK
    C = C_RWKV

    @pl.when(c == 0)
    def _():
        state[...] = jnp.zeros_like(state)
        prev[...] = jnp.zeros_like(prev)

    x = p_ref[...]
    rows = lax.broadcasted_iota(jnp.int32, x.shape, 0)
    shifted = jnp.where(rows == 0, prev[0:1, :], pltpu.roll(x, 1, 0))
    prev[0:1, :] = x[L - 1:L, :]
    x = x + mu_ref[...] * (shifted - x)
    r = x[:, 0:C]
    k = x[:, C:2 * C]
    v = x[:, 2 * C:3 * C]
    xw = x[:, 3 * C:3 * C + RANK_W]
    xa = x[:, 3 * C + RANK_W:3 * C + RANK_W + RANK_A]
    xg = x[:, 3 * C + RANK_W + RANK_A:]
    w = -jax.nn.softplus(-(w0_ref[...] + _dot(jnp.tanh(xw), w2_ref[...], precise=True))) - 0.5
    logw = -jnp.exp(w)
    a = jax.nn.sigmoid(a0_ref[...] + _dot(xa, a2_ref[...], precise=True))
    g = _dot(jax.nn.sigmoid(xg), g2_ref[...])
    kkf = k * kk_ref[...]
    kmod = k * (1.0 + (a - 1.0) * ka_ref[...])
    rk = r * kmod * rk_ref[...]

    incl = _tri(L, strict=False)
    strict = _tri(L, strict=True)
    tri_incl = jnp.where(incl, 1.0, 0.0)
    cum = _dot(tri_incl, logw, precise=True)
    cum_last = cum[L - 1:L, :]
    e_pos = jnp.exp(cum)
    e_prev = jnp.exp(cum - logw)
    e_neg = jnp.exp(-cum)
    e_rem = jnp.exp(cum_last - cum)
    eye = jnp.where(lax.broadcasted_iota(jnp.int32, (L, L), 0) == lax.broadcasted_iota(jnp.int32, (L, L), 1), 1.0, 0.0)

    outs = []
    for h in range(H_RWKV):
        sl = slice(h * HEAD_DIM, (h + 1) * HEAD_DIM)
        kk_h = kkf[:, sl]
        kk_h = kk_h / jnp.maximum(jnp.sqrt(jnp.sum(kk_h * kk_h, axis=-1, keepdims=True)), 1e-12)
        a_h = a[:, sl]
        v_h = v[:, sl]
        r_h = r[:, sl]
        alpha_t = -kk_h * e_prev[:, sl]
        r_t = r_h * e_pos[:, sl]
        beta_t = kk_h * a_h * e_neg[:, sl]
        k_t = kmod[:, sl] * e_neg[:, sl]
        beta_e = kk_h * a_h * e_rem[:, sl]
        k_e = kmod[:, sl] * e_rem[:, sl]
        lhs = jnp.concatenate([alpha_t, r_t], axis=0)
        rhs = jnp.concatenate([beta_t, k_t], axis=0)
        mm = _dot_nt(lhs, rhs, precise=True)
        m_ab = jnp.where(strict, mm[0:L, 0:L], 0.0)
        m_ak = jnp.where(strict, mm[0:L, L:2 * L], 0.0)
        m_rb = jnp.where(incl, mm[L:2 * L, 0:L], 0.0)
        m_rk = jnp.where(incl, mm[L:2 * L, L:2 * L], 0.0)
        t_inv = eye + m_ab
        pw = m_ab
        for _ in range(5):
            pw = _dot(pw, pw, precise=True)
            t_inv = t_inv + _dot(pw, t_inv, precise=True)
        z0 = state[h]
        mv = _dot(jnp.concatenate([m_ak, m_rk], axis=0), v_h, precise=True)
        u = _dot(t_inv, _dot(alpha_t, z0, precise=True) + mv[0:L], precise=True)
        y = _dot(r_t, z0, precise=True) + _dot(m_rb, u, precise=True) + mv[L:2 * L]
        gam = eye * jnp.exp(cum_last[:, sl])
        state[h] = _dot_tn(jnp.concatenate([beta_e, k_e, gam], axis=0),
                           jnp.concatenate([u, v_h, z0], axis=0), precise=True)
        mu_y = jnp.mean(y, axis=-1, keepdims=True)
        var = jnp.mean((y - mu_y) ** 2, axis=-1, keepdims=True)
        yn = (y - mu_y) * lax.rsqrt(var + RWKV_GN_EPS)
        bonus = jnp.sum(rk[:, sl], axis=-1, keepdims=True) * v_h
        outs.append((yn, bonus))
    yn = jnp.concatenate([o[0] for o in outs], axis=-1)
    bonus = jnp.concatenate([o[1] for o in outs], axis=-1)
    o_ref[...] = (yn * lnw_ref[...] + lnb_ref[...] + bonus) * g


def _rwkv(p_rw, mu, w0, w2, a0, a2, g2, k_k, k_a, r_k, ln_w, ln_b, bsz, seq):
    t = p_rw.shape[0]
    nc = seq // CHUNK
    row = lambda b, c: (b * nc + c, 0)
    fix = lambda b, c: (0, 0)
    vec = lambda a: a.reshape(1, -1)
    args = (vec(mu), vec(w0), w2, vec(a0), a2, g2, vec(k_k), vec(k_a), vec(r_k), vec(ln_w), vec(ln_b))
    return pl.pallas_call(
        _rwkv_body,
        out_shape=jax.ShapeDtypeStruct((t, C_RWKV), F32),
        grid=(bsz, nc),
        in_specs=[pl.BlockSpec((CHUNK, RWKV_W), row)] + [pl.BlockSpec(a.shape, fix) for a in args],
        out_specs=pl.BlockSpec((CHUNK, C_RWKV), row),
        scratch_shapes=[pltpu.VMEM((H_RWKV, HEAD_DIM, HEAD_DIM), F32), pltpu.VMEM((8, RWKV_W), F32)],
        compiler_params=_params("parallel", "arbitrary"),
        name="rwkv",
    )(p_rw, *args)


def _mlstm_body(p_ref, cw_ref, cb_ref, gb_ref, ng_ref, o_ref, cstate, mstate, prev):
    c = pl.program_id(1)
    L = CHUNK
    C = C_MLSTM
    D = HEAD_DIM

    @pl.when(c == 0)
    def _():
        cstate[...] = jnp.zeros_like(cstate)
        mstate[...] = jnp.zeros_like(mstate)
        prev[...] = jnp.zeros_like(prev)

    qk_in = p_ref[:, 0:2 * C]
    pv = prev[...]
    rows8 = lax.broadcasted_iota(jnp.int32, pv.shape, 0)
    conv = qk_in * cw_ref[CONV_WIDTH - 1:CONV_WIDTH, :] + cb_ref[...]
    for d in range(1, CONV_WIDTH):
        rolled = pltpu.roll(qk_in, d, 0)
        top = jnp.where(rows8 < d, pltpu.roll(pv, d, 0), rolled[0:8])
        sh = jnp.concatenate([top, rolled[8:]], axis=0)
        conv = conv + sh * cw_ref[CONV_WIDTH - 1 - d:CONV_WIDTH - d, :]
    prev[...] = qk_in[L - 8:L, :]
    qk = conv * jax.nn.sigmoid(conv)
    q = qk[:, 0:C]
    k = qk[:, C:2 * C] * (D ** -0.5)
    v = p_ref[:, 2 * C:3 * C]
    og = p_ref[:, 3 * C:4 * C]
    gates = p_ref[:, 4 * C:4 * C + 128] + gb_ref[...]
    lane = lax.broadcasted_iota(jnp.int32, gates.shape, 1)
    is_f = (lane >= H_MLSTM) & (lane < 2 * H_MLSTM)
    gl = jnp.where(is_f, jax.nn.log_sigmoid(gates), gates)
    tri_incl = jnp.where(_tri(L, strict=False), 1.0, 0.0)
    bcum = _dot(tri_incl, gl, precise=True)
    gl_t = gl.T
    bcum_t = _dot_nt(gl_t, tri_incl, precise=True)
    m_all = mstate[...]
    causal = _tri(L, strict=False)
    ones_col = jnp.where(lax.broadcasted_iota(jnp.int32, (L, D), 1) == 0, 1.0, 0.0)

    outs = []
    for h in range(H_MLSTM):
        sl = slice(h * D, (h + 1) * D)
        q_h, k_h, v_h = q[:, sl], k[:, sl], v[:, sl]
        v_ext = jnp.concatenate([v_h, ones_col], axis=-1)
        log_i = gl[:, h:h + 1]
        b_col = bcum[:, H_MLSTM + h:H_MLSTM + h + 1]
        log_i_row = gl_t[h:h + 1, :]
        b_row = bcum_t[H_MLSTM + h:H_MLSTM + h + 1, :]
        g_tot = b_col[L - 1:L, :]
        m0 = m_all[0:1, h:h + 1]
        ct0 = cstate[h]
        a_inter = b_col + m0
        dmat = jnp.where(causal, b_col - b_row + log_i_row, -jnp.inf)
        m_t = jnp.maximum(a_inter, jnp.max(dmat, axis=-1, keepdims=True))
        w_inter = jnp.exp(a_inter - m_t)
        s = jnp.exp(dmat - m_t) * _dot_nt(q_h, k_h)
        num = w_inter * _dot(q_h, ct0) + _dot(s, v_ext)
        den = num[:, D:D + 1]
        hh = num[:, 0:D] / jnp.maximum(jnp.abs(den), jnp.exp(-m_t))
        hh = hh * lax.rsqrt(jnp.mean(hh * hh, axis=-1, keepdims=True) + NORM_EPS)
        outs.append(hh)
        u = g_tot - b_col + log_i
        m_new = jnp.maximum(g_tot + m0, jnp.max(u, axis=0, keepdims=True))
        dec = jnp.exp(g_tot + m0 - m_new)
        wgt = jnp.exp(u - m_new)
        cstate[h] = dec * ct0 + _dot_tn(k_h, wgt * v_ext)
        lane8 = lax.broadcasted_iota(jnp.int32, m_all.shape, 1)
        m_all = jnp.where(lane8 == h, m_new, m_all)
    mstate[...] = m_all
    hcat = jnp.concatenate(outs, axis=-1)
    o_ref[...] = jax.nn.sigmoid(og) * (hcat * ng_ref[...])


def _mlstm(p_ml, conv_w, conv_b, gate_b, norm_g, bsz, seq):
    t = p_ml.shape[0]
    nc = seq // CHUNK
    row = lambda b, c: (b * nc + c, 0)
    fix = lambda b, c: (0, 0)
    args = (conv_w, conv_b.reshape(1, -1), gate_b, norm_g.reshape(1, -1))
    return pl.pallas_call(
        _mlstm_body,
        out_shape=jax.ShapeDtypeStruct((t, C_MLSTM), F32),
        grid=(bsz, nc),
        in_specs=[pl.BlockSpec((CHUNK, ML_W), row)] + [pl.BlockSpec(a.shape, fix) for a in args],
        out_specs=pl.BlockSpec((CHUNK, C_MLSTM), row),
        scratch_shapes=[pltpu.VMEM((H_MLSTM, HEAD_DIM, 128), F32), pltpu.VMEM((8, 128), F32),
                        pltpu.VMEM((8, 2 * C_MLSTM), F32)],
        compiler_params=_params("parallel", "arbitrary"),
        name="mlstm",
    )(p_ml, *args)


def _out_proj_body(a_ref, b_ref, c_ref, wa_ref, wb_ref, wc_ref, x_ref, gt_ref, g_ref, o_ref):
    y = (jnp.dot(a_ref[...].astype(BF16), wa_ref[...], preferred_element_type=F32)
         + jnp.dot(b_ref[...].astype(BF16), wb_ref[...], preferred_element_type=F32)
         + jnp.dot(c_ref[...].astype(BF16), wc_ref[...], preferred_element_type=F32))
    o_ref[...] = x_ref[...] + gt_ref[...] * _rms(y, g_ref[...])


def _out_proj(o_nsa, o_rw, o_ml, wa, wb, wc, x2, gt, g, seq):
    t, d = x2.shape
    tm = 512
    per_b = seq // tm
    row = lambda i: (i, 0)
    fix = lambda i: (0, 0)
    return pl.pallas_call(
        _out_proj_body,
        out_shape=jax.ShapeDtypeStruct((t, d), F32),
        grid=(t // tm,),
        in_specs=[
            pl.BlockSpec((tm, C_ATT), row), pl.BlockSpec((tm, C_RWKV), row), pl.BlockSpec((tm, C_MLSTM), row),
            pl.BlockSpec(wa.shape, fix), pl.BlockSpec(wb.shape, fix), pl.BlockSpec(wc.shape, fix),
            pl.BlockSpec((tm, d), row),
            pl.BlockSpec((None, 1, d), lambda i: (i // per_b, 0, 0)),
            pl.BlockSpec((1, d), fix),
        ],
        out_specs=pl.BlockSpec((tm, d), row),
        compiler_params=_params("parallel"),
        name="out_proj",
    )(o_nsa, o_rw, o_ml, wa, wb, wc, x2, gt, g)


def _ffn_body(x_ref, sc_ref, sh_ref, gt_ref, gpre_ref, gpost_ref, wg_ref, wu_ref, wd_ref, o_ref, hb, acc):
    f = pl.program_id(1)

    @pl.when(f == 0)
    def _():
        h = _rms(x_ref[...], gpre_ref[...]) * (1.0 + sc_ref[...]) + sh_ref[...]
        hb[...] = h.astype(BF16)
        acc[...] = jnp.zeros_like(acc)

    hv = hb[...]
    gate = jnp.dot(hv, wg_ref[...], preferred_element_type=F32)
    up = jnp.dot(hv, wu_ref[...], preferred_element_type=F32)
    act = (gate * jax.nn.sigmoid(gate) * up).astype(BF16)
    acc[...] += jnp.dot(act, wd_ref[...], preferred_element_type=F32)

    @pl.when(f == pl.num_programs(1) - 1)
    def _():
        o_ref[...] = x_ref[...] + gt_ref[...] * _rms(acc[...], gpost_ref[...])


def _ffn(x2, sc, sh, gt, g_pre, g_post, wg, wu, wd, seq):
    t, d = x2.shape
    dff = wg.shape[1]
    tm = 512
    tf = dff // 2
    per_b = seq // tm
    row = lambda i, f: (i, 0)
    bat = lambda i, f: (i // per_b, 0, 0)
    fix = lambda i, f: (0, 0)
    return pl.pallas_call(
        _ffn_body,
        out_shape=jax.ShapeDtypeStruct((t, d), F32),
        grid=(t // tm, dff // tf),
        in_specs=[
            pl.BlockSpec((tm, d), row),
            pl.BlockSpec((None, 1, d), bat), pl.BlockSpec((None, 1, d), bat), pl.BlockSpec((None, 1, d), bat),
            pl.BlockSpec((1, d), fix), pl.BlockSpec((1, d), fix),
            pl.BlockSpec((d, tf), lambda i, f: (0, f)),
            pl.BlockSpec((d, tf), lambda i, f: (0, f)),
            pl.BlockSpec((tf, d), lambda i, f: (f, 0)),
        ],
        out_specs=pl.BlockSpec((tm, d), row),
        scratch_shapes=[pltpu.VMEM((tm, d), BF16), pltpu.VMEM((tm, d), F32)],
        compiler_params=_params("parallel", "arbitrary"),
        name="ffn",
    )(x2, sc, sh, gt, g_pre, g_post, wg, wu, wd)


def _pad_cols(a, width):
    return jnp.pad(a, ((0, 0), (0, width - a.shape[1])))


def _layout_w_in(w_in):
    d_in_rw = NSA_IN + RWKV_W
    nsa = _pad_cols(w_in[:, :NSA_IN], NSA_W)
    rw = w_in[:, NSA_IN:d_in_rw]
    ml = _pad_cols(w_in[:, d_in_rw:], ML_W)
    return jnp.concatenate([nsa, rw, ml], axis=1).astype(BF16)


def _layout_cmp(ck_w1, cv_w1, ck_w2, cv_w2, pe_k, pe_v):
    half = CMP_BLOCK // 2
    ck = ck_w1.reshape(2, half, HEAD_DIM, HEAD_DIM)
    cv = cv_w1.reshape(2, half, HEAD_DIM, HEAD_DIM)
    z = jnp.zeros_like(ck[0])
    top = jnp.concatenate([ck[0], z, ck[1], z], axis=-1)
    bot = jnp.concatenate([z, cv[0], z, cv[1]], axis=-1)
    wc = jnp.concatenate([top, bot], axis=1).reshape(half * 2 * HEAD_DIM, 4 * HEAD_DIM)
    pe = jnp.concatenate([pe_k, pe_v], axis=-1).reshape(2, half * 2 * HEAD_DIM)
    pe8 = jnp.pad(pe, ((0, 6), (0, 0)))
    z2 = jnp.zeros_like(ck_w2)
    w2 = jnp.concatenate([jnp.concatenate([ck_w2, z2], axis=1), jnp.concatenate([z2, cv_w2], axis=1)], axis=0)
    return wc, pe8, w2


def kernel(x, c, w_mod, b_mod, g_pre_mix, g_post_mix, g_pre_ffn, g_post_ffn, w_in, w_out, nsa_pe_k, nsa_pe_v, nsa_ck_w1, nsa_ck_w2, nsa_cv_w1, nsa_cv_w2, nsa_gate_b, nsa_out_g, rw_mu, rw_w0, rw_w2, rw_a0, rw_a2, rw_g2, rw_kk, rw_ka, rw_rk, rw_ln_w, rw_ln_b, ml_conv_w, ml_conv_b, ml_ig_b, ml_fg_b, ml_norm_g, ffn_w_gate, ffn_w_up, ffn_w_down):
    bsz, seq, d = x.shape
    depth = w_mod.shape[0]
    t = bsz * seq
    half = CMP_BLOCK // 2
    mod = _mod(c, w_mod, b_mod)
    x2 = x.reshape(t, d)
    for l in range(depth):
        sh1, sc1, gt1, sh2, sc2, gt2 = [m.reshape(bsz, 1, d) for m in jnp.split(mod[l], 6, axis=-1)]
        p_nsa, p_rw, p_ml = _in_proj(x2, sc1, sh1, g_pre_mix[l].reshape(1, d), _layout_w_in(w_in[l]), seq)
        wc, pe8, w2 = _layout_cmp(nsa_ck_w1[l], nsa_cv_w1[l], nsa_ck_w2[l], nsa_cv_w2[l], nsa_pe_k[l], nsa_pe_v[l])
        xg = p_nsa[:, C_ATT:C_ATT + 2 * HEAD_DIM].reshape(bsz, seq // half, half * 2 * HEAD_DIM)
        kv_cmp = _nsa_cmp(xg, wc, pe8, w2)
        o_cmp, sel = _nsa_sel(p_nsa, kv_cmp, bsz, seq)
        kv = p_nsa[:, C_ATT + 2 * HEAD_DIM:C_ATT + 6 * HEAD_DIM].astype(BF16).reshape(bsz, seq, 4, HEAD_DIM)
        gate_b = jnp.pad(nsa_gate_b[l], (0, 128 - 3 * H_ATT)).reshape(1, 128)
        o_nsa = _nsa_att(p_nsa, o_cmp, sel, kv[:, :, 0], kv[:, :, 1], kv[:, :, 2], kv[:, :, 3], gate_b,
                         nsa_out_g[l].reshape(1, C_ATT), bsz, seq)
        o_rw = _rwkv(p_rw, rw_mu[l], rw_w0[l], rw_w2[l], rw_a0[l], rw_a2[l], rw_g2[l], rw_kk[l], rw_ka[l],
                     rw_rk[l], rw_ln_w[l], rw_ln_b[l], bsz, seq)
        ml_gate_b = jnp.pad(jnp.concatenate([ml_ig_b[l], ml_fg_b[l]]), (0, 128 - 2 * H_MLSTM)).reshape(1, 128)
        o_ml = _mlstm(p_ml, ml_conv_w[l], ml_conv_b[l], ml_gate_b, ml_norm_g[l], bsz, seq)
        wo = w_out[l].astype(BF16)
        x2 = _out_proj(o_nsa, o_rw, o_ml, wo[:C_ATT], wo[C_ATT:C_ATT + C_RWKV], wo[C_ATT + C_RWKV:], x2, gt1,
                       g_post_mix[l].reshape(1, d), seq)
        x2 = _ffn(x2, sc2, sh2, gt2, g_pre_ffn[l].reshape(1, d), g_post_ffn[l].reshape(1, d),
                  ffn_w_gate[l].astype(BF16), ffn_w_up[l].astype(BF16), ffn_w_down[l].astype(BF16), seq)
    return x2.reshape(bsz, seq, d)
```

```python
import functools

import jax
import jax.numpy as jnp
from jax import lax
from jax.experimental import pallas as pl
from jax.experimental.pallas import tpu as pltpu

F32 = jnp.float32
BF16 = jnp.bfloat16
HIGHEST = lax.Precision.HIGHEST

HEAD_DIM = 64
H_ATT = 4
C_ATT = H_ATT * HEAD_DIM
H_RWKV = 6
C_RWKV = H_RWKV * HEAD_DIM
H_MLSTM = 6
C_MLSTM = H_MLSTM * HEAD_DIM
CMP_BLOCK = 32
CMP_STRIDE = 16
SLC_BLOCK = 64
SLC_SHIFT = 6
N_SELECT = 16
WINDOW = 512
NEG = -1e30
FORCE = 1e9
MASK_BIG = 2.0 ** 100
LOG2E = 1.4426950408889634
RANK_W = 64
RANK_A = 64
RANK_G = 128
RWKV_GN_EPS = 64e-5
CHUNK = 64
CONV_WIDTH = 4
NORM_EPS = 1e-6

NSA_W = 768
RWKV_W = 3 * C_RWKV + RANK_W + RANK_A + RANK_G
ML_W = 4 * C_MLSTM + 128
NSA_IN = C_ATT + 6 * HEAD_DIM + 3 * H_ATT
ML_IN = 4 * C_MLSTM + 2 * H_MLSTM

VMEM_LIMIT = 56 * 1024 * 1024

TQ = 256
TK = 256


def _params(*sem):
    return pltpu.CompilerParams(dimension_semantics=sem, vmem_limit_bytes=VMEM_LIMIT)


def _dot(a, b, precise=False):
    if precise:
        return jnp.dot(a.astype(F32), b.astype(F32), preferred_element_type=F32, precision=HIGHEST)
    return jnp.dot(a.astype(BF16), b.astype(BF16), preferred_element_type=F32)


def _dot_nt(a, b, precise=False):
    dn = (((1,), (1,)), ((), ()))
    if precise:
        return lax.dot_general(a.astype(F32), b.astype(F32), dn, preferred_element_type=F32, precision=HIGHEST)
    return lax.dot_general(a.astype(BF16), b.astype(BF16), dn, preferred_element_type=F32)


def _dot_tn(a, b, precise=False):
    dn = (((0,), (0,)), ((), ()))
    if precise:
        return lax.dot_general(a.astype(F32), b.astype(F32), dn, preferred_element_type=F32, precision=HIGHEST)
    return lax.dot_general(a.astype(BF16), b.astype(BF16), dn, preferred_element_type=F32)


def _rms(x, g):
    return x * lax.rsqrt(jnp.mean(x * x, axis=-1, keepdims=True) + NORM_EPS) * g


def _mod_body(c_ref, w_ref, b_ref, o_ref):
    c = c_ref[...]
    cs = c * jax.nn.sigmoid(c)
    o_ref[...] = _dot(cs, w_ref[...], precise=True) + b_ref[...]


def _mod(c, w_mod, b_mod):
    depth, d, n = w_mod.shape
    bsz = c.shape[0]
    tn = 1536
    return pl.pallas_call(
        _mod_body,
        out_shape=jax.ShapeDtypeStruct((depth, bsz, n), F32),
        grid=(depth, n // tn),
        in_specs=[
            pl.BlockSpec((bsz, d), lambda l, j: (0, 0)),
            pl.BlockSpec((None, d, tn), lambda l, j: (l, 0, j)),
            pl.BlockSpec((None, 1, tn), lambda l, j: (l, 0, j)),
        ],
        out_specs=pl.BlockSpec((None, bsz, tn), lambda l, j: (l, 0, j)),
        compiler_params=_params("parallel", "parallel"),
        name="mod",
    )(c, w_mod, b_mod.reshape(depth, 1, n))


def _in_proj_body(x_ref, sc_ref, sh_ref, g_ref, w_ref, nsa_ref, rw_ref, ml_ref):
    h = _rms(x_ref[...], g_ref[...]) * (1.0 + sc_ref[...]) + sh_ref[...]
    hb = h.astype(BF16)
    nsa_ref[...] = jnp.dot(hb, w_ref[:, 0:NSA_W], preferred_element_type=F32)
    rw_ref[...] = jnp.dot(hb, w_ref[:, NSA_W:NSA_W + RWKV_W], preferred_element_type=F32)
    ml_ref[...] = jnp.dot(hb, w_ref[:, NSA_W + RWKV_W:], preferred_element_type=F32)


def _in_proj(x2, sc, sh, g, w_p, seq):
    t, d = x2.shape
    tm = 512
    per_b = seq // tm
    row = lambda i: (i, 0)
    bat = lambda i: (i // per_b, 0, 0)
    fix = lambda i: (0, 0)
    return pl.pallas_call(
        _in_proj_body,
        out_shape=(jax.ShapeDtypeStruct((t, NSA_W), F32),
                   jax.ShapeDtypeStruct((t, RWKV_W), F32),
                   jax.ShapeDtypeStruct((t, ML_W), F32)),
        grid=(t // tm,),
        in_specs=[
            pl.BlockSpec((tm, d), row),
            pl.BlockSpec((None, 1, d), bat),
            pl.BlockSpec((None, 1, d), bat),
            pl.BlockSpec((1, d), fix),
            pl.BlockSpec(w_p.shape, fix),
        ],
        out_specs=(pl.BlockSpec((tm, NSA_W), row), pl.BlockSpec((tm, RWKV_W), row),
                   pl.BlockSpec((tm, ML_W), row)),
        compiler_params=_params("parallel"),
        name="in_proj",
    )(x2, sc, sh, g, w_p)


def _nsa_cmp_body(x_ref, wc_ref, pe_ref, w2_ref, o_ref):
    wc = wc_ref[...]
    f = _dot(x_ref[...], wc, precise=True)
    c = _dot(pe_ref[...], wc, precise=True)
    g = f.shape[0]
    second = pltpu.roll(f[:, 128:256], g - 1, 0)
    pre = f[:, 0:128] + second + c[0:1, 0:128] + c[1:2, 128:256]
    act = jax.nn.gelu(pre, approximate=True)
    out = _dot(act, w2_ref[...], precise=True)
    rows = lax.broadcasted_iota(jnp.int32, out.shape, 0)
    o_ref[...] = jnp.where(rows < g - 1, out, 0.0)


def _nsa_cmp(xg, wc, pe8, w2):
    bsz, g, k = xg.shape
    return pl.pallas_call(
        _nsa_cmp_body,
        out_shape=jax.ShapeDtypeStruct((bsz, g, 128), F32),
        grid=(bsz,),
        in_specs=[
            pl.BlockSpec((None, g, k), lambda b: (b, 0, 0)),
            pl.BlockSpec(wc.shape, lambda b: (0, 0)),
            pl.BlockSpec(pe8.shape, lambda b: (0, 0)),
            pl.BlockSpec(w2.shape, lambda b: (0, 0)),
        ],
        out_specs=pl.BlockSpec((None, g, 128), lambda b: (b, 0, 0)),
        compiler_params=_params("parallel"),
        name="nsa_cmp",
    )(xg, wc, pe8, w2)


def _online_tile(k, vt, rhs, mask, m_ref, acc_ref):
    heads = range(H_ATT)
    s = [jnp.dot(k, rhs[h], preferred_element_type=F32) for h in heads]
    if mask is not None:
        s = [jnp.where(mask, x, NEG) for x in s]
    m_old = [m_ref[h:h + 1, :] for h in heads]
    m_new = [jnp.maximum(m_old[h], jnp.max(s[h], axis=0, keepdims=True)) for h in heads]
    p = [jnp.exp2(s[h] - m_new[h]).astype(BF16) for h in heads]
    pv = [jnp.dot(vt, p[h], preferred_element_type=F32) for h in heads]
    for h in heads:
        acc_ref[h] = jnp.exp2(m_old[h] - m_new[h]) * acc_ref[h] + pv[h]
        m_ref[h:h + 1, :] = m_new[h]


def _nsa_body(q_ref, gts_ref, kvc_ref, kse_ref, vse_ref, kw_ref, vwe_ref, gb_ref, g_ref, o_ref,
              m_s, acc_s, m_w, acc_w, ocmp, *, n_slc):
    tq = q_ref.shape[0]
    n_cmp = kvc_ref.shape[0]
    qi = pl.program_id(1)
    q0 = qi * tq
    scale = HEAD_DIM ** -0.5
    D = HEAD_DIM
    qt = q_ref[...].T

    kc = kvc_ref[:, 0:D]
    vc = kvc_ref[:, D:2 * D]
    nidx = lax.broadcasted_iota(jnp.int32, (n_cmp, tq), 0)
    pos = q0 + lax.broadcasted_iota(jnp.int32, (n_cmp, tq), 1)
    cmask = (nidx * CMP_STRIDE + (CMP_BLOCK - 1)) <= pos
    heads = range(H_ATT)
    sc = [_dot(kc, qt[h * D:(h + 1) * D], precise=True) for h in heads]
    sc = [jnp.where(cmask, x * scale, NEG) for x in sc]
    ec = [jnp.where(cmask, jnp.exp(x - jnp.max(x, axis=0, keepdims=True)), 0.0) for x in sc]
    pc = [e / jnp.maximum(jnp.sum(e, axis=0, keepdims=True), 1e-30) for e in ec]
    oc = [_dot_tn(vc, p) for p in pc]
    for h in heads:
        ocmp[h * D:(h + 1) * D, :] = oc[h]
    psum = sum(pc[1:], pc[0])

    jrow = lax.broadcasted_iota(jnp.int32, (n_slc, n_cmp), 0) * SLC_BLOCK
    ncol = lax.broadcasted_iota(jnp.int32, (n_slc, n_cmp), 1) * CMP_STRIDE
    overlap_t = jnp.where((ncol < jrow + SLC_BLOCK) & (ncol + CMP_BLOCK > jrow), 1.0, 0.0)
    imp_t = _dot(overlap_t, psum, precise=True)
    jj = lax.broadcasted_iota(jnp.int32, (n_slc, tq), 0)
    cur = jnp.right_shift(q0 + lax.broadcasted_iota(jnp.int32, (n_slc, tq), 1), SLC_SHIFT)
    forced = (jj == 0) | (jj == cur) | (jj == cur - 1)
    score = jnp.where(forced, FORCE, jnp.where(jj <= cur, imp_t, NEG))
    cnt = jnp.zeros((n_slc, tq), F32)
    for j2 in range(n_slc):
        row = score[j2:j2 + 1, :]
        ge = jnp.where(row >= score, 1.0, 0.0)
        gt = jnp.where(row > score, 1.0, 0.0)
        cnt = cnt + jnp.where(jj > j2, ge, gt)
    selm = jnp.where(cnt < min(N_SELECT, n_slc), 0.0, -1.0).astype(BF16)

    qs = (qt * (scale * LOG2E)).astype(BF16)
    rhs_w = [qs[h * D:(h + 1) * D] for h in range(H_ATT)]
    rhs_s = [jnp.concatenate([rhs_w[h], selm], axis=0) for h in range(H_ATT)]
    m_s[...] = jnp.full(m_s.shape, NEG, F32)
    m_w[...] = jnp.full(m_w.shape, NEG, F32)
    acc_s[...] = jnp.zeros_like(acc_s)
    acc_w[...] = jnp.zeros_like(acc_w)
    kpos = lax.broadcasted_iota(jnp.int32, (TK, tq), 0)
    qpos = lax.broadcasted_iota(jnp.int32, (TK, tq), 1)
    causal = kpos <= qpos

    def tile(ref, j, rows):
        start = pl.multiple_of(j * TK, TK)
        return ref[pl.ds(start, TK), :] if rows else ref[:, pl.ds(start, TK)]

    _online_tile(tile(kse_ref, qi, True), tile(vse_ref, qi, False), rhs_s, causal, m_s, acc_s)

    def body(j, carry):
        _online_tile(tile(kse_ref, j, True), tile(vse_ref, j, False), rhs_s, None, m_s, acc_s)
        return carry

    lax.fori_loop(0, qi, body, 0)
    _online_tile(tile(kw_ref, qi, True), tile(vwe_ref, qi, False), rhs_w, causal, m_w, acc_w)
    n_back = WINDOW // TK
    for d in range(1, n_back + 1):
        @pl.when(qi >= d)
        def _():
            mask = None if d < n_back else kpos > qpos
            _online_tile(tile(kw_ref, qi - d, True), tile(vwe_ref, qi - d, False), rhs_w, mask, m_w, acc_w)

    gate = jax.nn.sigmoid((gts_ref[...] + gb_ref[...]).T)
    outs = []
    for h in range(H_ATT):
        a_s = acc_s[h]
        a_w = acc_w[h]
        outs.append(gate[h:h + 1] * ocmp[h * D:(h + 1) * D, :]
                    + gate[H_ATT + h:H_ATT + h + 1] * (a_s[0:D] / a_s[D:D + 1])
                    + gate[2 * H_ATT + h:2 * H_ATT + h + 1] * (a_w[0:D] / a_w[D:D + 1]))
    ot = jnp.concatenate(outs, axis=0)
    ot = ot * lax.rsqrt(jnp.mean(ot * ot, axis=0, keepdims=True) + NORM_EPS)
    o_ref[...] = ot.T * g_ref[...]


def _nsa(p_nsa, kv_cmp, kse, vse, kw, vwe, gate_b, out_g, bsz, seq):
    t = p_nsa.shape[0]
    nq = seq // TQ
    n_slc = seq // SLC_BLOCK
    g = kv_cmp.shape[1]
    row = lambda b, i: (b * nq + i, 0)
    per_b = lambda b, i: (b, 0, 0)
    fix = lambda b, i: (0, 0)
    return pl.pallas_call(
        functools.partial(_nsa_body, n_slc=n_slc),
        out_shape=jax.ShapeDtypeStruct((t, C_ATT), F32),
        grid=(bsz, nq),
        in_specs=[
            pl.BlockSpec((TQ, C_ATT), row),
            pl.BlockSpec((TQ, 128), lambda b, i: (b * nq + i, NSA_W // 128 - 1)),
            pl.BlockSpec((None, g, 128), per_b),
            pl.BlockSpec((None, seq, HEAD_DIM + n_slc), per_b),
            pl.BlockSpec((None, 128, seq), per_b),
            pl.BlockSpec((None, seq, HEAD_DIM), per_b),
            pl.BlockSpec((None, 128, seq), per_b),
            pl.BlockSpec((1, 128), fix),
            pl.BlockSpec((1, C_ATT), fix),
        ],
        out_specs=pl.BlockSpec((TQ, C_ATT), row),
        scratch_shapes=[pltpu.VMEM((8, TQ), F32), pltpu.VMEM((H_ATT, 128, TQ), F32),
                        pltpu.VMEM((8, TQ), F32), pltpu.VMEM((H_ATT, 128, TQ), F32),
                        pltpu.VMEM((C_ATT, TQ), F32)],
        compiler_params=_params("parallel", "arbitrary"),
        name="nsa",
    )(p_nsa, p_nsa, kv_cmp, kse, vse, kw, vwe, gate_b, out_g)


def _nsa_kv_layout(p_nsa, bsz, seq):
    n_slc = seq // SLC_BLOCK
    kv = p_nsa[:, C_ATT + 2 * HEAD_DIM:C_ATT + 6 * HEAD_DIM].astype(BF16).reshape(bsz, seq, 4, HEAD_DIM)
    blk = jnp.arange(seq)[:, None] // SLC_BLOCK == jnp.arange(n_slc)[None, :]
    expand = jnp.where(blk, MASK_BIG, 0.0).astype(BF16)
    kse = jnp.concatenate([kv[:, :, 0], jnp.broadcast_to(expand, (bsz, seq, n_slc))], axis=-1)
    tail = jnp.concatenate([jnp.ones((bsz, 1, seq), BF16), jnp.zeros((bsz, 128 - HEAD_DIM - 1, seq), BF16)], axis=1)
    vse = jnp.concatenate([jnp.swapaxes(kv[:, :, 1], 1, 2), tail], axis=1)
    vwe = jnp.concatenate([jnp.swapaxes(kv[:, :, 3], 1, 2), tail], axis=1)
    return kse, vse, kv[:, :, 2], vwe


def _tri(n, strict):
    r = lax.broadcasted_iota(jnp.int32, (n, n), 0)
    c = lax.broadcasted_iota(jnp.int32, (n, n), 1)
    return (c < r) if strict else (c <= r)


def _unit_lower_inverse(ms, eye):
    n = eye.shape[0]
    idx = range(len(ms))
    r = lax.broadcasted_iota(jnp.int32, (n, n), 0)
    c = lax.broadcasted_iota(jnp.int32, (n, n), 1)
    base = 8
    diag = jnp.right_shift(r, 3) == jnp.right_shift(c, 3)
    m8 = [jnp.where(diag, m, 0.0) for m in ms]
    t = [eye + m for m in m8]
    p2 = [_dot(m, m) for m in m8]
    t = [t[i] + _dot(p2[i], t[i]) for i in idx]
    p4 = [_dot(p, p) for p in p2]
    t = [t[i] + _dot(p4[i], t[i]) for i in idx]
    b = base
    while b < n:
        sh = b.bit_length() - 1
        pair = jnp.right_shift(r, sh + 1) == jnp.right_shift(c, sh + 1)
        lower_left = pair & (jnp.right_shift(r, sh) != jnp.right_shift(c, sh))
        left = [_dot(t[i], jnp.where(lower_left, ms[i], 0.0)) for i in idx]
        t = [t[i] + _dot(left[i], t[i]) for i in idx]
        b *= 2
    return t


def _rwkv_body(p_ref, mu_ref, w0_ref, w2_ref, a0_ref, a2_ref, g2_ref, kk_ref, ka_ref, rk_ref, lnw_ref,
               lnb_ref, o_ref, state, prev):
    c = pl.program_id(1)
    L = CHUNK
    C = C_RWKV

    @pl.when(c == 0)
    def _():
        state[...] = jnp.zeros_like(state)
        prev[...] = jnp.zeros_like(prev)

    x = p_ref[...]
    rows = lax.broadcasted_iota(jnp.int32, x.shape, 0)
    shifted = jnp.where(rows == 0, prev[0:1, :], pltpu.roll(x, 1, 0))
    prev[0:1, :] = x[L - 1:L, :]
    x = x + mu_ref[...] * (shifted - x)
    r = x[:, 0:C]
    k = x[:, C:2 * C]
    v = x[:, 2 * C:3 * C]
    xw = x[:, 3 * C:3 * C + RANK_W]
    xa = x[:, 3 * C + RANK_W:3 * C + RANK_W + RANK_A]
    xg = x[:, 3 * C + RANK_W + RANK_A:]
    w = -jax.nn.softplus(-(w0_ref[...] + _dot(jnp.tanh(xw), w2_ref[...], precise=True))) - 0.5
    logw = -jnp.exp(w)
    a = jax.nn.sigmoid(a0_ref[...] + _dot(xa, a2_ref[...], precise=True))
    g = _dot(jax.nn.sigmoid(xg), g2_ref[...])
    kkf = k * kk_ref[...]
    kmod = k * (1.0 + (a - 1.0) * ka_ref[...])
    rk = r * kmod * rk_ref[...]

    incl = _tri(L, strict=False)
    strict = _tri(L, strict=True)
    tri_incl = jnp.where(incl, 1.0, 0.0)
    cum = _dot(tri_incl, logw, precise=True)
    cum_last = cum[L - 1:L, :]
    e_pos = jnp.exp(cum)
    e_prev = jnp.exp(cum - logw)
    e_neg = jnp.exp(-cum)
    e_rem = jnp.exp(cum_last - cum)
    eye = jnp.where(lax.broadcasted_iota(jnp.int32, (L, L), 0) == lax.broadcasted_iota(jnp.int32, (L, L), 1), 1.0, 0.0)

    heads = range(H_RWKV)
    cols = [slice(h * HEAD_DIM, (h + 1) * HEAD_DIM) for h in heads]
    kk_h = [kkf[:, sl] for sl in cols]
    kk_h = [x_ / jnp.maximum(jnp.sqrt(jnp.sum(x_ * x_, axis=-1, keepdims=True)), 1e-12) for x_ in kk_h]
    v_h = [v[:, sl] for sl in cols]
    alpha_t = [-kk_h[h] * e_prev[:, cols[h]] for h in heads]
    r_t = [r[:, cols[h]] * e_pos[:, cols[h]] for h in heads]
    beta = [kk_h[h] * a[:, cols[h]] for h in heads]
    beta_t = [beta[h] * e_neg[:, cols[h]] for h in heads]
    k_t = [kmod[:, cols[h]] * e_neg[:, cols[h]] for h in heads]
    beta_e = [beta[h] * e_rem[:, cols[h]] for h in heads]
    k_e = [kmod[:, cols[h]] * e_rem[:, cols[h]] for h in heads]
    mm = [_dot_nt(jnp.concatenate([alpha_t[h], r_t[h]], axis=0),
                  jnp.concatenate([beta_t[h], k_t[h]], axis=0)) for h in heads]
    m_ab = [jnp.where(strict, mm[h][0:L, 0:L], 0.0) for h in heads]
    m_akrk = [jnp.concatenate([jnp.where(strict, mm[h][0:L, L:2 * L], 0.0),
                               jnp.where(incl, mm[h][L:2 * L, L:2 * L], 0.0)], axis=0) for h in heads]
    m_rb = [jnp.where(incl, mm[h][L:2 * L, 0:L], 0.0) for h in heads]
    t_inv = _unit_lower_inverse(m_ab, eye)
    z0 = [state[h] for h in heads]
    mv = [_dot(m_akrk[h], v_h[h]) for h in heads]
    az = [_dot(jnp.concatenate([alpha_t[h], r_t[h]], axis=0), z0[h]) for h in heads]
    u = [_dot(t_inv[h], az[h][0:L] + mv[h][0:L]) for h in heads]
    y = [az[h][L:2 * L] + _dot(m_rb[h], u[h]) + mv[h][L:2 * L] for h in heads]
    upd = [_dot_tn(jnp.concatenate([beta_e[h], k_e[h]], axis=0), jnp.concatenate([u[h], v_h[h]], axis=0))
           for h in heads]
    for h in heads:
        gam_col = jnp.sum(eye * jnp.exp(cum_last[:, cols[h]]), axis=1, keepdims=True)
        state[h] = z0[h] * gam_col + upd[h]
    yn = []
    for h in heads:
        mu_y = jnp.mean(y[h], axis=-1, keepdims=True)
        var = jnp.mean((y[h] - mu_y) ** 2, axis=-1, keepdims=True)
        yn.append((y[h] - mu_y) * lax.rsqrt(var + RWKV_GN_EPS))
    bonus = [jnp.sum(rk[:, cols[h]], axis=-1, keepdims=True) * v_h[h] for h in heads]
    yn = jnp.concatenate(yn, axis=-1)
    bonus = jnp.concatenate(bonus, axis=-1)
    o_ref[...] = (yn * lnw_ref[...] + lnb_ref[...] + bonus) * g


def _rwkv(p_rw, mu, w0, w2, a0, a2, g2, k_k, k_a, r_k, ln_w, ln_b, bsz, seq):
    t = p_rw.shape[0]
    nc = seq // CHUNK
    row = lambda b, c: (b * nc + c, 0)
    fix = lambda b, c: (0, 0)
    vec = lambda a: a.reshape(1, -1)
    args = (vec(mu), vec(w0), w2, vec(a0), a2, g2, vec(k_k), vec(k_a), vec(r_k), vec(ln_w), vec(ln_b))
    return pl.pallas_call(
        _rwkv_body,
        out_shape=jax.ShapeDtypeStruct((t, C_RWKV), F32),
        grid=(bsz, nc),
        in_specs=[pl.BlockSpec((CHUNK, RWKV_W), row)] + [pl.BlockSpec(a.shape, fix) for a in args],
        out_specs=pl.BlockSpec((CHUNK, C_RWKV), row),
        scratch_shapes=[pltpu.VMEM((H_RWKV, HEAD_DIM, HEAD_DIM), F32), pltpu.VMEM((8, RWKV_W), F32)],
        compiler_params=_params("parallel", "arbitrary"),
        name="rwkv",
    )(p_rw, *args)


def _mlstm_body(p_ref, cw_ref, cb_ref, gb_ref, ng_ref, o_ref, cstate, mstate, prev):
    c = pl.program_id(1)
    L = CHUNK
    C = C_MLSTM
    D = HEAD_DIM

    @pl.when(c == 0)
    def _():
        cstate[...] = jnp.zeros_like(cstate)
        mstate[...] = jnp.zeros_like(mstate)
        prev[...] = jnp.zeros_like(prev)

    qk_in = p_ref[:, 0:2 * C]
    pv = prev[...]
    rows8 = lax.broadcasted_iota(jnp.int32, pv.shape, 0)
    conv = qk_in * cw_ref[CONV_WIDTH - 1:CONV_WIDTH, :] + cb_ref[...]
    for d in range(1, CONV_WIDTH):
        rolled = pltpu.roll(qk_in, d, 0)
        top = jnp.where(rows8 < d, pltpu.roll(pv, d, 0), rolled[0:8])
        sh = jnp.concatenate([top, rolled[8:]], axis=0)
        conv = conv + sh * cw_ref[CONV_WIDTH - 1 - d:CONV_WIDTH - d, :]
    prev[...] = qk_in[L - 8:L, :]
    qk = conv * jax.nn.sigmoid(conv)
    q = qk[:, 0:C]
    k = qk[:, C:2 * C] * (D ** -0.5)
    v = p_ref[:, 2 * C:3 * C]
    og = p_ref[:, 3 * C:4 * C]
    gates = p_ref[:, 4 * C:4 * C + 128] + gb_ref[...]
    lane = lax.broadcasted_iota(jnp.int32, gates.shape, 1)
    is_f = (lane >= H_MLSTM) & (lane < 2 * H_MLSTM)
    gl = jnp.where(is_f, jax.nn.log_sigmoid(gates), gates)
    tri_incl = jnp.where(_tri(L, strict=False), 1.0, 0.0)
    bcum = _dot(tri_incl, gl, precise=True)
    gl_t = gl.T
    bcum_t = _dot_nt(gl_t, tri_incl, precise=True)
    m_all = mstate[...]
    causal = _tri(L, strict=False)
    ones_col = jnp.where(lax.broadcasted_iota(jnp.int32, (L, D), 1) == 0, 1.0, 0.0)

    heads = range(H_MLSTM)
    cols = [slice(h * D, (h + 1) * D) for h in heads]
    q_h = [q[:, sl] for sl in cols]
    k_h = [k[:, sl] for sl in cols]
    v_ext = [jnp.concatenate([v[:, sl], ones_col], axis=-1) for sl in cols]
    log_i = [gl[:, h:h + 1] for h in heads]
    b_col = [bcum[:, H_MLSTM + h:H_MLSTM + h + 1] for h in heads]
    g_tot = [b[L - 1:L, :] for b in b_col]
    m0 = [m_all[0:1, h:h + 1] for h in heads]
    ct0 = [cstate[h] for h in heads]
    qk = [_dot_nt(q_h[h], k_h[h]) for h in heads]
    inter = [_dot(q_h[h], ct0[h]) for h in heads]
    a_inter = [b_col[h] + m0[h] for h in heads]
    dmat = [jnp.where(causal, b_col[h] - bcum_t[H_MLSTM + h:H_MLSTM + h + 1, :] + gl_t[h:h + 1, :], -jnp.inf)
            for h in heads]
    m_t = [jnp.maximum(a_inter[h], jnp.max(dmat[h], axis=-1, keepdims=True)) for h in heads]
    s = [jnp.exp(dmat[h] - m_t[h]) * qk[h] for h in heads]
    sv = [_dot(s[h], v_ext[h]) for h in heads]
    u = [g_tot[h] - b_col[h] + log_i[h] for h in heads]
    m_new = [jnp.maximum(g_tot[h] + m0[h], jnp.max(u[h], axis=0, keepdims=True)) for h in heads]
    kv = [_dot_tn(k_h[h], jnp.exp(u[h] - m_new[h]) * v_ext[h]) for h in heads]
    lane8 = lax.broadcasted_iota(jnp.int32, m_all.shape, 1)
    outs = []
    for h in heads:
        cstate[h] = jnp.exp(g_tot[h] + m0[h] - m_new[h]) * ct0[h] + kv[h]
        m_all = jnp.where(lane8 == h, m_new[h], m_all)
        num = jnp.exp(a_inter[h] - m_t[h]) * inter[h] + sv[h]
        hh = num[:, 0:D] / jnp.maximum(jnp.abs(num[:, D:D + 1]), jnp.exp(-m_t[h]))
        outs.append(hh * lax.rsqrt(jnp.mean(hh * hh, axis=-1, keepdims=True) + NORM_EPS))
    mstate[...] = m_all
    hcat = jnp.concatenate(outs, axis=-1)
    o_ref[...] = jax.nn.sigmoid(og) * (hcat * ng_ref[...])


def _mlstm(p_ml, conv_w, conv_b, gate_b, norm_g, bsz, seq):
    t = p_ml.shape[0]
    nc = seq // CHUNK
    row = lambda b, c: (b * nc + c, 0)
    fix = lambda b, c: (0, 0)
    args = (conv_w, conv_b.reshape(1, -1), gate_b, norm_g.reshape(1, -1))
    return pl.pallas_call(
        _mlstm_body,
        out_shape=jax.ShapeDtypeStruct((t, C_MLSTM), F32),
        grid=(bsz, nc),
        in_specs=[pl.BlockSpec((CHUNK, ML_W), row)] + [pl.BlockSpec(a.shape, fix) for a in args],
        out_specs=pl.BlockSpec((CHUNK, C_MLSTM), row),
        scratch_shapes=[pltpu.VMEM((H_MLSTM, HEAD_DIM, 128), F32), pltpu.VMEM((8, 128), F32),
                        pltpu.VMEM((8, 2 * C_MLSTM), F32)],
        compiler_params=_params("parallel", "arbitrary"),
        name="mlstm",
    )(p_ml, *args)


def _out_proj_body(a_ref, b_ref, c_ref, wa_ref, wb_ref, wc_ref, x_ref, gt_ref, g_ref, o_ref):
    y = (jnp.dot(a_ref[...].astype(BF16), wa_ref[...], preferred_element_type=F32)
         + jnp.dot(b_ref[...].astype(BF16), wb_ref[...], preferred_element_type=F32)
         + jnp.dot(c_ref[...].astype(BF16), wc_ref[...], preferred_element_type=F32))
    o_ref[...] = x_ref[...] + gt_ref[...] * _rms(y, g_ref[...])


def _out_proj(o_nsa, o_rw, o_ml, wa, wb, wc, x2, gt, g, seq):
    t, d = x2.shape
    tm = 512
    per_b = seq // tm
    row = lambda i: (i, 0)
    fix = lambda i: (0, 0)
    return pl.pallas_call(
        _out_proj_body,
        out_shape=jax.ShapeDtypeStruct((t, d), F32),
        grid=(t // tm,),
        in_specs=[
            pl.BlockSpec((tm, C_ATT), row), pl.BlockSpec((tm, C_RWKV), row), pl.BlockSpec((tm, C_MLSTM), row),
            pl.BlockSpec(wa.shape, fix), pl.BlockSpec(wb.shape, fix), pl.BlockSpec(wc.shape, fix),
            pl.BlockSpec((tm, d), row),
            pl.BlockSpec((None, 1, d), lambda i: (i // per_b, 0, 0)),
            pl.BlockSpec((1, d), fix),
        ],
        out_specs=pl.BlockSpec((tm, d), row),
        compiler_params=_params("parallel"),
        name="out_proj",
    )(o_nsa, o_rw, o_ml, wa, wb, wc, x2, gt, g)


def _ffn_body(x_ref, sc_ref, sh_ref, gt_ref, gpre_ref, gpost_ref, wg_ref, wu_ref, wd_ref, o_ref, hb, acc):
    f = pl.program_id(1)

    @pl.when(f == 0)
    def _():
        h = _rms(x_ref[...], gpre_ref[...]) * (1.0 + sc_ref[...]) + sh_ref[...]
        hb[...] = h.astype(BF16)
        acc[...] = jnp.zeros_like(acc)

    hv = hb[...]
    gate = jnp.dot(hv, wg_ref[...], preferred_element_type=F32)
    up = jnp.dot(hv, wu_ref[...], preferred_element_type=F32)
    act = (gate * jax.nn.sigmoid(gate) * up).astype(BF16)
    acc[...] += jnp.dot(act, wd_ref[...], preferred_element_type=F32)

    @pl.when(f == pl.num_programs(1) - 1)
    def _():
        o_ref[...] = x_ref[...] + gt_ref[...] * _rms(acc[...], gpost_ref[...])


def _ffn(x2, sc, sh, gt, g_pre, g_post, wg, wu, wd, seq):
    t, d = x2.shape
    dff = wg.shape[1]
    tm = 512
    tf = dff // 2
    per_b = seq // tm
    row = lambda i, f: (i, 0)
    bat = lambda i, f: (i // per_b, 0, 0)
    fix = lambda i, f: (0, 0)
    return pl.pallas_call(
        _ffn_body,
        out_shape=jax.ShapeDtypeStruct((t, d), F32),
        grid=(t // tm, dff // tf),
        in_specs=[
            pl.BlockSpec((tm, d), row),
            pl.BlockSpec((None, 1, d), bat), pl.BlockSpec((None, 1, d), bat), pl.BlockSpec((None, 1, d), bat),
            pl.BlockSpec((1, d), fix), pl.BlockSpec((1, d), fix),
            pl.BlockSpec((d, tf), lambda i, f: (0, f)),
            pl.BlockSpec((d, tf), lambda i, f: (0, f)),
            pl.BlockSpec((tf, d), lambda i, f: (f, 0)),
        ],
        out_specs=pl.BlockSpec((tm, d), row),
        scratch_shapes=[pltpu.VMEM((tm, d), BF16), pltpu.VMEM((tm, d), F32)],
        compiler_params=_params("parallel", "arbitrary"),
        name="ffn",
    )(x2, sc, sh, gt, g_pre, g_post, wg, wu, wd)


def _pad_cols(a, width):
    return jnp.pad(a, ((0, 0), (0, width - a.shape[1])))


def _layout_w_in(w_in):
    d_in_rw = NSA_IN + RWKV_W
    nsa = _pad_cols(w_in[:, :NSA_IN], NSA_W)
    rw = w_in[:, NSA_IN:d_in_rw]
    ml = _pad_cols(w_in[:, d_in_rw:], ML_W)
    return jnp.concatenate([nsa, rw, ml], axis=1).astype(BF16)


def _layout_cmp(ck_w1, cv_w1, ck_w2, cv_w2, pe_k, pe_v):
    half = CMP_BLOCK // 2
    ck = ck_w1.reshape(2, half, HEAD_DIM, HEAD_DIM)
    cv = cv_w1.reshape(2, half, HEAD_DIM, HEAD_DIM)
    z = jnp.zeros_like(ck[0])
    top = jnp.concatenate([ck[0], z, ck[1], z], axis=-1)
    bot = jnp.concatenate([z, cv[0], z, cv[1]], axis=-1)
    wc = jnp.concatenate([top, bot], axis=1).reshape(half * 2 * HEAD_DIM, 4 * HEAD_DIM)
    pe = jnp.concatenate([pe_k, pe_v], axis=-1).reshape(2, half * 2 * HEAD_DIM)
    pe8 = jnp.pad(pe, ((0, 6), (0, 0)))
    z2 = jnp.zeros_like(ck_w2)
    w2 = jnp.concatenate([jnp.concatenate([ck_w2, z2], axis=1), jnp.concatenate([z2, cv_w2], axis=1)], axis=0)
    return wc, pe8, w2


def kernel(x, c, w_mod, b_mod, g_pre_mix, g_post_mix, g_pre_ffn, g_post_ffn, w_in, w_out, nsa_pe_k, nsa_pe_v, nsa_ck_w1, nsa_ck_w2, nsa_cv_w1, nsa_cv_w2, nsa_gate_b, nsa_out_g, rw_mu, rw_w0, rw_w2, rw_a0, rw_a2, rw_g2, rw_kk, rw_ka, rw_rk, rw_ln_w, rw_ln_b, ml_conv_w, ml_conv_b, ml_ig_b, ml_fg_b, ml_norm_g, ffn_w_gate, ffn_w_up, ffn_w_down):
    bsz, seq, d = x.shape
    depth = w_mod.shape[0]
    t = bsz * seq
    half = CMP_BLOCK // 2
    mod = _mod(c, w_mod, b_mod)
    x2 = x.reshape(t, d)
    for l in range(depth):
        sh1, sc1, gt1, sh2, sc2, gt2 = [m.reshape(bsz, 1, d) for m in jnp.split(mod[l], 6, axis=-1)]
        p_nsa, p_rw, p_ml = _in_proj(x2, sc1, sh1, g_pre_mix[l].reshape(1, d), _layout_w_in(w_in[l]), seq)
        wc, pe8, w2 = _layout_cmp(nsa_ck_w1[l], nsa_cv_w1[l], nsa_ck_w2[l], nsa_cv_w2[l], nsa_pe_k[l], nsa_pe_v[l])
        xg = p_nsa[:, C_ATT:C_ATT + 2 * HEAD_DIM].reshape(bsz, seq // half, half * 2 * HEAD_DIM)
        kv_cmp = _nsa_cmp(xg, wc, pe8, w2)
        kse, vse, kw, vwe = _nsa_kv_layout(p_nsa, bsz, seq)
        gate_b = jnp.pad(nsa_gate_b[l], (0, 128 - 3 * H_ATT)).reshape(1, 128)
        o_nsa = _nsa(p_nsa, kv_cmp, kse, vse, kw, vwe, gate_b, nsa_out_g[l].reshape(1, C_ATT), bsz, seq)
        o_rw = _rwkv(p_rw, rw_mu[l], rw_w0[l], rw_w2[l], rw_a0[l], rw_a2[l], rw_g2[l], rw_kk[l], rw_ka[l],
                     rw_rk[l], rw_ln_w[l], rw_ln_b[l], bsz, seq)
        ml_gate_b = jnp.pad(jnp.concatenate([ml_ig_b[l], ml_fg_b[l]]), (0, 128 - 2 * H_MLSTM)).reshape(1, 128)
        o_ml = _mlstm(p_ml, ml_conv_w[l], ml_conv_b[l], ml_gate_b, ml_norm_g[l], bsz, seq)
        wo = w_out[l].astype(BF16)
        x2 = _out_proj(o_nsa, o_rw, o_ml, wo[:C_ATT], wo[C_ATT:C_ATT + C_RWKV], wo[C_ATT + C_RWKV:], x2, gt1,
                       g_post_mix[l].reshape(1, d), seq)
        x2 = _ffn(x2, sc2, sh2, gt2, g_pre_ffn[l].reshape(1, d), g_post_ffn[l].reshape(1, d),
                  ffn_w_gate[l].astype(BF16), ffn_w_up[l].astype(BF16), ffn_w_down[l].astype(BF16), seq)
    return x2.reshape(bsz, seq, d)
```

```python
import functools

import jax
import jax.numpy as jnp
from jax import lax
from jax.experimental import pallas as pl
from jax.experimental.pallas import tpu as pltpu

F32 = jnp.float32
BF16 = jnp.bfloat16
HIGHEST = lax.Precision.HIGHEST

HEAD_DIM = 64
H_ATT = 4
C_ATT = H_ATT * HEAD_DIM
H_RWKV = 6
C_RWKV = H_RWKV * HEAD_DIM
H_MLSTM = 6
C_MLSTM = H_MLSTM * HEAD_DIM
CMP_BLOCK = 32
CMP_STRIDE = 16
SLC_BLOCK = 64
SLC_SHIFT = 6
N_SELECT = 16
WINDOW = 512
NEG = -1e30
FORCE = 1e9
MASK_BIG = 2.0 ** 100
LOG2E = 1.4426950408889634
RANK_W = 64
RANK_A = 64
RANK_G = 128
RWKV_GN_EPS = 64e-5
CHUNK = 64
RWKV_BLOCK = 256
ML_BLOCK = 64
CONV_WIDTH = 4
NORM_EPS = 1e-6

NSA_W = 768
RWKV_W = 3 * C_RWKV + RANK_W + RANK_A + RANK_G
ML_W = 4 * C_MLSTM + 128
NSA_IN = C_ATT + 6 * HEAD_DIM + 3 * H_ATT
ML_IN = 4 * C_MLSTM + 2 * H_MLSTM

VMEM_LIMIT = 56 * 1024 * 1024

TQ = 256
TK = 256


def _params(*sem):
    return pltpu.CompilerParams(dimension_semantics=sem, vmem_limit_bytes=VMEM_LIMIT)


def _dot(a, b, precise=False):
    if precise:
        return jnp.dot(a.astype(F32), b.astype(F32), preferred_element_type=F32, precision=HIGHEST)
    return jnp.dot(a.astype(BF16), b.astype(BF16), preferred_element_type=F32)


def _dot_nt(a, b, precise=False):
    dn = (((1,), (1,)), ((), ()))
    if precise:
        return lax.dot_general(a.astype(F32), b.astype(F32), dn, preferred_element_type=F32, precision=HIGHEST)
    return lax.dot_general(a.astype(BF16), b.astype(BF16), dn, preferred_element_type=F32)


def _dot_tn(a, b, precise=False):
    dn = (((0,), (0,)), ((), ()))
    if precise:
        return lax.dot_general(a.astype(F32), b.astype(F32), dn, preferred_element_type=F32, precision=HIGHEST)
    return lax.dot_general(a.astype(BF16), b.astype(BF16), dn, preferred_element_type=F32)


def _rms(x, g):
    return x * lax.rsqrt(jnp.mean(x * x, axis=-1, keepdims=True) + NORM_EPS) * g


def _mod_body(c_ref, w_ref, b_ref, o_ref):
    c = c_ref[...]
    cs = c * jax.nn.sigmoid(c)
    o_ref[...] = _dot(cs, w_ref[...], precise=True) + b_ref[...]


def _mod(c, w_mod, b_mod):
    depth, d, n = w_mod.shape
    bsz = c.shape[0]
    tn = 1536
    return pl.pallas_call(
        _mod_body,
        out_shape=jax.ShapeDtypeStruct((depth, bsz, n), F32),
        grid=(depth, n // tn),
        in_specs=[
            pl.BlockSpec((bsz, d), lambda l, j: (0, 0)),
            pl.BlockSpec((None, d, tn), lambda l, j: (l, 0, j)),
            pl.BlockSpec((None, 1, tn), lambda l, j: (l, 0, j)),
        ],
        out_specs=pl.BlockSpec((None, bsz, tn), lambda l, j: (l, 0, j)),
        compiler_params=_params("parallel", "parallel"),
        name="mod",
    )(c, w_mod, b_mod.reshape(depth, 1, n))


def _in_proj_body(x_ref, sc_ref, sh_ref, g_ref, w_ref, nsa_ref, rw_ref, ml_ref):
    h = _rms(x_ref[...], g_ref[...]) * (1.0 + sc_ref[...]) + sh_ref[...]
    hb = h.astype(BF16)
    nsa_ref[...] = jnp.dot(hb, w_ref[:, 0:NSA_W], preferred_element_type=F32)
    rw_ref[...] = jnp.dot(hb, w_ref[:, NSA_W:NSA_W + RWKV_W], preferred_element_type=F32)
    ml_ref[...] = jnp.dot(hb, w_ref[:, NSA_W + RWKV_W:], preferred_element_type=F32)


def _in_proj(x2, sc, sh, g, w_p, seq):
    t, d = x2.shape
    tm = 512
    per_b = seq // tm
    row = lambda i: (i, 0)
    bat = lambda i: (i // per_b, 0, 0)
    fix = lambda i: (0, 0)
    return pl.pallas_call(
        _in_proj_body,
        out_shape=(jax.ShapeDtypeStruct((t, NSA_W), F32),
                   jax.ShapeDtypeStruct((t, RWKV_W), F32),
                   jax.ShapeDtypeStruct((t, ML_W), F32)),
        grid=(t // tm,),
        in_specs=[
            pl.BlockSpec((tm, d), row),
            pl.BlockSpec((None, 1, d), bat),
            pl.BlockSpec((None, 1, d), bat),
            pl.BlockSpec((1, d), fix),
            pl.BlockSpec(w_p.shape, fix),
        ],
        out_specs=(pl.BlockSpec((tm, NSA_W), row), pl.BlockSpec((tm, RWKV_W), row),
                   pl.BlockSpec((tm, ML_W), row)),
        compiler_params=_params("parallel"),
        name="in_proj",
    )(x2, sc, sh, g, w_p)


def _nsa_cmp_body(x_ref, wc_ref, pe_ref, w2_ref, o_ref):
    wc = wc_ref[...]
    f = _dot(x_ref[...], wc, precise=True)
    c = _dot(pe_ref[...], wc, precise=True)
    g = f.shape[0]
    second = pltpu.roll(f[:, 128:256], g - 1, 0)
    pre = f[:, 0:128] + second + c[0:1, 0:128] + c[1:2, 128:256]
    act = jax.nn.gelu(pre, approximate=True)
    out = _dot(act, w2_ref[...], precise=True)
    rows = lax.broadcasted_iota(jnp.int32, out.shape, 0)
    o_ref[...] = jnp.where(rows < g - 1, out, 0.0)


def _nsa_cmp(xg, wc, pe8, w2):
    bsz, g, k = xg.shape
    return pl.pallas_call(
        _nsa_cmp_body,
        out_shape=jax.ShapeDtypeStruct((bsz, g, 128), F32),
        grid=(bsz,),
        in_specs=[
            pl.BlockSpec((None, g, k), lambda b: (b, 0, 0)),
            pl.BlockSpec(wc.shape, lambda b: (0, 0)),
            pl.BlockSpec(pe8.shape, lambda b: (0, 0)),
            pl.BlockSpec(w2.shape, lambda b: (0, 0)),
        ],
        out_specs=pl.BlockSpec((None, g, 128), lambda b: (b, 0, 0)),
        compiler_params=_params("parallel"),
        name="nsa_cmp",
    )(xg, wc, pe8, w2)


def _nsa_body(q_ref, gts_ref, kvc_ref, kse_ref, vse_ref, kw_ref, vwe_ref, gb_ref, g_ref, o_ref,
              m_s, acc_s, s_even, s_odd, ocmp, *, n_slc):
    tq = q_ref.shape[0]
    n_cmp = kvc_ref.shape[0]
    qi = pl.program_id(1)
    q0 = qi * tq
    scale = HEAD_DIM ** -0.5
    D = HEAD_DIM
    qt = q_ref[...].T

    kc = kvc_ref[:, 0:D]
    vc = kvc_ref[:, D:2 * D]
    nidx = lax.broadcasted_iota(jnp.int32, (n_cmp, tq), 0)
    pos = q0 + lax.broadcasted_iota(jnp.int32, (n_cmp, tq), 1)
    cmask = (nidx * CMP_STRIDE + (CMP_BLOCK - 1)) <= pos
    heads = range(H_ATT)
    sc = [_dot(kc, qt[h * D:(h + 1) * D], precise=True) for h in heads]
    sc = [jnp.where(cmask, x * scale, NEG) for x in sc]
    ec = [jnp.where(cmask, jnp.exp(x - jnp.max(x, axis=0, keepdims=True)), 0.0) for x in sc]
    pc = [e / jnp.maximum(jnp.sum(e, axis=0, keepdims=True), 1e-30) for e in ec]
    oc = [_dot_tn(vc, p) for p in pc]
    for h in heads:
        ocmp[h * D:(h + 1) * D, :] = oc[h]
    psum = sum(pc[1:], pc[0])

    jrow = lax.broadcasted_iota(jnp.int32, (n_slc, n_cmp), 0) * SLC_BLOCK
    ncol = lax.broadcasted_iota(jnp.int32, (n_slc, n_cmp), 1) * CMP_STRIDE
    overlap_t = jnp.where((ncol < jrow + SLC_BLOCK) & (ncol + CMP_BLOCK > jrow), 1.0, 0.0)
    imp_t = _dot(overlap_t, psum, precise=True)
    jj = lax.broadcasted_iota(jnp.int32, (n_slc, tq), 0)
    cur = jnp.right_shift(q0 + lax.broadcasted_iota(jnp.int32, (n_slc, tq), 1), SLC_SHIFT)
    forced = (jj == 0) | (jj == cur) | (jj == cur - 1)
    score = jnp.where(forced, FORCE, jnp.where(jj <= cur, imp_t, NEG))
    group = 8
    cnt = [jnp.zeros((group, tq), F32) for _ in range(n_slc // group)]
    for j2 in range(n_slc):
        row = score[j2:j2 + 1, :]
        for gi in range(n_slc // group):
            sg = score[gi * group:(gi + 1) * group]
            if gi * group > j2:
                hit = jnp.where(row >= sg, 1.0, 0.0)
            elif (gi + 1) * group - 1 <= j2:
                hit = jnp.where(row > sg, 1.0, 0.0)
            else:
                hit = jnp.where(lax.broadcasted_iota(jnp.int32, (group, tq), 0) + gi * group > j2,
                                jnp.where(row >= sg, 1.0, 0.0), jnp.where(row > sg, 1.0, 0.0))
            cnt[gi] = cnt[gi] + hit
    cnt = jnp.concatenate(cnt, axis=0)
    selm = jnp.where(cnt < min(N_SELECT, n_slc), 0.0, -1.0).astype(BF16)

    qs = (qt * (scale * LOG2E)).astype(BF16)
    rhs_w = [qs[h * D:(h + 1) * D] for h in heads]
    rhs_s = [jnp.concatenate([rhs_w[h], selm], axis=0) for h in heads]
    m_s[...] = jnp.full(m_s.shape, NEG, F32)
    acc_s[...] = jnp.zeros_like(acc_s)

    def scores_into(buf, j):
        k = kse_ref[pl.ds(pl.multiple_of(j * TK, TK), TK), :]
        for h in heads:
            buf[h] = jnp.dot(k, rhs_s[h], preferred_element_type=F32)

    def consume(buf, j, mask):
        vt = vse_ref[:, pl.ds(pl.multiple_of(j * TK, TK), TK)]
        for h in heads:
            s = buf[h]
            if mask is not None:
                s = jnp.where(mask, s, NEG)
            m_old = m_s[h:h + 1, :]
            m_new = jnp.maximum(m_old, jnp.max(s, axis=0, keepdims=True))
            p = jnp.exp2(s - m_new).astype(BF16)
            acc_s[h] = jnp.exp2(m_old - m_new) * acc_s[h] + jnp.dot(vt, p, preferred_element_type=F32)
            m_s[h:h + 1, :] = m_new

    kpos = lax.broadcasted_iota(jnp.int32, (TK, tq), 0)
    qpos = lax.broadcasted_iota(jnp.int32, (TK, tq), 1)
    last = jnp.maximum(qi - 1, 0)
    scores_into(s_even, qi)
    scores_into(s_odd, 0)
    consume(s_even, qi, kpos <= qpos)

    def body(jp, carry):
        j = 2 * jp
        scores_into(s_even, jnp.minimum(j + 1, last))
        consume(s_odd, j, None)
        scores_into(s_odd, jnp.minimum(j + 2, last))
        consume(s_even, j + 1, None)
        return carry

    lax.fori_loop(0, qi // 2, body, 0)

    @pl.when(qi % 2 == 1)
    def _():
        consume(s_odd, qi - 1, None)

    span = WINDOW + tq
    w0 = pl.multiple_of(jnp.maximum(q0 - WINDOW, 0), TK)
    kw = kw_ref[pl.ds(w0, span), :]
    vwt = vwe_ref[:, pl.ds(w0, span)]
    kabs = w0 + lax.broadcasted_iota(jnp.int32, (span, tq), 0)
    qabs = q0 + lax.broadcasted_iota(jnp.int32, (span, tq), 1)
    wmask = (kabs <= qabs) & (kabs > qabs - WINDOW)
    sw = [jnp.where(wmask, jnp.dot(kw, rhs_w[h], preferred_element_type=F32), NEG) for h in heads]
    pw = [jnp.exp2(x - jnp.max(x, axis=0, keepdims=True)).astype(BF16) for x in sw]
    acc_w = [jnp.dot(vwt, p, preferred_element_type=F32) for p in pw]

    gate = jax.nn.sigmoid((gts_ref[...] + gb_ref[...]).T)
    outs = []
    for h in heads:
        a_s = acc_s[h]
        a_w = acc_w[h]
        outs.append(gate[h:h + 1] * ocmp[h * D:(h + 1) * D, :]
                    + gate[H_ATT + h:H_ATT + h + 1] * (a_s[0:D] / a_s[D:D + 1])
                    + gate[2 * H_ATT + h:2 * H_ATT + h + 1] * (a_w[0:D] / a_w[D:D + 1]))
    ot = jnp.concatenate(outs, axis=0)
    ot = ot * lax.rsqrt(jnp.mean(ot * ot, axis=0, keepdims=True) + NORM_EPS)
    o_ref[...] = ot.T * g_ref[...]


def _nsa(p_nsa, kv_cmp, kse, vse, kw, vwe, gate_b, out_g, bsz, seq):
    t = p_nsa.shape[0]
    nq = seq // TQ
    n_slc = seq // SLC_BLOCK
    g = kv_cmp.shape[1]
    row = lambda b, i: (b * nq + i, 0)
    per_b = lambda b, i: (b, 0, 0)
    fix = lambda b, i: (0, 0)
    return pl.pallas_call(
        functools.partial(_nsa_body, n_slc=n_slc),
        out_shape=jax.ShapeDtypeStruct((t, C_ATT), F32),
        grid=(bsz, nq),
        in_specs=[
            pl.BlockSpec((TQ, C_ATT), row),
            pl.BlockSpec((TQ, 128), lambda b, i: (b * nq + i, NSA_W // 128 - 1)),
            pl.BlockSpec((None, g, 128), per_b),
            pl.BlockSpec((None, seq, HEAD_DIM + n_slc), per_b),
            pl.BlockSpec((None, 128, seq), per_b),
            pl.BlockSpec((None, seq, HEAD_DIM), per_b),
            pl.BlockSpec((None, 128, seq), per_b),
            pl.BlockSpec((1, 128), fix),
            pl.BlockSpec((1, C_ATT), fix),
        ],
        out_specs=pl.BlockSpec((TQ, C_ATT), row),
        scratch_shapes=[pltpu.VMEM((8, TQ), F32), pltpu.VMEM((H_ATT, 128, TQ), F32),
                        pltpu.VMEM((H_ATT, TK, TQ), F32), pltpu.VMEM((H_ATT, TK, TQ), F32),
                        pltpu.VMEM((C_ATT, TQ), F32)],
        compiler_params=_params("parallel", "arbitrary"),
        name="nsa",
    )(p_nsa, p_nsa, kv_cmp, kse, vse, kw, vwe, gate_b, out_g)


def _nsa_kv_layout(p_nsa, bsz, seq):
    n_slc = seq // SLC_BLOCK
    kv = p_nsa[:, C_ATT + 2 * HEAD_DIM:C_ATT + 6 * HEAD_DIM].astype(BF16).reshape(bsz, seq, 4, HEAD_DIM)
    blk = jnp.arange(seq)[:, None] // SLC_BLOCK == jnp.arange(n_slc)[None, :]
    expand = jnp.where(blk, MASK_BIG, 0.0).astype(BF16)
    kse = jnp.concatenate([kv[:, :, 0], jnp.broadcast_to(expand, (bsz, seq, n_slc))], axis=-1)
    tail = jnp.concatenate([jnp.ones((bsz, 1, seq), BF16), jnp.zeros((bsz, 128 - HEAD_DIM - 1, seq), BF16)], axis=1)
    vse = jnp.concatenate([jnp.swapaxes(kv[:, :, 1], 1, 2), tail], axis=1)
    vwe = jnp.concatenate([jnp.swapaxes(kv[:, :, 3], 1, 2), tail], axis=1)
    return kse, vse, kv[:, :, 2], vwe


def _tri(n, strict):
    r = lax.broadcasted_iota(jnp.int32, (n, n), 0)
    c = lax.broadcasted_iota(jnp.int32, (n, n), 1)
    return (c < r) if strict else (c <= r)


def _unit_lower_inverse(ms, eye):
    n = eye.shape[0]
    idx = range(len(ms))
    r = lax.broadcasted_iota(jnp.int32, (n, n), 0)
    c = lax.broadcasted_iota(jnp.int32, (n, n), 1)
    base = 8
    diag = jnp.right_shift(r, 3) == jnp.right_shift(c, 3)
    m8 = [jnp.where(diag, m, 0.0) for m in ms]
    t = [eye + m for m in m8]
    p2 = [_dot(m, m) for m in m8]
    t = [t[i] + _dot(p2[i], t[i]) for i in idx]
    p4 = [_dot(p, p) for p in p2]
    t = [t[i] + _dot(p4[i], t[i]) for i in idx]
    b = base
    while b < n:
        sh = b.bit_length() - 1
        pair = jnp.right_shift(r, sh + 1) == jnp.right_shift(c, sh + 1)
        lower_left = pair & (jnp.right_shift(r, sh) != jnp.right_shift(c, sh))
        left = [_dot(t[i], jnp.where(lower_left, ms[i], 0.0)) for i in idx]
        t = [t[i] + _dot(left[i], t[i]) for i in idx]
        b *= 2
    return t


def _rwkv_body(p_ref, mu_ref, w0_ref, w2_ref, a0_ref, a2_ref, g2_ref, kk_ref, ka_ref, rk_ref, lnw_ref,
               lnb_ref, o_ref, state, prev):
    L = CHUNK
    C = C_RWKV
    tb = p_ref.shape[0]
    chunks = range(tb // L)

    @pl.when(pl.program_id(1) == 0)
    def _():
        state[...] = jnp.zeros_like(state)
        prev[...] = jnp.zeros_like(prev)

    x = p_ref[...]
    rows = lax.broadcasted_iota(jnp.int32, x.shape, 0)
    shifted = jnp.where(rows == 0, prev[0:1, :], pltpu.roll(x, 1, 0))
    prev[0:1, :] = x[tb - 1:tb, :]
    x = x + mu_ref[...] * (shifted - x)
    r = x[:, 0:C]
    k = x[:, C:2 * C]
    v = x[:, 2 * C:3 * C]
    xw = x[:, 3 * C:3 * C + RANK_W]
    xa = x[:, 3 * C + RANK_W:3 * C + RANK_W + RANK_A]
    xg = x[:, 3 * C + RANK_W + RANK_A:]
    w = -jax.nn.softplus(-(w0_ref[...] + _dot(jnp.tanh(xw), w2_ref[...], precise=True))) - 0.5
    logw = -jnp.exp(w)
    a = jax.nn.sigmoid(a0_ref[...] + _dot(xa, a2_ref[...], precise=True))
    g = _dot(jax.nn.sigmoid(xg), g2_ref[...])
    kkf = k * kk_ref[...]
    kmod = k * (1.0 + (a - 1.0) * ka_ref[...])
    rk = r * kmod * rk_ref[...]

    incl = _tri(L, strict=False)
    strict = _tri(L, strict=True)
    tri_incl = jnp.where(incl, 1.0, 0.0)
    cum = jnp.concatenate([_dot(tri_incl, logw[c * L:(c + 1) * L], precise=True) for c in chunks], axis=0)
    cum_last = [cum[(c + 1) * L - 1:(c + 1) * L, :] for c in chunks]
    cum_end = jnp.concatenate([jnp.broadcast_to(cl, (L, C)) for cl in cum_last], axis=0)
    e_pos = jnp.exp(cum)
    e_prev = jnp.exp(cum - logw)
    e_neg = jnp.exp(-cum)
    e_rem = jnp.exp(cum_end - cum)
    eye = jnp.where(lax.broadcasted_iota(jnp.int32, (L, L), 0) == lax.broadcasted_iota(jnp.int32, (L, L), 1), 1.0, 0.0)

    heads = range(H_RWKV)
    cols = [slice(h * HEAD_DIM, (h + 1) * HEAD_DIM) for h in heads]
    kk_h = [kkf[:, sl] for sl in cols]
    kk_h = [x_ / jnp.maximum(jnp.sqrt(jnp.sum(x_ * x_, axis=-1, keepdims=True)), 1e-12) for x_ in kk_h]
    v_h = [v[:, sl] for sl in cols]
    alpha_t = [-kk_h[h] * e_prev[:, cols[h]] for h in heads]
    r_t = [r[:, cols[h]] * e_pos[:, cols[h]] for h in heads]
    beta = [kk_h[h] * a[:, cols[h]] for h in heads]
    beta_t = [beta[h] * e_neg[:, cols[h]] for h in heads]
    k_t = [kmod[:, cols[h]] * e_neg[:, cols[h]] for h in heads]
    beta_e = [beta[h] * e_rem[:, cols[h]] for h in heads]
    k_e = [kmod[:, cols[h]] * e_rem[:, cols[h]] for h in heads]
    items = [(c, h) for c in chunks for h in heads]
    n_items = range(len(items))
    rs = [slice(c * L, (c + 1) * L) for c, _ in items]
    hs = [h for _, h in items]
    lhs = [jnp.concatenate([alpha_t[hs[i]][rs[i]], r_t[hs[i]][rs[i]]], axis=0) for i in n_items]
    rhs = [jnp.concatenate([beta_t[hs[i]][rs[i]], k_t[hs[i]][rs[i]]], axis=0) for i in n_items]
    v_i = [v_h[hs[i]][rs[i]] for i in n_items]
    mm = [_dot_nt(lhs[i], rhs[i]) for i in n_items]
    m_ab = [jnp.where(strict, mm[i][0:L, 0:L], 0.0) for i in n_items]
    m_akrk = [jnp.concatenate([jnp.where(strict, mm[i][0:L, L:2 * L], 0.0),
                               jnp.where(incl, mm[i][L:2 * L, L:2 * L], 0.0)], axis=0) for i in n_items]
    m_rb = [jnp.where(incl, mm[i][L:2 * L, 0:L], 0.0) for i in n_items]
    t_inv = _unit_lower_inverse(m_ab, eye)
    mv = [_dot(m_akrk[i], v_i[i]) for i in n_items]
    z = [state[h] for h in heads]
    y_parts = [[] for _ in heads]
    for c in chunks:
        idx = [c * H_RWKV + h for h in heads]
        az = [_dot(lhs[i], z[hs[i]]) for i in idx]
        u = [_dot(t_inv[i], az[h][0:L] + mv[i][0:L]) for h, i in enumerate(idx)]
        for h, i in enumerate(idx):
            y_parts[h].append(az[h][L:2 * L] + _dot(m_rb[i], u[h]) + mv[i][L:2 * L])
        upd = [_dot_tn(jnp.concatenate([beta_e[h][rs[i]], k_e[h][rs[i]]], axis=0),
                       jnp.concatenate([u[h], v_i[i]], axis=0)) for h, i in enumerate(idx)]
        z = [z[h] * jnp.sum(eye * jnp.exp(cum_last[c][:, cols[h]]), axis=1, keepdims=True) + upd[h] for h in heads]
    for h in heads:
        state[h] = z[h]
    yn = []
    for h in heads:
        y_h = jnp.concatenate(y_parts[h], axis=0)
        mu_y = jnp.mean(y_h, axis=-1, keepdims=True)
        var = jnp.mean((y_h - mu_y) ** 2, axis=-1, keepdims=True)
        yn.append((y_h - mu_y) * lax.rsqrt(var + RWKV_GN_EPS))
    bonus = [jnp.sum(rk[:, cols[h]], axis=-1, keepdims=True) * v_h[h] for h in heads]
    yn = jnp.concatenate(yn, axis=-1)
    bonus = jnp.concatenate(bonus, axis=-1)
    o_ref[...] = (yn * lnw_ref[...] + lnb_ref[...] + bonus) * g


def _rwkv(p_rw, mu, w0, w2, a0, a2, g2, k_k, k_a, r_k, ln_w, ln_b, bsz, seq):
    t = p_rw.shape[0]
    tb = RWKV_BLOCK
    nb = seq // tb
    row = lambda b, c: (b * nb + c, 0)
    fix = lambda b, c: (0, 0)
    vec = lambda a: a.reshape(1, -1)
    args = (vec(mu), vec(w0), w2, vec(a0), a2, g2, vec(k_k), vec(k_a), vec(r_k), vec(ln_w), vec(ln_b))
    return pl.pallas_call(
        _rwkv_body,
        out_shape=jax.ShapeDtypeStruct((t, C_RWKV), F32),
        grid=(bsz, nb),
        in_specs=[pl.BlockSpec((tb, RWKV_W), row)] + [pl.BlockSpec(a.shape, fix) for a in args],
        out_specs=pl.BlockSpec((tb, C_RWKV), row),
        scratch_shapes=[pltpu.VMEM((H_RWKV, HEAD_DIM, HEAD_DIM), F32), pltpu.VMEM((8, RWKV_W), F32)],
        compiler_params=_params("parallel", "arbitrary"),
        name="rwkv",
    )(p_rw, *args)


def _mlstm_body(p_ref, cw_ref, cb_ref, gb_ref, ng_ref, o_ref, cstate, mstate, prev):
    L = CHUNK
    C = C_MLSTM
    D = HEAD_DIM
    tb = p_ref.shape[0]
    chunks = range(tb // L)

    @pl.when(pl.program_id(1) == 0)
    def _():
        cstate[...] = jnp.zeros_like(cstate)
        mstate[...] = jnp.zeros_like(mstate)
        prev[...] = jnp.zeros_like(prev)

    qk_in = p_ref[:, 0:2 * C]
    pv = prev[...]
    rows8 = lax.broadcasted_iota(jnp.int32, pv.shape, 0)
    conv = qk_in * cw_ref[CONV_WIDTH - 1:CONV_WIDTH, :] + cb_ref[...]
    for d in range(1, CONV_WIDTH):
        rolled = pltpu.roll(qk_in, d, 0)
        top = jnp.where(rows8 < d, pltpu.roll(pv, d, 0), rolled[0:8])
        sh = jnp.concatenate([top, rolled[8:]], axis=0)
        conv = conv + sh * cw_ref[CONV_WIDTH - 1 - d:CONV_WIDTH - d, :]
    prev[...] = qk_in[tb - 8:tb, :]
    qk = conv * jax.nn.sigmoid(conv)
    q = qk[:, 0:C]
    k = qk[:, C:2 * C] * (D ** -0.5)
    v = p_ref[:, 2 * C:3 * C]
    og = p_ref[:, 3 * C:4 * C]
    gates = p_ref[:, 4 * C:4 * C + 128] + gb_ref[...]
    lane = lax.broadcasted_iota(jnp.int32, gates.shape, 1)
    is_f = (lane >= H_MLSTM) & (lane < 2 * H_MLSTM)
    gl = jnp.where(is_f, jax.nn.log_sigmoid(gates), gates)
    tri_incl = jnp.where(_tri(L, strict=False), 1.0, 0.0)
    bcum = [_dot(tri_incl, gl[c * L:(c + 1) * L], precise=True) for c in chunks]
    gl_t = [gl[c * L:(c + 1) * L].T for c in chunks]
    bcum_t = [_dot_nt(g_, tri_incl, precise=True) for g_ in gl_t]
    m_all = mstate[...]
    causal = _tri(L, strict=False)
    ones_col = jnp.where(lax.broadcasted_iota(jnp.int32, (L, D), 1) == 0, 1.0, 0.0)

    heads = range(H_MLSTM)
    items = [(c, h) for c in chunks for h in heads]
    n_items = range(len(items))
    rs = [slice(c * L, (c + 1) * L) for c, _ in items]
    cols = [slice(h * D, (h + 1) * D) for _, h in items]
    q_i = [q[rs[i], cols[i]] for i in n_items]
    k_i = [k[rs[i], cols[i]] for i in n_items]
    v_ext = [jnp.concatenate([v[rs[i], cols[i]], ones_col], axis=-1) for i in n_items]
    log_i = [gl[rs[i], h:h + 1] for i, (c, h) in enumerate(items)]
    b_col = [bcum[c][:, H_MLSTM + h:H_MLSTM + h + 1] for c, h in items]
    g_tot = [b[L - 1:L, :] for b in b_col]
    u = [g_tot[i] - b_col[i] + log_i[i] for i in n_items]
    u_max = [jnp.max(x_, axis=0, keepdims=True) for x_ in u]
    m0 = []
    m_run = [m_all[0:1, h:h + 1] for h in heads]
    for c in chunks:
        for h in heads:
            i = c * H_MLSTM + h
            m0.append(m_run[h])
            m_run[h] = jnp.maximum(g_tot[i] + m_run[h], u_max[i])
    m_next = [m0[i + H_MLSTM] if i + H_MLSTM < len(items) else m_run[items[i][1]] for i in n_items]
    qk = [_dot_nt(q_i[i], k_i[i]) for i in n_items]
    kv = [_dot_tn(k_i[i], jnp.exp(u[i] - m_next[i]) * v_ext[i]) for i in n_items]
    ct = []
    ct_run = [cstate[h] for h in heads]
    for c in chunks:
        for h in heads:
            i = c * H_MLSTM + h
            ct.append(ct_run[h])
            ct_run[h] = jnp.exp(g_tot[i] + m0[i] - m_next[i]) * ct_run[h] + kv[i]
    inter = [_dot(q_i[i], ct[i]) for i in n_items]
    a_inter = [b_col[i] + m0[i] for i in n_items]
    dmat = [jnp.where(causal, b_col[i] - bcum_t[c][H_MLSTM + h:H_MLSTM + h + 1, :] + gl_t[c][h:h + 1, :], -jnp.inf)
            for i, (c, h) in enumerate(items)]
    m_t = [jnp.maximum(a_inter[i], jnp.max(dmat[i], axis=-1, keepdims=True)) for i in n_items]
    s = [jnp.exp(dmat[i] - m_t[i]) * qk[i] for i in n_items]
    sv = [_dot(s[i], v_ext[i]) for i in n_items]
    lane8 = lax.broadcasted_iota(jnp.int32, m_all.shape, 1)
    for h in heads:
        cstate[h] = ct_run[h]
        m_all = jnp.where(lane8 == h, m_run[h], m_all)
    mstate[...] = m_all
    outs = []
    for i in n_items:
        num = jnp.exp(a_inter[i] - m_t[i]) * inter[i] + sv[i]
        hh = num[:, 0:D] / jnp.maximum(jnp.abs(num[:, D:D + 1]), jnp.exp(-m_t[i]))
        outs.append(hh * lax.rsqrt(jnp.mean(hh * hh, axis=-1, keepdims=True) + NORM_EPS))
    hcat = jnp.concatenate([jnp.concatenate(outs[c * H_MLSTM:(c + 1) * H_MLSTM], axis=-1) for c in chunks], axis=0)
    o_ref[...] = jax.nn.sigmoid(og) * (hcat * ng_ref[...])


def _mlstm(p_ml, conv_w, conv_b, gate_b, norm_g, bsz, seq):
    t = p_ml.shape[0]
    tb = ML_BLOCK
    nb = seq // tb
    row = lambda b, c: (b * nb + c, 0)
    fix = lambda b, c: (0, 0)
    args = (conv_w, conv_b.reshape(1, -1), gate_b, norm_g.reshape(1, -1))
    return pl.pallas_call(
        _mlstm_body,
        out_shape=jax.ShapeDtypeStruct((t, C_MLSTM), F32),
        grid=(bsz, nb),
        in_specs=[pl.BlockSpec((tb, ML_W), row)] + [pl.BlockSpec(a.shape, fix) for a in args],
        out_specs=pl.BlockSpec((tb, C_MLSTM), row),
        scratch_shapes=[pltpu.VMEM((H_MLSTM, HEAD_DIM, 128), F32), pltpu.VMEM((8, 128), F32),
                        pltpu.VMEM((8, 2 * C_MLSTM), F32)],
        compiler_params=_params("parallel", "arbitrary"),
        name="mlstm",
    )(p_ml, *args)


def _out_proj_body(a_ref, b_ref, c_ref, wa_ref, wb_ref, wc_ref, x_ref, gt_ref, g_ref, o_ref):
    y = (jnp.dot(a_ref[...].astype(BF16), wa_ref[...], preferred_element_type=F32)
         + jnp.dot(b_ref[...].astype(BF16), wb_ref[...], preferred_element_type=F32)
         + jnp.dot(c_ref[...].astype(BF16), wc_ref[...], preferred_element_type=F32))
    o_ref[...] = x_ref[...] + gt_ref[...] * _rms(y, g_ref[...])


def _out_proj(o_nsa, o_rw, o_ml, wa, wb, wc, x2, gt, g, seq):
    t, d = x2.shape
    tm = 512
    per_b = seq // tm
    row = lambda i: (i, 0)
    fix = lambda i: (0, 0)
    return pl.pallas_call(
        _out_proj_body,
        out_shape=jax.ShapeDtypeStruct((t, d), F32),
        grid=(t // tm,),
        in_specs=[
            pl.BlockSpec((tm, C_ATT), row), pl.BlockSpec((tm, C_RWKV), row), pl.BlockSpec((tm, C_MLSTM), row),
            pl.BlockSpec(wa.shape, fix), pl.BlockSpec(wb.shape, fix), pl.BlockSpec(wc.shape, fix),
            pl.BlockSpec((tm, d), row),
            pl.BlockSpec((None, 1, d), lambda i: (i // per_b, 0, 0)),
            pl.BlockSpec((1, d), fix),
        ],
        out_specs=pl.BlockSpec((tm, d), row),
        compiler_params=_params("parallel"),
        name="out_proj",
    )(o_nsa, o_rw, o_ml, wa, wb, wc, x2, gt, g)


def _ffn_body(x_ref, sc_ref, sh_ref, gt_ref, gpre_ref, gpost_ref, wg_ref, wu_ref, wd_ref, o_ref, hb, acc):
    f = pl.program_id(1)

    @pl.when(f == 0)
    def _():
        h = _rms(x_ref[...], gpre_ref[...]) * (1.0 + sc_ref[...]) + sh_ref[...]
        hb[...] = h.astype(BF16)
        acc[...] = jnp.zeros_like(acc)

    hv = hb[...]
    gate = jnp.dot(hv, wg_ref[...], preferred_element_type=F32)
    up = jnp.dot(hv, wu_ref[...], preferred_element_type=F32)
    act = (gate * jax.nn.sigmoid(gate) * up).astype(BF16)
    acc[...] += jnp.dot(act, wd_ref[...], preferred_element_type=F32)

    @pl.when(f == pl.num_programs(1) - 1)
    def _():
        o_ref[...] = x_ref[...] + gt_ref[...] * _rms(acc[...], gpost_ref[...])


def _ffn(x2, sc, sh, gt, g_pre, g_post, wg, wu, wd, seq):
    t, d = x2.shape
    dff = wg.shape[1]
    tm = 512
    tf = dff // 2
    per_b = seq // tm
    row = lambda i, f: (i, 0)
    bat = lambda i, f: (i // per_b, 0, 0)
    fix = lambda i, f: (0, 0)
    return pl.pallas_call(
        _ffn_body,
        out_shape=jax.ShapeDtypeStruct((t, d), F32),
        grid=(t // tm, dff // tf),
        in_specs=[
            pl.BlockSpec((tm, d), row),
            pl.BlockSpec((None, 1, d), bat), pl.BlockSpec((None, 1, d), bat), pl.BlockSpec((None, 1, d), bat),
            pl.BlockSpec((1, d), fix), pl.BlockSpec((1, d), fix),
            pl.BlockSpec((d, tf), lambda i, f: (0, f)),
            pl.BlockSpec((d, tf), lambda i, f: (0, f)),
            pl.BlockSpec((tf, d), lambda i, f: (f, 0)),
        ],
        out_specs=pl.BlockSpec((tm, d), row),
        scratch_shapes=[pltpu.VMEM((tm, d), BF16), pltpu.VMEM((tm, d), F32)],
        compiler_params=_params("parallel", "arbitrary"),
        name="ffn",
    )(x2, sc, sh, gt, g_pre, g_post, wg, wu, wd)


def _pad_cols(a, width):
    return jnp.pad(a, ((0, 0), (0, width - a.shape[1])))


def _layout_w_in(w_in):
    d_in_rw = NSA_IN + RWKV_W
    nsa = _pad_cols(w_in[:, :NSA_IN], NSA_W)
    rw = w_in[:, NSA_IN:d_in_rw]
    ml = _pad_cols(w_in[:, d_in_rw:], ML_W)
    return jnp.concatenate([nsa, rw, ml], axis=1).astype(BF16)


def _layout_cmp(ck_w1, cv_w1, ck_w2, cv_w2, pe_k, pe_v):
    half = CMP_BLOCK // 2
    ck = ck_w1.reshape(2, half, HEAD_DIM, HEAD_DIM)
    cv = cv_w1.reshape(2, half, HEAD_DIM, HEAD_DIM)
    z = jnp.zeros_like(ck[0])
    top = jnp.concatenate([ck[0], z, ck[1], z], axis=-1)
    bot = jnp.concatenate([z, cv[0], z, cv[1]], axis=-1)
    wc = jnp.concatenate([top, bot], axis=1).reshape(half * 2 * HEAD_DIM, 4 * HEAD_DIM)
    pe = jnp.concatenate([pe_k, pe_v], axis=-1).reshape(2, half * 2 * HEAD_DIM)
    pe8 = jnp.pad(pe, ((0, 6), (0, 0)))
    z2 = jnp.zeros_like(ck_w2)
    w2 = jnp.concatenate([jnp.concatenate([ck_w2, z2], axis=1), jnp.concatenate([z2, cv_w2], axis=1)], axis=0)
    return wc, pe8, w2


def kernel(x, c, w_mod, b_mod, g_pre_mix, g_post_mix, g_pre_ffn, g_post_ffn, w_in, w_out, nsa_pe_k, nsa_pe_v, nsa_ck_w1, nsa_ck_w2, nsa_cv_w1, nsa_cv_w2, nsa_gate_b, nsa_out_g, rw_mu, rw_w0, rw_w2, rw_a0, rw_a2, rw_g2, rw_kk, rw_ka, rw_rk, rw_ln_w, rw_ln_b, ml_conv_w, ml_conv_b, ml_ig_b, ml_fg_b, ml_norm_g, ffn_w_gate, ffn_w_up, ffn_w_down):
    bsz, seq, d = x.shape
    depth = w_mod.shape[0]
    t = bsz * seq
    half = CMP_BLOCK // 2
    mod = _mod(c, w_mod, b_mod)
    x2 = x.reshape(t, d)
    for l in range(depth):
        sh1, sc1, gt1, sh2, sc2, gt2 = [m.reshape(bsz, 1, d) for m in jnp.split(mod[l], 6, axis=-1)]
        p_nsa, p_rw, p_ml = _in_proj(x2, sc1, sh1, g_pre_mix[l].reshape(1, d), _layout_w_in(w_in[l]), seq)
        wc, pe8, w2 = _layout_cmp(nsa_ck_w1[l], nsa_cv_w1[l], nsa_ck_w2[l], nsa_cv_w2[l], nsa_pe_k[l], nsa_pe_v[l])
        xg = p_nsa[:, C_ATT:C_ATT + 2 * HEAD_DIM].reshape(bsz, seq // half, half * 2 * HEAD_DIM)
        kv_cmp = _nsa_cmp(xg, wc, pe8, w2)
        kse, vse, kw, vwe = _nsa_kv_layout(p_nsa, bsz, seq)
        gate_b = jnp.pad(nsa_gate_b[l], (0, 128 - 3 * H_ATT)).reshape(1, 128)
        o_nsa = _nsa(p_nsa, kv_cmp, kse, vse, kw, vwe, gate_b, nsa_out_g[l].reshape(1, C_ATT), bsz, seq)
        o_rw = _rwkv(p_rw, rw_mu[l], rw_w0[l], rw_w2[l], rw_a0[l], rw_a2[l], rw_g2[l], rw_kk[l], rw_ka[l],
                     rw_rk[l], rw_ln_w[l], rw_ln_b[l], bsz, seq)
        ml_gate_b = jnp.pad(jnp.concatenate([ml_ig_b[l], ml_fg_b[l]]), (0, 128 - 2 * H_MLSTM)).reshape(1, 128)
        o_ml = _mlstm(p_ml, ml_conv_w[l], ml_conv_b[l], ml_gate_b, ml_norm_g[l], bsz, seq)
        wo = w_out[l].astype(BF16)
        x2 = _out_proj(o_nsa, o_rw, o_ml, wo[:C_ATT], wo[C_ATT:C_ATT + C_RWKV], wo[C_ATT + C_RWKV:], x2, gt1,
                       g_post_mix[l].reshape(1, d), seq)
        x2 = _ffn(x2, sc2, sh2, gt2, g_pre_ffn[l].reshape(1, d), g_post_ffn[l].reshape(1, d),
                  ffn_w_gate[l].astype(BF16), ffn_w_up[l].astype(BF16), ffn_w_down[l].astype(BF16), seq)
    return x2.reshape(bsz, seq, d)
```

```python
import functools

import jax
import jax.numpy as jnp
from jax import lax
from jax.experimental import pallas as pl
from jax.experimental.pallas import tpu as pltpu

F32 = jnp.float32
BF16 = jnp.bfloat16
HIGHEST = lax.Precision.HIGHEST

HEAD_DIM = 64
H_ATT = 4
C_ATT = H_ATT * HEAD_DIM
H_RWKV = 6
C_RWKV = H_RWKV * HEAD_DIM
H_MLSTM = 6
C_MLSTM = H_MLSTM * HEAD_DIM
CMP_BLOCK = 32
CMP_STRIDE = 16
SLC_BLOCK = 64
SLC_SHIFT = 6
N_SELECT = 16
WINDOW = 512
NEG = -1e30
FORCE = 1e9
MASK_BIG = 2.0 ** 100
LOG2E = 1.4426950408889634
RANK_W = 64
RANK_A = 64
RANK_G = 128
RWKV_GN_EPS = 64e-5
CHUNK = 64
RWKV_BLOCK = 256
ML_BLOCK = 256
CONV_WIDTH = 4
NORM_EPS = 1e-6

NSA_W = 768
RWKV_W = 3 * C_RWKV + RANK_W + RANK_A + RANK_G
ML_W = 4 * C_MLSTM + 128
NSA_IN = C_ATT + 6 * HEAD_DIM + 3 * H_ATT
ML_IN = 4 * C_MLSTM + 2 * H_MLSTM

VMEM_LIMIT = 56 * 1024 * 1024

TQ = 256
TK = 256


def _params(*sem):
    return pltpu.CompilerParams(dimension_semantics=sem, vmem_limit_bytes=VMEM_LIMIT)


def _dot(a, b, precise=False):
    if precise:
        return jnp.dot(a.astype(F32), b.astype(F32), preferred_element_type=F32, precision=HIGHEST)
    return jnp.dot(a.astype(BF16), b.astype(BF16), preferred_element_type=F32)


def _dot_nt(a, b, precise=False):
    dn = (((1,), (1,)), ((), ()))
    if precise:
        return lax.dot_general(a.astype(F32), b.astype(F32), dn, preferred_element_type=F32, precision=HIGHEST)
    return lax.dot_general(a.astype(BF16), b.astype(BF16), dn, preferred_element_type=F32)


def _dot_tn(a, b, precise=False):
    dn = (((0,), (0,)), ((), ()))
    if precise:
        return lax.dot_general(a.astype(F32), b.astype(F32), dn, preferred_element_type=F32, precision=HIGHEST)
    return lax.dot_general(a.astype(BF16), b.astype(BF16), dn, preferred_element_type=F32)


def _rms(x, g):
    return x * lax.rsqrt(jnp.mean(x * x, axis=-1, keepdims=True) + NORM_EPS) * g


def _mod_body(c_ref, w_ref, b_ref, o_ref):
    c = c_ref[...]
    cs = c * jax.nn.sigmoid(c)
    o_ref[...] = _dot(cs, w_ref[...], precise=True) + b_ref[...]


def _mod(c, w_mod, b_mod):
    depth, d, n = w_mod.shape
    bsz = c.shape[0]
    tn = 1536
    return pl.pallas_call(
        _mod_body,
        out_shape=jax.ShapeDtypeStruct((depth, bsz, n), F32),
        grid=(depth, n // tn),
        in_specs=[
            pl.BlockSpec((bsz, d), lambda l, j: (0, 0)),
            pl.BlockSpec((None, d, tn), lambda l, j: (l, 0, j)),
            pl.BlockSpec((None, 1, tn), lambda l, j: (l, 0, j)),
        ],
        out_specs=pl.BlockSpec((None, bsz, tn), lambda l, j: (l, 0, j)),
        compiler_params=_params("parallel", "parallel"),
        name="mod",
    )(c, w_mod, b_mod.reshape(depth, 1, n))


def _in_proj_body(x_ref, sc_ref, sh_ref, g_ref, w_ref, nsa_ref, rw_ref, ml_ref):
    h = _rms(x_ref[...], g_ref[...]) * (1.0 + sc_ref[...]) + sh_ref[...]
    hb = h.astype(BF16)
    nsa_ref[...] = jnp.dot(hb, w_ref[:, 0:NSA_W], preferred_element_type=F32)
    rw_ref[...] = jnp.dot(hb, w_ref[:, NSA_W:NSA_W + RWKV_W], preferred_element_type=F32)
    ml_ref[...] = jnp.dot(hb, w_ref[:, NSA_W + RWKV_W:], preferred_element_type=F32)


def _in_proj(x2, sc, sh, g, w_p, seq):
    t, d = x2.shape
    tm = 512
    per_b = seq // tm
    row = lambda i: (i, 0)
    bat = lambda i: (i // per_b, 0, 0)
    fix = lambda i: (0, 0)
    return pl.pallas_call(
        _in_proj_body,
        out_shape=(jax.ShapeDtypeStruct((t, NSA_W), F32),
                   jax.ShapeDtypeStruct((t, RWKV_W), F32),
                   jax.ShapeDtypeStruct((t, ML_W), F32)),
        grid=(t // tm,),
        in_specs=[
            pl.BlockSpec((tm, d), row),
            pl.BlockSpec((None, 1, d), bat),
            pl.BlockSpec((None, 1, d), bat),
            pl.BlockSpec((1, d), fix),
            pl.BlockSpec(w_p.shape, fix),
        ],
        out_specs=(pl.BlockSpec((tm, NSA_W), row), pl.BlockSpec((tm, RWKV_W), row),
                   pl.BlockSpec((tm, ML_W), row)),
        compiler_params=_params("parallel"),
        name="in_proj",
    )(x2, sc, sh, g, w_p)


def _nsa_cmp_body(x_ref, wc_ref, pe_ref, w2_ref, o_ref):
    wc = wc_ref[...]
    f = _dot(x_ref[...], wc, precise=True)
    c = _dot(pe_ref[...], wc, precise=True)
    g = f.shape[0]
    second = pltpu.roll(f[:, 128:256], g - 1, 0)
    pre = f[:, 0:128] + second + c[0:1, 0:128] + c[1:2, 128:256]
    act = jax.nn.gelu(pre, approximate=True)
    out = _dot(act, w2_ref[...], precise=True)
    rows = lax.broadcasted_iota(jnp.int32, out.shape, 0)
    o_ref[...] = jnp.where(rows < g - 1, out, 0.0)


def _nsa_cmp(xg, wc, pe8, w2):
    bsz, g, k = xg.shape
    return pl.pallas_call(
        _nsa_cmp_body,
        out_shape=jax.ShapeDtypeStruct((bsz, g, 128), F32),
        grid=(bsz,),
        in_specs=[
            pl.BlockSpec((None, g, k), lambda b: (b, 0, 0)),
            pl.BlockSpec(wc.shape, lambda b: (0, 0)),
            pl.BlockSpec(pe8.shape, lambda b: (0, 0)),
            pl.BlockSpec(w2.shape, lambda b: (0, 0)),
        ],
        out_specs=pl.BlockSpec((None, g, 128), lambda b: (b, 0, 0)),
        compiler_params=_params("parallel"),
        name="nsa_cmp",
    )(xg, wc, pe8, w2)


def _nsa_body(q_ref, gts_ref, kvc_ref, kse_ref, vse_ref, kw_ref, vwe_ref, gb_ref, g_ref, o_ref,
              m_s, acc_s, s_even, s_odd, ocmp, *, n_slc):
    tq = q_ref.shape[0]
    n_cmp = kvc_ref.shape[0]
    qi = pl.program_id(1)
    q0 = qi * tq
    scale = HEAD_DIM ** -0.5
    D = HEAD_DIM
    qt = q_ref[...].T

    kc = kvc_ref[:, 0:D]
    vc = kvc_ref[:, D:2 * D]
    nidx = lax.broadcasted_iota(jnp.int32, (n_cmp, tq), 0)
    pos = q0 + lax.broadcasted_iota(jnp.int32, (n_cmp, tq), 1)
    cmask = (nidx * CMP_STRIDE + (CMP_BLOCK - 1)) <= pos
    heads = range(H_ATT)
    sc = [_dot(kc, qt[h * D:(h + 1) * D], precise=True) for h in heads]
    sc = [jnp.where(cmask, x * scale, NEG) for x in sc]
    ec = [jnp.where(cmask, jnp.exp(x - jnp.max(x, axis=0, keepdims=True)), 0.0) for x in sc]
    pc = [e / jnp.maximum(jnp.sum(e, axis=0, keepdims=True), 1e-30) for e in ec]
    oc = [_dot_tn(vc, p) for p in pc]
    for h in heads:
        ocmp[h * D:(h + 1) * D, :] = oc[h]
    psum = sum(pc[1:], pc[0])

    jrow = lax.broadcasted_iota(jnp.int32, (n_slc, n_cmp), 0) * SLC_BLOCK
    ncol = lax.broadcasted_iota(jnp.int32, (n_slc, n_cmp), 1) * CMP_STRIDE
    overlap_t = jnp.where((ncol < jrow + SLC_BLOCK) & (ncol + CMP_BLOCK > jrow), 1.0, 0.0)
    imp_t = _dot(overlap_t, psum, precise=True)
    jj = lax.broadcasted_iota(jnp.int32, (n_slc, tq), 0)
    cur = jnp.right_shift(q0 + lax.broadcasted_iota(jnp.int32, (n_slc, tq), 1), SLC_SHIFT)
    forced = (jj == 0) | (jj == cur) | (jj == cur - 1)
    score = jnp.where(forced, FORCE, jnp.where(jj <= cur, imp_t, NEG))
    group = 8
    cnt = [jnp.zeros((group, tq), F32) for _ in range(n_slc // group)]
    for j2 in range(n_slc):
        row = score[j2:j2 + 1, :]
        for gi in range(n_slc // group):
            sg = score[gi * group:(gi + 1) * group]
            if gi * group > j2:
                hit = jnp.where(row >= sg, 1.0, 0.0)
            elif (gi + 1) * group - 1 <= j2:
                hit = jnp.where(row > sg, 1.0, 0.0)
            else:
                hit = jnp.where(lax.broadcasted_iota(jnp.int32, (group, tq), 0) + gi * group > j2,
                                jnp.where(row >= sg, 1.0, 0.0), jnp.where(row > sg, 1.0, 0.0))
            cnt[gi] = cnt[gi] + hit
    cnt = jnp.concatenate(cnt, axis=0)
    selm = jnp.where(cnt < min(N_SELECT, n_slc), 0.0, -1.0).astype(BF16)

    qs = (qt * (scale * LOG2E)).astype(BF16)
    rhs_w = [qs[h * D:(h + 1) * D] for h in heads]
    rhs_s = [jnp.concatenate([rhs_w[h], selm], axis=0) for h in heads]
    m_s[...] = jnp.full(m_s.shape, NEG, F32)
    acc_s[...] = jnp.zeros_like(acc_s)

    def scores_into(buf, j):
        k = kse_ref[pl.ds(pl.multiple_of(j * TK, TK), TK), :]
        for h in heads:
            buf[h] = jnp.dot(k, rhs_s[h], preferred_element_type=F32)

    def consume(buf, j, mask):
        vt = vse_ref[:, pl.ds(pl.multiple_of(j * TK, TK), TK)]
        for h in heads:
            s = buf[h]
            if mask is not None:
                s = jnp.where(mask, s, NEG)
            m_old = m_s[h:h + 1, :]
            m_new = jnp.maximum(m_old, jnp.max(s, axis=0, keepdims=True))
            p = jnp.exp2(s - m_new).astype(BF16)
            acc_s[h] = jnp.exp2(m_old - m_new) * acc_s[h] + jnp.dot(vt, p, preferred_element_type=F32)
            m_s[h:h + 1, :] = m_new

    kpos = lax.broadcasted_iota(jnp.int32, (TK, tq), 0)
    qpos = lax.broadcasted_iota(jnp.int32, (TK, tq), 1)
    last = jnp.maximum(qi - 1, 0)
    scores_into(s_even, qi)
    scores_into(s_odd, 0)
    consume(s_even, qi, kpos <= qpos)

    def body(jp, carry):
        j = 2 * jp
        scores_into(s_even, jnp.minimum(j + 1, last))
        consume(s_odd, j, None)
        scores_into(s_odd, jnp.minimum(j + 2, last))
        consume(s_even, j + 1, None)
        return carry

    lax.fori_loop(0, qi // 2, body, 0)

    @pl.when(qi % 2 == 1)
    def _():
        consume(s_odd, qi - 1, None)

    span = WINDOW + tq
    w0 = pl.multiple_of(jnp.maximum(q0 - WINDOW, 0), TK)
    kw = kw_ref[pl.ds(w0, span), :]
    vwt = vwe_ref[:, pl.ds(w0, span)]
    kabs = w0 + lax.broadcasted_iota(jnp.int32, (span, tq), 0)
    qabs = q0 + lax.broadcasted_iota(jnp.int32, (span, tq), 1)
    wmask = (kabs <= qabs) & (kabs > qabs - WINDOW)
    sw = [jnp.where(wmask, jnp.dot(kw, rhs_w[h], preferred_element_type=F32), NEG) for h in heads]
    pw = [jnp.exp2(x - jnp.max(x, axis=0, keepdims=True)).astype(BF16) for x in sw]
    acc_w = [jnp.dot(vwt, p, preferred_element_type=F32) for p in pw]

    gate = jax.nn.sigmoid((gts_ref[...] + gb_ref[...]).T)
    outs = []
    for h in heads:
        a_s = acc_s[h]
        a_w = acc_w[h]
        outs.append(gate[h:h + 1] * ocmp[h * D:(h + 1) * D, :]
                    + gate[H_ATT + h:H_ATT + h + 1] * (a_s[0:D] / a_s[D:D + 1])
                    + gate[2 * H_ATT + h:2 * H_ATT + h + 1] * (a_w[0:D] / a_w[D:D + 1]))
    ot = jnp.concatenate(outs, axis=0)
    ot = ot * lax.rsqrt(jnp.mean(ot * ot, axis=0, keepdims=True) + NORM_EPS)
    o_ref[...] = ot.T * g_ref[...]


def _nsa(p_nsa, kv_cmp, kse, vse, kw, vwe, gate_b, out_g, bsz, seq):
    t = p_nsa.shape[0]
    nq = seq // TQ
    n_slc = seq // SLC_BLOCK
    g = kv_cmp.shape[1]
    row = lambda b, i: (b * nq + i, 0)
    per_b = lambda b, i: (b, 0, 0)
    fix = lambda b, i: (0, 0)
    return pl.pallas_call(
        functools.partial(_nsa_body, n_slc=n_slc),
        out_shape=jax.ShapeDtypeStruct((t, C_ATT), F32),
        grid=(bsz, nq),
        in_specs=[
            pl.BlockSpec((TQ, C_ATT), row),
            pl.BlockSpec((TQ, 128), lambda b, i: (b * nq + i, NSA_W // 128 - 1)),
            pl.BlockSpec((None, g, 128), per_b),
            pl.BlockSpec((None, seq, HEAD_DIM + n_slc), per_b),
            pl.BlockSpec((None, 128, seq), per_b),
            pl.BlockSpec((None, seq, HEAD_DIM), per_b),
            pl.BlockSpec((None, 128, seq), per_b),
            pl.BlockSpec((1, 128), fix),
            pl.BlockSpec((1, C_ATT), fix),
        ],
        out_specs=pl.BlockSpec((TQ, C_ATT), row),
        scratch_shapes=[pltpu.VMEM((8, TQ), F32), pltpu.VMEM((H_ATT, 128, TQ), F32),
                        pltpu.VMEM((H_ATT, TK, TQ), F32), pltpu.VMEM((H_ATT, TK, TQ), F32),
                        pltpu.VMEM((C_ATT, TQ), F32)],
        compiler_params=_params("parallel", "arbitrary"),
        name="nsa",
    )(p_nsa, p_nsa, kv_cmp, kse, vse, kw, vwe, gate_b, out_g)


def _nsa_kv_layout(p_nsa, bsz, seq):
    n_slc = seq // SLC_BLOCK
    kv = p_nsa[:, C_ATT + 2 * HEAD_DIM:C_ATT + 6 * HEAD_DIM].astype(BF16).reshape(bsz, seq, 4, HEAD_DIM)
    blk = jnp.arange(seq)[:, None] // SLC_BLOCK == jnp.arange(n_slc)[None, :]
    expand = jnp.where(blk, MASK_BIG, 0.0).astype(BF16)
    kse = jnp.concatenate([kv[:, :, 0], jnp.broadcast_to(expand, (bsz, seq, n_slc))], axis=-1)
    tail = jnp.concatenate([jnp.ones((bsz, 1, seq), BF16), jnp.zeros((bsz, 128 - HEAD_DIM - 1, seq), BF16)], axis=1)
    vse = jnp.concatenate([jnp.swapaxes(kv[:, :, 1], 1, 2), tail], axis=1)
    vwe = jnp.concatenate([jnp.swapaxes(kv[:, :, 3], 1, 2), tail], axis=1)
    return kse, vse, kv[:, :, 2], vwe


def _tri(n, strict):
    r = lax.broadcasted_iota(jnp.int32, (n, n), 0)
    c = lax.broadcasted_iota(jnp.int32, (n, n), 1)
    return (c < r) if strict else (c <= r)


def _unit_lower_inverse(ms, eye):
    n = eye.shape[0]
    idx = range(len(ms))
    r = lax.broadcasted_iota(jnp.int32, (n, n), 0)
    c = lax.broadcasted_iota(jnp.int32, (n, n), 1)
    base = 8
    diag = jnp.right_shift(r, 3) == jnp.right_shift(c, 3)
    m8 = [jnp.where(diag, m, 0.0) for m in ms]
    t = [eye + m for m in m8]
    p2 = [_dot(m, m) for m in m8]
    t = [t[i] + _dot(p2[i], t[i]) for i in idx]
    p4 = [_dot(p, p) for p in p2]
    t = [t[i] + _dot(p4[i], t[i]) for i in idx]
    b = base
    while b < n:
        sh = b.bit_length() - 1
        pair = jnp.right_shift(r, sh + 1) == jnp.right_shift(c, sh + 1)
        lower_left = pair & (jnp.right_shift(r, sh) != jnp.right_shift(c, sh))
        left = [_dot(t[i], jnp.where(lower_left, ms[i], 0.0)) for i in idx]
        t = [t[i] + _dot(left[i], t[i]) for i in idx]
        b *= 2
    return t


def _rwkv_body(p_ref, mu_ref, w0_ref, w2_ref, a0_ref, a2_ref, g2_ref, kk_ref, ka_ref, rk_ref, lnw_ref,
               lnb_ref, o_ref, state, prev):
    L = CHUNK
    C = C_RWKV
    tb = p_ref.shape[0]
    chunks = range(tb // L)

    @pl.when(pl.program_id(1) == 0)
    def _():
        state[...] = jnp.zeros_like(state)
        prev[...] = jnp.zeros_like(prev)

    x = p_ref[...]
    rows = lax.broadcasted_iota(jnp.int32, x.shape, 0)
    shifted = jnp.where(rows == 0, prev[0:1, :], pltpu.roll(x, 1, 0))
    prev[0:1, :] = x[tb - 1:tb, :]
    x = x + mu_ref[...] * (shifted - x)
    r = x[:, 0:C]
    k = x[:, C:2 * C]
    v = x[:, 2 * C:3 * C]
    xw = x[:, 3 * C:3 * C + RANK_W]
    xa = x[:, 3 * C + RANK_W:3 * C + RANK_W + RANK_A]
    xg = x[:, 3 * C + RANK_W + RANK_A:]
    w = -jax.nn.softplus(-(w0_ref[...] + _dot(jnp.tanh(xw), w2_ref[...], precise=True))) - 0.5
    logw = -jnp.exp(w)
    a = jax.nn.sigmoid(a0_ref[...] + _dot(xa, a2_ref[...], precise=True))
    g = _dot(jax.nn.sigmoid(xg), g2_ref[...])
    kkf = k * kk_ref[...]
    kmod = k * (1.0 + (a - 1.0) * ka_ref[...])
    rk = r * kmod * rk_ref[...]

    incl = _tri(L, strict=False)
    strict = _tri(L, strict=True)
    tri_incl = jnp.where(incl, 1.0, 0.0)
    cum = jnp.concatenate([_dot(tri_incl, logw[c * L:(c + 1) * L], precise=True) for c in chunks], axis=0)
    cum_last = [cum[(c + 1) * L - 1:(c + 1) * L, :] for c in chunks]
    cum_end = jnp.concatenate([jnp.broadcast_to(cl, (L, C)) for cl in cum_last], axis=0)
    e_pos = jnp.exp(cum)
    e_prev = jnp.exp(cum - logw)
    e_neg = jnp.exp(-cum)
    e_rem = jnp.exp(cum_end - cum)
    eye = jnp.where(lax.broadcasted_iota(jnp.int32, (L, L), 0) == lax.broadcasted_iota(jnp.int32, (L, L), 1), 1.0, 0.0)

    heads = range(H_RWKV)
    cols = [slice(h * HEAD_DIM, (h + 1) * HEAD_DIM) for h in heads]
    kk_h = [kkf[:, sl] for sl in cols]
    kk_h = [x_ / jnp.maximum(jnp.sqrt(jnp.sum(x_ * x_, axis=-1, keepdims=True)), 1e-12) for x_ in kk_h]
    v_h = [v[:, sl] for sl in cols]
    alpha_t = [-kk_h[h] * e_prev[:, cols[h]] for h in heads]
    r_t = [r[:, cols[h]] * e_pos[:, cols[h]] for h in heads]
    beta = [kk_h[h] * a[:, cols[h]] for h in heads]
    beta_t = [beta[h] * e_neg[:, cols[h]] for h in heads]
    k_t = [kmod[:, cols[h]] * e_neg[:, cols[h]] for h in heads]
    beta_e = [beta[h] * e_rem[:, cols[h]] for h in heads]
    k_e = [kmod[:, cols[h]] * e_rem[:, cols[h]] for h in heads]
    items = [(c, h) for c in chunks for h in heads]
    n_items = range(len(items))
    rs = [slice(c * L, (c + 1) * L) for c, _ in items]
    hs = [h for _, h in items]
    lhs = [jnp.concatenate([alpha_t[hs[i]][rs[i]], r_t[hs[i]][rs[i]]], axis=0) for i in n_items]
    rhs = [jnp.concatenate([beta_t[hs[i]][rs[i]], k_t[hs[i]][rs[i]]], axis=0) for i in n_items]
    v_i = [v_h[hs[i]][rs[i]] for i in n_items]
    mm = [_dot_nt(lhs[i], rhs[i]) for i in n_items]
    m_ab = [jnp.where(strict, mm[i][0:L, 0:L], 0.0) for i in n_items]
    m_akrk = [jnp.concatenate([jnp.where(strict, mm[i][0:L, L:2 * L], 0.0),
                               jnp.where(incl, mm[i][L:2 * L, L:2 * L], 0.0)], axis=0) for i in n_items]
    m_rb = [jnp.where(incl, mm[i][L:2 * L, 0:L], 0.0) for i in n_items]
    t_inv = _unit_lower_inverse(m_ab, eye)
    mv = [_dot(m_akrk[i], v_i[i]) for i in n_items]
    z = [state[h] for h in heads]
    y_parts = [[] for _ in heads]
    for c in chunks:
        idx = [c * H_RWKV + h for h in heads]
        az = [_dot(lhs[i], z[hs[i]]) for i in idx]
        u = [_dot(t_inv[i], az[h][0:L] + mv[i][0:L]) for h, i in enumerate(idx)]
        for h, i in enumerate(idx):
            y_parts[h].append(az[h][L:2 * L] + _dot(m_rb[i], u[h]) + mv[i][L:2 * L])
        upd = [_dot_tn(jnp.concatenate([beta_e[h][rs[i]], k_e[h][rs[i]]], axis=0),
                       jnp.concatenate([u[h], v_i[i]], axis=0)) for h, i in enumerate(idx)]
        z = [z[h] * jnp.sum(eye * jnp.exp(cum_last[c][:, cols[h]]), axis=1, keepdims=True) + upd[h] for h in heads]
    for h in heads:
        state[h] = z[h]
    yn = []
    for h in heads:
        y_h = jnp.concatenate(y_parts[h], axis=0)
        mu_y = jnp.mean(y_h, axis=-1, keepdims=True)
        var = jnp.mean((y_h - mu_y) ** 2, axis=-1, keepdims=True)
        yn.append((y_h - mu_y) * lax.rsqrt(var + RWKV_GN_EPS))
    bonus = [jnp.sum(rk[:, cols[h]], axis=-1, keepdims=True) * v_h[h] for h in heads]
    yn = jnp.concatenate(yn, axis=-1)
    bonus = jnp.concatenate(bonus, axis=-1)
    o_ref[...] = (yn * lnw_ref[...] + lnb_ref[...] + bonus) * g


def _rwkv(p_rw, mu, w0, w2, a0, a2, g2, k_k, k_a, r_k, ln_w, ln_b, bsz, seq):
    t = p_rw.shape[0]
    tb = RWKV_BLOCK
    nb = seq // tb
    row = lambda b, c: (b * nb + c, 0)
    fix = lambda b, c: (0, 0)
    vec = lambda a: a.reshape(1, -1)
    args = (vec(mu), vec(w0), w2, vec(a0), a2, g2, vec(k_k), vec(k_a), vec(r_k), vec(ln_w), vec(ln_b))
    return pl.pallas_call(
        _rwkv_body,
        out_shape=jax.ShapeDtypeStruct((t, C_RWKV), F32),
        grid=(bsz, nb),
        in_specs=[pl.BlockSpec((tb, RWKV_W), row)] + [pl.BlockSpec(a.shape, fix) for a in args],
        out_specs=pl.BlockSpec((tb, C_RWKV), row),
        scratch_shapes=[pltpu.VMEM((H_RWKV, HEAD_DIM, HEAD_DIM), F32), pltpu.VMEM((8, RWKV_W), F32)],
        compiler_params=_params("parallel", "arbitrary"),
        name="rwkv",
    )(p_rw, *args)


def _mlstm_body(p_ref, cw_ref, cb_ref, gb_ref, ng_ref, o_ref, cstate, mstate, prev):
    L = CHUNK
    C = C_MLSTM
    D = HEAD_DIM
    tb = p_ref.shape[0]
    chunks = range(tb // L)

    @pl.when(pl.program_id(1) == 0)
    def _():
        cstate[...] = jnp.zeros_like(cstate)
        mstate[...] = jnp.zeros_like(mstate)
        prev[...] = jnp.zeros_like(prev)

    qk_in = p_ref[:, 0:2 * C]
    pv = prev[...]
    rows8 = lax.broadcasted_iota(jnp.int32, pv.shape, 0)
    conv = qk_in * cw_ref[CONV_WIDTH - 1:CONV_WIDTH, :] + cb_ref[...]
    for d in range(1, CONV_WIDTH):
        rolled = pltpu.roll(qk_in, d, 0)
        top = jnp.where(rows8 < d, pltpu.roll(pv, d, 0), rolled[0:8])
        sh = jnp.concatenate([top, rolled[8:]], axis=0)
        conv = conv + sh * cw_ref[CONV_WIDTH - 1 - d:CONV_WIDTH - d, :]
    prev[...] = qk_in[tb - 8:tb, :]
    qk = conv * jax.nn.sigmoid(conv)
    q = qk[:, 0:C]
    k = qk[:, C:2 * C] * (D ** -0.5)
    v = p_ref[:, 2 * C:3 * C]
    og = p_ref[:, 3 * C:4 * C]
    gates = p_ref[:, 4 * C:4 * C + 128] + gb_ref[...]
    lane = lax.broadcasted_iota(jnp.int32, gates.shape, 1)
    is_f = (lane >= H_MLSTM) & (lane < 2 * H_MLSTM)
    gl = jnp.where(is_f, jax.nn.log_sigmoid(gates), gates)
    tri_incl = jnp.where(_tri(L, strict=False), 1.0, 0.0)
    bcum = [_dot(tri_incl, gl[c * L:(c + 1) * L], precise=True) for c in chunks]
    gl_t = [gl[c * L:(c + 1) * L].T for c in chunks]
    bcum_t = [_dot_nt(g_, tri_incl, precise=True) for g_ in gl_t]
    m_all = mstate[...]
    src = lax.broadcasted_iota(jnp.int32, (L, L), 0)
    qry = lax.broadcasted_iota(jnp.int32, (L, L), 1)
    causal_t = src <= qry
    ones_row = jnp.where(lax.broadcasted_iota(jnp.int32, (D, L), 0) == 0, 1.0, 0.0)

    heads = range(H_MLSTM)
    items = [(c, h) for c in chunks for h in heads]
    n_items = range(len(items))
    q_t = [q[c * L:(c + 1) * L].T for c in chunks]
    v_t = [v[c * L:(c + 1) * L].T for c in chunks]
    qt_i = [q_t[c][h * D:(h + 1) * D] for c, h in items]
    vt_ext = [jnp.concatenate([v_t[c][h * D:(h + 1) * D], ones_row], axis=0) for c, h in items]
    k_i = [k[c * L:(c + 1) * L, h * D:(h + 1) * D] for c, h in items]
    c_col = [gl[c * L:(c + 1) * L, h:h + 1] - bcum[c][:, H_MLSTM + h:H_MLSTM + h + 1] for c, h in items]
    b_row = [bcum_t[c][H_MLSTM + h:H_MLSTM + h + 1, :] for c, h in items]
    c_row = [gl_t[c][h:h + 1, :] - b_row[i] for i, (c, h) in enumerate(items)]
    g_tot = [bcum[c][L - 1:L, H_MLSTM + h:H_MLSTM + h + 1] for c, h in items]
    u_row = [g_tot[i] + c_row[i] for i in n_items]
    u_max = [jnp.max(x_, axis=1, keepdims=True) for x_ in u_row]
    m0 = []
    m_run = [m_all[0:1, h:h + 1] for h in heads]
    for c in chunks:
        for h in heads:
            i = c * H_MLSTM + h
            m0.append(m_run[h])
            m_run[h] = jnp.maximum(g_tot[i] + m_run[h], u_max[i])
    m_next = [m0[i + H_MLSTM] if i + H_MLSTM < len(items) else m_run[items[i][1]] for i in n_items]
    kq = [_dot(k_i[i], qt_i[i]) for i in n_items]
    kv = [_dot(vt_ext[i] * jnp.exp(u_row[i] - m_next[i]), k_i[i]) for i in n_items]
    cm = []
    cm_run = [cstate[h] for h in heads]
    for c in chunks:
        for h in heads:
            i = c * H_MLSTM + h
            cm.append(cm_run[h])
            cm_run[h] = jnp.exp(g_tot[i] + m0[i] - m_next[i]) * cm_run[h] + kv[i]
    inter = [_dot(cm[i], qt_i[i]) for i in n_items]
    c_b = [jnp.where(causal_t, jnp.broadcast_to(c_col[i], (L, L)), -jnp.inf) for i in n_items]
    m_row = [jnp.maximum(m0[i], jnp.max(c_b[i], axis=0, keepdims=True)) for i in n_items]
    s_t = [jnp.exp(c_b[i] - m_row[i]) * kq[i] for i in n_items]
    sv = [_dot(vt_ext[i], s_t[i]) for i in n_items]
    lane8 = lax.broadcasted_iota(jnp.int32, m_all.shape, 1)
    for h in heads:
        cstate[h] = cm_run[h]
        m_all = jnp.where(lane8 == h, m_run[h], m_all)
    mstate[...] = m_all
    outs = []
    for i in n_items:
        num = jnp.exp(m0[i] - m_row[i]) * inter[i] + sv[i]
        hh = num[0:D] / jnp.maximum(jnp.abs(num[D:D + 1]), jnp.exp(-(b_row[i] + m_row[i])))
        outs.append(hh * lax.rsqrt(jnp.mean(hh * hh, axis=0, keepdims=True) + NORM_EPS))
    hcat = jnp.concatenate([jnp.concatenate(outs[c * H_MLSTM:(c + 1) * H_MLSTM], axis=0).T for c in chunks], axis=0)
    o_ref[...] = jax.nn.sigmoid(og) * (hcat * ng_ref[...])


def _mlstm(p_ml, conv_w, conv_b, gate_b, norm_g, bsz, seq):
    t = p_ml.shape[0]
    tb = ML_BLOCK
    nb = seq // tb
    row = lambda b, c: (b * nb + c, 0)
    fix = lambda b, c: (0, 0)
    args = (conv_w, conv_b.reshape(1, -1), gate_b, norm_g.reshape(1, -1))
    return pl.pallas_call(
        _mlstm_body,
        out_shape=jax.ShapeDtypeStruct((t, C_MLSTM), F32),
        grid=(bsz, nb),
        in_specs=[pl.BlockSpec((tb, ML_W), row)] + [pl.BlockSpec(a.shape, fix) for a in args],
        out_specs=pl.BlockSpec((tb, C_MLSTM), row),
        scratch_shapes=[pltpu.VMEM((H_MLSTM, 128, HEAD_DIM), F32), pltpu.VMEM((8, 128), F32),
                        pltpu.VMEM((8, 2 * C_MLSTM), F32)],
        compiler_params=_params("parallel", "arbitrary"),
        name="mlstm",
    )(p_ml, *args)


def _out_proj_body(a_ref, b_ref, c_ref, wa_ref, wb_ref, wc_ref, x_ref, gt_ref, g_ref, o_ref):
    y = (jnp.dot(a_ref[...].astype(BF16), wa_ref[...], preferred_element_type=F32)
         + jnp.dot(b_ref[...].astype(BF16), wb_ref[...], preferred_element_type=F32)
         + jnp.dot(c_ref[...].astype(BF16), wc_ref[...], preferred_element_type=F32))
    o_ref[...] = x_ref[...] + gt_ref[...] * _rms(y, g_ref[...])


def _out_proj(o_nsa, o_rw, o_ml, wa, wb, wc, x2, gt, g, seq):
    t, d = x2.shape
    tm = 512
    per_b = seq // tm
    row = lambda i: (i, 0)
    fix = lambda i: (0, 0)
    return pl.pallas_call(
        _out_proj_body,
        out_shape=jax.ShapeDtypeStruct((t, d), F32),
        grid=(t // tm,),
        in_specs=[
            pl.BlockSpec((tm, C_ATT), row), pl.BlockSpec((tm, C_RWKV), row), pl.BlockSpec((tm, C_MLSTM), row),
            pl.BlockSpec(wa.shape, fix), pl.BlockSpec(wb.shape, fix), pl.BlockSpec(wc.shape, fix),
            pl.BlockSpec((tm, d), row),
            pl.BlockSpec((None, 1, d), lambda i: (i // per_b, 0, 0)),
            pl.BlockSpec((1, d), fix),
        ],
        out_specs=pl.BlockSpec((tm, d), row),
        compiler_params=_params("parallel"),
        name="out_proj",
    )(o_nsa, o_rw, o_ml, wa, wb, wc, x2, gt, g)


def _ffn_body(x_ref, sc_ref, sh_ref, gt_ref, gpre_ref, gpost_ref, wg_ref, wu_ref, wd_ref, o_ref, hb, acc):
    f = pl.program_id(1)

    @pl.when(f == 0)
    def _():
        h = _rms(x_ref[...], gpre_ref[...]) * (1.0 + sc_ref[...]) + sh_ref[...]
        hb[...] = h.astype(BF16)
        acc[...] = jnp.zeros_like(acc)

    hv = hb[...]
    gate = jnp.dot(hv, wg_ref[...], preferred_element_type=F32)
    up = jnp.dot(hv, wu_ref[...], preferred_element_type=F32)
    act = (gate * jax.nn.sigmoid(gate) * up).astype(BF16)
    acc[...] += jnp.dot(act, wd_ref[...], preferred_element_type=F32)

    @pl.when(f == pl.num_programs(1) - 1)
    def _():
        o_ref[...] = x_ref[...] + gt_ref[...] * _rms(acc[...], gpost_ref[...])


def _ffn(x2, sc, sh, gt, g_pre, g_post, wg, wu, wd, seq):
    t, d = x2.shape
    dff = wg.shape[1]
    tm = 512
    tf = dff // 2
    per_b = seq // tm
    row = lambda i, f: (i, 0)
    bat = lambda i, f: (i // per_b, 0, 0)
    fix = lambda i, f: (0, 0)
    return pl.pallas_call(
        _ffn_body,
        out_shape=jax.ShapeDtypeStruct((t, d), F32),
        grid=(t // tm, dff // tf),
        in_specs=[
            pl.BlockSpec((tm, d), row),
            pl.BlockSpec((None, 1, d), bat), pl.BlockSpec((None, 1, d), bat), pl.BlockSpec((None, 1, d), bat),
            pl.BlockSpec((1, d), fix), pl.BlockSpec((1, d), fix),
            pl.BlockSpec((d, tf), lambda i, f: (0, f)),
            pl.BlockSpec((d, tf), lambda i, f: (0, f)),
            pl.BlockSpec((tf, d), lambda i, f: (f, 0)),
        ],
        out_specs=pl.BlockSpec((tm, d), row),
        scratch_shapes=[pltpu.VMEM((tm, d), BF16), pltpu.VMEM((tm, d), F32)],
        compiler_params=_params("parallel", "arbitrary"),
        name="ffn",
    )(x2, sc, sh, gt, g_pre, g_post, wg, wu, wd)


def _pad_cols(a, width):
    return jnp.pad(a, ((0, 0), (0, width - a.shape[1])))


def _layout_w_in(w_in):
    d_in_rw = NSA_IN + RWKV_W
    nsa = _pad_cols(w_in[:, :NSA_IN], NSA_W)
    rw = w_in[:, NSA_IN:d_in_rw]
    ml = _pad_cols(w_in[:, d_in_rw:], ML_W)
    return jnp.concatenate([nsa, rw, ml], axis=1).astype(BF16)


def _layout_cmp(ck_w1, cv_w1, ck_w2, cv_w2, pe_k, pe_v):
    half = CMP_BLOCK // 2
    ck = ck_w1.reshape(2, half, HEAD_DIM, HEAD_DIM)
    cv = cv_w1.reshape(2, half, HEAD_DIM, HEAD_DIM)
    z = jnp.zeros_like(ck[0])
    top = jnp.concatenate([ck[0], z, ck[1], z], axis=-1)
    bot = jnp.concatenate([z, cv[0], z, cv[1]], axis=-1)
    wc = jnp.concatenate([top, bot], axis=1).reshape(half * 2 * HEAD_DIM, 4 * HEAD_DIM)
    pe = jnp.concatenate([pe_k, pe_v], axis=-1).reshape(2, half * 2 * HEAD_DIM)
    pe8 = jnp.pad(pe, ((0, 6), (0, 0)))
    z2 = jnp.zeros_like(ck_w2)
    w2 = jnp.concatenate([jnp.concatenate([ck_w2, z2], axis=1), jnp.concatenate([z2, cv_w2], axis=1)], axis=0)
    return wc, pe8, w2


def kernel(x, c, w_mod, b_mod, g_pre_mix, g_post_mix, g_pre_ffn, g_post_ffn, w_in, w_out, nsa_pe_k, nsa_pe_v, nsa_ck_w1, nsa_ck_w2, nsa_cv_w1, nsa_cv_w2, nsa_gate_b, nsa_out_g, rw_mu, rw_w0, rw_w2, rw_a0, rw_a2, rw_g2, rw_kk, rw_ka, rw_rk, rw_ln_w, rw_ln_b, ml_conv_w, ml_conv_b, ml_ig_b, ml_fg_b, ml_norm_g, ffn_w_gate, ffn_w_up, ffn_w_down):
    bsz, seq, d = x.shape
    depth = w_mod.shape[0]
    t = bsz * seq
    half = CMP_BLOCK // 2
    mod = _mod(c, w_mod, b_mod)
    x2 = x.reshape(t, d)
    for l in range(depth):
        sh1, sc1, gt1, sh2, sc2, gt2 = [m.reshape(bsz, 1, d) for m in jnp.split(mod[l], 6, axis=-1)]
        p_nsa, p_rw, p_ml = _in_proj(x2, sc1, sh1, g_pre_mix[l].reshape(1, d), _layout_w_in(w_in[l]), seq)
        wc, pe8, w2 = _layout_cmp(nsa_ck_w1[l], nsa_cv_w1[l], nsa_ck_w2[l], nsa_cv_w2[l], nsa_pe_k[l], nsa_pe_v[l])
        xg = p_nsa[:, C_ATT:C_ATT + 2 * HEAD_DIM].reshape(bsz, seq // half, half * 2 * HEAD_DIM)
        kv_cmp = _nsa_cmp(xg, wc, pe8, w2)
        kse, vse, kw, vwe = _nsa_kv_layout(p_nsa, bsz, seq)
        gate_b = jnp.pad(nsa_gate_b[l], (0, 128 - 3 * H_ATT)).reshape(1, 128)
        o_nsa = _nsa(p_nsa, kv_cmp, kse, vse, kw, vwe, gate_b, nsa_out_g[l].reshape(1, C_ATT), bsz, seq)
        o_rw = _rwkv(p_rw, rw_mu[l], rw_w0[l], rw_w2[l], rw_a0[l], rw_a2[l], rw_g2[l], rw_kk[l], rw_ka[l],
                     rw_rk[l], rw_ln_w[l], rw_ln_b[l], bsz, seq)
        ml_gate_b = jnp.pad(jnp.concatenate([ml_ig_b[l], ml_fg_b[l]]), (0, 128 - 2 * H_MLSTM)).reshape(1, 128)
        o_ml = _mlstm(p_ml, ml_conv_w[l], ml_conv_b[l], ml_gate_b, ml_norm_g[l], bsz, seq)
        wo = w_out[l].astype(BF16)
        x2 = _out_proj(o_nsa, o_rw, o_ml, wo[:C_ATT], wo[C_ATT:C_ATT + C_RWKV], wo[C_ATT + C_RWKV:], x2, gt1,
                       g_post_mix[l].reshape(1, d), seq)
        x2 = _ffn(x2, sc2, sh2, gt2, g_pre_ffn[l].reshape(1, d), g_post_ffn[l].reshape(1, d),
                  ffn_w_gate[l].astype(BF16), ffn_w_up[l].astype(BF16), ffn_w_down[l].astype(BF16), seq)
    return x2.reshape(bsz, seq, d)
```

```python
import functools

import jax
import jax.numpy as jnp
from jax import lax
from jax.experimental import pallas as pl
from jax.experimental.pallas import tpu as pltpu

F32 = jnp.float32
BF16 = jnp.bfloat16
HIGHEST = lax.Precision.HIGHEST

HEAD_DIM = 64
H_ATT = 4
C_ATT = H_ATT * HEAD_DIM
H_RWKV = 6
C_RWKV = H_RWKV * HEAD_DIM
H_MLSTM = 6
C_MLSTM = H_MLSTM * HEAD_DIM
CMP_BLOCK = 32
CMP_STRIDE = 16
SLC_BLOCK = 64
SLC_SHIFT = 6
N_SELECT = 16
WINDOW = 512
NEG = -1e30
FORCE = 1e9
MASK_BIG = 2.0 ** 100
LOG2E = 1.4426950408889634
RANK_W = 64
RANK_A = 64
RANK_G = 128
RWKV_GN_EPS = 64e-5
CHUNK = 64
RWKV_BLOCK = 256
ML_BLOCK = 256
CONV_WIDTH = 4
NORM_EPS = 1e-6

NSA_W = 768
RWKV_W = 3 * C_RWKV + RANK_W + RANK_A + RANK_G
ML_W = 4 * C_MLSTM + 128
NSA_IN = C_ATT + 6 * HEAD_DIM + 3 * H_ATT
ML_IN = 4 * C_MLSTM + 2 * H_MLSTM

VMEM_LIMIT = 56 * 1024 * 1024

TQ = 256
TK = 256


def _params(*sem):
    return pltpu.CompilerParams(dimension_semantics=sem, vmem_limit_bytes=VMEM_LIMIT)


def _dot(a, b, precise=False):
    if precise:
        return jnp.dot(a.astype(F32), b.astype(F32), preferred_element_type=F32, precision=HIGHEST)
    return jnp.dot(a.astype(BF16), b.astype(BF16), preferred_element_type=F32)


def _dot_nt(a, b, precise=False):
    dn = (((1,), (1,)), ((), ()))
    if precise:
        return lax.dot_general(a.astype(F32), b.astype(F32), dn, preferred_element_type=F32, precision=HIGHEST)
    return lax.dot_general(a.astype(BF16), b.astype(BF16), dn, preferred_element_type=F32)


def _dot_tn(a, b, precise=False):
    dn = (((0,), (0,)), ((), ()))
    if precise:
        return lax.dot_general(a.astype(F32), b.astype(F32), dn, preferred_element_type=F32, precision=HIGHEST)
    return lax.dot_general(a.astype(BF16), b.astype(BF16), dn, preferred_element_type=F32)


def _split3(x):
    hi = x.astype(BF16)
    rest = x - hi.astype(F32)
    mid = rest.astype(BF16)
    lo = (rest - mid.astype(F32)).astype(BF16)
    return hi, mid, lo


def _dot_mask_lhs(mask01, x):
    mb = mask01.astype(BF16)
    hi, mid, lo = _split3(x)
    dot = lambda p: jnp.dot(mb, p, preferred_element_type=F32)
    return (dot(lo) + dot(mid)) + dot(hi)


def _dot_nt_mask_rhs(x, mask01):
    mb = mask01.astype(BF16)
    hi, mid, lo = _split3(x)
    dot = lambda p: lax.dot_general(p, mb, (((1,), (1,)), ((), ())), preferred_element_type=F32)
    return (dot(lo) + dot(mid)) + dot(hi)


def _dot3(a, b):
    a_hi, a_lo, _ = _split3(a)
    b_hi, b_lo, _ = _split3(b)
    dot = lambda p, q: jnp.dot(p, q, preferred_element_type=F32)
    return (dot(a_lo, b_hi) + dot(a_hi, b_lo)) + dot(a_hi, b_hi)


def _rms(x, g):
    return x * lax.rsqrt(jnp.mean(x * x, axis=-1, keepdims=True) + NORM_EPS) * g


def _mod_body(c_ref, w_ref, b_ref, o_ref):
    c = c_ref[...]
    cs = c * jax.nn.sigmoid(c)
    o_ref[...] = _dot(cs, w_ref[...], precise=True) + b_ref[...]


def _mod(c, w_mod, b_mod):
    depth, d, n = w_mod.shape
    bsz = c.shape[0]
    tn = 1536
    return pl.pallas_call(
        _mod_body,
        out_shape=jax.ShapeDtypeStruct((depth, bsz, n), F32),
        grid=(depth, n // tn),
        in_specs=[
            pl.BlockSpec((bsz, d), lambda l, j: (0, 0)),
            pl.BlockSpec((None, d, tn), lambda l, j: (l, 0, j)),
            pl.BlockSpec((None, 1, tn), lambda l, j: (l, 0, j)),
        ],
        out_specs=pl.BlockSpec((None, bsz, tn), lambda l, j: (l, 0, j)),
        compiler_params=_params("parallel", "parallel"),
        name="mod",
    )(c, w_mod, b_mod.reshape(depth, 1, n))


def _in_proj_body(x_ref, sc_ref, sh_ref, g_ref, w_ref, nsa_ref, rw_ref, ml_ref, kse_ref, kw_ref, vse_ref,
                  vwe_ref, *, per_b):
    tm = x_ref.shape[0]
    h = _rms(x_ref[...], g_ref[...]) * (1.0 + sc_ref[...]) + sh_ref[...]
    hb = h.astype(BF16)
    nsa = jnp.dot(hb, w_ref[:, 0:NSA_W], preferred_element_type=F32)
    nsa_ref[...] = nsa
    rw_ref[...] = jnp.dot(hb, w_ref[:, NSA_W:NSA_W + RWKV_W], preferred_element_type=F32)
    ml_ref[...] = jnp.dot(hb, w_ref[:, NSA_W + RWKV_W:], preferred_element_type=F32)
    D = HEAD_DIM
    off = C_ATT + 2 * D
    lane = lax.broadcasted_iota(jnp.int32, (tm, 2 * D), 1)
    pos = (pl.program_id(0) % per_b) * tm + lax.broadcasted_iota(jnp.int32, (tm, 2 * D), 0)
    expand = jnp.where(lane - D == jnp.right_shift(pos, SLC_SHIFT), MASK_BIG, 0.0)
    kse_ref[...] = jnp.where(lane < D, nsa[:, off:off + 2 * D], expand).astype(BF16)
    kw_ref[...] = nsa[:, off + 2 * D:off + 3 * D].astype(BF16)
    tail = jnp.where(lax.broadcasted_iota(jnp.int32, (D, tm), 0) == 0, 1.0, 0.0)
    vse_ref[...] = jnp.concatenate([nsa[:, off + D:off + 2 * D].T, tail], axis=0).astype(BF16)
    vwe_ref[...] = jnp.concatenate([nsa[:, off + 3 * D:off + 4 * D].T, tail], axis=0).astype(BF16)


def _in_proj(x2, sc, sh, g, w_p, bsz, seq):
    t, d = x2.shape
    tm = 512
    per_b = seq // tm
    n_slc = seq // SLC_BLOCK
    assert n_slc == HEAD_DIM, "kse packs the key and one column per selection block into 128 lanes"
    row = lambda i: (i, 0)
    bat = lambda i: (i // per_b, 0, 0)
    fix = lambda i: (0, 0)
    col = lambda i: (i // per_b, 0, i % per_b)
    return pl.pallas_call(
        functools.partial(_in_proj_body, per_b=per_b),
        out_shape=(jax.ShapeDtypeStruct((t, NSA_W), F32),
                   jax.ShapeDtypeStruct((t, RWKV_W), F32),
                   jax.ShapeDtypeStruct((t, ML_W), F32),
                   jax.ShapeDtypeStruct((t, 2 * HEAD_DIM), BF16),
                   jax.ShapeDtypeStruct((t, HEAD_DIM), BF16),
                   jax.ShapeDtypeStruct((bsz, 128, seq), BF16),
                   jax.ShapeDtypeStruct((bsz, 128, seq), BF16)),
        grid=(t // tm,),
        in_specs=[
            pl.BlockSpec((tm, d), row),
            pl.BlockSpec((None, 1, d), bat),
            pl.BlockSpec((None, 1, d), bat),
            pl.BlockSpec((1, d), fix),
            pl.BlockSpec(w_p.shape, fix),
        ],
        out_specs=(pl.BlockSpec((tm, NSA_W), row), pl.BlockSpec((tm, RWKV_W), row),
                   pl.BlockSpec((tm, ML_W), row), pl.BlockSpec((tm, 2 * HEAD_DIM), row),
                   pl.BlockSpec((tm, HEAD_DIM), row), pl.BlockSpec((None, 128, tm), col),
                   pl.BlockSpec((None, 128, tm), col)),
        compiler_params=_params("parallel"),
        name="in_proj",
    )(x2, sc, sh, g, w_p)


def _nsa_cmp_body(x_ref, wc_ref, pe_ref, w2_ref, o_ref):
    wc = wc_ref[...]
    f = _dot(x_ref[...], wc, precise=True)
    c = _dot(pe_ref[...], wc, precise=True)
    g = f.shape[0]
    second = pltpu.roll(f[:, 128:256], g - 1, 0)
    pre = f[:, 0:128] + second + c[0:1, 0:128] + c[1:2, 128:256]
    act = jax.nn.gelu(pre, approximate=True)
    out = _dot(act, w2_ref[...], precise=True)
    rows = lax.broadcasted_iota(jnp.int32, out.shape, 0)
    o_ref[...] = jnp.where(rows < g - 1, out, 0.0)


def _nsa_cmp(xg, wc, pe8, w2):
    bsz, g, k = xg.shape
    return pl.pallas_call(
        _nsa_cmp_body,
        out_shape=jax.ShapeDtypeStruct((bsz, g, 128), F32),
        grid=(bsz,),
        in_specs=[
            pl.BlockSpec((None, g, k), lambda b: (b, 0, 0)),
            pl.BlockSpec(wc.shape, lambda b: (0, 0)),
            pl.BlockSpec(pe8.shape, lambda b: (0, 0)),
            pl.BlockSpec(w2.shape, lambda b: (0, 0)),
        ],
        out_specs=pl.BlockSpec((None, g, 128), lambda b: (b, 0, 0)),
        compiler_params=_params("parallel"),
        name="nsa_cmp",
    )(xg, wc, pe8, w2)


def _nsa_body(q_ref, gts_ref, kvc_ref, kse_ref, vse_ref, kw_ref, vwe_ref, gb_ref, g_ref, o_ref,
              m_s, acc_s, s_even, s_odd, ocmp, *, n_slc):
    tq = q_ref.shape[0]
    n_cmp = kvc_ref.shape[0]
    qi = pl.program_id(1)
    q0 = qi * tq
    scale = HEAD_DIM ** -0.5
    D = HEAD_DIM
    qt = q_ref[...].T

    kc = kvc_ref[:, 0:D]
    vc = kvc_ref[:, D:2 * D]
    nidx = lax.broadcasted_iota(jnp.int32, (n_cmp, tq), 0)
    pos = q0 + lax.broadcasted_iota(jnp.int32, (n_cmp, tq), 1)
    cmask = (nidx * CMP_STRIDE + (CMP_BLOCK - 1)) <= pos
    heads = range(H_ATT)
    sc = [_dot3(kc, qt[h * D:(h + 1) * D]) for h in heads]
    sc = [jnp.where(cmask, x * scale, NEG) for x in sc]
    ec = [jnp.where(cmask, jnp.exp(x - jnp.max(x, axis=0, keepdims=True)), 0.0) for x in sc]
    pc = [e / jnp.maximum(jnp.sum(e, axis=0, keepdims=True), 1e-30) for e in ec]
    oc = [_dot_tn(vc, p) for p in pc]
    for h in heads:
        ocmp[h * D:(h + 1) * D, :] = oc[h]
    psum = sum(pc[1:], pc[0])

    jrow = lax.broadcasted_iota(jnp.int32, (n_slc, n_cmp), 0) * SLC_BLOCK
    ncol = lax.broadcasted_iota(jnp.int32, (n_slc, n_cmp), 1) * CMP_STRIDE
    overlap_t = jnp.where((ncol < jrow + SLC_BLOCK) & (ncol + CMP_BLOCK > jrow), 1.0, 0.0)
    imp_t = _dot_mask_lhs(overlap_t, psum)
    jj = lax.broadcasted_iota(jnp.int32, (n_slc, tq), 0)
    cur = jnp.right_shift(q0 + lax.broadcasted_iota(jnp.int32, (n_slc, tq), 1), SLC_SHIFT)
    forced = (jj == 0) | (jj == cur) | (jj == cur - 1)
    score = jnp.where(forced, FORCE, jnp.where(jj <= cur, imp_t, NEG))
    group = 8
    cnt = [jnp.zeros((group, tq), F32) for _ in range(n_slc // group)]
    for j2 in range(n_slc):
        row = score[j2:j2 + 1, :]
        for gi in range(n_slc // group):
            sg = score[gi * group:(gi + 1) * group]
            if gi * group > j2:
                hit = jnp.where(row >= sg, 1.0, 0.0)
            elif (gi + 1) * group - 1 <= j2:
                hit = jnp.where(row > sg, 1.0, 0.0)
            else:
                hit = jnp.where(lax.broadcasted_iota(jnp.int32, (group, tq), 0) + gi * group > j2,
                                jnp.where(row >= sg, 1.0, 0.0), jnp.where(row > sg, 1.0, 0.0))
            cnt[gi] = cnt[gi] + hit
    cnt = jnp.concatenate(cnt, axis=0)
    selm = jnp.where(cnt < min(N_SELECT, n_slc), 0.0, -1.0).astype(BF16)

    qs = (qt * (scale * LOG2E)).astype(BF16)
    rhs_w = [qs[h * D:(h + 1) * D] for h in heads]
    rhs_s = [jnp.concatenate([rhs_w[h], selm], axis=0) for h in heads]
    m_s[...] = jnp.full(m_s.shape, NEG, F32)
    acc_s[...] = jnp.zeros_like(acc_s)

    def scores_into(buf, j):
        k = kse_ref[pl.ds(pl.multiple_of(j * TK, TK), TK), :]
        for h in heads:
            buf[h] = jnp.dot(k, rhs_s[h], preferred_element_type=F32)

    def consume(buf, j, mask):
        vt = vse_ref[:, pl.ds(pl.multiple_of(j * TK, TK), TK)]
        for h in heads:
            s = buf[h]
            if mask is not None:
                s = jnp.where(mask, s, NEG)
            m_old = m_s[h:h + 1, :]
            m_new = jnp.maximum(m_old, jnp.max(s, axis=0, keepdims=True))
            p = jnp.exp2(s - m_new).astype(BF16)
            acc_s[h] = jnp.exp2(m_old - m_new) * acc_s[h] + jnp.dot(vt, p, preferred_element_type=F32)
            m_s[h:h + 1, :] = m_new

    kpos = lax.broadcasted_iota(jnp.int32, (TK, tq), 0)
    qpos = lax.broadcasted_iota(jnp.int32, (TK, tq), 1)
    last = jnp.maximum(qi - 1, 0)
    scores_into(s_even, qi)
    scores_into(s_odd, 0)
    consume(s_even, qi, kpos <= qpos)

    def body(jp, carry):
        j = 2 * jp
        scores_into(s_even, jnp.minimum(j + 1, last))
        consume(s_odd, j, None)
        scores_into(s_odd, jnp.minimum(j + 2, last))
        consume(s_even, j + 1, None)
        return carry

    lax.fori_loop(0, qi // 2, body, 0)

    @pl.when(qi % 2 == 1)
    def _():
        consume(s_odd, qi - 1, None)

    span = WINDOW + tq
    w0 = pl.multiple_of(jnp.maximum(q0 - WINDOW, 0), TK)
    kw = kw_ref[pl.ds(w0, span), :]
    vwt = vwe_ref[:, pl.ds(w0, span)]
    kabs = w0 + lax.broadcasted_iota(jnp.int32, (span, tq), 0)
    qabs = q0 + lax.broadcasted_iota(jnp.int32, (span, tq), 1)
    wmask = (kabs <= qabs) & (kabs > qabs - WINDOW)
    sw = [jnp.where(wmask, jnp.dot(kw, rhs_w[h], preferred_element_type=F32), NEG) for h in heads]
    pw = [jnp.exp2(x - jnp.max(x, axis=0, keepdims=True)).astype(BF16) for x in sw]
    acc_w = [jnp.dot(vwt, p, preferred_element_type=F32) for p in pw]

    gate = jax.nn.sigmoid((gts_ref[...] + gb_ref[...]).T)
    outs = []
    for h in heads:
        a_s = acc_s[h]
        a_w = acc_w[h]
        outs.append(gate[h:h + 1] * ocmp[h * D:(h + 1) * D, :]
                    + gate[H_ATT + h:H_ATT + h + 1] * (a_s[0:D] / a_s[D:D + 1])
                    + gate[2 * H_ATT + h:2 * H_ATT + h + 1] * (a_w[0:D] / a_w[D:D + 1]))
    ot = jnp.concatenate(outs, axis=0)
    ot = ot * lax.rsqrt(jnp.mean(ot * ot, axis=0, keepdims=True) + NORM_EPS)
    o_ref[...] = ot.T * g_ref[...]


def _nsa(p_nsa, kv_cmp, kse, vse, kw, vwe, gate_b, out_g, bsz, seq):
    t = p_nsa.shape[0]
    nq = seq // TQ
    n_slc = seq // SLC_BLOCK
    g = kv_cmp.shape[1]
    row = lambda b, i: (b * nq + i, 0)
    per_b = lambda b, i: (b, 0, 0)
    fix = lambda b, i: (0, 0)
    return pl.pallas_call(
        functools.partial(_nsa_body, n_slc=n_slc),
        out_shape=jax.ShapeDtypeStruct((t, C_ATT), F32),
        grid=(bsz, nq),
        in_specs=[
            pl.BlockSpec((TQ, C_ATT), row),
            pl.BlockSpec((TQ, 128), lambda b, i: (b * nq + i, NSA_W // 128 - 1)),
            pl.BlockSpec((None, g, 128), per_b),
            pl.BlockSpec((None, seq, HEAD_DIM + n_slc), per_b),
            pl.BlockSpec((None, 128, seq), per_b),
            pl.BlockSpec((None, seq, HEAD_DIM), per_b),
            pl.BlockSpec((None, 128, seq), per_b),
            pl.BlockSpec((1, 128), fix),
            pl.BlockSpec((1, C_ATT), fix),
        ],
        out_specs=pl.BlockSpec((TQ, C_ATT), row),
        scratch_shapes=[pltpu.VMEM((8, TQ), F32), pltpu.VMEM((H_ATT, 128, TQ), F32),
                        pltpu.VMEM((H_ATT, TK, TQ), F32), pltpu.VMEM((H_ATT, TK, TQ), F32),
                        pltpu.VMEM((C_ATT, TQ), F32)],
        compiler_params=_params("parallel", "arbitrary"),
        name="nsa",
    )(p_nsa, p_nsa, kv_cmp, kse, vse, kw, vwe, gate_b, out_g)


def _tri(n, strict):
    r = lax.broadcasted_iota(jnp.int32, (n, n), 0)
    c = lax.broadcasted_iota(jnp.int32, (n, n), 1)
    return (c < r) if strict else (c <= r)


def _block_diag(x):
    xb = x.astype(BF16)
    first = lax.broadcasted_iota(jnp.int32, xb.shape, 1) < xb.shape[1] // 2
    zero = jnp.zeros_like(xb)
    return jnp.concatenate([jnp.where(first, xb, zero), jnp.where(first, zero, xb)], axis=0)


def _unit_lower_inverse_pairs(ms, eye2):
    n = eye2.shape[0]
    idx = range(len(ms))
    r = lax.broadcasted_iota(jnp.int32, (n, 2 * n), 0)
    c = jnp.bitwise_and(lax.broadcasted_iota(jnp.int32, (n, 2 * n), 1), n - 1)
    base = 8
    diag = jnp.right_shift(r, 3) == jnp.right_shift(c, 3)
    m8 = [jnp.where(diag, m, 0.0) for m in ms]
    t = [eye2 + m for m in m8]
    p2 = [_dot(m, _block_diag(m)) for m in m8]
    t = [t[i] + _dot(p2[i], _block_diag(t[i])) for i in idx]
    p4 = [_dot(p, _block_diag(p)) for p in p2]
    t = [t[i] + _dot(p4[i], _block_diag(t[i])) for i in idx]
    b = base
    while b < n:
        sh = b.bit_length() - 1
        pair = jnp.right_shift(r, sh + 1) == jnp.right_shift(c, sh + 1)
        lower_left = pair & (jnp.right_shift(r, sh) != jnp.right_shift(c, sh))
        left = [_dot(t[i], _block_diag(jnp.where(lower_left, ms[i], 0.0))) for i in idx]
        t = [t[i] + _dot(left[i], _block_diag(t[i])) for i in idx]
        b *= 2
    return t


def _rwkv_body(p_ref, mu_ref, w0_ref, w2_ref, a0_ref, a2_ref, g2_ref, kk_ref, ka_ref, rk_ref, lnw_ref,
               lnb_ref, o_ref, state, prev):
    L = CHUNK
    C = C_RWKV
    tb = p_ref.shape[0]
    chunks = range(tb // L)

    @pl.when(pl.program_id(1) == 0)
    def _():
        state[...] = jnp.zeros_like(state)
        prev[...] = jnp.zeros_like(prev)

    x = p_ref[...]
    rows = lax.broadcasted_iota(jnp.int32, x.shape, 0)
    shifted = jnp.where(rows == 0, prev[0:1, :], pltpu.roll(x, 1, 0))
    prev[0:1, :] = x[tb - 1:tb, :]
    x = x + mu_ref[...] * (shifted - x)
    r = x[:, 0:C]
    k = x[:, C:2 * C]
    v = x[:, 2 * C:3 * C]
    xw = x[:, 3 * C:3 * C + RANK_W]
    xa = x[:, 3 * C + RANK_W:3 * C + RANK_W + RANK_A]
    xg = x[:, 3 * C + RANK_W + RANK_A:]
    w = -jax.nn.softplus(-(w0_ref[...] + _dot3(jnp.tanh(xw), w2_ref[...]))) - 0.5
    logw = -jnp.exp(w)
    a = jax.nn.sigmoid(a0_ref[...] + _dot3(xa, a2_ref[...]))
    g = _dot(jax.nn.sigmoid(xg), g2_ref[...])
    kkf = k * kk_ref[...]
    kmod = k * (1.0 + (a - 1.0) * ka_ref[...])
    rk = r * kmod * rk_ref[...]

    tri_incl = jnp.where(_tri(L, strict=False), 1.0, 0.0)
    cum = jnp.concatenate([_dot_mask_lhs(tri_incl, logw[c * L:(c + 1) * L]) for c in chunks], axis=0)
    cum_last = [cum[(c + 1) * L - 1:(c + 1) * L, :] for c in chunks]
    cum_end = jnp.concatenate([jnp.broadcast_to(cl, (L, C)) for cl in cum_last], axis=0)
    e_pos = jnp.exp(cum)
    e_prev = jnp.exp(cum - logw)
    e_neg = jnp.exp(-cum)
    e_rem = jnp.exp(cum_end - cum)
    N = HEAD_DIM
    PW = 2 * N
    pairs = range(H_RWKV // 2)
    lane_l = lax.broadcasted_iota(jnp.int32, (L, PW), 1)
    row_l = lax.broadcasted_iota(jnp.int32, (L, PW), 0)
    col_l = jnp.bitwise_and(lane_l, N - 1)
    first = lane_l < N
    first2 = lax.broadcasted_iota(jnp.int32, (2 * L, PW), 1) < N
    strict2 = col_l < row_l
    incl2 = col_l <= row_l
    eye2 = jnp.where(col_l == row_l, 1.0, 0.0)

    def halves(x_):
        lane = lax.broadcasted_iota(jnp.int32, x_.shape, 1)
        sa = jnp.sum(jnp.where(lane < N, x_, 0.0), axis=-1, keepdims=True)
        sb = jnp.sum(jnp.where(lane < N, 0.0, x_), axis=-1, keepdims=True)
        return jnp.where(lane < N, sa, sb)

    ps = [slice(p * PW, (p + 1) * PW) for p in pairs]
    kk_p = [kkf[:, sl] for sl in ps]
    kk_p = [x_ / jnp.maximum(jnp.sqrt(halves(x_ * x_)), 1e-12) for x_ in kk_p]
    alpha_f = [-kk_p[p] * e_prev[:, ps[p]] for p in pairs]
    r_f = [r[:, ps[p]] * e_pos[:, ps[p]] for p in pairs]
    beta = [kk_p[p] * a[:, ps[p]] for p in pairs]
    beta_f = [beta[p] * e_neg[:, ps[p]] for p in pairs]
    k_f = [kmod[:, ps[p]] * e_neg[:, ps[p]] for p in pairs]
    beta_e = [beta[p] * e_rem[:, ps[p]] for p in pairs]
    k_e = [kmod[:, ps[p]] * e_rem[:, ps[p]] for p in pairs]
    items = [(c, p) for c in chunks for p in pairs]
    n_items = range(len(items))
    rs = [slice(c * L, (c + 1) * L) for c, _ in items]
    pi = [p for _, p in items]
    lhs = [jnp.concatenate([alpha_f[pi[i]][rs[i]], r_f[pi[i]][rs[i]]], axis=0).astype(BF16) for i in n_items]
    zero16 = jnp.zeros((2 * L, PW), BF16)
    lhs4 = [jnp.concatenate([jnp.where(first2, x_, zero16), jnp.where(first2, zero16, x_)], axis=0) for x_ in lhs]
    bk = [(beta_f[pi[i]][rs[i]].astype(BF16), k_f[pi[i]][rs[i]].astype(BF16)) for i in n_items]
    v_i = [v[rs[i], ps[pi[i]]] for i in n_items]
    out1 = [_dot_nt(lhs4[i], jnp.concatenate([bk[i][0], bk[i][1]], axis=0)) for i in n_items]
    out2 = [_dot_nt(lhs4[i], jnp.concatenate([bk[i][1], bk[i][0]], axis=0)) for i in n_items]
    m_ab = [jnp.where(strict2, jnp.where(first, out1[i][0:L], out2[i][2 * L:3 * L]), 0.0) for i in n_items]
    m_ak = [jnp.where(strict2, jnp.where(first, out2[i][0:L], out1[i][2 * L:3 * L]), 0.0) for i in n_items]
    m_rb = [jnp.where(incl2, jnp.where(first, out1[i][L:2 * L], out2[i][3 * L:4 * L]), 0.0) for i in n_items]
    m_rk = [jnp.where(incl2, jnp.where(first, out2[i][L:2 * L], out1[i][3 * L:4 * L]), 0.0) for i in n_items]
    t_inv = _unit_lower_inverse_pairs(m_ab, eye2)
    mv = [_dot(jnp.concatenate([m_ak[i], m_rk[i]], axis=0), _block_diag(v_i[i])) for i in n_items]
    z = [state[p] for p in pairs]
    y_parts = [[] for _ in pairs]
    for c in chunks:
        idx = [c * len(pairs) + p for p in pairs]
        az = [_dot(lhs[i], _block_diag(z[p])) for p, i in enumerate(idx)]
        u = [_dot(t_inv[i], _block_diag(az[p][0:L] + mv[i][0:L])) for p, i in enumerate(idx)]
        for p, i in enumerate(idx):
            y_parts[p].append(az[p][L:2 * L] + _dot(m_rb[i], _block_diag(u[p])) + mv[i][L:2 * L])
        cross = [_dot_tn(jnp.concatenate([beta_e[p][rs[i]], k_e[p][rs[i]]], axis=0),
                         jnp.concatenate([u[p], v_i[i]], axis=0)) for p, i in enumerate(idx)]
        for p in pairs:
            gam = eye2 * jnp.exp(cum_last[c][:, ps[p]])
            z[p] = z[p] * halves(gam) + jnp.where(first, cross[p][0:N], cross[p][N:2 * N])
    for p in pairs:
        state[p] = z[p]
    yn = []
    for p in pairs:
        y_p = jnp.concatenate(y_parts[p], axis=0)
        mu_y = halves(y_p) * (1.0 / N)
        dev = y_p - mu_y
        var = halves(dev * dev) * (1.0 / N)
        yn.append(dev * lax.rsqrt(var + RWKV_GN_EPS))
    bonus = [halves(rk[:, ps[p]]) * v[:, ps[p]] for p in pairs]
    yn = jnp.concatenate(yn, axis=-1)
    bonus = jnp.concatenate(bonus, axis=-1)
    o_ref[...] = (yn * lnw_ref[...] + lnb_ref[...] + bonus) * g


def _rwkv(p_rw, mu, w0, w2, a0, a2, g2, k_k, k_a, r_k, ln_w, ln_b, bsz, seq):
    t = p_rw.shape[0]
    tb = RWKV_BLOCK
    nb = seq // tb
    row = lambda b, c: (b * nb + c, 0)
    fix = lambda b, c: (0, 0)
    vec = lambda a: a.reshape(1, -1)
    args = (vec(mu), vec(w0), w2, vec(a0), a2, g2, vec(k_k), vec(k_a), vec(r_k), vec(ln_w), vec(ln_b))
    return pl.pallas_call(
        _rwkv_body,
        out_shape=jax.ShapeDtypeStruct((t, C_RWKV), F32),
        grid=(bsz, nb),
        in_specs=[pl.BlockSpec((tb, RWKV_W), row)] + [pl.BlockSpec(a.shape, fix) for a in args],
        out_specs=pl.BlockSpec((tb, C_RWKV), row),
        scratch_shapes=[pltpu.VMEM((H_RWKV // 2, HEAD_DIM, 2 * HEAD_DIM), F32), pltpu.VMEM((8, RWKV_W), F32)],
        compiler_params=_params("parallel", "arbitrary"),
        name="rwkv",
    )(p_rw, *args)


def _mlstm_body(p_ref, cw_ref, cb_ref, gb_ref, ng_ref, o_ref, cstate, mstate, prev):
    L = CHUNK
    C = C_MLSTM
    D = HEAD_DIM
    tb = p_ref.shape[0]
    chunks = range(tb // L)

    @pl.when(pl.program_id(1) == 0)
    def _():
        cstate[...] = jnp.zeros_like(cstate)
        mstate[...] = jnp.zeros_like(mstate)
        prev[...] = jnp.zeros_like(prev)

    qk_in = p_ref[:, 0:2 * C]
    pv = prev[...]
    rows8 = lax.broadcasted_iota(jnp.int32, pv.shape, 0)
    conv = qk_in * cw_ref[CONV_WIDTH - 1:CONV_WIDTH, :] + cb_ref[...]
    for d in range(1, CONV_WIDTH):
        rolled = pltpu.roll(qk_in, d, 0)
        top = jnp.where(rows8 < d, pltpu.roll(pv, d, 0), rolled[0:8])
        sh = jnp.concatenate([top, rolled[8:]], axis=0)
        conv = conv + sh * cw_ref[CONV_WIDTH - 1 - d:CONV_WIDTH - d, :]
    prev[...] = qk_in[tb - 8:tb, :]
    qk = conv * jax.nn.sigmoid(conv)
    q = qk[:, 0:C]
    k = qk[:, C:2 * C] * (D ** -0.5)
    v = p_ref[:, 2 * C:3 * C]
    og = p_ref[:, 3 * C:4 * C]
    gates = p_ref[:, 4 * C:4 * C + 128] + gb_ref[...]
    lane = lax.broadcasted_iota(jnp.int32, gates.shape, 1)
    is_f = (lane >= H_MLSTM) & (lane < 2 * H_MLSTM)
    gl = jnp.where(is_f, jax.nn.log_sigmoid(gates), gates)
    tri_incl = jnp.where(_tri(L, strict=False), 1.0, 0.0)
    bcum = [_dot_mask_lhs(tri_incl, gl[c * L:(c + 1) * L]) for c in chunks]
    gl_t = [gl[c * L:(c + 1) * L].T for c in chunks]
    bcum_t = [_dot_nt_mask_rhs(g_, tri_incl) for g_ in gl_t]
    m_all = mstate[...]
    src = lax.broadcasted_iota(jnp.int32, (L, L), 0)
    qry = lax.broadcasted_iota(jnp.int32, (L, L), 1)
    causal_t = src <= qry
    ones_row = jnp.where(lax.broadcasted_iota(jnp.int32, (D, L), 0) == 0, 1.0, 0.0)

    heads = range(H_MLSTM)
    items = [(c, h) for c in chunks for h in heads]
    n_items = range(len(items))
    q_t = [q[c * L:(c + 1) * L].T for c in chunks]
    v_t = [v[c * L:(c + 1) * L].T for c in chunks]
    qt_i = [q_t[c][h * D:(h + 1) * D] for c, h in items]
    vt_ext = [jnp.concatenate([v_t[c][h * D:(h + 1) * D], ones_row], axis=0) for c, h in items]
    k_i = [k[c * L:(c + 1) * L, h * D:(h + 1) * D] for c, h in items]
    c_col = [gl[c * L:(c + 1) * L, h:h + 1] - bcum[c][:, H_MLSTM + h:H_MLSTM + h + 1] for c, h in items]
    b_row = [bcum_t[c][H_MLSTM + h:H_MLSTM + h + 1, :] for c, h in items]
    c_row = [gl_t[c][h:h + 1, :] - b_row[i] for i, (c, h) in enumerate(items)]
    g_tot = [bcum[c][L - 1:L, H_MLSTM + h:H_MLSTM + h + 1] for c, h in items]
    u_row = [g_tot[i] + c_row[i] for i in n_items]
    u_max = [jnp.max(x_, axis=1, keepdims=True) for x_ in u_row]
    m0 = []
    m_run = [m_all[0:1, h:h + 1] for h in heads]
    for c in chunks:
        for h in heads:
            i = c * H_MLSTM + h
            m0.append(m_run[h])
            m_run[h] = jnp.maximum(g_tot[i] + m_run[h], u_max[i])
    m_next = [m0[i + H_MLSTM] if i + H_MLSTM < len(items) else m_run[items[i][1]] for i in n_items]
    kq = [_dot(k_i[i], qt_i[i]) for i in n_items]
    kv = [_dot(vt_ext[i] * jnp.exp(u_row[i] - m_next[i]), k_i[i]) for i in n_items]
    cm = []
    cm_run = [cstate[h] for h in heads]
    for c in chunks:
        for h in heads:
            i = c * H_MLSTM + h
            cm.append(cm_run[h])
            cm_run[h] = jnp.exp(g_tot[i] + m0[i] - m_next[i]) * cm_run[h] + kv[i]
    inter = [_dot(cm[i], qt_i[i]) for i in n_items]
    c_b = [jnp.where(causal_t, jnp.broadcast_to(c_col[i], (L, L)), -jnp.inf) for i in n_items]
    m_row = [jnp.maximum(m0[i], jnp.max(c_b[i], axis=0, keepdims=True)) for i in n_items]
    s_t = [jnp.exp(c_b[i] - m_row[i]) * kq[i] for i in n_items]
    sv = [_dot(vt_ext[i], s_t[i]) for i in n_items]
    lane8 = lax.broadcasted_iota(jnp.int32, m_all.shape, 1)
    for h in heads:
        cstate[h] = cm_run[h]
        m_all = jnp.where(lane8 == h, m_run[h], m_all)
    mstate[...] = m_all
    outs = []
    for i in n_items:
        num = jnp.exp(m0[i] - m_row[i]) * inter[i] + sv[i]
        hh = num[0:D] / jnp.maximum(jnp.abs(num[D:D + 1]), jnp.exp(-(b_row[i] + m_row[i])))
        outs.append(hh * lax.rsqrt(jnp.mean(hh * hh, axis=0, keepdims=True) + NORM_EPS))
    hcat = jnp.concatenate([jnp.concatenate(outs[c * H_MLSTM:(c + 1) * H_MLSTM], axis=0).T for c in chunks], axis=0)
    o_ref[...] = jax.nn.sigmoid(og) * (hcat * ng_ref[...])


def _mlstm(p_ml, conv_w, conv_b, gate_b, norm_g, bsz, seq):
    t = p_ml.shape[0]
    tb = ML_BLOCK
    nb = seq // tb
    row = lambda b, c: (b * nb + c, 0)
    fix = lambda b, c: (0, 0)
    args = (conv_w, conv_b.reshape(1, -1), gate_b, norm_g.reshape(1, -1))
    return pl.pallas_call(
        _mlstm_body,
        out_shape=jax.ShapeDtypeStruct((t, C_MLSTM), F32),
        grid=(bsz, nb),
        in_specs=[pl.BlockSpec((tb, ML_W), row)] + [pl.BlockSpec(a.shape, fix) for a in args],
        out_specs=pl.BlockSpec((tb, C_MLSTM), row),
        scratch_shapes=[pltpu.VMEM((H_MLSTM, 128, HEAD_DIM), F32), pltpu.VMEM((8, 128), F32),
                        pltpu.VMEM((8, 2 * C_MLSTM), F32)],
        compiler_params=_params("parallel", "arbitrary"),
        name="mlstm",
    )(p_ml, *args)


def _out_proj_body(a_ref, b_ref, c_ref, wa_ref, wb_ref, wc_ref, x_ref, gt_ref, g_ref, o_ref):
    y = (jnp.dot(a_ref[...].astype(BF16), wa_ref[...], preferred_element_type=F32)
         + jnp.dot(b_ref[...].astype(BF16), wb_ref[...], preferred_element_type=F32)
         + jnp.dot(c_ref[...].astype(BF16), wc_ref[...], preferred_element_type=F32))
    o_ref[...] = x_ref[...] + gt_ref[...] * _rms(y, g_ref[...])


def _out_proj(o_nsa, o_rw, o_ml, wa, wb, wc, x2, gt, g, seq):
    t, d = x2.shape
    tm = 512
    per_b = seq // tm
    row = lambda i: (i, 0)
    fix = lambda i: (0, 0)
    return pl.pallas_call(
        _out_proj_body,
        out_shape=jax.ShapeDtypeStruct((t, d), F32),
        grid=(t // tm,),
        in_specs=[
            pl.BlockSpec((tm, C_ATT), row), pl.BlockSpec((tm, C_RWKV), row), pl.BlockSpec((tm, C_MLSTM), row),
            pl.BlockSpec(wa.shape, fix), pl.BlockSpec(wb.shape, fix), pl.BlockSpec(wc.shape, fix),
            pl.BlockSpec((tm, d), row),
            pl.BlockSpec((None, 1, d), lambda i: (i // per_b, 0, 0)),
            pl.BlockSpec((1, d), fix),
        ],
        out_specs=pl.BlockSpec((tm, d), row),
        compiler_params=_params("parallel"),
        name="out_proj",
    )(o_nsa, o_rw, o_ml, wa, wb, wc, x2, gt, g)


def _ffn_body(x_ref, sc_ref, sh_ref, gt_ref, gpre_ref, gpost_ref, wg_ref, wu_ref, wd_ref, o_ref, hb, acc):
    f = pl.program_id(1)

    @pl.when(f == 0)
    def _():
        h = _rms(x_ref[...], gpre_ref[...]) * (1.0 + sc_ref[...]) + sh_ref[...]
        hb[...] = h.astype(BF16)
        acc[...] = jnp.zeros_like(acc)

    hv = hb[...]
    gate = jnp.dot(hv, wg_ref[...], preferred_element_type=F32)
    up = jnp.dot(hv, wu_ref[...], preferred_element_type=F32)
    act = (gate * jax.nn.sigmoid(gate) * up).astype(BF16)
    acc[...] += jnp.dot(act, wd_ref[...], preferred_element_type=F32)

    @pl.when(f == pl.num_programs(1) - 1)
    def _():
        o_ref[...] = x_ref[...] + gt_ref[...] * _rms(acc[...], gpost_ref[...])


def _ffn(x2, sc, sh, gt, g_pre, g_post, wg, wu, wd, seq):
    t, d = x2.shape
    dff = wg.shape[1]
    tm = 512
    tf = dff // 2
    per_b = seq // tm
    row = lambda i, f: (i, 0)
    bat = lambda i, f: (i // per_b, 0, 0)
    fix = lambda i, f: (0, 0)
    return pl.pallas_call(
        _ffn_body,
        out_shape=jax.ShapeDtypeStruct((t, d), F32),
        grid=(t // tm, dff // tf),
        in_specs=[
            pl.BlockSpec((tm, d), row),
            pl.BlockSpec((None, 1, d), bat), pl.BlockSpec((None, 1, d), bat), pl.BlockSpec((None, 1, d), bat),
            pl.BlockSpec((1, d), fix), pl.BlockSpec((1, d), fix),
            pl.BlockSpec((d, tf), lambda i, f: (0, f)),
            pl.BlockSpec((d, tf), lambda i, f: (0, f)),
            pl.BlockSpec((tf, d), lambda i, f: (f, 0)),
        ],
        out_specs=pl.BlockSpec((tm, d), row),
        scratch_shapes=[pltpu.VMEM((tm, d), BF16), pltpu.VMEM((tm, d), F32)],
        compiler_params=_params("parallel", "arbitrary"),
        name="ffn",
    )(x2, sc, sh, gt, g_pre, g_post, wg, wu, wd)


def _pad_cols(a, width):
    return jnp.pad(a, ((0, 0), (0, width - a.shape[1])))


def _layout_w_in(w_in):
    d_in_rw = NSA_IN + RWKV_W
    nsa = _pad_cols(w_in[:, :NSA_IN], NSA_W)
    rw = w_in[:, NSA_IN:d_in_rw]
    ml = _pad_cols(w_in[:, d_in_rw:], ML_W)
    return jnp.concatenate([nsa, rw, ml], axis=1).astype(BF16)


def _layout_cmp(ck_w1, cv_w1, ck_w2, cv_w2, pe_k, pe_v):
    half = CMP_BLOCK // 2
    ck = ck_w1.reshape(2, half, HEAD_DIM, HEAD_DIM)
    cv = cv_w1.reshape(2, half, HEAD_DIM, HEAD_DIM)
    z = jnp.zeros_like(ck[0])
    top = jnp.concatenate([ck[0], z, ck[1], z], axis=-1)
    bot = jnp.concatenate([z, cv[0], z, cv[1]], axis=-1)
    wc = jnp.concatenate([top, bot], axis=1).reshape(half * 2 * HEAD_DIM, 4 * HEAD_DIM)
    pe = jnp.concatenate([pe_k, pe_v], axis=-1).reshape(2, half * 2 * HEAD_DIM)
    pe8 = jnp.pad(pe, ((0, 6), (0, 0)))
    z2 = jnp.zeros_like(ck_w2)
    w2 = jnp.concatenate([jnp.concatenate([ck_w2, z2], axis=1), jnp.concatenate([z2, cv_w2], axis=1)], axis=0)
    return wc, pe8, w2


def kernel(x, c, w_mod, b_mod, g_pre_mix, g_post_mix, g_pre_ffn, g_post_ffn, w_in, w_out, nsa_pe_k, nsa_pe_v, nsa_ck_w1, nsa_ck_w2, nsa_cv_w1, nsa_cv_w2, nsa_gate_b, nsa_out_g, rw_mu, rw_w0, rw_w2, rw_a0, rw_a2, rw_g2, rw_kk, rw_ka, rw_rk, rw_ln_w, rw_ln_b, ml_conv_w, ml_conv_b, ml_ig_b, ml_fg_b, ml_norm_g, ffn_w_gate, ffn_w_up, ffn_w_down):
    bsz, seq, d = x.shape
    depth = w_mod.shape[0]
    t = bsz * seq
    half = CMP_BLOCK // 2
    mod = _mod(c, w_mod, b_mod)
    x2 = x.reshape(t, d)
    for l in range(depth):
        sh1, sc1, gt1, sh2, sc2, gt2 = [m.reshape(bsz, 1, d) for m in jnp.split(mod[l], 6, axis=-1)]
        p_nsa, p_rw, p_ml, kse, kw, vse, vwe = _in_proj(x2, sc1, sh1, g_pre_mix[l].reshape(1, d),
                                                        _layout_w_in(w_in[l]), bsz, seq)
        wc, pe8, w2 = _layout_cmp(nsa_ck_w1[l], nsa_cv_w1[l], nsa_ck_w2[l], nsa_cv_w2[l], nsa_pe_k[l], nsa_pe_v[l])
        xg = p_nsa[:, C_ATT:C_ATT + 2 * HEAD_DIM].reshape(bsz, seq // half, half * 2 * HEAD_DIM)
        kv_cmp = _nsa_cmp(xg, wc, pe8, w2)
        gate_b = jnp.pad(nsa_gate_b[l], (0, 128 - 3 * H_ATT)).reshape(1, 128)
        o_nsa = _nsa(p_nsa, kv_cmp, kse.reshape(bsz, seq, -1), vse, kw.reshape(bsz, seq, -1), vwe, gate_b,
                     nsa_out_g[l].reshape(1, C_ATT), bsz, seq)
        o_rw = _rwkv(p_rw, rw_mu[l], rw_w0[l], rw_w2[l], rw_a0[l], rw_a2[l], rw_g2[l], rw_kk[l], rw_ka[l],
                     rw_rk[l], rw_ln_w[l], rw_ln_b[l], bsz, seq)
        ml_gate_b = jnp.pad(jnp.concatenate([ml_ig_b[l], ml_fg_b[l]]), (0, 128 - 2 * H_MLSTM)).reshape(1, 128)
        o_ml = _mlstm(p_ml, ml_conv_w[l], ml_conv_b[l], ml_gate_b, ml_norm_g[l], bsz, seq)
        wo = w_out[l].astype(BF16)
        x2 = _out_proj(o_nsa, o_rw, o_ml, wo[:C_ATT], wo[C_ATT:C_ATT + C_RWKV], wo[C_ATT + C_RWKV:], x2, gt1,
                       g_post_mix[l].reshape(1, d), seq)
        x2 = _ffn(x2, sc2, sh2, gt2, g_pre_ffn[l].reshape(1, d), g_post_ffn[l].reshape(1, d),
                  ffn_w_gate[l].astype(BF16), ffn_w_up[l].astype(BF16), ffn_w_down[l].astype(BF16), seq)
    return x2.reshape(bsz, seq, d)
```

```python
import functools

import jax
import jax.numpy as jnp
from jax import lax
from jax.experimental import pallas as pl
from jax.experimental.pallas import tpu as pltpu

F32 = jnp.float32
BF16 = jnp.bfloat16
HIGHEST = lax.Precision.HIGHEST

HEAD_DIM = 64
H_ATT = 4
C_ATT = H_ATT * HEAD_DIM
H_RWKV = 6
C_RWKV = H_RWKV * HEAD_DIM
H_MLSTM = 6
C_MLSTM = H_MLSTM * HEAD_DIM
CMP_BLOCK = 32
CMP_STRIDE = 16
SLC_BLOCK = 64
SLC_SHIFT = 6
N_SELECT = 16
WINDOW = 512
NEG = -1e30
FORCE = 1e9
MASK_BIG = 2.0 ** 100
LOG2E = 1.4426950408889634
RANK_W = 64
RANK_A = 64
RANK_G = 128
RWKV_GN_EPS = 64e-5
CHUNK = 64
RWKV_BLOCK = 256
ML_BLOCK = 256
CONV_WIDTH = 4
NORM_EPS = 1e-6

NSA_W = 768
RWKV_W = 3 * C_RWKV + RANK_W + RANK_A + RANK_G
ML_W = 4 * C_MLSTM + 128
NSA_IN = C_ATT + 6 * HEAD_DIM + 3 * H_ATT
ML_IN = 4 * C_MLSTM + 2 * H_MLSTM

VMEM_LIMIT = 56 * 1024 * 1024

TQ = 256
TK = 256


def _params(*sem):
    return pltpu.CompilerParams(dimension_semantics=sem, vmem_limit_bytes=VMEM_LIMIT)


def _dot(a, b, precise=False):
    if precise:
        return jnp.dot(a.astype(F32), b.astype(F32), preferred_element_type=F32, precision=HIGHEST)
    return jnp.dot(a.astype(BF16), b.astype(BF16), preferred_element_type=F32)


def _dot_nt(a, b, precise=False):
    dn = (((1,), (1,)), ((), ()))
    if precise:
        return lax.dot_general(a.astype(F32), b.astype(F32), dn, preferred_element_type=F32, precision=HIGHEST)
    return lax.dot_general(a.astype(BF16), b.astype(BF16), dn, preferred_element_type=F32)


def _dot_tn(a, b, precise=False):
    dn = (((0,), (0,)), ((), ()))
    if precise:
        return lax.dot_general(a.astype(F32), b.astype(F32), dn, preferred_element_type=F32, precision=HIGHEST)
    return lax.dot_general(a.astype(BF16), b.astype(BF16), dn, preferred_element_type=F32)


def _split3(x):
    hi = x.astype(BF16)
    rest = x - hi.astype(F32)
    mid = rest.astype(BF16)
    lo = (rest - mid.astype(F32)).astype(BF16)
    return hi, mid, lo


def _dot_mask_lhs(mask01, x):
    mb = mask01.astype(BF16)
    hi, mid, lo = _split3(x)
    dot = lambda p: jnp.dot(mb, p, preferred_element_type=F32)
    return (dot(lo) + dot(mid)) + dot(hi)


def _dot_nt_mask_rhs(x, mask01):
    mb = mask01.astype(BF16)
    hi, mid, lo = _split3(x)
    dot = lambda p: lax.dot_general(p, mb, (((1,), (1,)), ((), ())), preferred_element_type=F32)
    return (dot(lo) + dot(mid)) + dot(hi)


def _dot3(a, b):
    a_hi, a_lo, _ = _split3(a)
    b_hi, b_lo, _ = _split3(b)
    dot = lambda p, q: jnp.dot(p, q, preferred_element_type=F32)
    return (dot(a_lo, b_hi) + dot(a_hi, b_lo)) + dot(a_hi, b_hi)


def _rms(x, g):
    return x * lax.rsqrt(jnp.mean(x * x, axis=-1, keepdims=True) + NORM_EPS) * g


def _mod_body(c_ref, w_ref, b_ref, o_ref):
    c = c_ref[...]
    cs = c * jax.nn.sigmoid(c)
    o_ref[...] = _dot(cs, w_ref[...], precise=True) + b_ref[...]


def _mod(c, w_mod, b_mod):
    depth, d, n = w_mod.shape
    bsz = c.shape[0]
    tn = 1536
    return pl.pallas_call(
        _mod_body,
        out_shape=jax.ShapeDtypeStruct((depth, bsz, n), F32),
        grid=(depth, n // tn),
        in_specs=[
            pl.BlockSpec((bsz, d), lambda l, j: (0, 0)),
            pl.BlockSpec((None, d, tn), lambda l, j: (l, 0, j)),
            pl.BlockSpec((None, 1, tn), lambda l, j: (l, 0, j)),
        ],
        out_specs=pl.BlockSpec((None, bsz, tn), lambda l, j: (l, 0, j)),
        compiler_params=_params("parallel", "parallel"),
        name="mod",
    )(c, w_mod, b_mod.reshape(depth, 1, n))


def _in_proj_body(x_ref, sc_ref, sh_ref, g_ref, w_ref, nsa_ref, rw_ref, ml_ref, kse_ref, kw_ref, vse_ref,
                  vwe_ref, *, per_b):
    tm = x_ref.shape[0]
    h = _rms(x_ref[...], g_ref[...]) * (1.0 + sc_ref[...]) + sh_ref[...]
    hb = h.astype(BF16)
    nsa = jnp.dot(hb, w_ref[:, 0:NSA_W], preferred_element_type=F32)
    nsa_ref[...] = nsa
    rw_ref[...] = jnp.dot(hb, w_ref[:, NSA_W:NSA_W + RWKV_W], preferred_element_type=F32)
    ml_ref[...] = jnp.dot(hb, w_ref[:, NSA_W + RWKV_W:], preferred_element_type=F32)
    D = HEAD_DIM
    off = C_ATT + 2 * D
    lane = lax.broadcasted_iota(jnp.int32, (tm, 2 * D), 1)
    pos = (pl.program_id(0) % per_b) * tm + lax.broadcasted_iota(jnp.int32, (tm, 2 * D), 0)
    expand = jnp.where(lane - D == jnp.right_shift(pos, SLC_SHIFT), MASK_BIG, 0.0)
    kse_ref[...] = jnp.where(lane < D, nsa[:, off:off + 2 * D], expand).astype(BF16)
    kw_ref[...] = nsa[:, off + 2 * D:off + 3 * D].astype(BF16)
    tail = jnp.where(lax.broadcasted_iota(jnp.int32, (D, tm), 0) == 0, 1.0, 0.0)
    vse_ref[...] = jnp.concatenate([nsa[:, off + D:off + 2 * D].T, tail], axis=0).astype(BF16)
    vwe_ref[...] = jnp.concatenate([nsa[:, off + 3 * D:off + 4 * D].T, tail], axis=0).astype(BF16)


def _in_proj(x2, sc, sh, g, w_p, bsz, seq):
    t, d = x2.shape
    tm = 512
    per_b = seq // tm
    n_slc = seq // SLC_BLOCK
    assert n_slc == HEAD_DIM, "kse packs the key and one column per selection block into 128 lanes"
    row = lambda i: (i, 0)
    bat = lambda i: (i // per_b, 0, 0)
    fix = lambda i: (0, 0)
    col = lambda i: (i // per_b, 0, i % per_b)
    return pl.pallas_call(
        functools.partial(_in_proj_body, per_b=per_b),
        out_shape=(jax.ShapeDtypeStruct((t, NSA_W), F32),
                   jax.ShapeDtypeStruct((t, RWKV_W), F32),
                   jax.ShapeDtypeStruct((t, ML_W), F32),
                   jax.ShapeDtypeStruct((t, 2 * HEAD_DIM), BF16),
                   jax.ShapeDtypeStruct((t, HEAD_DIM), BF16),
                   jax.ShapeDtypeStruct((bsz, 128, seq), BF16),
                   jax.ShapeDtypeStruct((bsz, 128, seq), BF16)),
        grid=(t // tm,),
        in_specs=[
            pl.BlockSpec((tm, d), row),
            pl.BlockSpec((None, 1, d), bat),
            pl.BlockSpec((None, 1, d), bat),
            pl.BlockSpec((1, d), fix),
            pl.BlockSpec(w_p.shape, fix),
        ],
        out_specs=(pl.BlockSpec((tm, NSA_W), row), pl.BlockSpec((tm, RWKV_W), row),
                   pl.BlockSpec((tm, ML_W), row), pl.BlockSpec((tm, 2 * HEAD_DIM), row),
                   pl.BlockSpec((tm, HEAD_DIM), row), pl.BlockSpec((None, 128, tm), col),
                   pl.BlockSpec((None, 128, tm), col)),
        compiler_params=_params("parallel"),
        name="in_proj",
    )(x2, sc, sh, g, w_p)


def _nsa_cmp_body(x_ref, wc_ref, pe_ref, w2_ref, o_ref):
    wc = wc_ref[...]
    f = _dot(x_ref[...], wc, precise=True)
    c = _dot(pe_ref[...], wc, precise=True)
    g = f.shape[0]
    second = pltpu.roll(f[:, 128:256], g - 1, 0)
    pre = f[:, 0:128] + second + c[0:1, 0:128] + c[1:2, 128:256]
    act = jax.nn.gelu(pre, approximate=True)
    out = _dot(act, w2_ref[...], precise=True)
    rows = lax.broadcasted_iota(jnp.int32, out.shape, 0)
    o_ref[...] = jnp.where(rows < g - 1, out, 0.0)


def _nsa_cmp(xg, wc, pe8, w2):
    bsz, g, k = xg.shape
    return pl.pallas_call(
        _nsa_cmp_body,
        out_shape=jax.ShapeDtypeStruct((bsz, g, 128), F32),
        grid=(bsz,),
        in_specs=[
            pl.BlockSpec((None, g, k), lambda b: (b, 0, 0)),
            pl.BlockSpec(wc.shape, lambda b: (0, 0)),
            pl.BlockSpec(pe8.shape, lambda b: (0, 0)),
            pl.BlockSpec(w2.shape, lambda b: (0, 0)),
        ],
        out_specs=pl.BlockSpec((None, g, 128), lambda b: (b, 0, 0)),
        compiler_params=_params("parallel"),
        name="nsa_cmp",
    )(xg, wc, pe8, w2)


def _nsa_body(q_ref, gts_ref, kvc_ref, kse_ref, vse_ref, kw_ref, vwe_ref, gb_ref, g_ref, o_ref,
              m_s, acc_s, s_even, s_odd, ocmp, *, n_slc):
    tq = q_ref.shape[0]
    n_cmp = kvc_ref.shape[0]
    qi = pl.program_id(1)
    q0 = qi * tq
    scale = HEAD_DIM ** -0.5
    D = HEAD_DIM
    qt = q_ref[...].T

    kc = kvc_ref[:, 0:D]
    vc = kvc_ref[:, D:2 * D]
    nidx = lax.broadcasted_iota(jnp.int32, (n_cmp, tq), 0)
    pos = q0 + lax.broadcasted_iota(jnp.int32, (n_cmp, tq), 1)
    cmask = (nidx * CMP_STRIDE + (CMP_BLOCK - 1)) <= pos
    heads = range(H_ATT)
    sc = [_dot3(kc, qt[h * D:(h + 1) * D]) for h in heads]
    sc = [jnp.where(cmask, x * scale, NEG) for x in sc]
    ec = [jnp.where(cmask, jnp.exp(x - jnp.max(x, axis=0, keepdims=True)), 0.0) for x in sc]
    pc = [e / jnp.maximum(jnp.sum(e, axis=0, keepdims=True), 1e-30) for e in ec]
    oc = [_dot_tn(vc, p) for p in pc]
    for h in heads:
        ocmp[h * D:(h + 1) * D, :] = oc[h]
    psum = sum(pc[1:], pc[0])

    jrow = lax.broadcasted_iota(jnp.int32, (n_slc, n_cmp), 0) * SLC_BLOCK
    ncol = lax.broadcasted_iota(jnp.int32, (n_slc, n_cmp), 1) * CMP_STRIDE
    overlap_t = jnp.where((ncol < jrow + SLC_BLOCK) & (ncol + CMP_BLOCK > jrow), 1.0, 0.0)
    imp_t = _dot_mask_lhs(overlap_t, psum)
    jj = lax.broadcasted_iota(jnp.int32, (n_slc, tq), 0)
    cur = jnp.right_shift(q0 + lax.broadcasted_iota(jnp.int32, (n_slc, tq), 1), SLC_SHIFT)
    forced = (jj == 0) | (jj == cur) | (jj == cur - 1)
    score = jnp.where(forced, FORCE, jnp.where(jj <= cur, imp_t, NEG))
    group = 8
    cnt = [jnp.zeros((group, tq), F32) for _ in range(n_slc // group)]
    for j2 in range(n_slc):
        row = score[j2:j2 + 1, :]
        for gi in range(n_slc // group):
            sg = score[gi * group:(gi + 1) * group]
            if gi * group > j2:
                hit = jnp.where(row >= sg, 1.0, 0.0)
            elif (gi + 1) * group - 1 <= j2:
                hit = jnp.where(row > sg, 1.0, 0.0)
            else:
                hit = jnp.where(lax.broadcasted_iota(jnp.int32, (group, tq), 0) + gi * group > j2,
                                jnp.where(row >= sg, 1.0, 0.0), jnp.where(row > sg, 1.0, 0.0))
            cnt[gi] = cnt[gi] + hit
    cnt = jnp.concatenate(cnt, axis=0)
    selm = jnp.where(cnt < min(N_SELECT, n_slc), 0.0, -1.0).astype(BF16)

    qs = (qt * (scale * LOG2E)).astype(BF16)
    rhs_w = [qs[h * D:(h + 1) * D] for h in heads]
    rhs_s = [jnp.concatenate([rhs_w[h], selm], axis=0) for h in heads]
    m_s[...] = jnp.full(m_s.shape, NEG, F32)
    acc_s[...] = jnp.zeros_like(acc_s)

    def scores_into(buf, j):
        k = kse_ref[pl.ds(pl.multiple_of(j * TK, TK), TK), :]
        for h in heads:
            buf[h] = jnp.dot(k, rhs_s[h], preferred_element_type=F32)

    def consume(buf, j, mask):
        vt = vse_ref[:, pl.ds(pl.multiple_of(j * TK, TK), TK)]
        for h in heads:
            s = buf[h]
            if mask is not None:
                s = jnp.where(mask, s, NEG)
            m_old = m_s[h:h + 1, :]
            m_new = jnp.maximum(m_old, jnp.max(s, axis=0, keepdims=True))
            p = jnp.exp2(s - m_new).astype(BF16)
            acc_s[h] = jnp.exp2(m_old - m_new) * acc_s[h] + jnp.dot(vt, p, preferred_element_type=F32)
            m_s[h:h + 1, :] = m_new

    kpos = lax.broadcasted_iota(jnp.int32, (TK, tq), 0)
    qpos = lax.broadcasted_iota(jnp.int32, (TK, tq), 1)
    last = jnp.maximum(qi - 1, 0)
    scores_into(s_even, qi)
    scores_into(s_odd, 0)
    consume(s_even, qi, kpos <= qpos)

    def body(jp, carry):
        j = 2 * jp
        scores_into(s_even, jnp.minimum(j + 1, last))
        consume(s_odd, j, None)
        scores_into(s_odd, jnp.minimum(j + 2, last))
        consume(s_even, j + 1, None)
        return carry

    lax.fori_loop(0, qi // 2, body, 0)

    @pl.when(qi % 2 == 1)
    def _():
        consume(s_odd, qi - 1, None)

    span = WINDOW + tq
    w0 = pl.multiple_of(jnp.maximum(q0 - WINDOW, 0), TK)
    kw = kw_ref[pl.ds(w0, span), :]
    vwt = vwe_ref[:, pl.ds(w0, span)]
    kabs = w0 + lax.broadcasted_iota(jnp.int32, (span, tq), 0)
    qabs = q0 + lax.broadcasted_iota(jnp.int32, (span, tq), 1)
    wmask = (kabs <= qabs) & (kabs > qabs - WINDOW)
    sw = [jnp.where(wmask, jnp.dot(kw, rhs_w[h], preferred_element_type=F32), NEG) for h in heads]
    pw = [jnp.exp2(x - jnp.max(x, axis=0, keepdims=True)).astype(BF16) for x in sw]
    acc_w = [jnp.dot(vwt, p, preferred_element_type=F32) for p in pw]

    gate = jax.nn.sigmoid((gts_ref[...] + gb_ref[...]).T)
    outs = []
    for h in heads:
        a_s = acc_s[h]
        a_w = acc_w[h]
        outs.append(gate[h:h + 1] * ocmp[h * D:(h + 1) * D, :]
                    + gate[H_ATT + h:H_ATT + h + 1] * (a_s[0:D] / a_s[D:D + 1])
                    + gate[2 * H_ATT + h:2 * H_ATT + h + 1] * (a_w[0:D] / a_w[D:D + 1]))
    ot = jnp.concatenate(outs, axis=0)
    ot = ot * lax.rsqrt(jnp.mean(ot * ot, axis=0, keepdims=True) + NORM_EPS)
    o_ref[...] = ot.T * g_ref[...]


def _nsa(p_nsa, kv_cmp, kse, vse, kw, vwe, gate_b, out_g, bsz, seq):
    t = p_nsa.shape[0]
    nq = seq // TQ
    n_slc = seq // SLC_BLOCK
    g = kv_cmp.shape[1]
    row = lambda b, i: (b * nq + i, 0)
    per_b = lambda b, i: (b, 0, 0)
    fix = lambda b, i: (0, 0)
    return pl.pallas_call(
        functools.partial(_nsa_body, n_slc=n_slc),
        out_shape=jax.ShapeDtypeStruct((t, C_ATT), F32),
        grid=(bsz, nq),
        in_specs=[
            pl.BlockSpec((TQ, C_ATT), row),
            pl.BlockSpec((TQ, 128), lambda b, i: (b * nq + i, NSA_W // 128 - 1)),
            pl.BlockSpec((None, g, 128), per_b),
            pl.BlockSpec((None, seq, HEAD_DIM + n_slc), per_b),
            pl.BlockSpec((None, 128, seq), per_b),
            pl.BlockSpec((None, seq, HEAD_DIM), per_b),
            pl.BlockSpec((None, 128, seq), per_b),
            pl.BlockSpec((1, 128), fix),
            pl.BlockSpec((1, C_ATT), fix),
        ],
        out_specs=pl.BlockSpec((TQ, C_ATT), row),
        scratch_shapes=[pltpu.VMEM((8, TQ), F32), pltpu.VMEM((H_ATT, 128, TQ), F32),
                        pltpu.VMEM((H_ATT, TK, TQ), F32), pltpu.VMEM((H_ATT, TK, TQ), F32),
                        pltpu.VMEM((C_ATT, TQ), F32)],
        compiler_params=_params("parallel", "arbitrary"),
        name="nsa",
    )(p_nsa, p_nsa, kv_cmp, kse, vse, kw, vwe, gate_b, out_g)


def _tri(n, strict):
    r = lax.broadcasted_iota(jnp.int32, (n, n), 0)
    c = lax.broadcasted_iota(jnp.int32, (n, n), 1)
    return (c < r) if strict else (c <= r)


def _block_diag(x):
    xb = x.astype(BF16)
    first = lax.broadcasted_iota(jnp.int32, xb.shape, 1) < xb.shape[1] // 2
    zero = jnp.zeros_like(xb)
    return jnp.concatenate([jnp.where(first, xb, zero), jnp.where(first, zero, xb)], axis=0)


def _unit_lower_inverse_pairs(ms, eye2):
    n = eye2.shape[0]
    idx = range(len(ms))
    r = lax.broadcasted_iota(jnp.int32, (n, 2 * n), 0)
    c = jnp.bitwise_and(lax.broadcasted_iota(jnp.int32, (n, 2 * n), 1), n - 1)
    base = 8
    diag = jnp.right_shift(r, 3) == jnp.right_shift(c, 3)
    m8 = [jnp.where(diag, m, 0.0) for m in ms]
    t = [eye2 + m for m in m8]
    p2 = [_dot(m, _block_diag(m)) for m in m8]
    t = [t[i] + _dot(p2[i], _block_diag(t[i])) for i in idx]
    p4 = [_dot(p, _block_diag(p)) for p in p2]
    t = [t[i] + _dot(p4[i], _block_diag(t[i])) for i in idx]
    b = base
    while b < n:
        sh = b.bit_length() - 1
        pair = jnp.right_shift(r, sh + 1) == jnp.right_shift(c, sh + 1)
        lower_left = pair & (jnp.right_shift(r, sh) != jnp.right_shift(c, sh))
        left = [_dot(t[i], _block_diag(jnp.where(lower_left, ms[i], 0.0))) for i in idx]
        t = [t[i] + _dot(left[i], _block_diag(t[i])) for i in idx]
        b *= 2
    return t


def _rwkv_body(p_ref, mu_ref, w0_ref, w2_ref, a0_ref, a2_ref, g2_ref, kk_ref, ka_ref, rk_ref, lnw_ref,
               lnb_ref, o_ref, state, prev):
    L = CHUNK
    C = C_RWKV
    tb = p_ref.shape[0]
    chunks = range(tb // L)

    @pl.when(pl.program_id(1) == 0)
    def _():
        state[...] = jnp.zeros_like(state)
        prev[...] = jnp.zeros_like(prev)

    x = p_ref[...]
    rows = lax.broadcasted_iota(jnp.int32, x.shape, 0)
    shifted = jnp.where(rows == 0, prev[0:1, :], pltpu.roll(x, 1, 0))
    prev[0:1, :] = x[tb - 1:tb, :]
    x = x + mu_ref[...] * (shifted - x)
    r = x[:, 0:C]
    k = x[:, C:2 * C]
    v = x[:, 2 * C:3 * C]
    xw = x[:, 3 * C:3 * C + RANK_W]
    xa = x[:, 3 * C + RANK_W:3 * C + RANK_W + RANK_A]
    xg = x[:, 3 * C + RANK_W + RANK_A:]
    w = -jax.nn.softplus(-(w0_ref[...] + _dot3(jnp.tanh(xw), w2_ref[...]))) - 0.5
    logw = -jnp.exp(w)
    a = jax.nn.sigmoid(a0_ref[...] + _dot3(xa, a2_ref[...]))
    g = _dot(jax.nn.sigmoid(xg), g2_ref[...])
    kkf = k * kk_ref[...]
    kmod = k * (1.0 + (a - 1.0) * ka_ref[...])
    rk = r * kmod * rk_ref[...]

    tri_incl = jnp.where(_tri(L, strict=False), 1.0, 0.0)
    cum = jnp.concatenate([_dot_mask_lhs(tri_incl, logw[c * L:(c + 1) * L]) for c in chunks], axis=0)
    cum_last = [cum[(c + 1) * L - 1:(c + 1) * L, :] for c in chunks]
    cum_end = jnp.concatenate([jnp.broadcast_to(cl, (L, C)) for cl in cum_last], axis=0)
    e_pos = jnp.exp(cum)
    e_prev = jnp.exp(cum - logw)
    e_neg = jnp.exp(-cum)
    e_rem = jnp.exp(cum_end - cum)
    N = HEAD_DIM
    PW = 2 * N
    pairs = range(H_RWKV // 2)
    lane_l = lax.broadcasted_iota(jnp.int32, (L, PW), 1)
    row_l = lax.broadcasted_iota(jnp.int32, (L, PW), 0)
    col_l = jnp.bitwise_and(lane_l, N - 1)
    first = lane_l < N
    first2 = lax.broadcasted_iota(jnp.int32, (2 * L, PW), 1) < N
    strict2 = col_l < row_l
    incl2 = col_l <= row_l
    eye2 = jnp.where(col_l == row_l, 1.0, 0.0)

    def halves(x_):
        lane = lax.broadcasted_iota(jnp.int32, x_.shape, 1)
        sa = jnp.sum(jnp.where(lane < N, x_, 0.0), axis=-1, keepdims=True)
        sb = jnp.sum(jnp.where(lane < N, 0.0, x_), axis=-1, keepdims=True)
        return jnp.where(lane < N, sa, sb)

    ps = [slice(p * PW, (p + 1) * PW) for p in pairs]
    kk_p = [kkf[:, sl] for sl in ps]
    kk_p = [x_ / jnp.maximum(jnp.sqrt(halves(x_ * x_)), 1e-12) for x_ in kk_p]
    alpha_f = [-kk_p[p] * e_prev[:, ps[p]] for p in pairs]
    r_f = [r[:, ps[p]] * e_pos[:, ps[p]] for p in pairs]
    beta = [kk_p[p] * a[:, ps[p]] for p in pairs]
    beta_f = [beta[p] * e_neg[:, ps[p]] for p in pairs]
    k_f = [kmod[:, ps[p]] * e_neg[:, ps[p]] for p in pairs]
    beta_e = [beta[p] * e_rem[:, ps[p]] for p in pairs]
    k_e = [kmod[:, ps[p]] * e_rem[:, ps[p]] for p in pairs]
    items = [(c, p) for c in chunks for p in pairs]
    n_items = range(len(items))
    rs = [slice(c * L, (c + 1) * L) for c, _ in items]
    pi = [p for _, p in items]
    lhs = [jnp.concatenate([alpha_f[pi[i]][rs[i]], r_f[pi[i]][rs[i]]], axis=0).astype(BF16) for i in n_items]
    zero16 = jnp.zeros((2 * L, PW), BF16)
    lhs4 = [jnp.concatenate([jnp.where(first2, x_, zero16), jnp.where(first2, zero16, x_)], axis=0) for x_ in lhs]
    bk = [(beta_f[pi[i]][rs[i]].astype(BF16), k_f[pi[i]][rs[i]].astype(BF16)) for i in n_items]
    v_i = [v[rs[i], ps[pi[i]]] for i in n_items]
    out1 = [_dot_nt(lhs4[i], jnp.concatenate([bk[i][0], bk[i][1]], axis=0)) for i in n_items]
    out2 = [_dot_nt(lhs4[i], jnp.concatenate([bk[i][1], bk[i][0]], axis=0)) for i in n_items]
    m_ab = [jnp.where(strict2, jnp.where(first, out1[i][0:L], out2[i][2 * L:3 * L]), 0.0) for i in n_items]
    m_ak = [jnp.where(strict2, jnp.where(first, out2[i][0:L], out1[i][2 * L:3 * L]), 0.0) for i in n_items]
    m_rb = [jnp.where(incl2, jnp.where(first, out1[i][L:2 * L], out2[i][3 * L:4 * L]), 0.0) for i in n_items]
    m_rk = [jnp.where(incl2, jnp.where(first, out2[i][L:2 * L], out1[i][3 * L:4 * L]), 0.0) for i in n_items]
    t_inv = _unit_lower_inverse_pairs(m_ab, eye2)
    mv = [_dot(jnp.concatenate([m_ak[i], m_rk[i]], axis=0), _block_diag(v_i[i])) for i in n_items]
    z = [state[p] for p in pairs]
    y_parts = [[] for _ in pairs]
    for c in chunks:
        idx = [c * len(pairs) + p for p in pairs]
        az = [_dot(lhs[i], _block_diag(z[p])) for p, i in enumerate(idx)]
        u = [_dot(t_inv[i], _block_diag(az[p][0:L] + mv[i][0:L])) for p, i in enumerate(idx)]
        for p, i in enumerate(idx):
            y_parts[p].append(az[p][L:2 * L] + _dot(m_rb[i], _block_diag(u[p])) + mv[i][L:2 * L])
        cross = [_dot_tn(jnp.concatenate([beta_e[p][rs[i]], k_e[p][rs[i]]], axis=0),
                         jnp.concatenate([u[p], v_i[i]], axis=0)) for p, i in enumerate(idx)]
        for p in pairs:
            gam = eye2 * jnp.exp(cum_last[c][:, ps[p]])
            z[p] = z[p] * halves(gam) + jnp.where(first, cross[p][0:N], cross[p][N:2 * N])
    for p in pairs:
        state[p] = z[p]
    yn = []
    for p in pairs:
        y_p = jnp.concatenate(y_parts[p], axis=0)
        mu_y = halves(y_p) * (1.0 / N)
        dev = y_p - mu_y
        var = halves(dev * dev) * (1.0 / N)
        yn.append(dev * lax.rsqrt(var + RWKV_GN_EPS))
    bonus = [halves(rk[:, ps[p]]) * v[:, ps[p]] for p in pairs]
    yn = jnp.concatenate(yn, axis=-1)
    bonus = jnp.concatenate(bonus, axis=-1)
    o_ref[...] = (yn * lnw_ref[...] + lnb_ref[...] + bonus) * g


def _rwkv(p_rw, mu, w0, w2, a0, a2, g2, k_k, k_a, r_k, ln_w, ln_b, bsz, seq):
    t = p_rw.shape[0]
    tb = RWKV_BLOCK
    nb = seq // tb
    row = lambda b, c: (b * nb + c, 0)
    fix = lambda b, c: (0, 0)
    vec = lambda a: a.reshape(1, -1)
    args = (vec(mu), vec(w0), w2, vec(a0), a2, g2, vec(k_k), vec(k_a), vec(r_k), vec(ln_w), vec(ln_b))
    return pl.pallas_call(
        _rwkv_body,
        out_shape=jax.ShapeDtypeStruct((t, C_RWKV), F32),
        grid=(bsz, nb),
        in_specs=[pl.BlockSpec((tb, RWKV_W), row)] + [pl.BlockSpec(a.shape, fix) for a in args],
        out_specs=pl.BlockSpec((tb, C_RWKV), row),
        scratch_shapes=[pltpu.VMEM((H_RWKV // 2, HEAD_DIM, 2 * HEAD_DIM), F32), pltpu.VMEM((8, RWKV_W), F32)],
        compiler_params=_params("parallel", "arbitrary"),
        name="rwkv",
    )(p_rw, *args)


def _mlstm_body(p_ref, cw_ref, cb_ref, gb_ref, ng_ref, o_ref, cstate, mstate, prev):
    L = CHUNK
    C = C_MLSTM
    D = HEAD_DIM
    tb = p_ref.shape[0]
    chunks = range(tb // L)

    @pl.when(pl.program_id(1) == 0)
    def _():
        cstate[...] = jnp.zeros_like(cstate)
        mstate[...] = jnp.zeros_like(mstate)
        prev[...] = jnp.zeros_like(prev)

    qk_in = p_ref[:, 0:2 * C]
    pv = prev[...]
    rows8 = lax.broadcasted_iota(jnp.int32, pv.shape, 0)
    conv = qk_in * cw_ref[CONV_WIDTH - 1:CONV_WIDTH, :] + cb_ref[...]
    for d in range(1, CONV_WIDTH):
        rolled = pltpu.roll(qk_in, d, 0)
        top = jnp.where(rows8 < d, pltpu.roll(pv, d, 0), rolled[0:8])
        sh = jnp.concatenate([top, rolled[8:]], axis=0)
        conv = conv + sh * cw_ref[CONV_WIDTH - 1 - d:CONV_WIDTH - d, :]
    prev[...] = qk_in[tb - 8:tb, :]
    qk = conv * jax.nn.sigmoid(conv)
    q = qk[:, 0:C]
    k = qk[:, C:2 * C] * (D ** -0.5)
    v = p_ref[:, 2 * C:3 * C]
    og = p_ref[:, 3 * C:4 * C]
    gates = p_ref[:, 4 * C:4 * C + 128] + gb_ref[...]
    lane = lax.broadcasted_iota(jnp.int32, gates.shape, 1)
    is_f = (lane >= H_MLSTM) & (lane < 2 * H_MLSTM)
    gl = jnp.where(is_f, jax.nn.log_sigmoid(gates), gates)
    tri_incl = jnp.where(_tri(L, strict=False), 1.0, 0.0)
    bcum = [_dot_mask_lhs(tri_incl, gl[c * L:(c + 1) * L]) for c in chunks]
    gl_t = [gl[c * L:(c + 1) * L].T for c in chunks]
    bcum_t = [_dot_nt_mask_rhs(g_, tri_incl) for g_ in gl_t]
    m_all = mstate[...]
    src = lax.broadcasted_iota(jnp.int32, (L, L), 0)
    qry = lax.broadcasted_iota(jnp.int32, (L, L), 1)
    causal_t = src <= qry
    ones_row = jnp.where(lax.broadcasted_iota(jnp.int32, (D, L), 0) == 0, 1.0, 0.0)

    heads = range(H_MLSTM)
    items = [(c, h) for c in chunks for h in heads]
    n_items = range(len(items))
    q_t = [q[c * L:(c + 1) * L].T for c in chunks]
    v_t = [v[c * L:(c + 1) * L].T for c in chunks]
    qt_i = [q_t[c][h * D:(h + 1) * D] for c, h in items]
    vt_ext = [jnp.concatenate([v_t[c][h * D:(h + 1) * D], ones_row], axis=0) for c, h in items]
    k_i = [k[c * L:(c + 1) * L, h * D:(h + 1) * D] for c, h in items]
    c_col = [gl[c * L:(c + 1) * L, h:h + 1] - bcum[c][:, H_MLSTM + h:H_MLSTM + h + 1] for c, h in items]
    b_row = [bcum_t[c][H_MLSTM + h:H_MLSTM + h + 1, :] for c, h in items]
    c_row = [gl_t[c][h:h + 1, :] - b_row[i] for i, (c, h) in enumerate(items)]
    g_tot = [bcum[c][L - 1:L, H_MLSTM + h:H_MLSTM + h + 1] for c, h in items]
    u_row = [g_tot[i] + c_row[i] for i in n_items]
    u_max = [jnp.max(x_, axis=1, keepdims=True) for x_ in u_row]
    m0 = []
    m_run = [m_all[0:1, h:h + 1] for h in heads]
    for c in chunks:
        for h in heads:
            i = c * H_MLSTM + h
            m0.append(m_run[h])
            m_run[h] = jnp.maximum(g_tot[i] + m_run[h], u_max[i])
    m_next = [m0[i + H_MLSTM] if i + H_MLSTM < len(items) else m_run[items[i][1]] for i in n_items]
    kq = [_dot(k_i[i], qt_i[i]) for i in n_items]
    kv = [_dot(vt_ext[i] * jnp.exp(u_row[i] - m_next[i]), k_i[i]) for i in n_items]
    cm = []
    cm_run = [cstate[h] for h in heads]
    for c in chunks:
        for h in heads:
            i = c * H_MLSTM + h
            cm.append(cm_run[h])
            cm_run[h] = jnp.exp(g_tot[i] + m0[i] - m_next[i]) * cm_run[h] + kv[i]
    inter = [_dot(cm[i], qt_i[i]) for i in n_items]
    c_b = [jnp.where(causal_t, jnp.broadcast_to(c_col[i], (L, L)), -jnp.inf) for i in n_items]
    m_row = [jnp.maximum(m0[i], jnp.max(c_b[i], axis=0, keepdims=True)) for i in n_items]
    s_t = [jnp.exp(c_b[i] - m_row[i]) * kq[i] for i in n_items]
    sv = [_dot(vt_ext[i], s_t[i]) for i in n_items]
    lane8 = lax.broadcasted_iota(jnp.int32, m_all.shape, 1)
    for h in heads:
        cstate[h] = cm_run[h]
        m_all = jnp.where(lane8 == h, m_run[h], m_all)
    mstate[...] = m_all
    outs = []
    for i in n_items:
        num = jnp.exp(m0[i] - m_row[i]) * inter[i] + sv[i]
        hh = num[0:D] / jnp.maximum(jnp.abs(num[D:D + 1]), jnp.exp(-(b_row[i] + m_row[i])))
        outs.append(hh * lax.rsqrt(jnp.mean(hh * hh, axis=0, keepdims=True) + NORM_EPS))
    hcat = jnp.concatenate([jnp.concatenate(outs[c * H_MLSTM:(c + 1) * H_MLSTM], axis=0).T for c in chunks], axis=0)
    o_ref[...] = jax.nn.sigmoid(og) * (hcat * ng_ref[...])


def _mlstm(p_ml, conv_w, conv_b, gate_b, norm_g, bsz, seq):
    t = p_ml.shape[0]
    tb = ML_BLOCK
    nb = seq // tb
    row = lambda b, c: (b * nb + c, 0)
    fix = lambda b, c: (0, 0)
    args = (conv_w, conv_b.reshape(1, -1), gate_b, norm_g.reshape(1, -1))
    return pl.pallas_call(
        _mlstm_body,
        out_shape=jax.ShapeDtypeStruct((t, C_MLSTM), F32),
        grid=(bsz, nb),
        in_specs=[pl.BlockSpec((tb, ML_W), row)] + [pl.BlockSpec(a.shape, fix) for a in args],
        out_specs=pl.BlockSpec((tb, C_MLSTM), row),
        scratch_shapes=[pltpu.VMEM((H_MLSTM, 128, HEAD_DIM), F32), pltpu.VMEM((8, 128), F32),
                        pltpu.VMEM((8, 2 * C_MLSTM), F32)],
        compiler_params=_params("parallel", "arbitrary"),
        name="mlstm",
    )(p_ml, *args)


def _out_proj_body(a_ref, b_ref, c_ref, wa_ref, wb_ref, wc_ref, x_ref, gt_ref, g_ref, o_ref, *, sub):
    parts = [slice(s * sub, (s + 1) * sub) for s in range(x_ref.shape[0] // sub)]
    y = [jnp.dot(a_ref[p, :].astype(BF16), wa_ref[...], preferred_element_type=F32)
         + jnp.dot(b_ref[p, :].astype(BF16), wb_ref[...], preferred_element_type=F32)
         + jnp.dot(c_ref[p, :].astype(BF16), wc_ref[...], preferred_element_type=F32) for p in parts]
    for p, y_ in zip(parts, y):
        o_ref[p, :] = x_ref[p, :] + gt_ref[...] * _rms(y_, g_ref[...])


def _out_proj(o_nsa, o_rw, o_ml, wa, wb, wc, x2, gt, g, seq):
    t, d = x2.shape
    tm = 1024
    per_b = seq // tm
    row = lambda i: (i, 0)
    fix = lambda i: (0, 0)
    return pl.pallas_call(
        functools.partial(_out_proj_body, sub=tm // 2),
        out_shape=jax.ShapeDtypeStruct((t, d), F32),
        grid=(t // tm,),
        in_specs=[
            pl.BlockSpec((tm, C_ATT), row), pl.BlockSpec((tm, C_RWKV), row), pl.BlockSpec((tm, C_MLSTM), row),
            pl.BlockSpec(wa.shape, fix), pl.BlockSpec(wb.shape, fix), pl.BlockSpec(wc.shape, fix),
            pl.BlockSpec((tm, d), row),
            pl.BlockSpec((None, 1, d), lambda i: (i // per_b, 0, 0)),
            pl.BlockSpec((1, d), fix),
        ],
        out_specs=pl.BlockSpec((tm, d), row),
        compiler_params=_params("parallel"),
        name="out_proj",
    )(o_nsa, o_rw, o_ml, wa, wb, wc, x2, gt, g)


FFN_SLAB = 1024


def _ffn_body(x_ref, sc_ref, sh_ref, gt_ref, gpre_ref, gpost_ref, wg_ref, wu_ref, wd_ref, o_ref, *, sub):
    tm = x_ref.shape[0]
    dff = wg_ref.shape[1]
    parts = [slice(s * sub, (s + 1) * sub) for s in range(tm // sub)]
    slabs = [slice(a, min(a + FFN_SLAB, dff)) for a in range(0, dff, FFN_SLAB)]
    hb = [(_rms(x_ref[p, :], gpre_ref[...]) * (1.0 + sc_ref[...]) + sh_ref[...]).astype(BF16) for p in parts]
    y = [None for _ in parts]
    for sl in slabs:
        gate = [jnp.dot(h, wg_ref[:, sl], preferred_element_type=F32) for h in hb]
        up = [jnp.dot(h, wu_ref[:, sl], preferred_element_type=F32) for h in hb]
        act = [(g_ * jax.nn.sigmoid(g_) * u_).astype(BF16) for g_, u_ in zip(gate, up)]
        down = [jnp.dot(a_, wd_ref[sl, :], preferred_element_type=F32) for a_ in act]
        y = [d_ if y_ is None else y_ + d_ for y_, d_ in zip(y, down)]
    for p, y_ in zip(parts, y):
        o_ref[p, :] = x_ref[p, :] + gt_ref[...] * _rms(y_, gpost_ref[...])


def _ffn(x2, sc, sh, gt, g_pre, g_post, wg, wu, wd, seq):
    t, d = x2.shape
    tm = 1024
    per_b = seq // tm
    row = lambda i: (i, 0)
    bat = lambda i: (i // per_b, 0, 0)
    fix = lambda i: (0, 0)
    resident = lambda a: pl.BlockSpec(a.shape, fix, pipeline_mode=pl.Buffered(1))
    return pl.pallas_call(
        functools.partial(_ffn_body, sub=tm // 2),
        out_shape=jax.ShapeDtypeStruct((t, d), F32),
        grid=(t // tm,),
        in_specs=[
            pl.BlockSpec((tm, d), row),
            pl.BlockSpec((None, 1, d), bat), pl.BlockSpec((None, 1, d), bat), pl.BlockSpec((None, 1, d), bat),
            pl.BlockSpec((1, d), fix), pl.BlockSpec((1, d), fix),
            resident(wg), resident(wu), resident(wd),
        ],
        out_specs=pl.BlockSpec((tm, d), row),
        compiler_params=_params("parallel"),
        name="ffn",
    )(x2, sc, sh, gt, g_pre, g_post, wg, wu, wd)


def _pad_cols(a, width):
    return jnp.pad(a, ((0, 0), (0, width - a.shape[1])))


def _layout_w_in(w_in):
    d_in_rw = NSA_IN + RWKV_W
    nsa = _pad_cols(w_in[:, :NSA_IN], NSA_W)
    rw = w_in[:, NSA_IN:d_in_rw]
    ml = _pad_cols(w_in[:, d_in_rw:], ML_W)
    return jnp.concatenate([nsa, rw, ml], axis=1).astype(BF16)


def _layout_cmp(ck_w1, cv_w1, ck_w2, cv_w2, pe_k, pe_v):
    half = CMP_BLOCK // 2
    ck = ck_w1.reshape(2, half, HEAD_DIM, HEAD_DIM)
    cv = cv_w1.reshape(2, half, HEAD_DIM, HEAD_DIM)
    z = jnp.zeros_like(ck[0])
    top = jnp.concatenate([ck[0], z, ck[1], z], axis=-1)
    bot = jnp.concatenate([z, cv[0], z, cv[1]], axis=-1)
    wc = jnp.concatenate([top, bot], axis=1).reshape(half * 2 * HEAD_DIM, 4 * HEAD_DIM)
    pe = jnp.concatenate([pe_k, pe_v], axis=-1).reshape(2, half * 2 * HEAD_DIM)
    pe8 = jnp.pad(pe, ((0, 6), (0, 0)))
    z2 = jnp.zeros_like(ck_w2)
    w2 = jnp.concatenate([jnp.concatenate([ck_w2, z2], axis=1), jnp.concatenate([z2, cv_w2], axis=1)], axis=0)
    return wc, pe8, w2


def kernel(x, c, w_mod, b_mod, g_pre_mix, g_post_mix, g_pre_ffn, g_post_ffn, w_in, w_out, nsa_pe_k, nsa_pe_v, nsa_ck_w1, nsa_ck_w2, nsa_cv_w1, nsa_cv_w2, nsa_gate_b, nsa_out_g, rw_mu, rw_w0, rw_w2, rw_a0, rw_a2, rw_g2, rw_kk, rw_ka, rw_rk, rw_ln_w, rw_ln_b, ml_conv_w, ml_conv_b, ml_ig_b, ml_fg_b, ml_norm_g, ffn_w_gate, ffn_w_up, ffn_w_down):
    bsz, seq, d = x.shape
    depth = w_mod.shape[0]
    t = bsz * seq
    half = CMP_BLOCK // 2
    mod = _mod(c, w_mod, b_mod)
    x2 = x.reshape(t, d)
    for l in range(depth):
        sh1, sc1, gt1, sh2, sc2, gt2 = [m.reshape(bsz, 1, d) for m in jnp.split(mod[l], 6, axis=-1)]
        p_nsa, p_rw, p_ml, kse, kw, vse, vwe = _in_proj(x2, sc1, sh1, g_pre_mix[l].reshape(1, d),
                                                        _layout_w_in(w_in[l]), bsz, seq)
        wc, pe8, w2 = _layout_cmp(nsa_ck_w1[l], nsa_cv_w1[l], nsa_ck_w2[l], nsa_cv_w2[l], nsa_pe_k[l], nsa_pe_v[l])
        xg = p_nsa[:, C_ATT:C_ATT + 2 * HEAD_DIM].reshape(bsz, seq // half, half * 2 * HEAD_DIM)
        kv_cmp = _nsa_cmp(xg, wc, pe8, w2)
        gate_b = jnp.pad(nsa_gate_b[l], (0, 128 - 3 * H_ATT)).reshape(1, 128)
        o_nsa = _nsa(p_nsa, kv_cmp, kse.reshape(bsz, seq, -1), vse, kw.reshape(bsz, seq, -1), vwe, gate_b,
                     nsa_out_g[l].reshape(1, C_ATT), bsz, seq)
        o_rw = _rwkv(p_rw, rw_mu[l], rw_w0[l], rw_w2[l], rw_a0[l], rw_a2[l], rw_g2[l], rw_kk[l], rw_ka[l],
                     rw_rk[l], rw_ln_w[l], rw_ln_b[l], bsz, seq)
        ml_gate_b = jnp.pad(jnp.concatenate([ml_ig_b[l], ml_fg_b[l]]), (0, 128 - 2 * H_MLSTM)).reshape(1, 128)
        o_ml = _mlstm(p_ml, ml_conv_w[l], ml_conv_b[l], ml_gate_b, ml_norm_g[l], bsz, seq)
        wo = w_out[l].astype(BF16)
        x2 = _out_proj(o_nsa, o_rw, o_ml, wo[:C_ATT], wo[C_ATT:C_ATT + C_RWKV], wo[C_ATT + C_RWKV:], x2, gt1,
                       g_post_mix[l].reshape(1, d), seq)
        x2 = _ffn(x2, sc2, sh2, gt2, g_pre_ffn[l].reshape(1, d), g_post_ffn[l].reshape(1, d),
                  ffn_w_gate[l].astype(BF16), ffn_w_up[l].astype(BF16), ffn_w_down[l].astype(BF16), seq)
    return x2.reshape(bsz, seq, d)
```

```python
import functools

import jax
import jax.numpy as jnp
from jax import lax
from jax.experimental import pallas as pl
from jax.experimental.pallas import tpu as pltpu

F32 = jnp.float32
BF16 = jnp.bfloat16
HIGHEST = lax.Precision.HIGHEST

HEAD_DIM = 64
H_ATT = 4
C_ATT = H_ATT * HEAD_DIM
H_RWKV = 6
C_RWKV = H_RWKV * HEAD_DIM
H_MLSTM = 6
C_MLSTM = H_MLSTM * HEAD_DIM
CMP_BLOCK = 32
CMP_STRIDE = 16
SLC_BLOCK = 64
SLC_SHIFT = 6
N_SELECT = 16
WINDOW = 512
NEG = -1e30
FORCE = 1e9
MASK_BIG = 2.0 ** 100
LOG2E = 1.4426950408889634
RANK_W = 64
RANK_A = 64
RANK_G = 128
RWKV_GN_EPS = 64e-5
CHUNK = 64
RWKV_BLOCK = 512
ML_BLOCK = 512
CONV_WIDTH = 4
NORM_EPS = 1e-6

NSA_W = 768
RWKV_W = 3 * C_RWKV + RANK_W + RANK_A + RANK_G
ML_W = 4 * C_MLSTM + 128
NSA_IN = C_ATT + 6 * HEAD_DIM + 3 * H_ATT
ML_IN = 4 * C_MLSTM + 2 * H_MLSTM

VMEM_LIMIT = 56 * 1024 * 1024

TQ = 256
TK = 256


def _params(*sem):
    return pltpu.CompilerParams(dimension_semantics=sem, vmem_limit_bytes=VMEM_LIMIT)


def _dot(a, b, precise=False):
    if precise:
        return jnp.dot(a.astype(F32), b.astype(F32), preferred_element_type=F32, precision=HIGHEST)
    return jnp.dot(a.astype(BF16), b.astype(BF16), preferred_element_type=F32)


def _dot_nt(a, b, precise=False):
    dn = (((1,), (1,)), ((), ()))
    if precise:
        return lax.dot_general(a.astype(F32), b.astype(F32), dn, preferred_element_type=F32, precision=HIGHEST)
    return lax.dot_general(a.astype(BF16), b.astype(BF16), dn, preferred_element_type=F32)


def _dot_tn(a, b, precise=False):
    dn = (((0,), (0,)), ((), ()))
    if precise:
        return lax.dot_general(a.astype(F32), b.astype(F32), dn, preferred_element_type=F32, precision=HIGHEST)
    return lax.dot_general(a.astype(BF16), b.astype(BF16), dn, preferred_element_type=F32)


def _split3(x):
    hi = x.astype(BF16)
    rest = x - hi.astype(F32)
    mid = rest.astype(BF16)
    lo = (rest - mid.astype(F32)).astype(BF16)
    return hi, mid, lo


def _dot_mask_lhs(mask01, x):
    mb = mask01.astype(BF16)
    hi, mid, lo = _split3(x)
    dot = lambda p: jnp.dot(mb, p, preferred_element_type=F32)
    return (dot(lo) + dot(mid)) + dot(hi)


def _dot_nt_mask_rhs(x, mask01):
    mb = mask01.astype(BF16)
    hi, mid, lo = _split3(x)
    dot = lambda p: lax.dot_general(p, mb, (((1,), (1,)), ((), ())), preferred_element_type=F32)
    return (dot(lo) + dot(mid)) + dot(hi)


def _dot3(a, b):
    a_hi, a_lo, _ = _split3(a)
    b_hi, b_lo, _ = _split3(b)
    dot = lambda p, q: jnp.dot(p, q, preferred_element_type=F32)
    return (dot(a_lo, b_hi) + dot(a_hi, b_lo)) + dot(a_hi, b_hi)


def _rms(x, g):
    return x * lax.rsqrt(jnp.mean(x * x, axis=-1, keepdims=True) + NORM_EPS) * g


def _mod_body(c_ref, w_ref, b_ref, o_ref):
    c = c_ref[...]
    cs = c * jax.nn.sigmoid(c)
    o_ref[...] = _dot(cs, w_ref[...], precise=True) + b_ref[...]


def _mod(c, w_mod, b_mod):
    depth, d, n = w_mod.shape
    bsz = c.shape[0]
    tn = 1536
    return pl.pallas_call(
        _mod_body,
        out_shape=jax.ShapeDtypeStruct((depth, bsz, n), F32),
        grid=(depth, n // tn),
        in_specs=[
            pl.BlockSpec((bsz, d), lambda l, j: (0, 0)),
            pl.BlockSpec((None, d, tn), lambda l, j: (l, 0, j)),
            pl.BlockSpec((None, 1, tn), lambda l, j: (l, 0, j)),
        ],
        out_specs=pl.BlockSpec((None, bsz, tn), lambda l, j: (l, 0, j)),
        compiler_params=_params("parallel", "parallel"),
        name="mod",
    )(c, w_mod, b_mod.reshape(depth, 1, n))


def _in_proj_body(x_ref, sc_ref, sh_ref, g_ref, w_ref, nsa_ref, rw_ref, ml_ref, kse_ref, kw_ref, vse_ref,
                  vwe_ref, *, per_b):
    tm = x_ref.shape[0]
    h = _rms(x_ref[...], g_ref[...]) * (1.0 + sc_ref[...]) + sh_ref[...]
    hb = h.astype(BF16)
    nsa = jnp.dot(hb, w_ref[:, 0:NSA_W], preferred_element_type=F32)
    nsa_ref[...] = nsa
    rw_ref[...] = jnp.dot(hb, w_ref[:, NSA_W:NSA_W + RWKV_W], preferred_element_type=F32)
    ml_ref[...] = jnp.dot(hb, w_ref[:, NSA_W + RWKV_W:], preferred_element_type=F32)
    D = HEAD_DIM
    off = C_ATT + 2 * D
    lane = lax.broadcasted_iota(jnp.int32, (tm, 2 * D), 1)
    pos = (pl.program_id(0) % per_b) * tm + lax.broadcasted_iota(jnp.int32, (tm, 2 * D), 0)
    expand = jnp.where(lane - D == jnp.right_shift(pos, SLC_SHIFT), MASK_BIG, 0.0)
    kse_ref[...] = jnp.where(lane < D, nsa[:, off:off + 2 * D], expand).astype(BF16)
    kw_ref[...] = nsa[:, off + 2 * D:off + 3 * D].astype(BF16)
    tail = jnp.where(lax.broadcasted_iota(jnp.int32, (D, tm), 0) == 0, 1.0, 0.0)
    vse_ref[...] = jnp.concatenate([nsa[:, off + D:off + 2 * D].T, tail], axis=0).astype(BF16)
    vwe_ref[...] = jnp.concatenate([nsa[:, off + 3 * D:off + 4 * D].T, tail], axis=0).astype(BF16)


def _in_proj(x2, sc, sh, g, w_p, bsz, seq):
    t, d = x2.shape
    tm = 512
    per_b = seq // tm
    n_slc = seq // SLC_BLOCK
    assert n_slc == HEAD_DIM, "kse packs the key and one column per selection block into 128 lanes"
    row = lambda i: (i, 0)
    bat = lambda i: (i // per_b, 0, 0)
    fix = lambda i: (0, 0)
    col = lambda i: (i // per_b, 0, i % per_b)
    return pl.pallas_call(
        functools.partial(_in_proj_body, per_b=per_b),
        out_shape=(jax.ShapeDtypeStruct((t, NSA_W), F32),
                   jax.ShapeDtypeStruct((t, RWKV_W), F32),
                   jax.ShapeDtypeStruct((t, ML_W), F32),
                   jax.ShapeDtypeStruct((t, 2 * HEAD_DIM), BF16),
                   jax.ShapeDtypeStruct((t, HEAD_DIM), BF16),
                   jax.ShapeDtypeStruct((bsz, 128, seq), BF16),
                   jax.ShapeDtypeStruct((bsz, 128, seq), BF16)),
        grid=(t // tm,),
        in_specs=[
            pl.BlockSpec((tm, d), row),
            pl.BlockSpec((None, 1, d), bat),
            pl.BlockSpec((None, 1, d), bat),
            pl.BlockSpec((1, d), fix),
            pl.BlockSpec(w_p.shape, fix),
        ],
        out_specs=(pl.BlockSpec((tm, NSA_W), row), pl.BlockSpec((tm, RWKV_W), row),
                   pl.BlockSpec((tm, ML_W), row), pl.BlockSpec((tm, 2 * HEAD_DIM), row),
                   pl.BlockSpec((tm, HEAD_DIM), row), pl.BlockSpec((None, 128, tm), col),
                   pl.BlockSpec((None, 128, tm), col)),
        compiler_params=_params("parallel"),
        name="in_proj",
    )(x2, sc, sh, g, w_p)


def _nsa_cmp_body(x_ref, wc_ref, pe_ref, w2_ref, o_ref):
    wc = wc_ref[...]
    f = _dot(x_ref[...], wc, precise=True)
    c = _dot(pe_ref[...], wc, precise=True)
    g = f.shape[0]
    second = pltpu.roll(f[:, 128:256], g - 1, 0)
    pre = f[:, 0:128] + second + c[0:1, 0:128] + c[1:2, 128:256]
    act = jax.nn.gelu(pre, approximate=True)
    out = _dot(act, w2_ref[...], precise=True)
    rows = lax.broadcasted_iota(jnp.int32, out.shape, 0)
    o_ref[...] = jnp.where(rows < g - 1, out, 0.0)


def _nsa_cmp(xg, wc, pe8, w2):
    bsz, g, k = xg.shape
    return pl.pallas_call(
        _nsa_cmp_body,
        out_shape=jax.ShapeDtypeStruct((bsz, g, 128), F32),
        grid=(bsz,),
        in_specs=[
            pl.BlockSpec((None, g, k), lambda b: (b, 0, 0)),
            pl.BlockSpec(wc.shape, lambda b: (0, 0)),
            pl.BlockSpec(pe8.shape, lambda b: (0, 0)),
            pl.BlockSpec(w2.shape, lambda b: (0, 0)),
        ],
        out_specs=pl.BlockSpec((None, g, 128), lambda b: (b, 0, 0)),
        compiler_params=_params("parallel"),
        name="nsa_cmp",
    )(xg, wc, pe8, w2)


def _nsa_body(q_ref, gts_ref, kvc_ref, kse_ref, vse_ref, kw_ref, vwe_ref, gb_ref, g_ref, o_ref,
              m_s, acc_s, s_even, s_odd, ocmp, *, n_slc):
    tq = q_ref.shape[0]
    n_cmp = kvc_ref.shape[0]
    qi = pl.program_id(1)
    q0 = qi * tq
    scale = HEAD_DIM ** -0.5
    D = HEAD_DIM
    qt = q_ref[...].T

    kc = kvc_ref[:, 0:D]
    vc = kvc_ref[:, D:2 * D]
    nidx = lax.broadcasted_iota(jnp.int32, (n_cmp, tq), 0)
    pos = q0 + lax.broadcasted_iota(jnp.int32, (n_cmp, tq), 1)
    cmask = (nidx * CMP_STRIDE + (CMP_BLOCK - 1)) <= pos
    heads = range(H_ATT)
    sc = [_dot3(kc, qt[h * D:(h + 1) * D]) for h in heads]
    sc = [jnp.where(cmask, x * scale, NEG) for x in sc]
    ec = [jnp.where(cmask, jnp.exp(x - jnp.max(x, axis=0, keepdims=True)), 0.0) for x in sc]
    pc = [e / jnp.maximum(jnp.sum(e, axis=0, keepdims=True), 1e-30) for e in ec]
    oc = [_dot_tn(vc, p) for p in pc]
    for h in heads:
        ocmp[h * D:(h + 1) * D, :] = oc[h]
    psum = sum(pc[1:], pc[0])

    jrow = lax.broadcasted_iota(jnp.int32, (n_slc, n_cmp), 0) * SLC_BLOCK
    ncol = lax.broadcasted_iota(jnp.int32, (n_slc, n_cmp), 1) * CMP_STRIDE
    overlap_t = jnp.where((ncol < jrow + SLC_BLOCK) & (ncol + CMP_BLOCK > jrow), 1.0, 0.0)
    imp_t = _dot_mask_lhs(overlap_t, psum)
    jj = lax.broadcasted_iota(jnp.int32, (n_slc, tq), 0)
    cur = jnp.right_shift(q0 + lax.broadcasted_iota(jnp.int32, (n_slc, tq), 1), SLC_SHIFT)
    forced = (jj == 0) | (jj == cur) | (jj == cur - 1)
    score = jnp.where(forced, FORCE, jnp.where(jj <= cur, imp_t, NEG))
    group = 8
    cnt = [jnp.zeros((group, tq), F32) for _ in range(n_slc // group)]
    for j2 in range(n_slc):
        row = score[j2:j2 + 1, :]
        for gi in range(n_slc // group):
            sg = score[gi * group:(gi + 1) * group]
            if gi * group > j2:
                hit = jnp.where(row >= sg, 1.0, 0.0)
            elif (gi + 1) * group - 1 <= j2:
                hit = jnp.where(row > sg, 1.0, 0.0)
            else:
                hit = jnp.where(lax.broadcasted_iota(jnp.int32, (group, tq), 0) + gi * group > j2,
                                jnp.where(row >= sg, 1.0, 0.0), jnp.where(row > sg, 1.0, 0.0))
            cnt[gi] = cnt[gi] + hit
    cnt = jnp.concatenate(cnt, axis=0)
    selm = jnp.where(cnt < min(N_SELECT, n_slc), 0.0, -1.0).astype(BF16)

    qs = (qt * (scale * LOG2E)).astype(BF16)
    rhs_w = [qs[h * D:(h + 1) * D] for h in heads]
    rhs_s = [jnp.concatenate([rhs_w[h], selm], axis=0) for h in heads]
    m_s[...] = jnp.full(m_s.shape, NEG, F32)
    acc_s[...] = jnp.zeros_like(acc_s)

    def scores_into(buf, j):
        k = kse_ref[pl.ds(pl.multiple_of(j * TK, TK), TK), :]
        for h in heads:
            buf[h] = jnp.dot(k, rhs_s[h], preferred_element_type=F32)

    def consume(buf, j, mask):
        vt = vse_ref[:, pl.ds(pl.multiple_of(j * TK, TK), TK)]
        for h in heads:
            s = buf[h]
            if mask is not None:
                s = jnp.where(mask, s, NEG)
            m_old = m_s[h:h + 1, :]
            m_new = jnp.maximum(m_old, jnp.max(s, axis=0, keepdims=True))
            p = jnp.exp2(s - m_new).astype(BF16)
            acc_s[h] = jnp.exp2(m_old - m_new) * acc_s[h] + jnp.dot(vt, p, preferred_element_type=F32)
            m_s[h:h + 1, :] = m_new

    kpos = lax.broadcasted_iota(jnp.int32, (TK, tq), 0)
    qpos = lax.broadcasted_iota(jnp.int32, (TK, tq), 1)
    last = jnp.maximum(qi - 1, 0)
    scores_into(s_even, qi)
    scores_into(s_odd, 0)
    consume(s_even, qi, kpos <= qpos)

    def body(jp, carry):
        j = 2 * jp
        scores_into(s_even, jnp.minimum(j + 1, last))
        consume(s_odd, j, None)
        scores_into(s_odd, jnp.minimum(j + 2, last))
        consume(s_even, j + 1, None)
        return carry

    lax.fori_loop(0, qi // 2, body, 0)

    @pl.when(qi % 2 == 1)
    def _():
        consume(s_odd, qi - 1, None)

    span = WINDOW + tq
    w0 = pl.multiple_of(jnp.maximum(q0 - WINDOW, 0), TK)
    kw = kw_ref[pl.ds(w0, span), :]
    vwt = vwe_ref[:, pl.ds(w0, span)]
    kabs = w0 + lax.broadcasted_iota(jnp.int32, (span, tq), 0)
    qabs = q0 + lax.broadcasted_iota(jnp.int32, (span, tq), 1)
    wmask = (kabs <= qabs) & (kabs > qabs - WINDOW)
    sw = [jnp.where(wmask, jnp.dot(kw, rhs_w[h], preferred_element_type=F32), NEG) for h in heads]
    pw = [jnp.exp2(x - jnp.max(x, axis=0, keepdims=True)).astype(BF16) for x in sw]
    acc_w = [jnp.dot(vwt, p, preferred_element_type=F32) for p in pw]

    gate = jax.nn.sigmoid((gts_ref[...] + gb_ref[...]).T)
    outs = []
    for h in heads:
        a_s = acc_s[h]
        a_w = acc_w[h]
        outs.append(gate[h:h + 1] * ocmp[h * D:(h + 1) * D, :]
                    + gate[H_ATT + h:H_ATT + h + 1] * (a_s[0:D] / a_s[D:D + 1])
                    + gate[2 * H_ATT + h:2 * H_ATT + h + 1] * (a_w[0:D] / a_w[D:D + 1]))
    ot = jnp.concatenate(outs, axis=0)
    ot = ot * lax.rsqrt(jnp.mean(ot * ot, axis=0, keepdims=True) + NORM_EPS)
    o_ref[...] = ot.T * g_ref[...]


def _nsa(p_nsa, kv_cmp, kse, vse, kw, vwe, gate_b, out_g, bsz, seq):
    t = p_nsa.shape[0]
    nq = seq // TQ
    n_slc = seq // SLC_BLOCK
    g = kv_cmp.shape[1]
    row = lambda b, i: (b * nq + i, 0)
    per_b = lambda b, i: (b, 0, 0)
    fix = lambda b, i: (0, 0)
    return pl.pallas_call(
        functools.partial(_nsa_body, n_slc=n_slc),
        out_shape=jax.ShapeDtypeStruct((t, C_ATT), F32),
        grid=(bsz, nq),
        in_specs=[
            pl.BlockSpec((TQ, C_ATT), row),
            pl.BlockSpec((TQ, 128), lambda b, i: (b * nq + i, NSA_W // 128 - 1)),
            pl.BlockSpec((None, g, 128), per_b),
            pl.BlockSpec((None, seq, HEAD_DIM + n_slc), per_b),
            pl.BlockSpec((None, 128, seq), per_b),
            pl.BlockSpec((None, seq, HEAD_DIM), per_b),
            pl.BlockSpec((None, 128, seq), per_b),
            pl.BlockSpec((1, 128), fix),
            pl.BlockSpec((1, C_ATT), fix),
        ],
        out_specs=pl.BlockSpec((TQ, C_ATT), row),
        scratch_shapes=[pltpu.VMEM((8, TQ), F32), pltpu.VMEM((H_ATT, 128, TQ), F32),
                        pltpu.VMEM((H_ATT, TK, TQ), F32), pltpu.VMEM((H_ATT, TK, TQ), F32),
                        pltpu.VMEM((C_ATT, TQ), F32)],
        compiler_params=_params("parallel", "arbitrary"),
        name="nsa",
    )(p_nsa, p_nsa, kv_cmp, kse, vse, kw, vwe, gate_b, out_g)


def _tri(n, strict):
    r = lax.broadcasted_iota(jnp.int32, (n, n), 0)
    c = lax.broadcasted_iota(jnp.int32, (n, n), 1)
    return (c < r) if strict else (c <= r)


def _block_diag(x):
    xb = x.astype(BF16)
    first = lax.broadcasted_iota(jnp.int32, xb.shape, 1) < xb.shape[1] // 2
    zero = jnp.zeros_like(xb)
    return jnp.concatenate([jnp.where(first, xb, zero), jnp.where(first, zero, xb)], axis=0)


def _unit_lower_inverse_pairs(ms, eye2):
    n = eye2.shape[0]
    idx = range(len(ms))
    r = lax.broadcasted_iota(jnp.int32, (n, 2 * n), 0)
    c = jnp.bitwise_and(lax.broadcasted_iota(jnp.int32, (n, 2 * n), 1), n - 1)
    base = 8
    diag = jnp.right_shift(r, 3) == jnp.right_shift(c, 3)
    m8 = [jnp.where(diag, m, 0.0) for m in ms]
    t = [eye2 + m for m in m8]
    p2 = [_dot(m, _block_diag(m)) for m in m8]
    t = [t[i] + _dot(p2[i], _block_diag(t[i])) for i in idx]
    p4 = [_dot(p, _block_diag(p)) for p in p2]
    t = [t[i] + _dot(p4[i], _block_diag(t[i])) for i in idx]
    b = base
    while b < n:
        sh = b.bit_length() - 1
        pair = jnp.right_shift(r, sh + 1) == jnp.right_shift(c, sh + 1)
        lower_left = pair & (jnp.right_shift(r, sh) != jnp.right_shift(c, sh))
        left = [_dot(t[i], _block_diag(jnp.where(lower_left, ms[i], 0.0))) for i in idx]
        t = [t[i] + _dot(left[i], _block_diag(t[i])) for i in idx]
        b *= 2
    return t


def _rwkv_body(p_ref, mu_ref, w0_ref, w2_ref, a0_ref, a2_ref, g2_ref, kk_ref, ka_ref, rk_ref, lnw_ref,
               lnb_ref, o_ref, state, prev):
    L = CHUNK
    C = C_RWKV
    tb = p_ref.shape[0]
    chunks = range(tb // L)

    @pl.when(pl.program_id(1) == 0)
    def _():
        state[...] = jnp.zeros_like(state)
        prev[...] = jnp.zeros_like(prev)

    x = p_ref[...]
    rows = lax.broadcasted_iota(jnp.int32, x.shape, 0)
    shifted = jnp.where(rows == 0, prev[0:1, :], pltpu.roll(x, 1, 0))
    prev[0:1, :] = x[tb - 1:tb, :]
    x = x + mu_ref[...] * (shifted - x)
    r = x[:, 0:C]
    k = x[:, C:2 * C]
    v = x[:, 2 * C:3 * C]
    xw = x[:, 3 * C:3 * C + RANK_W]
    xa = x[:, 3 * C + RANK_W:3 * C + RANK_W + RANK_A]
    xg = x[:, 3 * C + RANK_W + RANK_A:]
    w = -jax.nn.softplus(-(w0_ref[...] + _dot3(jnp.tanh(xw), w2_ref[...]))) - 0.5
    logw = -jnp.exp(w)
    a = jax.nn.sigmoid(a0_ref[...] + _dot3(xa, a2_ref[...]))
    g = _dot(jax.nn.sigmoid(xg), g2_ref[...])
    kkf = k * kk_ref[...]
    kmod = k * (1.0 + (a - 1.0) * ka_ref[...])
    rk = r * kmod * rk_ref[...]

    tri_incl = jnp.where(_tri(L, strict=False), 1.0, 0.0)
    cum = jnp.concatenate([_dot_mask_lhs(tri_incl, logw[c * L:(c + 1) * L]) for c in chunks], axis=0)
    cum_last = [cum[(c + 1) * L - 1:(c + 1) * L, :] for c in chunks]
    cum_end = jnp.concatenate([jnp.broadcast_to(cl, (L, C)) for cl in cum_last], axis=0)
    e_pos = jnp.exp(cum)
    e_prev = jnp.exp(cum - logw)
    e_neg = jnp.exp(-cum)
    e_rem = jnp.exp(cum_end - cum)
    N = HEAD_DIM
    PW = 2 * N
    pairs = range(H_RWKV // 2)
    lane_l = lax.broadcasted_iota(jnp.int32, (L, PW), 1)
    row_l = lax.broadcasted_iota(jnp.int32, (L, PW), 0)
    col_l = jnp.bitwise_and(lane_l, N - 1)
    first = lane_l < N
    first2 = lax.broadcasted_iota(jnp.int32, (2 * L, PW), 1) < N
    strict2 = col_l < row_l
    incl2 = col_l <= row_l
    eye2 = jnp.where(col_l == row_l, 1.0, 0.0)

    def halves(x_):
        lane = lax.broadcasted_iota(jnp.int32, x_.shape, 1)
        sa = jnp.sum(jnp.where(lane < N, x_, 0.0), axis=-1, keepdims=True)
        sb = jnp.sum(jnp.where(lane < N, 0.0, x_), axis=-1, keepdims=True)
        return jnp.where(lane < N, sa, sb)

    ps = [slice(p * PW, (p + 1) * PW) for p in pairs]
    kk_p = [kkf[:, sl] for sl in ps]
    kk_p = [x_ / jnp.maximum(jnp.sqrt(halves(x_ * x_)), 1e-12) for x_ in kk_p]
    alpha_f = [-kk_p[p] * e_prev[:, ps[p]] for p in pairs]
    r_f = [r[:, ps[p]] * e_pos[:, ps[p]] for p in pairs]
    beta = [kk_p[p] * a[:, ps[p]] for p in pairs]
    beta_f = [beta[p] * e_neg[:, ps[p]] for p in pairs]
    k_f = [kmod[:, ps[p]] * e_neg[:, ps[p]] for p in pairs]
    beta_e = [beta[p] * e_rem[:, ps[p]] for p in pairs]
    k_e = [kmod[:, ps[p]] * e_rem[:, ps[p]] for p in pairs]
    items = [(c, p) for c in chunks for p in pairs]
    n_items = range(len(items))
    rs = [slice(c * L, (c + 1) * L) for c, _ in items]
    pi = [p for _, p in items]
    lhs = [jnp.concatenate([alpha_f[pi[i]][rs[i]], r_f[pi[i]][rs[i]]], axis=0).astype(BF16) for i in n_items]
    zero16 = jnp.zeros((2 * L, PW), BF16)
    lhs4 = [jnp.concatenate([jnp.where(first2, x_, zero16), jnp.where(first2, zero16, x_)], axis=0) for x_ in lhs]
    bk = [(beta_f[pi[i]][rs[i]].astype(BF16), k_f[pi[i]][rs[i]].astype(BF16)) for i in n_items]
    v_i = [v[rs[i], ps[pi[i]]] for i in n_items]
    out1 = [_dot_nt(lhs4[i], jnp.concatenate([bk[i][0], bk[i][1]], axis=0)) for i in n_items]
    out2 = [_dot_nt(lhs4[i], jnp.concatenate([bk[i][1], bk[i][0]], axis=0)) for i in n_items]
    m_ab = [jnp.where(strict2, jnp.where(first, out1[i][0:L], out2[i][2 * L:3 * L]), 0.0) for i in n_items]
    m_ak = [jnp.where(strict2, jnp.where(first, out2[i][0:L], out1[i][2 * L:3 * L]), 0.0) for i in n_items]
    m_rb = [jnp.where(incl2, jnp.where(first, out1[i][L:2 * L], out2[i][3 * L:4 * L]), 0.0) for i in n_items]
    m_rk = [jnp.where(incl2, jnp.where(first, out2[i][L:2 * L], out1[i][3 * L:4 * L]), 0.0) for i in n_items]
    t_inv = _unit_lower_inverse_pairs(m_ab, eye2)
    mv = [_dot(jnp.concatenate([m_ak[i], m_rk[i]], axis=0), _block_diag(v_i[i])) for i in n_items]
    pq = [_dot(t_inv[i], jnp.concatenate([_block_diag(alpha_f[pi[i]][rs[i]]), _block_diag(mv[i][0:L])], axis=1))
          for i in n_items]
    ry = [_dot(m_rb[i], jnp.concatenate([_block_diag(pq[i][:, 0:PW]), _block_diag(pq[i][:, PW:2 * PW])], axis=1))
          for i in n_items]
    r_eff = [r_f[pi[i]][rs[i]] + ry[i][:, 0:PW] for i in n_items]
    y0 = [ry[i][:, PW:2 * PW] + mv[i][L:2 * L] for i in n_items]
    zero_l = jnp.zeros((L, PW), F32)
    gh = [_dot_tn(jnp.concatenate([beta_e[pi[i]][rs[i]], k_e[pi[i]][rs[i]]], axis=0),
                  jnp.concatenate([pq[i], jnp.concatenate([zero_l, v_i[i]], axis=1)], axis=0)) for i in n_items]
    g_off = [jnp.where(first, gh[i][0:N, 0:PW], gh[i][N:2 * N, 0:PW]) for i in n_items]
    h_add = [jnp.where(first, gh[i][0:N, PW:2 * PW], gh[i][N:2 * N, PW:2 * PW]) for i in n_items]
    gam = [halves(eye2 * jnp.exp(cum_last[c][:, ps[p]])) for c, p in items]
    z = [state[p] for p in pairs]
    y_parts = [[] for _ in pairs]
    for c in chunks:
        idx = [c * len(pairs) + p for p in pairs]
        zb = [_block_diag(z[p]) for p in pairs]
        for p, i in enumerate(idx):
            y_parts[p].append(_dot(r_eff[i], zb[p]) + y0[i])
        z = [z[p] * gam[i] + _dot(g_off[i], zb[p]) + h_add[i] for p, i in enumerate(idx)]
    for p in pairs:
        state[p] = z[p]
    yn = []
    for p in pairs:
        y_p = jnp.concatenate(y_parts[p], axis=0)
        mu_y = halves(y_p) * (1.0 / N)
        dev = y_p - mu_y
        var = halves(dev * dev) * (1.0 / N)
        yn.append(dev * lax.rsqrt(var + RWKV_GN_EPS))
    bonus = [halves(rk[:, ps[p]]) * v[:, ps[p]] for p in pairs]
    yn = jnp.concatenate(yn, axis=-1)
    bonus = jnp.concatenate(bonus, axis=-1)
    o_ref[...] = (yn * lnw_ref[...] + lnb_ref[...] + bonus) * g


def _rwkv(p_rw, mu, w0, w2, a0, a2, g2, k_k, k_a, r_k, ln_w, ln_b, bsz, seq):
    t = p_rw.shape[0]
    tb = RWKV_BLOCK
    nb = seq // tb
    row = lambda b, c: (b * nb + c, 0)
    fix = lambda b, c: (0, 0)
    vec = lambda a: a.reshape(1, -1)
    args = (vec(mu), vec(w0), w2, vec(a0), a2, g2, vec(k_k), vec(k_a), vec(r_k), vec(ln_w), vec(ln_b))
    return pl.pallas_call(
        _rwkv_body,
        out_shape=jax.ShapeDtypeStruct((t, C_RWKV), F32),
        grid=(bsz, nb),
        in_specs=[pl.BlockSpec((tb, RWKV_W), row)] + [pl.BlockSpec(a.shape, fix) for a in args],
        out_specs=pl.BlockSpec((tb, C_RWKV), row),
        scratch_shapes=[pltpu.VMEM((H_RWKV // 2, HEAD_DIM, 2 * HEAD_DIM), F32), pltpu.VMEM((8, RWKV_W), F32)],
        compiler_params=_params("parallel", "arbitrary"),
        name="rwkv",
    )(p_rw, *args)


def _mlstm_body(p_ref, cw_ref, cb_ref, gb_ref, ng_ref, o_ref, cstate, mstate, prev):
    L = CHUNK
    C = C_MLSTM
    D = HEAD_DIM
    tb = p_ref.shape[0]
    chunks = range(tb // L)

    @pl.when(pl.program_id(1) == 0)
    def _():
        cstate[...] = jnp.zeros_like(cstate)
        mstate[...] = jnp.zeros_like(mstate)
        prev[...] = jnp.zeros_like(prev)

    qk_in = p_ref[:, 0:2 * C]
    pv = prev[...]
    rows8 = lax.broadcasted_iota(jnp.int32, pv.shape, 0)
    conv = qk_in * cw_ref[CONV_WIDTH - 1:CONV_WIDTH, :] + cb_ref[...]
    for d in range(1, CONV_WIDTH):
        rolled = pltpu.roll(qk_in, d, 0)
        top = jnp.where(rows8 < d, pltpu.roll(pv, d, 0), rolled[0:8])
        sh = jnp.concatenate([top, rolled[8:]], axis=0)
        conv = conv + sh * cw_ref[CONV_WIDTH - 1 - d:CONV_WIDTH - d, :]
    prev[...] = qk_in[tb - 8:tb, :]
    qk = conv * jax.nn.sigmoid(conv)
    q = qk[:, 0:C]
    k = qk[:, C:2 * C] * (D ** -0.5)
    v = p_ref[:, 2 * C:3 * C]
    og = p_ref[:, 3 * C:4 * C]
    gates = p_ref[:, 4 * C:4 * C + 128] + gb_ref[...]
    lane = lax.broadcasted_iota(jnp.int32, gates.shape, 1)
    is_f = (lane >= H_MLSTM) & (lane < 2 * H_MLSTM)
    gl = jnp.where(is_f, jax.nn.log_sigmoid(gates), gates)
    tri_incl = jnp.where(_tri(L, strict=False), 1.0, 0.0)
    bcum = [_dot_mask_lhs(tri_incl, gl[c * L:(c + 1) * L]) for c in chunks]
    gl_t = [gl[c * L:(c + 1) * L].T for c in chunks]
    bcum_t = [_dot_nt_mask_rhs(g_, tri_incl) for g_ in gl_t]
    m_all = mstate[...]
    src = lax.broadcasted_iota(jnp.int32, (L, L), 0)
    qry = lax.broadcasted_iota(jnp.int32, (L, L), 1)
    causal_t = src <= qry
    ones_row = jnp.where(lax.broadcasted_iota(jnp.int32, (D, L), 0) == 0, 1.0, 0.0)

    heads = range(H_MLSTM)
    items = [(c, h) for c in chunks for h in heads]
    n_items = range(len(items))
    q_t = [q[c * L:(c + 1) * L].T for c in chunks]
    v_t = [v[c * L:(c + 1) * L].T for c in chunks]
    qt_i = [q_t[c][h * D:(h + 1) * D] for c, h in items]
    vt_ext = [jnp.concatenate([v_t[c][h * D:(h + 1) * D], ones_row], axis=0) for c, h in items]
    k_i = [k[c * L:(c + 1) * L, h * D:(h + 1) * D] for c, h in items]
    c_col = [gl[c * L:(c + 1) * L, h:h + 1] - bcum[c][:, H_MLSTM + h:H_MLSTM + h + 1] for c, h in items]
    b_row = [bcum_t[c][H_MLSTM + h:H_MLSTM + h + 1, :] for c, h in items]
    c_row = [gl_t[c][h:h + 1, :] - b_row[i] for i, (c, h) in enumerate(items)]
    g_tot = [bcum[c][L - 1:L, H_MLSTM + h:H_MLSTM + h + 1] for c, h in items]
    u_row = [g_tot[i] + c_row[i] for i in n_items]
    u_max = [jnp.max(x_, axis=1, keepdims=True) for x_ in u_row]
    m0 = []
    m_run = [m_all[0:1, h:h + 1] for h in heads]
    for c in chunks:
        for h in heads:
            i = c * H_MLSTM + h
            m0.append(m_run[h])
            m_run[h] = jnp.maximum(g_tot[i] + m_run[h], u_max[i])
    m_next = [m0[i + H_MLSTM] if i + H_MLSTM < len(items) else m_run[items[i][1]] for i in n_items]
    kq = [_dot(k_i[i], qt_i[i]) for i in n_items]
    kv = [_dot(vt_ext[i] * jnp.exp(u_row[i] - m_next[i]), k_i[i]) for i in n_items]
    cm = []
    cm_run = [cstate[h] for h in heads]
    for c in chunks:
        for h in heads:
            i = c * H_MLSTM + h
            cm.append(cm_run[h])
            cm_run[h] = jnp.exp(g_tot[i] + m0[i] - m_next[i]) * cm_run[h] + kv[i]
    inter = [_dot(cm[i], qt_i[i]) for i in n_items]
    c_b = [jnp.where(causal_t, jnp.broadcast_to(c_col[i], (L, L)), -jnp.inf) for i in n_items]
    m_row = [jnp.maximum(m0[i], jnp.max(c_b[i], axis=0, keepdims=True)) for i in n_items]
    s_t = [jnp.exp(c_b[i] - m_row[i]) * kq[i] for i in n_items]
    sv = [_dot(vt_ext[i], s_t[i]) for i in n_items]
    lane8 = lax.broadcasted_iota(jnp.int32, m_all.shape, 1)
    for h in heads:
        cstate[h] = cm_run[h]
        m_all = jnp.where(lane8 == h, m_run[h], m_all)
    mstate[...] = m_all
    outs = []
    for i in n_items:
        num = jnp.exp(m0[i] - m_row[i]) * inter[i] + sv[i]
        hh = num[0:D] / jnp.maximum(jnp.abs(num[D:D + 1]), jnp.exp(-(b_row[i] + m_row[i])))
        outs.append(hh * lax.rsqrt(jnp.mean(hh * hh, axis=0, keepdims=True) + NORM_EPS))
    hcat = jnp.concatenate([jnp.concatenate(outs[c * H_MLSTM:(c + 1) * H_MLSTM], axis=0).T for c in chunks], axis=0)
    o_ref[...] = jax.nn.sigmoid(og) * (hcat * ng_ref[...])


def _mlstm(p_ml, conv_w, conv_b, gate_b, norm_g, bsz, seq):
    t = p_ml.shape[0]
    tb = ML_BLOCK
    nb = seq // tb
    row = lambda b, c: (b * nb + c, 0)
    fix = lambda b, c: (0, 0)
    args = (conv_w, conv_b.reshape(1, -1), gate_b, norm_g.reshape(1, -1))
    return pl.pallas_call(
        _mlstm_body,
        out_shape=jax.ShapeDtypeStruct((t, C_MLSTM), F32),
        grid=(bsz, nb),
        in_specs=[pl.BlockSpec((tb, ML_W), row)] + [pl.BlockSpec(a.shape, fix) for a in args],
        out_specs=pl.BlockSpec((tb, C_MLSTM), row),
        scratch_shapes=[pltpu.VMEM((H_MLSTM, 128, HEAD_DIM), F32), pltpu.VMEM((8, 128), F32),
                        pltpu.VMEM((8, 2 * C_MLSTM), F32)],
        compiler_params=_params("parallel", "arbitrary"),
        name="mlstm",
    )(p_ml, *args)


def _out_proj_body(a_ref, b_ref, c_ref, wa_ref, wb_ref, wc_ref, x_ref, gt_ref, g_ref, o_ref, *, sub):
    parts = [slice(s * sub, (s + 1) * sub) for s in range(x_ref.shape[0] // sub)]
    y = [jnp.dot(a_ref[p, :].astype(BF16), wa_ref[...], preferred_element_type=F32)
         + jnp.dot(b_ref[p, :].astype(BF16), wb_ref[...], preferred_element_type=F32)
         + jnp.dot(c_ref[p, :].astype(BF16), wc_ref[...], preferred_element_type=F32) for p in parts]
    for p, y_ in zip(parts, y):
        o_ref[p, :] = x_ref[p, :] + gt_ref[...] * _rms(y_, g_ref[...])


def _out_proj(o_nsa, o_rw, o_ml, wa, wb, wc, x2, gt, g, seq):
    t, d = x2.shape
    tm = 1024
    per_b = seq // tm
    row = lambda i: (i, 0)
    fix = lambda i: (0, 0)
    return pl.pallas_call(
        functools.partial(_out_proj_body, sub=tm // 2),
        out_shape=jax.ShapeDtypeStruct((t, d), F32),
        grid=(t // tm,),
        in_specs=[
            pl.BlockSpec((tm, C_ATT), row), pl.BlockSpec((tm, C_RWKV), row), pl.BlockSpec((tm, C_MLSTM), row),
            pl.BlockSpec(wa.shape, fix), pl.BlockSpec(wb.shape, fix), pl.BlockSpec(wc.shape, fix),
            pl.BlockSpec((tm, d), row),
            pl.BlockSpec((None, 1, d), lambda i: (i // per_b, 0, 0)),
            pl.BlockSpec((1, d), fix),
        ],
        out_specs=pl.BlockSpec((tm, d), row),
        compiler_params=_params("parallel"),
        name="out_proj",
    )(o_nsa, o_rw, o_ml, wa, wb, wc, x2, gt, g)


FFN_SLAB = 1024


def _ffn_body(x_ref, sc_ref, sh_ref, gt_ref, gpre_ref, gpost_ref, wg_ref, wu_ref, wd_ref, o_ref, *, sub):
    tm = x_ref.shape[0]
    dff = wg_ref.shape[1]
    parts = [slice(s * sub, (s + 1) * sub) for s in range(tm // sub)]
    slabs = [slice(a, min(a + FFN_SLAB, dff)) for a in range(0, dff, FFN_SLAB)]
    hb = [(_rms(x_ref[p, :], gpre_ref[...]) * (1.0 + sc_ref[...]) + sh_ref[...]).astype(BF16) for p in parts]
    y = [None for _ in parts]
    for sl in slabs:
        gate = [jnp.dot(h, wg_ref[:, sl], preferred_element_type=F32) for h in hb]
        up = [jnp.dot(h, wu_ref[:, sl], preferred_element_type=F32) for h in hb]
        act = [(g_ * jax.nn.sigmoid(g_) * u_).astype(BF16) for g_, u_ in zip(gate, up)]
        down = [jnp.dot(a_, wd_ref[sl, :], preferred_element_type=F32) for a_ in act]
        y = [d_ if y_ is None else y_ + d_ for y_, d_ in zip(y, down)]
    for p, y_ in zip(parts, y):
        o_ref[p, :] = x_ref[p, :] + gt_ref[...] * _rms(y_, gpost_ref[...])


def _ffn(x2, sc, sh, gt, g_pre, g_post, wg, wu, wd, seq):
    t, d = x2.shape
    tm = 1024
    per_b = seq // tm
    row = lambda i: (i, 0)
    bat = lambda i: (i // per_b, 0, 0)
    fix = lambda i: (0, 0)
    resident = lambda a: pl.BlockSpec(a.shape, fix, pipeline_mode=pl.Buffered(1))
    return pl.pallas_call(
        functools.partial(_ffn_body, sub=tm // 2),
        out_shape=jax.ShapeDtypeStruct((t, d), F32),
        grid=(t // tm,),
        in_specs=[
            pl.BlockSpec((tm, d), row),
            pl.BlockSpec((None, 1, d), bat), pl.BlockSpec((None, 1, d), bat), pl.BlockSpec((None, 1, d), bat),
            pl.BlockSpec((1, d), fix), pl.BlockSpec((1, d), fix),
            resident(wg), resident(wu), resident(wd),
        ],
        out_specs=pl.BlockSpec((tm, d), row),
        compiler_params=_params("parallel"),
        name="ffn",
    )(x2, sc, sh, gt, g_pre, g_post, wg, wu, wd)


def _pad_cols(a, width):
    return jnp.pad(a, ((0, 0), (0, width - a.shape[1])))


def _layout_w_in(w_in):
    d_in_rw = NSA_IN + RWKV_W
    nsa = _pad_cols(w_in[:, :NSA_IN], NSA_W)
    rw = w_in[:, NSA_IN:d_in_rw]
    ml = _pad_cols(w_in[:, d_in_rw:], ML_W)
    return jnp.concatenate([nsa, rw, ml], axis=1).astype(BF16)


def _layout_cmp(ck_w1, cv_w1, ck_w2, cv_w2, pe_k, pe_v):
    half = CMP_BLOCK // 2
    ck = ck_w1.reshape(2, half, HEAD_DIM, HEAD_DIM)
    cv = cv_w1.reshape(2, half, HEAD_DIM, HEAD_DIM)
    z = jnp.zeros_like(ck[0])
    top = jnp.concatenate([ck[0], z, ck[1], z], axis=-1)
    bot = jnp.concatenate([z, cv[0], z, cv[1]], axis=-1)
    wc = jnp.concatenate([top, bot], axis=1).reshape(half * 2 * HEAD_DIM, 4 * HEAD_DIM)
    pe = jnp.concatenate([pe_k, pe_v], axis=-1).reshape(2, half * 2 * HEAD_DIM)
    pe8 = jnp.pad(pe, ((0, 6), (0, 0)))
    z2 = jnp.zeros_like(ck_w2)
    w2 = jnp.concatenate([jnp.concatenate([ck_w2, z2], axis=1), jnp.concatenate([z2, cv_w2], axis=1)], axis=0)
    return wc, pe8, w2


def kernel(x, c, w_mod, b_mod, g_pre_mix, g_post_mix, g_pre_ffn, g_post_ffn, w_in, w_out, nsa_pe_k, nsa_pe_v, nsa_ck_w1, nsa_ck_w2, nsa_cv_w1, nsa_cv_w2, nsa_gate_b, nsa_out_g, rw_mu, rw_w0, rw_w2, rw_a0, rw_a2, rw_g2, rw_kk, rw_ka, rw_rk, rw_ln_w, rw_ln_b, ml_conv_w, ml_conv_b, ml_ig_b, ml_fg_b, ml_norm_g, ffn_w_gate, ffn_w_up, ffn_w_down):
    bsz, seq, d = x.shape
    depth = w_mod.shape[0]
    t = bsz * seq
    half = CMP_BLOCK // 2
    mod = _mod(c, w_mod, b_mod)
    x2 = x.reshape(t, d)
    for l in range(depth):
        sh1, sc1, gt1, sh2, sc2, gt2 = [m.reshape(bsz, 1, d) for m in jnp.split(mod[l], 6, axis=-1)]
        p_nsa, p_rw, p_ml, kse, kw, vse, vwe = _in_proj(x2, sc1, sh1, g_pre_mix[l].reshape(1, d),
                                                        _layout_w_in(w_in[l]), bsz, seq)
        wc, pe8, w2 = _layout_cmp(nsa_ck_w1[l], nsa_cv_w1[l], nsa_ck_w2[l], nsa_cv_w2[l], nsa_pe_k[l], nsa_pe_v[l])
        xg = p_nsa[:, C_ATT:C_ATT + 2 * HEAD_DIM].reshape(bsz, seq // half, half * 2 * HEAD_DIM)
        kv_cmp = _nsa_cmp(xg, wc, pe8, w2)
        gate_b = jnp.pad(nsa_gate_b[l], (0, 128 - 3 * H_ATT)).reshape(1, 128)
        o_nsa = _nsa(p_nsa, kv_cmp, kse.reshape(bsz, seq, -1), vse, kw.reshape(bsz, seq, -1), vwe, gate_b,
                     nsa_out_g[l].reshape(1, C_ATT), bsz, seq)
        o_rw = _rwkv(p_rw, rw_mu[l], rw_w0[l], rw_w2[l], rw_a0[l], rw_a2[l], rw_g2[l], rw_kk[l], rw_ka[l],
                     rw_rk[l], rw_ln_w[l], rw_ln_b[l], bsz, seq)
        ml_gate_b = jnp.pad(jnp.concatenate([ml_ig_b[l], ml_fg_b[l]]), (0, 128 - 2 * H_MLSTM)).reshape(1, 128)
        o_ml = _mlstm(p_ml, ml_conv_w[l], ml_conv_b[l], ml_gate_b, ml_norm_g[l], bsz, seq)
        wo = w_out[l].astype(BF16)
        x2 = _out_proj(o_nsa, o_rw, o_ml, wo[:C_ATT], wo[C_ATT:C_ATT + C_RWKV], wo[C_ATT + C_RWKV:], x2, gt1,
                       g_post_mix[l].reshape(1, d), seq)
        x2 = _ffn(x2, sc2, sh2, gt2, g_pre_ffn[l].reshape(1, d), g_post_ffn[l].reshape(1, d),
                  ffn_w_gate[l].astype(BF16), ffn_w_up[l].astype(BF16), ffn_w_down[l].astype(BF16), seq)
    return x2.reshape(bsz, seq, d)
```

```python
import functools

import jax
import jax.numpy as jnp
from jax import lax
from jax.experimental import pallas as pl
from jax.experimental.pallas import tpu as pltpu

F32 = jnp.float32
BF16 = jnp.bfloat16
HIGHEST = lax.Precision.HIGHEST

HEAD_DIM = 64
H_ATT = 4
C_ATT = H_ATT * HEAD_DIM
H_RWKV = 6
C_RWKV = H_RWKV * HEAD_DIM
H_MLSTM = 6
C_MLSTM = H_MLSTM * HEAD_DIM
CMP_BLOCK = 32
CMP_STRIDE = 16
SLC_BLOCK = 64
SLC_SHIFT = 6
N_SELECT = 16
WINDOW = 512
NEG = -1e30
FORCE = 1e9
MASK_BIG = 2.0 ** 100
LOG2E = 1.4426950408889634
RANK_W = 64
RANK_A = 64
RANK_G = 128
RWKV_GN_EPS = 64e-5
CHUNK = 64
SEQ_BLOCK = 512
CONV_WIDTH = 4
NORM_EPS = 1e-6

NSA_W = 768
RWKV_W = 3 * C_RWKV + RANK_W + RANK_A + RANK_G
ML_W = 4 * C_MLSTM + 128
NSA_IN = C_ATT + 6 * HEAD_DIM + 3 * H_ATT
ML_IN = 4 * C_MLSTM + 2 * H_MLSTM

VMEM_LIMIT = 56 * 1024 * 1024

TQ = 256
TK = 256


def _params(*sem):
    return pltpu.CompilerParams(dimension_semantics=sem, vmem_limit_bytes=VMEM_LIMIT)


def _dot(a, b, precise=False):
    if precise:
        return jnp.dot(a.astype(F32), b.astype(F32), preferred_element_type=F32, precision=HIGHEST)
    return jnp.dot(a.astype(BF16), b.astype(BF16), preferred_element_type=F32)


def _dot_nt(a, b, precise=False):
    dn = (((1,), (1,)), ((), ()))
    if precise:
        return lax.dot_general(a.astype(F32), b.astype(F32), dn, preferred_element_type=F32, precision=HIGHEST)
    return lax.dot_general(a.astype(BF16), b.astype(BF16), dn, preferred_element_type=F32)


def _dot_tn(a, b, precise=False):
    dn = (((0,), (0,)), ((), ()))
    if precise:
        return lax.dot_general(a.astype(F32), b.astype(F32), dn, preferred_element_type=F32, precision=HIGHEST)
    return lax.dot_general(a.astype(BF16), b.astype(BF16), dn, preferred_element_type=F32)


def _split3(x):
    hi = x.astype(BF16)
    rest = x - hi.astype(F32)
    mid = rest.astype(BF16)
    lo = (rest - mid.astype(F32)).astype(BF16)
    return hi, mid, lo


def _dot_mask_lhs(mask01, x):
    mb = mask01.astype(BF16)
    hi, mid, lo = _split3(x)
    dot = lambda p: jnp.dot(mb, p, preferred_element_type=F32)
    return (dot(lo) + dot(mid)) + dot(hi)


def _dot_nt_mask_rhs(x, mask01):
    mb = mask01.astype(BF16)
    hi, mid, lo = _split3(x)
    dot = lambda p: lax.dot_general(p, mb, (((1,), (1,)), ((), ())), preferred_element_type=F32)
    return (dot(lo) + dot(mid)) + dot(hi)


def _dot3(a, b):
    a_hi, a_lo, _ = _split3(a)
    b_hi, b_lo, _ = _split3(b)
    dot = lambda p, q: jnp.dot(p, q, preferred_element_type=F32)
    return (dot(a_lo, b_hi) + dot(a_hi, b_lo)) + dot(a_hi, b_hi)


def _rms(x, g):
    return x * lax.rsqrt(jnp.mean(x * x, axis=-1, keepdims=True) + NORM_EPS) * g


def _mod_body(c_ref, w_ref, b_ref, o_ref):
    c = c_ref[...]
    cs = c * jax.nn.sigmoid(c)
    o_ref[...] = _dot(cs, w_ref[...], precise=True) + b_ref[...]


def _mod(c, w_mod, b_mod):
    depth, d, n = w_mod.shape
    bsz = c.shape[0]
    tn = 1536
    return pl.pallas_call(
        _mod_body,
        out_shape=jax.ShapeDtypeStruct((depth, bsz, n), F32),
        grid=(depth, n // tn),
        in_specs=[
            pl.BlockSpec((bsz, d), lambda l, j: (0, 0)),
            pl.BlockSpec((None, d, tn), lambda l, j: (l, 0, j)),
            pl.BlockSpec((None, 1, tn), lambda l, j: (l, 0, j)),
        ],
        out_specs=pl.BlockSpec((None, bsz, tn), lambda l, j: (l, 0, j)),
        compiler_params=_params("parallel", "parallel"),
        name="mod",
    )(c, w_mod, b_mod.reshape(depth, 1, n))


def _in_proj_body(x_ref, sc_ref, sh_ref, g_ref, w_ref, nsa_ref, rw_ref, ml_ref, kse_ref, kw_ref, vse_ref,
                  vwe_ref, *, per_b):
    tm = x_ref.shape[0]
    h = _rms(x_ref[...], g_ref[...]) * (1.0 + sc_ref[...]) + sh_ref[...]
    hb = h.astype(BF16)
    nsa = jnp.dot(hb, w_ref[:, 0:NSA_W], preferred_element_type=F32)
    nsa_ref[...] = nsa
    rw_ref[...] = jnp.dot(hb, w_ref[:, NSA_W:NSA_W + RWKV_W], preferred_element_type=F32)
    ml_ref[...] = jnp.dot(hb, w_ref[:, NSA_W + RWKV_W:], preferred_element_type=F32)
    D = HEAD_DIM
    off = C_ATT + 2 * D
    lane = lax.broadcasted_iota(jnp.int32, (tm, 2 * D), 1)
    pos = (pl.program_id(0) % per_b) * tm + lax.broadcasted_iota(jnp.int32, (tm, 2 * D), 0)
    expand = jnp.where(lane - D == jnp.right_shift(pos, SLC_SHIFT), MASK_BIG, 0.0)
    kse_ref[...] = jnp.where(lane < D, nsa[:, off:off + 2 * D], expand).astype(BF16)
    kw_ref[...] = nsa[:, off + 2 * D:off + 3 * D].astype(BF16)
    tail = jnp.where(lax.broadcasted_iota(jnp.int32, (D, tm), 0) == 0, 1.0, 0.0)
    vse_ref[...] = jnp.concatenate([nsa[:, off + D:off + 2 * D].T, tail], axis=0).astype(BF16)
    vwe_ref[...] = jnp.concatenate([nsa[:, off + 3 * D:off + 4 * D].T, tail], axis=0).astype(BF16)


def _in_proj(x2, sc, sh, g, w_p, bsz, seq):
    t, d = x2.shape
    tm = 512
    per_b = seq // tm
    n_slc = seq // SLC_BLOCK
    assert n_slc == HEAD_DIM, "kse packs the key and one column per selection block into 128 lanes"
    row = lambda i: (i, 0)
    bat = lambda i: (i // per_b, 0, 0)
    fix = lambda i: (0, 0)
    col = lambda i: (i // per_b, 0, i % per_b)
    return pl.pallas_call(
        functools.partial(_in_proj_body, per_b=per_b),
        out_shape=(jax.ShapeDtypeStruct((t, NSA_W), F32),
                   jax.ShapeDtypeStruct((t, RWKV_W), F32),
                   jax.ShapeDtypeStruct((t, ML_W), F32),
                   jax.ShapeDtypeStruct((t, 2 * HEAD_DIM), BF16),
                   jax.ShapeDtypeStruct((t, HEAD_DIM), BF16),
                   jax.ShapeDtypeStruct((bsz, 128, seq), BF16),
                   jax.ShapeDtypeStruct((bsz, 128, seq), BF16)),
        grid=(t // tm,),
        in_specs=[
            pl.BlockSpec((tm, d), row),
            pl.BlockSpec((None, 1, d), bat),
            pl.BlockSpec((None, 1, d), bat),
            pl.BlockSpec((1, d), fix),
            pl.BlockSpec(w_p.shape, fix),
        ],
        out_specs=(pl.BlockSpec((tm, NSA_W), row), pl.BlockSpec((tm, RWKV_W), row),
                   pl.BlockSpec((tm, ML_W), row), pl.BlockSpec((tm, 2 * HEAD_DIM), row),
                   pl.BlockSpec((tm, HEAD_DIM), row), pl.BlockSpec((None, 128, tm), col),
                   pl.BlockSpec((None, 128, tm), col)),
        compiler_params=_params("parallel"),
        name="in_proj",
    )(x2, sc, sh, g, w_p)


def _nsa_cmp_body(x_ref, wc_ref, pe_ref, w2_ref, o_ref):
    wc = wc_ref[...]
    f = _dot(x_ref[...], wc, precise=True)
    c = _dot(pe_ref[...], wc, precise=True)
    g = f.shape[0]
    second = pltpu.roll(f[:, 128:256], g - 1, 0)
    pre = f[:, 0:128] + second + c[0:1, 0:128] + c[1:2, 128:256]
    act = jax.nn.gelu(pre, approximate=True)
    out = _dot(act, w2_ref[...], precise=True)
    rows = lax.broadcasted_iota(jnp.int32, out.shape, 0)
    o_ref[...] = jnp.where(rows < g - 1, out, 0.0)


def _nsa_cmp(xg, wc, pe8, w2):
    bsz, g, k = xg.shape
    return pl.pallas_call(
        _nsa_cmp_body,
        out_shape=jax.ShapeDtypeStruct((bsz, g, 128), F32),
        grid=(bsz,),
        in_specs=[
            pl.BlockSpec((None, g, k), lambda b: (b, 0, 0)),
            pl.BlockSpec(wc.shape, lambda b: (0, 0)),
            pl.BlockSpec(pe8.shape, lambda b: (0, 0)),
            pl.BlockSpec(w2.shape, lambda b: (0, 0)),
        ],
        out_specs=pl.BlockSpec((None, g, 128), lambda b: (b, 0, 0)),
        compiler_params=_params("parallel"),
        name="nsa_cmp",
    )(xg, wc, pe8, w2)


def _nsa_body(q_ref, gts_ref, kvc_ref, kse_ref, vse_ref, kw_ref, vwe_ref, gb_ref, g_ref, o_ref,
              m_s, acc_s, s_even, s_odd, ocmp, *, n_slc):
    tq = q_ref.shape[0]
    n_cmp = kvc_ref.shape[0]
    qi = pl.program_id(1)
    q0 = qi * tq
    scale = HEAD_DIM ** -0.5
    D = HEAD_DIM
    qt = q_ref[...].T

    kc = kvc_ref[:, 0:D]
    vc = kvc_ref[:, D:2 * D]
    nidx = lax.broadcasted_iota(jnp.int32, (n_cmp, tq), 0)
    pos = q0 + lax.broadcasted_iota(jnp.int32, (n_cmp, tq), 1)
    cmask = (nidx * CMP_STRIDE + (CMP_BLOCK - 1)) <= pos
    heads = range(H_ATT)
    sc = [_dot3(kc, qt[h * D:(h + 1) * D]) for h in heads]
    sc = [jnp.where(cmask, x * scale, NEG) for x in sc]
    ec = [jnp.where(cmask, jnp.exp(x - jnp.max(x, axis=0, keepdims=True)), 0.0) for x in sc]
    pc = [e / jnp.maximum(jnp.sum(e, axis=0, keepdims=True), 1e-30) for e in ec]
    oc = [_dot_tn(vc, p) for p in pc]
    for h in heads:
        ocmp[h * D:(h + 1) * D, :] = oc[h]
    psum = sum(pc[1:], pc[0])

    jrow = lax.broadcasted_iota(jnp.int32, (n_slc, n_cmp), 0) * SLC_BLOCK
    ncol = lax.broadcasted_iota(jnp.int32, (n_slc, n_cmp), 1) * CMP_STRIDE
    overlap_t = jnp.where((ncol < jrow + SLC_BLOCK) & (ncol + CMP_BLOCK > jrow), 1.0, 0.0)
    imp_t = _dot_mask_lhs(overlap_t, psum)
    jj = lax.broadcasted_iota(jnp.int32, (n_slc, tq), 0)
    cur = jnp.right_shift(q0 + lax.broadcasted_iota(jnp.int32, (n_slc, tq), 1), SLC_SHIFT)
    forced = (jj == 0) | (jj == cur) | (jj == cur - 1)
    score = jnp.where(forced, FORCE, jnp.where(jj <= cur, imp_t, NEG))
    group = 8
    cnt = [jnp.zeros((group, tq), F32) for _ in range(n_slc // group)]
    for j2 in range(n_slc):
        row = score[j2:j2 + 1, :]
        for gi in range(n_slc // group):
            sg = score[gi * group:(gi + 1) * group]
            if gi * group > j2:
                hit = jnp.where(row >= sg, 1.0, 0.0)
            elif (gi + 1) * group - 1 <= j2:
                hit = jnp.where(row > sg, 1.0, 0.0)
            else:
                hit = jnp.where(lax.broadcasted_iota(jnp.int32, (group, tq), 0) + gi * group > j2,
                                jnp.where(row >= sg, 1.0, 0.0), jnp.where(row > sg, 1.0, 0.0))
            cnt[gi] = cnt[gi] + hit
    cnt = jnp.concatenate(cnt, axis=0)
    selm = jnp.where(cnt < min(N_SELECT, n_slc), 0.0, -1.0).astype(BF16)

    qs = (qt * (scale * LOG2E)).astype(BF16)
    rhs_w = [qs[h * D:(h + 1) * D] for h in heads]
    rhs_s = [jnp.concatenate([rhs_w[h], selm], axis=0) for h in heads]
    m_s[...] = jnp.full(m_s.shape, NEG, F32)
    acc_s[...] = jnp.zeros_like(acc_s)

    def scores_into(buf, j):
        k = kse_ref[pl.ds(pl.multiple_of(j * TK, TK), TK), :]
        for h in heads:
            buf[h] = jnp.dot(k, rhs_s[h], preferred_element_type=F32)

    def consume(buf, j, mask):
        vt = vse_ref[:, pl.ds(pl.multiple_of(j * TK, TK), TK)]
        for h in heads:
            s = buf[h]
            if mask is not None:
                s = jnp.where(mask, s, NEG)
            m_old = m_s[h:h + 1, :]
            m_new = jnp.maximum(m_old, jnp.max(s, axis=0, keepdims=True))
            p = jnp.exp2(s - m_new).astype(BF16)
            acc_s[h] = jnp.exp2(m_old - m_new) * acc_s[h] + jnp.dot(vt, p, preferred_element_type=F32)
            m_s[h:h + 1, :] = m_new

    kpos = lax.broadcasted_iota(jnp.int32, (TK, tq), 0)
    qpos = lax.broadcasted_iota(jnp.int32, (TK, tq), 1)
    last = jnp.maximum(qi - 1, 0)
    scores_into(s_even, qi)
    scores_into(s_odd, 0)
    consume(s_even, qi, kpos <= qpos)

    def body(jp, carry):
        j = 2 * jp
        scores_into(s_even, jnp.minimum(j + 1, last))
        consume(s_odd, j, None)
        scores_into(s_odd, jnp.minimum(j + 2, last))
        consume(s_even, j + 1, None)
        return carry

    lax.fori_loop(0, qi // 2, body, 0)

    @pl.when(qi % 2 == 1)
    def _():
        consume(s_odd, qi - 1, None)

    span = WINDOW + tq
    w0 = pl.multiple_of(jnp.maximum(q0 - WINDOW, 0), TK)
    kw = kw_ref[pl.ds(w0, span), :]
    vwt = vwe_ref[:, pl.ds(w0, span)]
    kabs = w0 + lax.broadcasted_iota(jnp.int32, (span, tq), 0)
    qabs = q0 + lax.broadcasted_iota(jnp.int32, (span, tq), 1)
    wmask = (kabs <= qabs) & (kabs > qabs - WINDOW)
    sw = [jnp.where(wmask, jnp.dot(kw, rhs_w[h], preferred_element_type=F32), NEG) for h in heads]
    pw = [jnp.exp2(x - jnp.max(x, axis=0, keepdims=True)).astype(BF16) for x in sw]
    acc_w = [jnp.dot(vwt, p, preferred_element_type=F32) for p in pw]

    gate = jax.nn.sigmoid((gts_ref[...] + gb_ref[...]).T)
    outs = []
    for h in heads:
        a_s = acc_s[h]
        a_w = acc_w[h]
        outs.append(gate[h:h + 1] * ocmp[h * D:(h + 1) * D, :]
                    + gate[H_ATT + h:H_ATT + h + 1] * (a_s[0:D] / a_s[D:D + 1])
                    + gate[2 * H_ATT + h:2 * H_ATT + h + 1] * (a_w[0:D] / a_w[D:D + 1]))
    ot = jnp.concatenate(outs, axis=0)
    ot = ot * lax.rsqrt(jnp.mean(ot * ot, axis=0, keepdims=True) + NORM_EPS)
    o_ref[...] = ot.T * g_ref[...]


def _nsa(p_nsa, kv_cmp, kse, vse, kw, vwe, gate_b, out_g, bsz, seq):
    t = p_nsa.shape[0]
    nq = seq // TQ
    n_slc = seq // SLC_BLOCK
    g = kv_cmp.shape[1]
    row = lambda b, i: (b * nq + i, 0)
    per_b = lambda b, i: (b, 0, 0)
    fix = lambda b, i: (0, 0)
    return pl.pallas_call(
        functools.partial(_nsa_body, n_slc=n_slc),
        out_shape=jax.ShapeDtypeStruct((t, C_ATT), F32),
        grid=(bsz, nq),
        in_specs=[
            pl.BlockSpec((TQ, C_ATT), row),
            pl.BlockSpec((TQ, 128), lambda b, i: (b * nq + i, NSA_W // 128 - 1)),
            pl.BlockSpec((None, g, 128), per_b),
            pl.BlockSpec((None, seq, HEAD_DIM + n_slc), per_b),
            pl.BlockSpec((None, 128, seq), per_b),
            pl.BlockSpec((None, seq, HEAD_DIM), per_b),
            pl.BlockSpec((None, 128, seq), per_b),
            pl.BlockSpec((1, 128), fix),
            pl.BlockSpec((1, C_ATT), fix),
        ],
        out_specs=pl.BlockSpec((TQ, C_ATT), row),
        scratch_shapes=[pltpu.VMEM((8, TQ), F32), pltpu.VMEM((H_ATT, 128, TQ), F32),
                        pltpu.VMEM((H_ATT, TK, TQ), F32), pltpu.VMEM((H_ATT, TK, TQ), F32),
                        pltpu.VMEM((C_ATT, TQ), F32)],
        compiler_params=_params("parallel", "arbitrary"),
        name="nsa",
    )(p_nsa, p_nsa, kv_cmp, kse, vse, kw, vwe, gate_b, out_g)


def _tri(n, strict):
    r = lax.broadcasted_iota(jnp.int32, (n, n), 0)
    c = lax.broadcasted_iota(jnp.int32, (n, n), 1)
    return (c < r) if strict else (c <= r)


def _block_diag(x):
    xb = x.astype(BF16)
    first = lax.broadcasted_iota(jnp.int32, xb.shape, 1) < xb.shape[1] // 2
    zero = jnp.zeros_like(xb)
    return jnp.concatenate([jnp.where(first, xb, zero), jnp.where(first, zero, xb)], axis=0)


def _unit_lower_inverse_pairs(ms, eye2):
    n = eye2.shape[0]
    idx = range(len(ms))
    r = lax.broadcasted_iota(jnp.int32, (n, 2 * n), 0)
    c = jnp.bitwise_and(lax.broadcasted_iota(jnp.int32, (n, 2 * n), 1), n - 1)
    base = 8
    diag = jnp.right_shift(r, 3) == jnp.right_shift(c, 3)
    m8 = [jnp.where(diag, m, 0.0) for m in ms]
    t = [eye2 + m for m in m8]
    p2 = [_dot(m, _block_diag(m)) for m in m8]
    t = [t[i] + _dot(p2[i], _block_diag(t[i])) for i in idx]
    p4 = [_dot(p, _block_diag(p)) for p in p2]
    t = [t[i] + _dot(p4[i], _block_diag(t[i])) for i in idx]
    b = base
    while b < n:
        sh = b.bit_length() - 1
        pair = jnp.right_shift(r, sh + 1) == jnp.right_shift(c, sh + 1)
        lower_left = pair & (jnp.right_shift(r, sh) != jnp.right_shift(c, sh))
        left = [_dot(t[i], _block_diag(jnp.where(lower_left, ms[i], 0.0))) for i in idx]
        t = [t[i] + _dot(left[i], _block_diag(t[i])) for i in idx]
        b *= 2
    return t


def _rwkv_body(p_ref, mu_ref, w0_ref, w2_ref, a0_ref, a2_ref, g2_ref, kk_ref, ka_ref, rk_ref, lnw_ref,
               lnb_ref, o_ref, state, prev):
    L = CHUNK
    C = C_RWKV
    tb = p_ref.shape[0]
    chunks = range(tb // L)

    x = p_ref[...]
    rows = lax.broadcasted_iota(jnp.int32, x.shape, 0)
    shifted = jnp.where(rows == 0, prev[0:1, :], pltpu.roll(x, 1, 0))
    prev[0:1, :] = x[tb - 1:tb, :]
    x = x + mu_ref[...] * (shifted - x)
    r = x[:, 0:C]
    k = x[:, C:2 * C]
    v = x[:, 2 * C:3 * C]
    xw = x[:, 3 * C:3 * C + RANK_W]
    xa = x[:, 3 * C + RANK_W:3 * C + RANK_W + RANK_A]
    xg = x[:, 3 * C + RANK_W + RANK_A:]
    w = -jax.nn.softplus(-(w0_ref[...] + _dot3(jnp.tanh(xw), w2_ref[...]))) - 0.5
    logw = -jnp.exp(w)
    a = jax.nn.sigmoid(a0_ref[...] + _dot3(xa, a2_ref[...]))
    g = _dot(jax.nn.sigmoid(xg), g2_ref[...])
    kkf = k * kk_ref[...]
    kmod = k * (1.0 + (a - 1.0) * ka_ref[...])
    rk = r * kmod * rk_ref[...]

    tri_incl = jnp.where(_tri(L, strict=False), 1.0, 0.0)
    cum = jnp.concatenate([_dot_mask_lhs(tri_incl, logw[c * L:(c + 1) * L]) for c in chunks], axis=0)
    cum_last = [cum[(c + 1) * L - 1:(c + 1) * L, :] for c in chunks]
    cum_end = jnp.concatenate([jnp.broadcast_to(cl, (L, C)) for cl in cum_last], axis=0)
    e_pos = jnp.exp(cum)
    e_prev = jnp.exp(cum - logw)
    e_neg = jnp.exp(-cum)
    e_rem = jnp.exp(cum_end - cum)
    N = HEAD_DIM
    PW = 2 * N
    pairs = range(H_RWKV // 2)
    lane_l = lax.broadcasted_iota(jnp.int32, (L, PW), 1)
    row_l = lax.broadcasted_iota(jnp.int32, (L, PW), 0)
    col_l = jnp.bitwise_and(lane_l, N - 1)
    first = lane_l < N
    first2 = lax.broadcasted_iota(jnp.int32, (2 * L, PW), 1) < N
    strict2 = col_l < row_l
    incl2 = col_l <= row_l
    eye2 = jnp.where(col_l == row_l, 1.0, 0.0)

    def halves(x_):
        lane = lax.broadcasted_iota(jnp.int32, x_.shape, 1)
        sa = jnp.sum(jnp.where(lane < N, x_, 0.0), axis=-1, keepdims=True)
        sb = jnp.sum(jnp.where(lane < N, 0.0, x_), axis=-1, keepdims=True)
        return jnp.where(lane < N, sa, sb)

    ps = [slice(p * PW, (p + 1) * PW) for p in pairs]
    kk_p = [kkf[:, sl] for sl in ps]
    kk_p = [x_ / jnp.maximum(jnp.sqrt(halves(x_ * x_)), 1e-12) for x_ in kk_p]
    alpha_f = [-kk_p[p] * e_prev[:, ps[p]] for p in pairs]
    r_f = [r[:, ps[p]] * e_pos[:, ps[p]] for p in pairs]
    beta = [kk_p[p] * a[:, ps[p]] for p in pairs]
    beta_f = [beta[p] * e_neg[:, ps[p]] for p in pairs]
    k_f = [kmod[:, ps[p]] * e_neg[:, ps[p]] for p in pairs]
    beta_e = [beta[p] * e_rem[:, ps[p]] for p in pairs]
    k_e = [kmod[:, ps[p]] * e_rem[:, ps[p]] for p in pairs]
    items = [(c, p) for c in chunks for p in pairs]
    n_items = range(len(items))
    rs = [slice(c * L, (c + 1) * L) for c, _ in items]
    pi = [p for _, p in items]
    lhs = [jnp.concatenate([alpha_f[pi[i]][rs[i]], r_f[pi[i]][rs[i]]], axis=0).astype(BF16) for i in n_items]
    zero16 = jnp.zeros((2 * L, PW), BF16)
    lhs4 = [jnp.concatenate([jnp.where(first2, x_, zero16), jnp.where(first2, zero16, x_)], axis=0) for x_ in lhs]
    bk = [(beta_f[pi[i]][rs[i]].astype(BF16), k_f[pi[i]][rs[i]].astype(BF16)) for i in n_items]
    v_i = [v[rs[i], ps[pi[i]]] for i in n_items]
    out1 = [_dot_nt(lhs4[i], jnp.concatenate([bk[i][0], bk[i][1]], axis=0)) for i in n_items]
    out2 = [_dot_nt(lhs4[i], jnp.concatenate([bk[i][1], bk[i][0]], axis=0)) for i in n_items]
    m_ab = [jnp.where(strict2, jnp.where(first, out1[i][0:L], out2[i][2 * L:3 * L]), 0.0) for i in n_items]
    m_ak = [jnp.where(strict2, jnp.where(first, out2[i][0:L], out1[i][2 * L:3 * L]), 0.0) for i in n_items]
    m_rb = [jnp.where(incl2, jnp.where(first, out1[i][L:2 * L], out2[i][3 * L:4 * L]), 0.0) for i in n_items]
    m_rk = [jnp.where(incl2, jnp.where(first, out2[i][L:2 * L], out1[i][3 * L:4 * L]), 0.0) for i in n_items]
    t_inv = _unit_lower_inverse_pairs(m_ab, eye2)
    mv = [_dot(jnp.concatenate([m_ak[i], m_rk[i]], axis=0), _block_diag(v_i[i])) for i in n_items]
    pq = [_dot(t_inv[i], jnp.concatenate([_block_diag(alpha_f[pi[i]][rs[i]]), _block_diag(mv[i][0:L])], axis=1))
          for i in n_items]
    ry = [_dot(m_rb[i], jnp.concatenate([_block_diag(pq[i][:, 0:PW]), _block_diag(pq[i][:, PW:2 * PW])], axis=1))
          for i in n_items]
    r_eff = [r_f[pi[i]][rs[i]] + ry[i][:, 0:PW] for i in n_items]
    y0 = [ry[i][:, PW:2 * PW] + mv[i][L:2 * L] for i in n_items]
    zero_l = jnp.zeros((L, PW), F32)
    gh = [_dot_tn(jnp.concatenate([beta_e[pi[i]][rs[i]], k_e[pi[i]][rs[i]]], axis=0),
                  jnp.concatenate([pq[i], jnp.concatenate([zero_l, v_i[i]], axis=1)], axis=0)) for i in n_items]
    g_off = [jnp.where(first, gh[i][0:N, 0:PW], gh[i][N:2 * N, 0:PW]) for i in n_items]
    h_add = [jnp.where(first, gh[i][0:N, PW:2 * PW], gh[i][N:2 * N, PW:2 * PW]) for i in n_items]
    gam = [halves(eye2 * jnp.exp(cum_last[c][:, ps[p]])) for c, p in items]
    z = [state[p] for p in pairs]
    y_parts = [[] for _ in pairs]
    for c in chunks:
        idx = [c * len(pairs) + p for p in pairs]
        zb = [_block_diag(z[p]) for p in pairs]
        for p, i in enumerate(idx):
            y_parts[p].append(_dot(r_eff[i], zb[p]) + y0[i])
        z = [z[p] * gam[i] + _dot(g_off[i], zb[p]) + h_add[i] for p, i in enumerate(idx)]
    for p in pairs:
        state[p] = z[p]
    yn = []
    for p in pairs:
        y_p = jnp.concatenate(y_parts[p], axis=0)
        mu_y = halves(y_p) * (1.0 / N)
        dev = y_p - mu_y
        var = halves(dev * dev) * (1.0 / N)
        yn.append(dev * lax.rsqrt(var + RWKV_GN_EPS))
    bonus = [halves(rk[:, ps[p]]) * v[:, ps[p]] for p in pairs]
    yn = jnp.concatenate(yn, axis=-1)
    bonus = jnp.concatenate(bonus, axis=-1)
    o_ref[...] = (yn * lnw_ref[...] + lnb_ref[...] + bonus) * g


N_RWKV_ARGS = 11
N_ML_ARGS = 4


def _recurrent_body(*refs):
    p_rw, rw_args = refs[0], refs[1:1 + N_RWKV_ARGS]
    p_ml, ml_args = refs[1 + N_RWKV_ARGS], refs[2 + N_RWKV_ARGS:2 + N_RWKV_ARGS + N_ML_ARGS]
    o_rw, o_ml, state, prev_rw, cstate, mstate, prev_ml = refs[2 + N_RWKV_ARGS + N_ML_ARGS:]

    @pl.when(pl.program_id(1) == 0)
    def _():
        for ref in (state, prev_rw, cstate, mstate, prev_ml):
            ref[...] = jnp.zeros_like(ref)

    _rwkv_body(p_rw, *rw_args, o_rw, state, prev_rw)
    _mlstm_body(p_ml, *ml_args, o_ml, cstate, mstate, prev_ml)


def _recurrent(p_rw, rw_args, p_ml, ml_args, bsz, seq):
    t = p_rw.shape[0]
    tb = SEQ_BLOCK
    nb = seq // tb
    row = lambda b, c: (b * nb + c, 0)
    fix = lambda b, c: (0, 0)
    assert len(rw_args) == N_RWKV_ARGS and len(ml_args) == N_ML_ARGS
    return pl.pallas_call(
        _recurrent_body,
        out_shape=(jax.ShapeDtypeStruct((t, C_RWKV), F32), jax.ShapeDtypeStruct((t, C_MLSTM), F32)),
        grid=(bsz, nb),
        in_specs=([pl.BlockSpec((tb, RWKV_W), row)] + [pl.BlockSpec(a.shape, fix) for a in rw_args]
                  + [pl.BlockSpec((tb, ML_W), row)] + [pl.BlockSpec(a.shape, fix) for a in ml_args]),
        out_specs=(pl.BlockSpec((tb, C_RWKV), row), pl.BlockSpec((tb, C_MLSTM), row)),
        scratch_shapes=[pltpu.VMEM((H_RWKV // 2, HEAD_DIM, 2 * HEAD_DIM), F32), pltpu.VMEM((8, RWKV_W), F32),
                        pltpu.VMEM((H_MLSTM, 128, HEAD_DIM), F32), pltpu.VMEM((8, 128), F32),
                        pltpu.VMEM((8, 2 * C_MLSTM), F32)],
        compiler_params=_params("parallel", "arbitrary"),
        name="recurrent",
    )(p_rw, *rw_args, p_ml, *ml_args)


def _mlstm_body(p_ref, cw_ref, cb_ref, gb_ref, ng_ref, o_ref, cstate, mstate, prev):
    L = CHUNK
    C = C_MLSTM
    D = HEAD_DIM
    tb = p_ref.shape[0]
    chunks = range(tb // L)

    qk_in = p_ref[:, 0:2 * C]
    pv = prev[...]
    rows8 = lax.broadcasted_iota(jnp.int32, pv.shape, 0)
    conv = qk_in * cw_ref[CONV_WIDTH - 1:CONV_WIDTH, :] + cb_ref[...]
    for d in range(1, CONV_WIDTH):
        rolled = pltpu.roll(qk_in, d, 0)
        top = jnp.where(rows8 < d, pltpu.roll(pv, d, 0), rolled[0:8])
        sh = jnp.concatenate([top, rolled[8:]], axis=0)
        conv = conv + sh * cw_ref[CONV_WIDTH - 1 - d:CONV_WIDTH - d, :]
    prev[...] = qk_in[tb - 8:tb, :]
    qk = conv * jax.nn.sigmoid(conv)
    q = qk[:, 0:C]
    k = qk[:, C:2 * C] * (D ** -0.5)
    v = p_ref[:, 2 * C:3 * C]
    og = p_ref[:, 3 * C:4 * C]
    gates = p_ref[:, 4 * C:4 * C + 128] + gb_ref[...]
    lane = lax.broadcasted_iota(jnp.int32, gates.shape, 1)
    is_f = (lane >= H_MLSTM) & (lane < 2 * H_MLSTM)
    gl = jnp.where(is_f, jax.nn.log_sigmoid(gates), gates)
    tri_incl = jnp.where(_tri(L, strict=False), 1.0, 0.0)
    bcum = [_dot_mask_lhs(tri_incl, gl[c * L:(c + 1) * L]) for c in chunks]
    gl_t = [gl[c * L:(c + 1) * L].T for c in chunks]
    bcum_t = [_dot_nt_mask_rhs(g_, tri_incl) for g_ in gl_t]
    m_all = mstate[...]
    src = lax.broadcasted_iota(jnp.int32, (L, L), 0)
    qry = lax.broadcasted_iota(jnp.int32, (L, L), 1)
    causal_t = src <= qry
    ones_row = jnp.where(lax.broadcasted_iota(jnp.int32, (D, L), 0) == 0, 1.0, 0.0)

    heads = range(H_MLSTM)
    items = [(c, h) for c in chunks for h in heads]
    n_items = range(len(items))
    q_t = [q[c * L:(c + 1) * L].T for c in chunks]
    v_t = [v[c * L:(c + 1) * L].T for c in chunks]
    qt_i = [q_t[c][h * D:(h + 1) * D] for c, h in items]
    vt_ext = [jnp.concatenate([v_t[c][h * D:(h + 1) * D], ones_row], axis=0) for c, h in items]
    k_i = [k[c * L:(c + 1) * L, h * D:(h + 1) * D] for c, h in items]
    c_col = [gl[c * L:(c + 1) * L, h:h + 1] - bcum[c][:, H_MLSTM + h:H_MLSTM + h + 1] for c, h in items]
    b_row = [bcum_t[c][H_MLSTM + h:H_MLSTM + h + 1, :] for c, h in items]
    c_row = [gl_t[c][h:h + 1, :] - b_row[i] for i, (c, h) in enumerate(items)]
    g_tot = [bcum[c][L - 1:L, H_MLSTM + h:H_MLSTM + h + 1] for c, h in items]
    u_row = [g_tot[i] + c_row[i] for i in n_items]
    u_max = [jnp.max(x_, axis=1, keepdims=True) for x_ in u_row]
    m0 = []
    m_run = [m_all[0:1, h:h + 1] for h in heads]
    for c in chunks:
        for h in heads:
            i = c * H_MLSTM + h
            m0.append(m_run[h])
            m_run[h] = jnp.maximum(g_tot[i] + m_run[h], u_max[i])
    m_next = [m0[i + H_MLSTM] if i + H_MLSTM < len(items) else m_run[items[i][1]] for i in n_items]
    kq = [_dot(k_i[i], qt_i[i]) for i in n_items]
    kv = [_dot(vt_ext[i] * jnp.exp(u_row[i] - m_next[i]), k_i[i]) for i in n_items]
    cm = []
    cm_run = [cstate[h] for h in heads]
    for c in chunks:
        for h in heads:
            i = c * H_MLSTM + h
            cm.append(cm_run[h])
            cm_run[h] = jnp.exp(g_tot[i] + m0[i] - m_next[i]) * cm_run[h] + kv[i]
    inter = [_dot(cm[i], qt_i[i]) for i in n_items]
    c_b = [jnp.where(causal_t, jnp.broadcast_to(c_col[i], (L, L)), -jnp.inf) for i in n_items]
    m_row = [jnp.maximum(m0[i], jnp.max(c_b[i], axis=0, keepdims=True)) for i in n_items]
    s_t = [jnp.exp(c_b[i] - m_row[i]) * kq[i] for i in n_items]
    sv = [_dot(vt_ext[i], s_t[i]) for i in n_items]
    lane8 = lax.broadcasted_iota(jnp.int32, m_all.shape, 1)
    for h in heads:
        cstate[h] = cm_run[h]
        m_all = jnp.where(lane8 == h, m_run[h], m_all)
    mstate[...] = m_all
    outs = []
    for i in n_items:
        num = jnp.exp(m0[i] - m_row[i]) * inter[i] + sv[i]
        hh = num[0:D] / jnp.maximum(jnp.abs(num[D:D + 1]), jnp.exp(-(b_row[i] + m_row[i])))
        outs.append(hh * lax.rsqrt(jnp.mean(hh * hh, axis=0, keepdims=True) + NORM_EPS))
    hcat = jnp.concatenate([jnp.concatenate(outs[c * H_MLSTM:(c + 1) * H_MLSTM], axis=0).T for c in chunks], axis=0)
    o_ref[...] = jax.nn.sigmoid(og) * (hcat * ng_ref[...])


def _out_proj_body(a_ref, b_ref, c_ref, wa_ref, wb_ref, wc_ref, x_ref, gt_ref, g_ref, o_ref, *, sub):
    parts = [slice(s * sub, (s + 1) * sub) for s in range(x_ref.shape[0] // sub)]
    y = [jnp.dot(a_ref[p, :].astype(BF16), wa_ref[...], preferred_element_type=F32)
         + jnp.dot(b_ref[p, :].astype(BF16), wb_ref[...], preferred_element_type=F32)
         + jnp.dot(c_ref[p, :].astype(BF16), wc_ref[...], preferred_element_type=F32) for p in parts]
    for p, y_ in zip(parts, y):
        o_ref[p, :] = x_ref[p, :] + gt_ref[...] * _rms(y_, g_ref[...])


def _out_proj(o_nsa, o_rw, o_ml, wa, wb, wc, x2, gt, g, seq):
    t, d = x2.shape
    tm = 1024
    per_b = seq // tm
    row = lambda i: (i, 0)
    fix = lambda i: (0, 0)
    return pl.pallas_call(
        functools.partial(_out_proj_body, sub=tm // 2),
        out_shape=jax.ShapeDtypeStruct((t, d), F32),
        grid=(t // tm,),
        in_specs=[
            pl.BlockSpec((tm, C_ATT), row), pl.BlockSpec((tm, C_RWKV), row), pl.BlockSpec((tm, C_MLSTM), row),
            pl.BlockSpec(wa.shape, fix), pl.BlockSpec(wb.shape, fix), pl.BlockSpec(wc.shape, fix),
            pl.BlockSpec((tm, d), row),
            pl.BlockSpec((None, 1, d), lambda i: (i // per_b, 0, 0)),
            pl.BlockSpec((1, d), fix),
        ],
        out_specs=pl.BlockSpec((tm, d), row),
        compiler_params=_params("parallel"),
        name="out_proj",
    )(o_nsa, o_rw, o_ml, wa, wb, wc, x2, gt, g)


FFN_SLAB = 1024


def _ffn_body(x_ref, sc_ref, sh_ref, gt_ref, gpre_ref, gpost_ref, wg_ref, wu_ref, wd_ref, o_ref, *, sub):
    tm = x_ref.shape[0]
    dff = wg_ref.shape[1]
    parts = [slice(s * sub, (s + 1) * sub) for s in range(tm // sub)]
    slabs = [slice(a, min(a + FFN_SLAB, dff)) for a in range(0, dff, FFN_SLAB)]
    hb = [(_rms(x_ref[p, :], gpre_ref[...]) * (1.0 + sc_ref[...]) + sh_ref[...]).astype(BF16) for p in parts]
    y = [None for _ in parts]
    for sl in slabs:
        gate = [jnp.dot(h, wg_ref[:, sl], preferred_element_type=F32) for h in hb]
        up = [jnp.dot(h, wu_ref[:, sl], preferred_element_type=F32) for h in hb]
        act = [(g_ * jax.nn.sigmoid(g_) * u_).astype(BF16) for g_, u_ in zip(gate, up)]
        down = [jnp.dot(a_, wd_ref[sl, :], preferred_element_type=F32) for a_ in act]
        y = [d_ if y_ is None else y_ + d_ for y_, d_ in zip(y, down)]
    for p, y_ in zip(parts, y):
        o_ref[p, :] = x_ref[p, :] + gt_ref[...] * _rms(y_, gpost_ref[...])


def _ffn(x2, sc, sh, gt, g_pre, g_post, wg, wu, wd, seq):
    t, d = x2.shape
    tm = 1024
    per_b = seq // tm
    row = lambda i: (i, 0)
    bat = lambda i: (i // per_b, 0, 0)
    fix = lambda i: (0, 0)
    resident = lambda a: pl.BlockSpec(a.shape, fix, pipeline_mode=pl.Buffered(1))
    return pl.pallas_call(
        functools.partial(_ffn_body, sub=tm // 2),
        out_shape=jax.ShapeDtypeStruct((t, d), F32),
        grid=(t // tm,),
        in_specs=[
            pl.BlockSpec((tm, d), row),
            pl.BlockSpec((None, 1, d), bat), pl.BlockSpec((None, 1, d), bat), pl.BlockSpec((None, 1, d), bat),
            pl.BlockSpec((1, d), fix), pl.BlockSpec((1, d), fix),
            resident(wg), resident(wu), resident(wd),
        ],
        out_specs=pl.BlockSpec((tm, d), row),
        compiler_params=_params("parallel"),
        name="ffn",
    )(x2, sc, sh, gt, g_pre, g_post, wg, wu, wd)


def _pad_cols(a, width):
    return jnp.pad(a, ((0, 0), (0, width - a.shape[1])))


def _layout_w_in(w_in):
    d_in_rw = NSA_IN + RWKV_W
    nsa = _pad_cols(w_in[:, :NSA_IN], NSA_W)
    rw = w_in[:, NSA_IN:d_in_rw]
    ml = _pad_cols(w_in[:, d_in_rw:], ML_W)
    return jnp.concatenate([nsa, rw, ml], axis=1).astype(BF16)


def _layout_cmp(ck_w1, cv_w1, ck_w2, cv_w2, pe_k, pe_v):
    half = CMP_BLOCK // 2
    ck = ck_w1.reshape(2, half, HEAD_DIM, HEAD_DIM)
    cv = cv_w1.reshape(2, half, HEAD_DIM, HEAD_DIM)
    z = jnp.zeros_like(ck[0])
    top = jnp.concatenate([ck[0], z, ck[1], z], axis=-1)
    bot = jnp.concatenate([z, cv[0], z, cv[1]], axis=-1)
    wc = jnp.concatenate([top, bot], axis=1).reshape(half * 2 * HEAD_DIM, 4 * HEAD_DIM)
    pe = jnp.concatenate([pe_k, pe_v], axis=-1).reshape(2, half * 2 * HEAD_DIM)
    pe8 = jnp.pad(pe, ((0, 6), (0, 0)))
    z2 = jnp.zeros_like(ck_w2)
    w2 = jnp.concatenate([jnp.concatenate([ck_w2, z2], axis=1), jnp.concatenate([z2, cv_w2], axis=1)], axis=0)
    return wc, pe8, w2


def kernel(x, c, w_mod, b_mod, g_pre_mix, g_post_mix, g_pre_ffn, g_post_ffn, w_in, w_out, nsa_pe_k, nsa_pe_v, nsa_ck_w1, nsa_ck_w2, nsa_cv_w1, nsa_cv_w2, nsa_gate_b, nsa_out_g, rw_mu, rw_w0, rw_w2, rw_a0, rw_a2, rw_g2, rw_kk, rw_ka, rw_rk, rw_ln_w, rw_ln_b, ml_conv_w, ml_conv_b, ml_ig_b, ml_fg_b, ml_norm_g, ffn_w_gate, ffn_w_up, ffn_w_down):
    bsz, seq, d = x.shape
    depth = w_mod.shape[0]
    t = bsz * seq
    half = CMP_BLOCK // 2
    mod = _mod(c, w_mod, b_mod)
    x2 = x.reshape(t, d)
    for l in range(depth):
        sh1, sc1, gt1, sh2, sc2, gt2 = [m.reshape(bsz, 1, d) for m in jnp.split(mod[l], 6, axis=-1)]
        p_nsa, p_rw, p_ml, kse, kw, vse, vwe = _in_proj(x2, sc1, sh1, g_pre_mix[l].reshape(1, d),
                                                        _layout_w_in(w_in[l]), bsz, seq)
        wc, pe8, w2 = _layout_cmp(nsa_ck_w1[l], nsa_cv_w1[l], nsa_ck_w2[l], nsa_cv_w2[l], nsa_pe_k[l], nsa_pe_v[l])
        xg = p_nsa[:, C_ATT:C_ATT + 2 * HEAD_DIM].reshape(bsz, seq // half, half * 2 * HEAD_DIM)
        kv_cmp = _nsa_cmp(xg, wc, pe8, w2)
        gate_b = jnp.pad(nsa_gate_b[l], (0, 128 - 3 * H_ATT)).reshape(1, 128)
        o_nsa = _nsa(p_nsa, kv_cmp, kse.reshape(bsz, seq, -1), vse, kw.reshape(bsz, seq, -1), vwe, gate_b,
                     nsa_out_g[l].reshape(1, C_ATT), bsz, seq)
        vec = lambda a: a.reshape(1, -1)
        rw_args = (vec(rw_mu[l]), vec(rw_w0[l]), rw_w2[l], vec(rw_a0[l]), rw_a2[l], rw_g2[l], vec(rw_kk[l]),
                   vec(rw_ka[l]), vec(rw_rk[l]), vec(rw_ln_w[l]), vec(rw_ln_b[l]))
        ml_gate_b = jnp.pad(jnp.concatenate([ml_ig_b[l], ml_fg_b[l]]), (0, 128 - 2 * H_MLSTM)).reshape(1, 128)
        ml_args = (ml_conv_w[l], vec(ml_conv_b[l]), ml_gate_b, vec(ml_norm_g[l]))
        o_rw, o_ml = _recurrent(p_rw, rw_args, p_ml, ml_args, bsz, seq)
        wo = w_out[l].astype(BF16)
        x2 = _out_proj(o_nsa, o_rw, o_ml, wo[:C_ATT], wo[C_ATT:C_ATT + C_RWKV], wo[C_ATT + C_RWKV:], x2, gt1,
                       g_post_mix[l].reshape(1, d), seq)
        x2 = _ffn(x2, sc2, sh2, gt2, g_pre_ffn[l].reshape(1, d), g_post_ffn[l].reshape(1, d),
                  ffn_w_gate[l].astype(BF16), ffn_w_up[l].astype(BF16), ffn_w_down[l].astype(BF16), seq)
    return x2.reshape(bsz, seq, d)
```

```python
import functools

import jax
import jax.numpy as jnp
from jax import lax
from jax.experimental import pallas as pl
from jax.experimental.pallas import tpu as pltpu

F32 = jnp.float32
BF16 = jnp.bfloat16
HIGHEST = lax.Precision.HIGHEST

HEAD_DIM = 64
H_ATT = 4
C_ATT = H_ATT * HEAD_DIM
H_RWKV = 6
C_RWKV = H_RWKV * HEAD_DIM
H_MLSTM = 6
C_MLSTM = H_MLSTM * HEAD_DIM
CMP_BLOCK = 32
CMP_STRIDE = 16
SLC_BLOCK = 64
SLC_SHIFT = 6
N_SELECT = 16
WINDOW = 512
NEG = -1e30
FORCE = 1e9
MASK_BIG = 2.0 ** 100
LOG2E = 1.4426950408889634
RANK_W = 64
RANK_A = 64
RANK_G = 128
RWKV_GN_EPS = 64e-5
CHUNK = 64
SEQ_BLOCK = 512
CONV_WIDTH = 4
NORM_EPS = 1e-6

NSA_W = 768
RWKV_W = 3 * C_RWKV + RANK_W + RANK_A + RANK_G
ML_W = 4 * C_MLSTM + 128
NSA_IN = C_ATT + 6 * HEAD_DIM + 3 * H_ATT
ML_IN = 4 * C_MLSTM + 2 * H_MLSTM

VMEM_LIMIT = 56 * 1024 * 1024

TQ = 256
TK = 256


def _params(*sem):
    return pltpu.CompilerParams(dimension_semantics=sem, vmem_limit_bytes=VMEM_LIMIT)


def _dot(a, b, precise=False):
    if precise:
        return jnp.dot(a.astype(F32), b.astype(F32), preferred_element_type=F32, precision=HIGHEST)
    return jnp.dot(a.astype(BF16), b.astype(BF16), preferred_element_type=F32)


def _dot_nt(a, b, precise=False):
    dn = (((1,), (1,)), ((), ()))
    if precise:
        return lax.dot_general(a.astype(F32), b.astype(F32), dn, preferred_element_type=F32, precision=HIGHEST)
    return lax.dot_general(a.astype(BF16), b.astype(BF16), dn, preferred_element_type=F32)


def _dot_tn(a, b, precise=False):
    dn = (((0,), (0,)), ((), ()))
    if precise:
        return lax.dot_general(a.astype(F32), b.astype(F32), dn, preferred_element_type=F32, precision=HIGHEST)
    return lax.dot_general(a.astype(BF16), b.astype(BF16), dn, preferred_element_type=F32)


def _split3(x):
    hi = x.astype(BF16)
    rest = x - hi.astype(F32)
    mid = rest.astype(BF16)
    lo = (rest - mid.astype(F32)).astype(BF16)
    return hi, mid, lo


def _dot_mask_lhs(mask01, x):
    mb = mask01.astype(BF16)
    hi, mid, lo = _split3(x)
    dot = lambda p: jnp.dot(mb, p, preferred_element_type=F32)
    return (dot(lo) + dot(mid)) + dot(hi)


def _dot_nt_mask_rhs(x, mask01):
    mb = mask01.astype(BF16)
    hi, mid, lo = _split3(x)
    dot = lambda p: lax.dot_general(p, mb, (((1,), (1,)), ((), ())), preferred_element_type=F32)
    return (dot(lo) + dot(mid)) + dot(hi)


def _dot3(a, b):
    a_hi, a_lo, _ = _split3(a)
    b_hi, b_lo, _ = _split3(b)
    dot = lambda p, q: jnp.dot(p, q, preferred_element_type=F32)
    return (dot(a_lo, b_hi) + dot(a_hi, b_lo)) + dot(a_hi, b_hi)


def _rms(x, g):
    return x * lax.rsqrt(jnp.mean(x * x, axis=-1, keepdims=True) + NORM_EPS) * g


def _mod_body(c_ref, w_ref, b_ref, o_ref):
    c = c_ref[...]
    cs = c * jax.nn.sigmoid(c)
    o_ref[...] = _dot(cs, w_ref[...], precise=True) + b_ref[...]


def _mod(c, w_mod, b_mod):
    depth, d, n = w_mod.shape
    bsz = c.shape[0]
    tn = 1536
    return pl.pallas_call(
        _mod_body,
        out_shape=jax.ShapeDtypeStruct((depth, bsz, n), F32),
        grid=(depth, n // tn),
        in_specs=[
            pl.BlockSpec((bsz, d), lambda l, j: (0, 0)),
            pl.BlockSpec((None, d, tn), lambda l, j: (l, 0, j)),
            pl.BlockSpec((None, 1, tn), lambda l, j: (l, 0, j)),
        ],
        out_specs=pl.BlockSpec((None, bsz, tn), lambda l, j: (l, 0, j)),
        compiler_params=_params("parallel", "parallel"),
        name="mod",
    )(c, w_mod, b_mod.reshape(depth, 1, n))


def _in_proj_body(x_ref, sc_ref, sh_ref, g_ref, w_ref, nsa_ref, rw_ref, ml_ref, kse_ref, kw_ref, vse_ref,
                  vwe_ref, *, per_b):
    tm = x_ref.shape[0]
    h = _rms(x_ref[...], g_ref[...]) * (1.0 + sc_ref[...]) + sh_ref[...]
    hb = h.astype(BF16)
    nsa = jnp.dot(hb, w_ref[:, 0:NSA_W], preferred_element_type=F32)
    nsa_ref[...] = nsa
    rw_ref[...] = jnp.dot(hb, w_ref[:, NSA_W:NSA_W + RWKV_W], preferred_element_type=F32)
    ml_ref[...] = jnp.dot(hb, w_ref[:, NSA_W + RWKV_W:], preferred_element_type=F32)
    D = HEAD_DIM
    off = C_ATT + 2 * D
    lane = lax.broadcasted_iota(jnp.int32, (tm, 2 * D), 1)
    pos = (pl.program_id(0) % per_b) * tm + lax.broadcasted_iota(jnp.int32, (tm, 2 * D), 0)
    expand = jnp.where(lane - D == jnp.right_shift(pos, SLC_SHIFT), MASK_BIG, 0.0)
    kse_ref[...] = jnp.where(lane < D, nsa[:, off:off + 2 * D], expand).astype(BF16)
    kw_ref[...] = nsa[:, off + 2 * D:off + 3 * D].astype(BF16)
    tail = jnp.where(lax.broadcasted_iota(jnp.int32, (D, tm), 0) == 0, 1.0, 0.0)
    vse_ref[...] = jnp.concatenate([nsa[:, off + D:off + 2 * D].T, tail], axis=0).astype(BF16)
    vwe_ref[...] = jnp.concatenate([nsa[:, off + 3 * D:off + 4 * D].T, tail], axis=0).astype(BF16)


def _in_proj(x2, sc, sh, g, w_p, bsz, seq):
    t, d = x2.shape
    tm = 512
    per_b = seq // tm
    n_slc = seq // SLC_BLOCK
    assert n_slc == HEAD_DIM, "kse packs the key and one column per selection block into 128 lanes"
    row = lambda i: (i, 0)
    bat = lambda i: (i // per_b, 0, 0)
    fix = lambda i: (0, 0)
    col = lambda i: (i // per_b, 0, i % per_b)
    return pl.pallas_call(
        functools.partial(_in_proj_body, per_b=per_b),
        out_shape=(jax.ShapeDtypeStruct((t, NSA_W), F32),
                   jax.ShapeDtypeStruct((t, RWKV_W), F32),
                   jax.ShapeDtypeStruct((t, ML_W), F32),
                   jax.ShapeDtypeStruct((t, 2 * HEAD_DIM), BF16),
                   jax.ShapeDtypeStruct((t, HEAD_DIM), BF16),
                   jax.ShapeDtypeStruct((bsz, 128, seq), BF16),
                   jax.ShapeDtypeStruct((bsz, 128, seq), BF16)),
        grid=(t // tm,),
        in_specs=[
            pl.BlockSpec((tm, d), row),
            pl.BlockSpec((None, 1, d), bat),
            pl.BlockSpec((None, 1, d), bat),
            pl.BlockSpec((1, d), fix),
            pl.BlockSpec(w_p.shape, fix),
        ],
        out_specs=(pl.BlockSpec((tm, NSA_W), row), pl.BlockSpec((tm, RWKV_W), row),
                   pl.BlockSpec((tm, ML_W), row), pl.BlockSpec((tm, 2 * HEAD_DIM), row),
                   pl.BlockSpec((tm, HEAD_DIM), row), pl.BlockSpec((None, 128, tm), col),
                   pl.BlockSpec((None, 128, tm), col)),
        compiler_params=_params("parallel"),
        name="in_proj",
    )(x2, sc, sh, g, w_p)


def _nsa_cmp_body(x_ref, wc_ref, pe_ref, w2_ref, o_ref):
    half = CMP_BLOCK // 2
    width = x_ref.shape[1]
    g = x_ref.shape[0] // half
    f = None
    for tl in range(half):
        part = _dot(x_ref[pl.ds(tl, g, stride=half), :], wc_ref[tl * width:(tl + 1) * width, :], precise=True)
        f = part if f is None else f + part
    c = _dot(pe_ref[...], wc_ref[...], precise=True)
    second = pltpu.roll(f[:, 128:256], g - 1, 0)
    pre = f[:, 0:128] + second + c[0:1, 0:128] + c[1:2, 128:256]
    act = jax.nn.gelu(pre, approximate=True)
    out = _dot(act, w2_ref[...], precise=True)
    rows = lax.broadcasted_iota(jnp.int32, out.shape, 0)
    o_ref[...] = jnp.where(rows < g - 1, out, 0.0)


def _nsa_cmp(p_nsa, wc, pe8, w2, bsz, seq):
    g = seq // (CMP_BLOCK // 2)
    kc_block = C_ATT // (2 * HEAD_DIM)
    return pl.pallas_call(
        _nsa_cmp_body,
        out_shape=jax.ShapeDtypeStruct((bsz, g, 128), F32),
        grid=(bsz,),
        in_specs=[
            pl.BlockSpec((seq, 2 * HEAD_DIM), lambda b: (b, kc_block)),
            pl.BlockSpec(wc.shape, lambda b: (0, 0)),
            pl.BlockSpec(pe8.shape, lambda b: (0, 0)),
            pl.BlockSpec(w2.shape, lambda b: (0, 0)),
        ],
        out_specs=pl.BlockSpec((None, g, 128), lambda b: (b, 0, 0)),
        compiler_params=_params("parallel"),
        name="nsa_cmp",
    )(p_nsa, wc, pe8, w2)


def _nsa_body(q_ref, gts_ref, kvc_ref, kse_ref, vse_ref, kw_ref, vwe_ref, gb_ref, g_ref, o_ref,
              m_s, acc_s, s_even, s_odd, ocmp, *, n_slc):
    tq = q_ref.shape[0]
    n_cmp = kvc_ref.shape[0]
    qi = pl.program_id(1)
    q0 = qi * tq
    scale = HEAD_DIM ** -0.5
    D = HEAD_DIM
    qt = q_ref[...].T

    kc = kvc_ref[:, 0:D]
    vc = kvc_ref[:, D:2 * D]
    nidx = lax.broadcasted_iota(jnp.int32, (n_cmp, tq), 0)
    pos = q0 + lax.broadcasted_iota(jnp.int32, (n_cmp, tq), 1)
    cmask = (nidx * CMP_STRIDE + (CMP_BLOCK - 1)) <= pos
    heads = range(H_ATT)
    sc = [_dot3(kc, qt[h * D:(h + 1) * D]) for h in heads]
    sc = [jnp.where(cmask, x * scale, NEG) for x in sc]
    ec = [jnp.where(cmask, jnp.exp(x - jnp.max(x, axis=0, keepdims=True)), 0.0) for x in sc]
    pc = [e / jnp.maximum(jnp.sum(e, axis=0, keepdims=True), 1e-30) for e in ec]
    oc = [_dot_tn(vc, p) for p in pc]
    for h in heads:
        ocmp[h * D:(h + 1) * D, :] = oc[h]
    psum = sum(pc[1:], pc[0])

    jrow = lax.broadcasted_iota(jnp.int32, (n_slc, n_cmp), 0) * SLC_BLOCK
    ncol = lax.broadcasted_iota(jnp.int32, (n_slc, n_cmp), 1) * CMP_STRIDE
    overlap_t = jnp.where((ncol < jrow + SLC_BLOCK) & (ncol + CMP_BLOCK > jrow), 1.0, 0.0)
    imp_t = _dot_mask_lhs(overlap_t, psum)
    jj = lax.broadcasted_iota(jnp.int32, (n_slc, tq), 0)
    cur = jnp.right_shift(q0 + lax.broadcasted_iota(jnp.int32, (n_slc, tq), 1), SLC_SHIFT)
    forced = (jj == 0) | (jj == cur) | (jj == cur - 1)
    score = jnp.where(forced, FORCE, jnp.where(jj <= cur, imp_t, NEG))
    group = 8
    cnt = [jnp.zeros((group, tq), F32) for _ in range(n_slc // group)]
    for j2 in range(n_slc):
        row = score[j2:j2 + 1, :]
        for gi in range(n_slc // group):
            sg = score[gi * group:(gi + 1) * group]
            if gi * group > j2:
                hit = jnp.where(row >= sg, 1.0, 0.0)
            elif (gi + 1) * group - 1 <= j2:
                hit = jnp.where(row > sg, 1.0, 0.0)
            else:
                hit = jnp.where(lax.broadcasted_iota(jnp.int32, (group, tq), 0) + gi * group > j2,
                                jnp.where(row >= sg, 1.0, 0.0), jnp.where(row > sg, 1.0, 0.0))
            cnt[gi] = cnt[gi] + hit
    cnt = jnp.concatenate(cnt, axis=0)
    selm = jnp.where(cnt < min(N_SELECT, n_slc), 0.0, -1.0).astype(BF16)

    qs = (qt * (scale * LOG2E)).astype(BF16)
    rhs_w = [qs[h * D:(h + 1) * D] for h in heads]
    rhs_s = [jnp.concatenate([rhs_w[h], selm], axis=0) for h in heads]
    m_s[...] = jnp.full(m_s.shape, NEG, F32)
    acc_s[...] = jnp.zeros_like(acc_s)

    def scores_into(buf, j):
        k = kse_ref[pl.ds(pl.multiple_of(j * TK, TK), TK), :]
        for h in heads:
            buf[h] = jnp.dot(k, rhs_s[h], preferred_element_type=F32)

    def consume(buf, j, mask):
        vt = vse_ref[:, pl.ds(pl.multiple_of(j * TK, TK), TK)]
        for h in heads:
            s = buf[h]
            if mask is not None:
                s = jnp.where(mask, s, NEG)
            m_old = m_s[h:h + 1, :]
            m_new = jnp.maximum(m_old, jnp.max(s, axis=0, keepdims=True))
            p = jnp.exp2(s - m_new).astype(BF16)
            acc_s[h] = jnp.exp2(m_old - m_new) * acc_s[h] + jnp.dot(vt, p, preferred_element_type=F32)
            m_s[h:h + 1, :] = m_new

    kpos = lax.broadcasted_iota(jnp.int32, (TK, tq), 0)
    qpos = lax.broadcasted_iota(jnp.int32, (TK, tq), 1)
    last = jnp.maximum(qi - 1, 0)
    scores_into(s_even, qi)
    scores_into(s_odd, 0)
    consume(s_even, qi, kpos <= qpos)

    def body(jp, carry):
        j = 2 * jp
        scores_into(s_even, jnp.minimum(j + 1, last))
        consume(s_odd, j, None)
        scores_into(s_odd, jnp.minimum(j + 2, last))
        consume(s_even, j + 1, None)
        return carry

    lax.fori_loop(0, qi // 2, body, 0)

    @pl.when(qi % 2 == 1)
    def _():
        consume(s_odd, qi - 1, None)

    span = WINDOW + tq
    w0 = pl.multiple_of(jnp.maximum(q0 - WINDOW, 0), TK)
    kw = kw_ref[pl.ds(w0, span), :]
    vwt = vwe_ref[:, pl.ds(w0, span)]
    kabs = w0 + lax.broadcasted_iota(jnp.int32, (span, tq), 0)
    qabs = q0 + lax.broadcasted_iota(jnp.int32, (span, tq), 1)
    wmask = (kabs <= qabs) & (kabs > qabs - WINDOW)
    sw = [jnp.where(wmask, jnp.dot(kw, rhs_w[h], preferred_element_type=F32), NEG) for h in heads]
    pw = [jnp.exp2(x - jnp.max(x, axis=0, keepdims=True)).astype(BF16) for x in sw]
    acc_w = [jnp.dot(vwt, p, preferred_element_type=F32) for p in pw]

    gate = jax.nn.sigmoid((gts_ref[...] + gb_ref[...]).T)
    outs = []
    for h in heads:
        a_s = acc_s[h]
        a_w = acc_w[h]
        outs.append(gate[h:h + 1] * ocmp[h * D:(h + 1) * D, :]
                    + gate[H_ATT + h:H_ATT + h + 1] * (a_s[0:D] / a_s[D:D + 1])
                    + gate[2 * H_ATT + h:2 * H_ATT + h + 1] * (a_w[0:D] / a_w[D:D + 1]))
    ot = jnp.concatenate(outs, axis=0)
    ot = ot * lax.rsqrt(jnp.mean(ot * ot, axis=0, keepdims=True) + NORM_EPS)
    o_ref[...] = ot.T * g_ref[...]


def _nsa(p_nsa, kv_cmp, kse, vse, kw, vwe, gate_b, out_g, bsz, seq):
    t = p_nsa.shape[0]
    nq = seq // TQ
    n_slc = seq // SLC_BLOCK
    g = kv_cmp.shape[1]
    row = lambda b, i: (b * nq + i, 0)
    per_b = lambda b, i: (b, 0, 0)
    fix = lambda b, i: (0, 0)
    return pl.pallas_call(
        functools.partial(_nsa_body, n_slc=n_slc),
        out_shape=jax.ShapeDtypeStruct((t, C_ATT), F32),
        grid=(bsz, nq),
        in_specs=[
            pl.BlockSpec((TQ, C_ATT), row),
            pl.BlockSpec((TQ, 128), lambda b, i: (b * nq + i, NSA_W // 128 - 1)),
            pl.BlockSpec((None, g, 128), per_b),
            pl.BlockSpec((None, seq, HEAD_DIM + n_slc), per_b),
            pl.BlockSpec((None, 128, seq), per_b),
            pl.BlockSpec((None, seq, HEAD_DIM), per_b),
            pl.BlockSpec((None, 128, seq), per_b),
            pl.BlockSpec((1, 128), fix),
            pl.BlockSpec((1, C_ATT), fix),
        ],
        out_specs=pl.BlockSpec((TQ, C_ATT), row),
        scratch_shapes=[pltpu.VMEM((8, TQ), F32), pltpu.VMEM((H_ATT, 128, TQ), F32),
                        pltpu.VMEM((H_ATT, TK, TQ), F32), pltpu.VMEM((H_ATT, TK, TQ), F32),
                        pltpu.VMEM((C_ATT, TQ), F32)],
        compiler_params=_params("parallel", "arbitrary"),
        name="nsa",
    )(p_nsa, p_nsa, kv_cmp, kse, vse, kw, vwe, gate_b, out_g)


def _tri(n, strict):
    r = lax.broadcasted_iota(jnp.int32, (n, n), 0)
    c = lax.broadcasted_iota(jnp.int32, (n, n), 1)
    return (c < r) if strict else (c <= r)


def _block_diag(x):
    xb = x.astype(BF16)
    first = lax.broadcasted_iota(jnp.int32, xb.shape, 1) < xb.shape[1] // 2
    zero = jnp.zeros_like(xb)
    return jnp.concatenate([jnp.where(first, xb, zero), jnp.where(first, zero, xb)], axis=0)


def _unit_lower_inverse_pairs(ms, eye2):
    n = eye2.shape[0]
    idx = range(len(ms))
    r = lax.broadcasted_iota(jnp.int32, (n, 2 * n), 0)
    c = jnp.bitwise_and(lax.broadcasted_iota(jnp.int32, (n, 2 * n), 1), n - 1)
    base = 8
    diag = jnp.right_shift(r, 3) == jnp.right_shift(c, 3)
    m8 = [jnp.where(diag, m, 0.0) for m in ms]
    t = [eye2 + m for m in m8]
    p2 = [_dot(m, _block_diag(m)) for m in m8]
    t = [t[i] + _dot(p2[i], _block_diag(t[i])) for i in idx]
    p4 = [_dot(p, _block_diag(p)) for p in p2]
    t = [t[i] + _dot(p4[i], _block_diag(t[i])) for i in idx]
    b = base
    while b < n:
        sh = b.bit_length() - 1
        pair = jnp.right_shift(r, sh + 1) == jnp.right_shift(c, sh + 1)
        lower_left = pair & (jnp.right_shift(r, sh) != jnp.right_shift(c, sh))
        left = [_dot(t[i], _block_diag(jnp.where(lower_left, ms[i], 0.0))) for i in idx]
        t = [t[i] + _dot(left[i], _block_diag(t[i])) for i in idx]
        b *= 2
    return t


def _rwkv_body(p_ref, mu_ref, w0_ref, w2_ref, a0_ref, a2_ref, g2_ref, kk_ref, ka_ref, rk_ref, lnw_ref,
               lnb_ref, o_ref, state, prev):
    L = CHUNK
    C = C_RWKV
    tb = p_ref.shape[0]
    chunks = range(tb // L)

    x = p_ref[...]
    rows = lax.broadcasted_iota(jnp.int32, x.shape, 0)
    shifted = jnp.where(rows == 0, prev[0:1, :], pltpu.roll(x, 1, 0))
    prev[0:1, :] = x[tb - 1:tb, :]
    x = x + mu_ref[...] * (shifted - x)
    r = x[:, 0:C]
    k = x[:, C:2 * C]
    v = x[:, 2 * C:3 * C]
    xw = x[:, 3 * C:3 * C + RANK_W]
    xa = x[:, 3 * C + RANK_W:3 * C + RANK_W + RANK_A]
    xg = x[:, 3 * C + RANK_W + RANK_A:]
    w = -jax.nn.softplus(-(w0_ref[...] + _dot3(jnp.tanh(xw), w2_ref[...]))) - 0.5
    logw = -jnp.exp(w)
    a = jax.nn.sigmoid(a0_ref[...] + _dot3(xa, a2_ref[...]))
    g = _dot(jax.nn.sigmoid(xg), g2_ref[...])
    kkf = k * kk_ref[...]
    kmod = k * (1.0 + (a - 1.0) * ka_ref[...])
    rk = r * kmod * rk_ref[...]

    tri_incl = jnp.where(_tri(L, strict=False), 1.0, 0.0)
    cum = jnp.concatenate([_dot_mask_lhs(tri_incl, logw[c * L:(c + 1) * L]) for c in chunks], axis=0)
    cum_last = [cum[(c + 1) * L - 1:(c + 1) * L, :] for c in chunks]
    cum_end = jnp.concatenate([jnp.broadcast_to(cl, (L, C)) for cl in cum_last], axis=0)
    e_pos = jnp.exp(cum)
    e_prev = jnp.exp(cum - logw)
    e_neg = jnp.exp(-cum)
    e_rem = jnp.exp(cum_end - cum)
    N = HEAD_DIM
    PW = 2 * N
    pairs = range(H_RWKV // 2)
    lane_l = lax.broadcasted_iota(jnp.int32, (L, PW), 1)
    row_l = lax.broadcasted_iota(jnp.int32, (L, PW), 0)
    col_l = jnp.bitwise_and(lane_l, N - 1)
    first = lane_l < N
    first2 = lax.broadcasted_iota(jnp.int32, (2 * L, PW), 1) < N
    strict2 = col_l < row_l
    incl2 = col_l <= row_l
    eye2 = jnp.where(col_l == row_l, 1.0, 0.0)

    def halves(x_):
        lane = lax.broadcasted_iota(jnp.int32, x_.shape, 1)
        sa = jnp.sum(jnp.where(lane < N, x_, 0.0), axis=-1, keepdims=True)
        sb = jnp.sum(jnp.where(lane < N, 0.0, x_), axis=-1, keepdims=True)
        return jnp.where(lane < N, sa, sb)

    ps = [slice(p * PW, (p + 1) * PW) for p in pairs]
    kk_p = [kkf[:, sl] for sl in ps]
    kk_p = [x_ / jnp.maximum(jnp.sqrt(halves(x_ * x_)), 1e-12) for x_ in kk_p]
    alpha_f = [-kk_p[p] * e_prev[:, ps[p]] for p in pairs]
    r_f = [r[:, ps[p]] * e_pos[:, ps[p]] for p in pairs]
    beta = [kk_p[p] * a[:, ps[p]] for p in pairs]
    beta_f = [beta[p] * e_neg[:, ps[p]] for p in pairs]
    k_f = [kmod[:, ps[p]] * e_neg[:, ps[p]] for p in pairs]
    beta_e = [beta[p] * e_rem[:, ps[p]] for p in pairs]
    k_e = [kmod[:, ps[p]] * e_rem[:, ps[p]] for p in pairs]
    items = [(c, p) for c in chunks for p in pairs]
    n_items = range(len(items))
    rs = [slice(c * L, (c + 1) * L) for c, _ in items]
    pi = [p for _, p in items]
    lhs = [jnp.concatenate([alpha_f[pi[i]][rs[i]], r_f[pi[i]][rs[i]]], axis=0).astype(BF16) for i in n_items]
    zero16 = jnp.zeros((2 * L, PW), BF16)
    lhs4 = [jnp.concatenate([jnp.where(first2, x_, zero16), jnp.where(first2, zero16, x_)], axis=0) for x_ in lhs]
    bk = [(beta_f[pi[i]][rs[i]].astype(BF16), k_f[pi[i]][rs[i]].astype(BF16)) for i in n_items]
    v_i = [v[rs[i], ps[pi[i]]] for i in n_items]
    out1 = [_dot_nt(lhs4[i], jnp.concatenate([bk[i][0], bk[i][1]], axis=0)) for i in n_items]
    out2 = [_dot_nt(lhs4[i], jnp.concatenate([bk[i][1], bk[i][0]], axis=0)) for i in n_items]
    m_ab = [jnp.where(strict2, jnp.where(first, out1[i][0:L], out2[i][2 * L:3 * L]), 0.0) for i in n_items]
    m_ak = [jnp.where(strict2, jnp.where(first, out2[i][0:L], out1[i][2 * L:3 * L]), 0.0) for i in n_items]
    m_rb = [jnp.where(incl2, jnp.where(first, out1[i][L:2 * L], out2[i][3 * L:4 * L]), 0.0) for i in n_items]
    m_rk = [jnp.where(incl2, jnp.where(first, out2[i][L:2 * L], out1[i][3 * L:4 * L]), 0.0) for i in n_items]
    t_inv = _unit_lower_inverse_pairs(m_ab, eye2)
    mv = [_dot(jnp.concatenate([m_ak[i], m_rk[i]], axis=0), _block_diag(v_i[i])) for i in n_items]
    pq = [_dot(t_inv[i], jnp.concatenate([_block_diag(alpha_f[pi[i]][rs[i]]), _block_diag(mv[i][0:L])], axis=1))
          for i in n_items]
    ry = [_dot(m_rb[i], jnp.concatenate([_block_diag(pq[i][:, 0:PW]), _block_diag(pq[i][:, PW:2 * PW])], axis=1))
          for i in n_items]
    r_eff = [r_f[pi[i]][rs[i]] + ry[i][:, 0:PW] for i in n_items]
    y0 = [ry[i][:, PW:2 * PW] + mv[i][L:2 * L] for i in n_items]
    zero_l = jnp.zeros((L, PW), F32)
    gh = [_dot_tn(jnp.concatenate([beta_e[pi[i]][rs[i]], k_e[pi[i]][rs[i]]], axis=0),
                  jnp.concatenate([pq[i], jnp.concatenate([zero_l, v_i[i]], axis=1)], axis=0)) for i in n_items]
    g_off = [jnp.where(first, gh[i][0:N, 0:PW], gh[i][N:2 * N, 0:PW]) for i in n_items]
    h_add = [jnp.where(first, gh[i][0:N, PW:2 * PW], gh[i][N:2 * N, PW:2 * PW]) for i in n_items]
    gam = [halves(eye2 * jnp.exp(cum_last[c][:, ps[p]])) for c, p in items]
    z = [state[p] for p in pairs]
    y_parts = [[] for _ in pairs]
    for c in chunks:
        idx = [c * len(pairs) + p for p in pairs]
        zb = [_block_diag(z[p]) for p in pairs]
        for p, i in enumerate(idx):
            y_parts[p].append(_dot(r_eff[i], zb[p]) + y0[i])
        z = [z[p] * gam[i] + _dot(g_off[i], zb[p]) + h_add[i] for p, i in enumerate(idx)]
    for p in pairs:
        state[p] = z[p]
    yn = []
    for p in pairs:
        y_p = jnp.concatenate(y_parts[p], axis=0)
        mu_y = halves(y_p) * (1.0 / N)
        dev = y_p - mu_y
        var = halves(dev * dev) * (1.0 / N)
        yn.append(dev * lax.rsqrt(var + RWKV_GN_EPS))
    bonus = [halves(rk[:, ps[p]]) * v[:, ps[p]] for p in pairs]
    yn = jnp.concatenate(yn, axis=-1)
    bonus = jnp.concatenate(bonus, axis=-1)
    o_ref[...] = (yn * lnw_ref[...] + lnb_ref[...] + bonus) * g


N_RWKV_ARGS = 11
N_ML_ARGS = 4


def _recurrent_body(*refs):
    p_rw, rw_args = refs[0], refs[1:1 + N_RWKV_ARGS]
    p_ml, ml_args = refs[1 + N_RWKV_ARGS], refs[2 + N_RWKV_ARGS:2 + N_RWKV_ARGS + N_ML_ARGS]
    o_rw, o_ml, state, prev_rw, cstate, mstate, prev_ml = refs[2 + N_RWKV_ARGS + N_ML_ARGS:]

    @pl.when(pl.program_id(1) == 0)
    def _():
        for ref in (state, prev_rw, cstate, mstate, prev_ml):
            ref[...] = jnp.zeros_like(ref)

    _rwkv_body(p_rw, *rw_args, o_rw, state, prev_rw)
    _mlstm_body(p_ml, *ml_args, o_ml, cstate, mstate, prev_ml)


def _recurrent(p_rw, rw_args, p_ml, ml_args, bsz, seq):
    t = p_rw.shape[0]
    tb = SEQ_BLOCK
    nb = seq // tb
    row = lambda b, c: (b * nb + c, 0)
    fix = lambda b, c: (0, 0)
    assert len(rw_args) == N_RWKV_ARGS and len(ml_args) == N_ML_ARGS
    return pl.pallas_call(
        _recurrent_body,
        out_shape=(jax.ShapeDtypeStruct((t, C_RWKV), F32), jax.ShapeDtypeStruct((t, C_MLSTM), F32)),
        grid=(bsz, nb),
        in_specs=([pl.BlockSpec((tb, RWKV_W), row)] + [pl.BlockSpec(a.shape, fix) for a in rw_args]
                  + [pl.BlockSpec((tb, ML_W), row)] + [pl.BlockSpec(a.shape, fix) for a in ml_args]),
        out_specs=(pl.BlockSpec((tb, C_RWKV), row), pl.BlockSpec((tb, C_MLSTM), row)),
        scratch_shapes=[pltpu.VMEM((H_RWKV // 2, HEAD_DIM, 2 * HEAD_DIM), F32), pltpu.VMEM((8, RWKV_W), F32),
                        pltpu.VMEM((H_MLSTM, 128, HEAD_DIM), F32), pltpu.VMEM((8, 128), F32),
                        pltpu.VMEM((8, 2 * C_MLSTM), F32)],
        compiler_params=_params("parallel", "arbitrary"),
        name="recurrent",
    )(p_rw, *rw_args, p_ml, *ml_args)


def _mlstm_body(p_ref, cw_ref, cb_ref, gb_ref, ng_ref, o_ref, cstate, mstate, prev):
    L = CHUNK
    C = C_MLSTM
    D = HEAD_DIM
    tb = p_ref.shape[0]
    chunks = range(tb // L)

    qk_in = p_ref[:, 0:2 * C]
    pv = prev[...]
    rows8 = lax.broadcasted_iota(jnp.int32, pv.shape, 0)
    conv = qk_in * cw_ref[CONV_WIDTH - 1:CONV_WIDTH, :] + cb_ref[...]
    for d in range(1, CONV_WIDTH):
        rolled = pltpu.roll(qk_in, d, 0)
        top = jnp.where(rows8 < d, pltpu.roll(pv, d, 0), rolled[0:8])
        sh = jnp.concatenate([top, rolled[8:]], axis=0)
        conv = conv + sh * cw_ref[CONV_WIDTH - 1 - d:CONV_WIDTH - d, :]
    prev[...] = qk_in[tb - 8:tb, :]
    qk = conv * jax.nn.sigmoid(conv)
    q = qk[:, 0:C]
    k = qk[:, C:2 * C] * (D ** -0.5)
    v = p_ref[:, 2 * C:3 * C]
    og = p_ref[:, 3 * C:4 * C]
    gates = p_ref[:, 4 * C:4 * C + 128] + gb_ref[...]
    lane = lax.broadcasted_iota(jnp.int32, gates.shape, 1)
    is_f = (lane >= H_MLSTM) & (lane < 2 * H_MLSTM)
    gl = jnp.where(is_f, jax.nn.log_sigmoid(gates), gates)
    tri_incl = jnp.where(_tri(L, strict=False), 1.0, 0.0)
    bcum = [_dot_mask_lhs(tri_incl, gl[c * L:(c + 1) * L]) for c in chunks]
    gl_t = [gl[c * L:(c + 1) * L].T for c in chunks]
    bcum_t = [_dot_nt_mask_rhs(g_, tri_incl) for g_ in gl_t]
    m_all = mstate[...]
    src = lax.broadcasted_iota(jnp.int32, (L, L), 0)
    qry = lax.broadcasted_iota(jnp.int32, (L, L), 1)
    causal_t = src <= qry
    ones_row = jnp.where(lax.broadcasted_iota(jnp.int32, (D, L), 0) == 0, 1.0, 0.0)

    heads = range(H_MLSTM)
    items = [(c, h) for c in chunks for h in heads]
    n_items = range(len(items))
    q_t = [q[c * L:(c + 1) * L].T for c in chunks]
    v_t = [v[c * L:(c + 1) * L].T for c in chunks]
    qt_i = [q_t[c][h * D:(h + 1) * D] for c, h in items]
    vt_ext = [jnp.concatenate([v_t[c][h * D:(h + 1) * D], ones_row], axis=0) for c, h in items]
    k_i = [k[c * L:(c + 1) * L, h * D:(h + 1) * D] for c, h in items]
    c_col = [gl[c * L:(c + 1) * L, h:h + 1] - bcum[c][:, H_MLSTM + h:H_MLSTM + h + 1] for c, h in items]
    b_row = [bcum_t[c][H_MLSTM + h:H_MLSTM + h + 1, :] for c, h in items]
    c_row = [gl_t[c][h:h + 1, :] - b_row[i] for i, (c, h) in enumerate(items)]
    g_tot = [bcum[c][L - 1:L, H_MLSTM + h:H_MLSTM + h + 1] for c, h in items]
    u_row = [g_tot[i] + c_row[i] for i in n_items]
    u_max = [jnp.max(x_, axis=1, keepdims=True) for x_ in u_row]
    m0 = []
    m_run = [m_all[0:1, h:h + 1] for h in heads]
    for c in chunks:
        for h in heads:
            i = c * H_MLSTM + h
            m0.append(m_run[h])
            m_run[h] = jnp.maximum(g_tot[i] + m_run[h], u_max[i])
    m_next = [m0[i + H_MLSTM] if i + H_MLSTM < len(items) else m_run[items[i][1]] for i in n_items]
    kq = [_dot(k_i[i], qt_i[i]) for i in n_items]
    kv = [_dot(vt_ext[i] * jnp.exp(u_row[i] - m_next[i]), k_i[i]) for i in n_items]
    cm = []
    cm_run = [cstate[h] for h in heads]
    for c in chunks:
        for h in heads:
            i = c * H_MLSTM + h
            cm.append(cm_run[h])
            cm_run[h] = jnp.exp(g_tot[i] + m0[i] - m_next[i]) * cm_run[h] + kv[i]
    inter = [_dot(cm[i], qt_i[i]) for i in n_items]
    c_b = [jnp.where(causal_t, jnp.broadcast_to(c_col[i], (L, L)), -jnp.inf) for i in n_items]
    m_row = [jnp.maximum(m0[i], jnp.max(c_b[i], axis=0, keepdims=True)) for i in n_items]
    s_t = [jnp.exp(c_b[i] - m_row[i]) * kq[i] for i in n_items]
    sv = [_dot(vt_ext[i], s_t[i]) for i in n_items]
    lane8 = lax.broadcasted_iota(jnp.int32, m_all.shape, 1)
    for h in heads:
        cstate[h] = cm_run[h]
        m_all = jnp.where(lane8 == h, m_run[h], m_all)
    mstate[...] = m_all
    outs = []
    for i in n_items:
        num = jnp.exp(m0[i] - m_row[i]) * inter[i] + sv[i]
        hh = num[0:D] / jnp.maximum(jnp.abs(num[D:D + 1]), jnp.exp(-(b_row[i] + m_row[i])))
        outs.append(hh * lax.rsqrt(jnp.mean(hh * hh, axis=0, keepdims=True) + NORM_EPS))
    hcat = jnp.concatenate([jnp.concatenate(outs[c * H_MLSTM:(c + 1) * H_MLSTM], axis=0).T for c in chunks], axis=0)
    o_ref[...] = jax.nn.sigmoid(og) * (hcat * ng_ref[...])


def _out_proj_body(a_ref, b_ref, c_ref, wa_ref, wb_ref, wc_ref, x_ref, gt_ref, g_ref, o_ref, *, sub):
    parts = [slice(s * sub, (s + 1) * sub) for s in range(x_ref.shape[0] // sub)]
    y = [jnp.dot(a_ref[p, :].astype(BF16), wa_ref[...], preferred_element_type=F32)
         + jnp.dot(b_ref[p, :].astype(BF16), wb_ref[...], preferred_element_type=F32)
         + jnp.dot(c_ref[p, :].astype(BF16), wc_ref[...], preferred_element_type=F32) for p in parts]
    for p, y_ in zip(parts, y):
        o_ref[p, :] = x_ref[p, :] + gt_ref[...] * _rms(y_, g_ref[...])


def _out_proj(o_nsa, o_rw, o_ml, wa, wb, wc, x2, gt, g, seq):
    t, d = x2.shape
    tm = 1024
    per_b = seq // tm
    row = lambda i: (i, 0)
    fix = lambda i: (0, 0)
    return pl.pallas_call(
        functools.partial(_out_proj_body, sub=tm // 2),
        out_shape=jax.ShapeDtypeStruct((t, d), F32),
        grid=(t // tm,),
        in_specs=[
            pl.BlockSpec((tm, C_ATT), row), pl.BlockSpec((tm, C_RWKV), row), pl.BlockSpec((tm, C_MLSTM), row),
            pl.BlockSpec(wa.shape, fix), pl.BlockSpec(wb.shape, fix), pl.BlockSpec(wc.shape, fix),
            pl.BlockSpec((tm, d), row),
            pl.BlockSpec((None, 1, d), lambda i: (i // per_b, 0, 0)),
            pl.BlockSpec((1, d), fix),
        ],
        out_specs=pl.BlockSpec((tm, d), row),
        compiler_params=_params("parallel"),
        name="out_proj",
    )(o_nsa, o_rw, o_ml, wa, wb, wc, x2, gt, g)


FFN_SLAB = 1024


def _ffn_body(x_ref, sc_ref, sh_ref, gt_ref, gpre_ref, gpost_ref, wg_ref, wu_ref, wd_ref, o_ref, *, sub):
    tm = x_ref.shape[0]
    dff = wg_ref.shape[1]
    parts = [slice(s * sub, (s + 1) * sub) for s in range(tm // sub)]
    slabs = [slice(a, min(a + FFN_SLAB, dff)) for a in range(0, dff, FFN_SLAB)]
    hb = [(_rms(x_ref[p, :], gpre_ref[...]) * (1.0 + sc_ref[...]) + sh_ref[...]).astype(BF16) for p in parts]
    y = [None for _ in parts]
    for sl in slabs:
        gate = [jnp.dot(h, wg_ref[:, sl], preferred_element_type=F32) for h in hb]
        up = [jnp.dot(h, wu_ref[:, sl], preferred_element_type=F32) for h in hb]
        act = [(g_ * jax.nn.sigmoid(g_) * u_).astype(BF16) for g_, u_ in zip(gate, up)]
        down = [jnp.dot(a_, wd_ref[sl, :], preferred_element_type=F32) for a_ in act]
        y = [d_ if y_ is None else y_ + d_ for y_, d_ in zip(y, down)]
    for p, y_ in zip(parts, y):
        o_ref[p, :] = x_ref[p, :] + gt_ref[...] * _rms(y_, gpost_ref[...])


def _ffn(x2, sc, sh, gt, g_pre, g_post, wg, wu, wd, seq):
    t, d = x2.shape
    tm = 1024
    per_b = seq // tm
    row = lambda i: (i, 0)
    bat = lambda i: (i // per_b, 0, 0)
    fix = lambda i: (0, 0)
    resident = lambda a: pl.BlockSpec(a.shape, fix, pipeline_mode=pl.Buffered(1))
    return pl.pallas_call(
        functools.partial(_ffn_body, sub=tm // 2),
        out_shape=jax.ShapeDtypeStruct((t, d), F32),
        grid=(t // tm,),
        in_specs=[
            pl.BlockSpec((tm, d), row),
            pl.BlockSpec((None, 1, d), bat), pl.BlockSpec((None, 1, d), bat), pl.BlockSpec((None, 1, d), bat),
            pl.BlockSpec((1, d), fix), pl.BlockSpec((1, d), fix),
            resident(wg), resident(wu), resident(wd),
        ],
        out_specs=pl.BlockSpec((tm, d), row),
        compiler_params=_params("parallel"),
        name="ffn",
    )(x2, sc, sh, gt, g_pre, g_post, wg, wu, wd)


def _pad_cols(a, width):
    return jnp.pad(a, ((0, 0), (0, width - a.shape[1])))


def _layout_w_in(w_in):
    d_in_rw = NSA_IN + RWKV_W
    w16 = w_in.astype(BF16)
    nsa = _pad_cols(w16[:, :NSA_IN], NSA_W)
    rw = w16[:, NSA_IN:d_in_rw]
    ml = _pad_cols(w16[:, d_in_rw:], ML_W)
    return jnp.concatenate([nsa, rw, ml], axis=1)


def _layout_cmp(ck_w1, cv_w1, ck_w2, cv_w2, pe_k, pe_v):
    half = CMP_BLOCK // 2
    ck = ck_w1.reshape(2, half, HEAD_DIM, HEAD_DIM)
    cv = cv_w1.reshape(2, half, HEAD_DIM, HEAD_DIM)
    z = jnp.zeros_like(ck[0])
    top = jnp.concatenate([ck[0], z, ck[1], z], axis=-1)
    bot = jnp.concatenate([z, cv[0], z, cv[1]], axis=-1)
    wc = jnp.concatenate([top, bot], axis=1).reshape(half * 2 * HEAD_DIM, 4 * HEAD_DIM)
    pe = jnp.concatenate([pe_k, pe_v], axis=-1).reshape(2, half * 2 * HEAD_DIM)
    pe8 = jnp.pad(pe, ((0, 6), (0, 0)))
    z2 = jnp.zeros_like(ck_w2)
    w2 = jnp.concatenate([jnp.concatenate([ck_w2, z2], axis=1), jnp.concatenate([z2, cv_w2], axis=1)], axis=0)
    return wc, pe8, w2


def kernel(x, c, w_mod, b_mod, g_pre_mix, g_post_mix, g_pre_ffn, g_post_ffn, w_in, w_out, nsa_pe_k, nsa_pe_v, nsa_ck_w1, nsa_ck_w2, nsa_cv_w1, nsa_cv_w2, nsa_gate_b, nsa_out_g, rw_mu, rw_w0, rw_w2, rw_a0, rw_a2, rw_g2, rw_kk, rw_ka, rw_rk, rw_ln_w, rw_ln_b, ml_conv_w, ml_conv_b, ml_ig_b, ml_fg_b, ml_norm_g, ffn_w_gate, ffn_w_up, ffn_w_down):
    bsz, seq, d = x.shape
    depth = w_mod.shape[0]
    t = bsz * seq
    mod = _mod(c, w_mod, b_mod)
    x2 = x.reshape(t, d)
    for l in range(depth):
        sh1, sc1, gt1, sh2, sc2, gt2 = [m.reshape(bsz, 1, d) for m in jnp.split(mod[l], 6, axis=-1)]
        p_nsa, p_rw, p_ml, kse, kw, vse, vwe = _in_proj(x2, sc1, sh1, g_pre_mix[l].reshape(1, d),
                                                        _layout_w_in(w_in[l]), bsz, seq)
        wc, pe8, w2 = _layout_cmp(nsa_ck_w1[l], nsa_cv_w1[l], nsa_ck_w2[l], nsa_cv_w2[l], nsa_pe_k[l], nsa_pe_v[l])
        kv_cmp = _nsa_cmp(p_nsa, wc, pe8, w2, bsz, seq)
        gate_b = jnp.pad(nsa_gate_b[l], (0, 128 - 3 * H_ATT)).reshape(1, 128)
        o_nsa = _nsa(p_nsa, kv_cmp, kse.reshape(bsz, seq, -1), vse, kw.reshape(bsz, seq, -1), vwe, gate_b,
                     nsa_out_g[l].reshape(1, C_ATT), bsz, seq)
        vec = lambda a: a.reshape(1, -1)
        rw_args = (vec(rw_mu[l]), vec(rw_w0[l]), rw_w2[l], vec(rw_a0[l]), rw_a2[l], rw_g2[l], vec(rw_kk[l]),
                   vec(rw_ka[l]), vec(rw_rk[l]), vec(rw_ln_w[l]), vec(rw_ln_b[l]))
        ml_gate_b = jnp.pad(jnp.concatenate([ml_ig_b[l], ml_fg_b[l]]), (0, 128 - 2 * H_MLSTM)).reshape(1, 128)
        ml_args = (ml_conv_w[l], vec(ml_conv_b[l]), ml_gate_b, vec(ml_norm_g[l]))
        o_rw, o_ml = _recurrent(p_rw, rw_args, p_ml, ml_args, bsz, seq)
        wo = w_out[l].astype(BF16)
        x2 = _out_proj(o_nsa, o_rw, o_ml, wo[:C_ATT], wo[C_ATT:C_ATT + C_RWKV], wo[C_ATT + C_RWKV:], x2, gt1,
                       g_post_mix[l].reshape(1, d), seq)
        x2 = _ffn(x2, sc2, sh2, gt2, g_pre_ffn[l].reshape(1, d), g_post_ffn[l].reshape(1, d),
                  ffn_w_gate[l].astype(BF16), ffn_w_up[l].astype(BF16), ffn_w_down[l].astype(BF16), seq)
    return x2.reshape(bsz, seq, d)
```

```python
import functools

import jax
import jax.numpy as jnp
from jax import lax
from jax.experimental import pallas as pl
from jax.experimental.pallas import tpu as pltpu

F32 = jnp.float32
BF16 = jnp.bfloat16
HIGHEST = lax.Precision.HIGHEST

HEAD_DIM = 64
H_ATT = 4
C_ATT = H_ATT * HEAD_DIM
H_RWKV = 6
C_RWKV = H_RWKV * HEAD_DIM
H_MLSTM = 6
C_MLSTM = H_MLSTM * HEAD_DIM
CMP_BLOCK = 32
CMP_STRIDE = 16
SLC_BLOCK = 64
SLC_SHIFT = 6
N_SELECT = 16
WINDOW = 512
NEG = -1e30
FORCE = 1e9
MASK_BIG = 2.0 ** 100
LOG2E = 1.4426950408889634
RANK_W = 64
RANK_A = 64
RANK_G = 128
RWKV_GN_EPS = 64e-5
CHUNK = 64
SEQ_BLOCK = 512
CONV_WIDTH = 4
NORM_EPS = 1e-6

NSA_W = 768
RWKV_W = 3 * C_RWKV + RANK_W + RANK_A + RANK_G
ML_W = 4 * C_MLSTM + 128
NSA_IN = C_ATT + 6 * HEAD_DIM + 3 * H_ATT
ML_IN = 4 * C_MLSTM + 2 * H_MLSTM

VMEM_LIMIT = 56 * 1024 * 1024

TQ = 256
TK = 256


def _params(*sem):
    return pltpu.CompilerParams(dimension_semantics=sem, vmem_limit_bytes=VMEM_LIMIT)


def _dot(a, b, precise=False):
    if precise:
        return jnp.dot(a.astype(F32), b.astype(F32), preferred_element_type=F32, precision=HIGHEST)
    return jnp.dot(a.astype(BF16), b.astype(BF16), preferred_element_type=F32)


def _dot_nt(a, b, precise=False):
    dn = (((1,), (1,)), ((), ()))
    if precise:
        return lax.dot_general(a.astype(F32), b.astype(F32), dn, preferred_element_type=F32, precision=HIGHEST)
    return lax.dot_general(a.astype(BF16), b.astype(BF16), dn, preferred_element_type=F32)


def _dot_tn(a, b, precise=False):
    dn = (((0,), (0,)), ((), ()))
    if precise:
        return lax.dot_general(a.astype(F32), b.astype(F32), dn, preferred_element_type=F32, precision=HIGHEST)
    return lax.dot_general(a.astype(BF16), b.astype(BF16), dn, preferred_element_type=F32)


def _split3(x):
    hi = x.astype(BF16)
    rest = x - hi.astype(F32)
    mid = rest.astype(BF16)
    lo = (rest - mid.astype(F32)).astype(BF16)
    return hi, mid, lo


def _dot_mask_lhs(mask01, x):
    mb = mask01.astype(BF16)
    hi, mid, lo = _split3(x)
    dot = lambda p: jnp.dot(mb, p, preferred_element_type=F32)
    return (dot(lo) + dot(mid)) + dot(hi)


def _dot_nt_mask_rhs(x, mask01):
    mb = mask01.astype(BF16)
    hi, mid, lo = _split3(x)
    dot = lambda p: lax.dot_general(p, mb, (((1,), (1,)), ((), ())), preferred_element_type=F32)
    return (dot(lo) + dot(mid)) + dot(hi)


def _dot3(a, b):
    a_hi, a_lo, _ = _split3(a)
    b_hi, b_lo, _ = _split3(b)
    dot = lambda p, q: jnp.dot(p, q, preferred_element_type=F32)
    return (dot(a_lo, b_hi) + dot(a_hi, b_lo)) + dot(a_hi, b_hi)


def _rms(x, g):
    return x * lax.rsqrt(jnp.mean(x * x, axis=-1, keepdims=True) + NORM_EPS) * g


def _mod_body(c_ref, w_ref, b_ref, o_ref):
    c = c_ref[...]
    cs = c * jax.nn.sigmoid(c)
    o_ref[...] = _dot(cs, w_ref[...], precise=True) + b_ref[...]


def _mod(c, w_mod, b_mod):
    depth, d, n = w_mod.shape
    bsz = c.shape[0]
    tn = 1536
    return pl.pallas_call(
        _mod_body,
        out_shape=jax.ShapeDtypeStruct((depth, bsz, n), F32),
        grid=(depth, n // tn),
        in_specs=[
            pl.BlockSpec((bsz, d), lambda l, j: (0, 0)),
            pl.BlockSpec((None, d, tn), lambda l, j: (l, 0, j)),
            pl.BlockSpec((None, 1, tn), lambda l, j: (l, 0, j)),
        ],
        out_specs=pl.BlockSpec((None, bsz, tn), lambda l, j: (l, 0, j)),
        compiler_params=_params("parallel", "parallel"),
        name="mod",
    )(c, w_mod, b_mod.reshape(depth, 1, n))


def _in_proj_body(x_ref, sc_ref, sh_ref, g_ref, w_ref, nsa_ref, rw_ref, ml_ref, kse_ref, kw_ref, vse_ref,
                  vwe_ref, *, per_b):
    tm = x_ref.shape[0]
    h = _rms(x_ref[...], g_ref[...]) * (1.0 + sc_ref[...]) + sh_ref[...]
    hb = h.astype(BF16)
    nsa = jnp.dot(hb, w_ref[:, 0:NSA_W], preferred_element_type=F32)
    nsa_ref[...] = nsa
    rw_ref[...] = jnp.dot(hb, w_ref[:, NSA_W:NSA_W + RWKV_W], preferred_element_type=F32)
    ml_ref[...] = jnp.dot(hb, w_ref[:, NSA_W + RWKV_W:], preferred_element_type=F32)
    D = HEAD_DIM
    off = C_ATT + 2 * D
    lane = lax.broadcasted_iota(jnp.int32, (tm, 2 * D), 1)
    pos = (pl.program_id(0) % per_b) * tm + lax.broadcasted_iota(jnp.int32, (tm, 2 * D), 0)
    expand = jnp.where(lane - D == jnp.right_shift(pos, SLC_SHIFT), MASK_BIG, 0.0)
    kse_ref[...] = jnp.where(lane < D, nsa[:, off:off + 2 * D], expand).astype(BF16)
    kw_ref[...] = nsa[:, off + 2 * D:off + 3 * D].astype(BF16)
    tail = jnp.where(lax.broadcasted_iota(jnp.int32, (D, tm), 0) == 0, 1.0, 0.0)
    vse_ref[...] = jnp.concatenate([nsa[:, off + D:off + 2 * D].T, tail], axis=0).astype(BF16)
    vwe_ref[...] = jnp.concatenate([nsa[:, off + 3 * D:off + 4 * D].T, tail], axis=0).astype(BF16)


def _in_proj(x2, sc, sh, g, w_p, bsz, seq):
    t, d = x2.shape
    tm = 512
    per_b = seq // tm
    n_slc = seq // SLC_BLOCK
    assert n_slc == HEAD_DIM, "kse packs the key and one column per selection block into 128 lanes"
    row = lambda i: (i, 0)
    bat = lambda i: (i // per_b, 0, 0)
    fix = lambda i: (0, 0)
    col = lambda i: (i // per_b, 0, i % per_b)
    return pl.pallas_call(
        functools.partial(_in_proj_body, per_b=per_b),
        out_shape=(jax.ShapeDtypeStruct((t, NSA_W), F32),
                   jax.ShapeDtypeStruct((t, RWKV_W), F32),
                   jax.ShapeDtypeStruct((t, ML_W), F32),
                   jax.ShapeDtypeStruct((t, 2 * HEAD_DIM), BF16),
                   jax.ShapeDtypeStruct((t, HEAD_DIM), BF16),
                   jax.ShapeDtypeStruct((bsz, 128, seq), BF16),
                   jax.ShapeDtypeStruct((bsz, 128, seq), BF16)),
        grid=(t // tm,),
        in_specs=[
            pl.BlockSpec((tm, d), row),
            pl.BlockSpec((None, 1, d), bat),
            pl.BlockSpec((None, 1, d), bat),
            pl.BlockSpec((1, d), fix),
            pl.BlockSpec(w_p.shape, fix),
        ],
        out_specs=(pl.BlockSpec((tm, NSA_W), row), pl.BlockSpec((tm, RWKV_W), row),
                   pl.BlockSpec((tm, ML_W), row), pl.BlockSpec((tm, 2 * HEAD_DIM), row),
                   pl.BlockSpec((tm, HEAD_DIM), row), pl.BlockSpec((None, 128, tm), col),
                   pl.BlockSpec((None, 128, tm), col)),
        compiler_params=_params("parallel"),
        name="in_proj",
    )(x2, sc, sh, g, w_p)


def _nsa_cmp_body(x_ref, wc_ref, pe_ref, w2_ref, o_ref):
    half = CMP_BLOCK // 2
    width = x_ref.shape[1]
    g = x_ref.shape[0] // half
    f = None
    for tl in range(half):
        part = _dot3(x_ref[pl.ds(tl, g, stride=half), :], wc_ref[tl * width:(tl + 1) * width, :])
        f = part if f is None else f + part
    c = _dot(pe_ref[...], wc_ref[...], precise=True)
    second = pltpu.roll(f[:, 128:256], g - 1, 0)
    pre = f[:, 0:128] + second + c[0:1, 0:128] + c[1:2, 128:256]
    act = jax.nn.gelu(pre, approximate=True)
    out = _dot(act, w2_ref[...], precise=True)
    rows = lax.broadcasted_iota(jnp.int32, out.shape, 0)
    o_ref[...] = jnp.where(rows < g - 1, out, 0.0)


def _nsa_cmp(p_nsa, wc, pe8, w2, bsz, seq):
    g = seq // (CMP_BLOCK // 2)
    kc_block = C_ATT // (2 * HEAD_DIM)
    return pl.pallas_call(
        _nsa_cmp_body,
        out_shape=jax.ShapeDtypeStruct((bsz, g, 128), F32),
        grid=(bsz,),
        in_specs=[
            pl.BlockSpec((seq, 2 * HEAD_DIM), lambda b: (b, kc_block)),
            pl.BlockSpec(wc.shape, lambda b: (0, 0)),
            pl.BlockSpec(pe8.shape, lambda b: (0, 0)),
            pl.BlockSpec(w2.shape, lambda b: (0, 0)),
        ],
        out_specs=pl.BlockSpec((None, g, 128), lambda b: (b, 0, 0)),
        compiler_params=_params("parallel"),
        name="nsa_cmp",
    )(p_nsa, wc, pe8, w2)


def _nsa_body(q_ref, gts_ref, kvc_ref, kse_ref, vse_ref, kw_ref, vwe_ref, gb_ref, g_ref, o_ref,
              m_s, acc_s, s_even, s_odd, ocmp, *, n_slc):
    tq = q_ref.shape[0]
    n_cmp = kvc_ref.shape[0]
    qi = pl.program_id(1)
    q0 = qi * tq
    scale = HEAD_DIM ** -0.5
    D = HEAD_DIM
    qt = q_ref[...].T

    kc = kvc_ref[:, 0:D]
    vc = kvc_ref[:, D:2 * D]
    nidx = lax.broadcasted_iota(jnp.int32, (n_cmp, tq), 0)
    pos = q0 + lax.broadcasted_iota(jnp.int32, (n_cmp, tq), 1)
    cmask = (nidx * CMP_STRIDE + (CMP_BLOCK - 1)) <= pos
    heads = range(H_ATT)
    sc = [_dot3(kc, qt[h * D:(h + 1) * D]) for h in heads]
    sc = [jnp.where(cmask, x * scale, NEG) for x in sc]
    ec = [jnp.where(cmask, jnp.exp(x - jnp.max(x, axis=0, keepdims=True)), 0.0) for x in sc]
    pc = [e / jnp.maximum(jnp.sum(e, axis=0, keepdims=True), 1e-30) for e in ec]
    oc = [_dot_tn(vc, p) for p in pc]
    for h in heads:
        ocmp[h * D:(h + 1) * D, :] = oc[h]
    psum = sum(pc[1:], pc[0])

    jrow = lax.broadcasted_iota(jnp.int32, (n_slc, n_cmp), 0) * SLC_BLOCK
    ncol = lax.broadcasted_iota(jnp.int32, (n_slc, n_cmp), 1) * CMP_STRIDE
    overlap_t = jnp.where((ncol < jrow + SLC_BLOCK) & (ncol + CMP_BLOCK > jrow), 1.0, 0.0)
    imp_t = _dot_mask_lhs(overlap_t, psum)
    jj = lax.broadcasted_iota(jnp.int32, (n_slc, tq), 0)
    cur = jnp.right_shift(q0 + lax.broadcasted_iota(jnp.int32, (n_slc, tq), 1), SLC_SHIFT)
    forced = (jj == 0) | (jj == cur) | (jj == cur - 1)
    score = jnp.where(forced, FORCE, jnp.where(jj <= cur, imp_t, NEG))
    group = 8
    cnt = [jnp.zeros((group, tq), F32) for _ in range(n_slc // group)]
    for j2 in range(n_slc):
        row = score[j2:j2 + 1, :]
        for gi in range(n_slc // group):
            sg = score[gi * group:(gi + 1) * group]
            if gi * group > j2:
                hit = jnp.where(row >= sg, 1.0, 0.0)
            elif (gi + 1) * group - 1 <= j2:
                hit = jnp.where(row > sg, 1.0, 0.0)
            else:
                hit = jnp.where(lax.broadcasted_iota(jnp.int32, (group, tq), 0) + gi * group > j2,
                                jnp.where(row >= sg, 1.0, 0.0), jnp.where(row > sg, 1.0, 0.0))
            cnt[gi] = cnt[gi] + hit
    cnt = jnp.concatenate(cnt, axis=0)
    selm = jnp.where(cnt < min(N_SELECT, n_slc), 0.0, -1.0).astype(BF16)

    qs = (qt * (scale * LOG2E)).astype(BF16)
    rhs_w = [qs[h * D:(h + 1) * D] for h in heads]
    rhs_s = [jnp.concatenate([rhs_w[h], selm], axis=0) for h in heads]
    m_s[...] = jnp.full(m_s.shape, NEG, F32)
    acc_s[...] = jnp.zeros_like(acc_s)

    def scores_into(buf, j):
        k = kse_ref[pl.ds(pl.multiple_of(j * TK, TK), TK), :]
        for h in heads:
            buf[h] = jnp.dot(k, rhs_s[h], preferred_element_type=F32)

    def consume(buf, j, mask):
        vt = vse_ref[:, pl.ds(pl.multiple_of(j * TK, TK), TK)]
        for h in heads:
            s = buf[h]
            if mask is not None:
                s = jnp.where(mask, s, NEG)
            m_old = m_s[h:h + 1, :]
            m_new = jnp.maximum(m_old, jnp.max(s, axis=0, keepdims=True))
            p = jnp.exp2(s - m_new).astype(BF16)
            acc_s[h] = jnp.exp2(m_old - m_new) * acc_s[h] + jnp.dot(vt, p, preferred_element_type=F32)
            m_s[h:h + 1, :] = m_new

    kpos = lax.broadcasted_iota(jnp.int32, (TK, tq), 0)
    qpos = lax.broadcasted_iota(jnp.int32, (TK, tq), 1)
    last = jnp.maximum(qi - 1, 0)
    scores_into(s_even, qi)
    scores_into(s_odd, 0)
    consume(s_even, qi, kpos <= qpos)

    def body(jp, carry):
        j = 2 * jp
        scores_into(s_even, jnp.minimum(j + 1, last))
        consume(s_odd, j, None)
        scores_into(s_odd, jnp.minimum(j + 2, last))
        consume(s_even, j + 1, None)
        return carry

    lax.fori_loop(0, qi // 2, body, 0)

    @pl.when(qi % 2 == 1)
    def _():
        consume(s_odd, qi - 1, None)

    span = WINDOW + tq
    w0 = pl.multiple_of(jnp.maximum(q0 - WINDOW, 0), TK)
    kw = kw_ref[pl.ds(w0, span), :]
    vwt = vwe_ref[:, pl.ds(w0, span)]
    kabs = w0 + lax.broadcasted_iota(jnp.int32, (span, tq), 0)
    qabs = q0 + lax.broadcasted_iota(jnp.int32, (span, tq), 1)
    wmask = (kabs <= qabs) & (kabs > qabs - WINDOW)
    sw = [jnp.where(wmask, jnp.dot(kw, rhs_w[h], preferred_element_type=F32), NEG) for h in heads]
    pw = [jnp.exp2(x - jnp.max(x, axis=0, keepdims=True)).astype(BF16) for x in sw]
    acc_w = [jnp.dot(vwt, p, preferred_element_type=F32) for p in pw]

    gate = jax.nn.sigmoid((gts_ref[...] + gb_ref[...]).T)
    outs = []
    for h in heads:
        a_s = acc_s[h]
        a_w = acc_w[h]
        outs.append(gate[h:h + 1] * ocmp[h * D:(h + 1) * D, :]
                    + gate[H_ATT + h:H_ATT + h + 1] * (a_s[0:D] / a_s[D:D + 1])
                    + gate[2 * H_ATT + h:2 * H_ATT + h + 1] * (a_w[0:D] / a_w[D:D + 1]))
    ot = jnp.concatenate(outs, axis=0)
    ot = ot * lax.rsqrt(jnp.mean(ot * ot, axis=0, keepdims=True) + NORM_EPS)
    o_ref[...] = ot.T * g_ref[...]


def _nsa(p_nsa, kv_cmp, kse, vse, kw, vwe, gate_b, out_g, bsz, seq):
    t = p_nsa.shape[0]
    nq = seq // TQ
    n_slc = seq // SLC_BLOCK
    g = kv_cmp.shape[1]
    row = lambda b, i: (b * nq + i, 0)
    per_b = lambda b, i: (b, 0, 0)
    fix = lambda b, i: (0, 0)
    return pl.pallas_call(
        functools.partial(_nsa_body, n_slc=n_slc),
        out_shape=jax.ShapeDtypeStruct((t, C_ATT), F32),
        grid=(bsz, nq),
        in_specs=[
            pl.BlockSpec((TQ, C_ATT), row),
            pl.BlockSpec((TQ, 128), lambda b, i: (b * nq + i, NSA_W // 128 - 1)),
            pl.BlockSpec((None, g, 128), per_b),
            pl.BlockSpec((None, seq, HEAD_DIM + n_slc), per_b),
            pl.BlockSpec((None, 128, seq), per_b),
            pl.BlockSpec((None, seq, HEAD_DIM), per_b),
            pl.BlockSpec((None, 128, seq), per_b),
            pl.BlockSpec((1, 128), fix),
            pl.BlockSpec((1, C_ATT), fix),
        ],
        out_specs=pl.BlockSpec((TQ, C_ATT), row),
        scratch_shapes=[pltpu.VMEM((8, TQ), F32), pltpu.VMEM((H_ATT, 128, TQ), F32),
                        pltpu.VMEM((H_ATT, TK, TQ), F32), pltpu.VMEM((H_ATT, TK, TQ), F32),
                        pltpu.VMEM((C_ATT, TQ), F32)],
        compiler_params=_params("parallel", "arbitrary"),
        name="nsa",
    )(p_nsa, p_nsa, kv_cmp, kse, vse, kw, vwe, gate_b, out_g)


def _tri(n, strict):
    r = lax.broadcasted_iota(jnp.int32, (n, n), 0)
    c = lax.broadcasted_iota(jnp.int32, (n, n), 1)
    return (c < r) if strict else (c <= r)


def _block_diag(x):
    xb = x.astype(BF16)
    first = lax.broadcasted_iota(jnp.int32, xb.shape, 1) < xb.shape[1] // 2
    zero = jnp.zeros_like(xb)
    return jnp.concatenate([jnp.where(first, xb, zero), jnp.where(first, zero, xb)], axis=0)


def _unit_lower_inverse_pairs(ms, eye2):
    n = eye2.shape[0]
    idx = range(len(ms))
    r = lax.broadcasted_iota(jnp.int32, (n, 2 * n), 0)
    c = jnp.bitwise_and(lax.broadcasted_iota(jnp.int32, (n, 2 * n), 1), n - 1)
    base = 8
    diag = jnp.right_shift(r, 3) == jnp.right_shift(c, 3)
    m8 = [jnp.where(diag, m, 0.0) for m in ms]
    t = [eye2 + m for m in m8]
    p2 = [_dot(m, _block_diag(m)) for m in m8]
    t = [t[i] + _dot(p2[i], _block_diag(t[i])) for i in idx]
    p4 = [_dot(p, _block_diag(p)) for p in p2]
    t = [t[i] + _dot(p4[i], _block_diag(t[i])) for i in idx]
    b = base
    while b < n:
        sh = b.bit_length() - 1
        pair = jnp.right_shift(r, sh + 1) == jnp.right_shift(c, sh + 1)
        lower_left = pair & (jnp.right_shift(r, sh) != jnp.right_shift(c, sh))
        left = [_dot(t[i], _block_diag(jnp.where(lower_left, ms[i], 0.0))) for i in idx]
        t = [t[i] + _dot(left[i], _block_diag(t[i])) for i in idx]
        b *= 2
    return t


def _rwkv_body(p_ref, mu_ref, w0_ref, w2_ref, a0_ref, a2_ref, g2_ref, kk_ref, ka_ref, rk_ref, lnw_ref,
               lnb_ref, o_ref, state, prev):
    L = CHUNK
    C = C_RWKV
    tb = p_ref.shape[0]
    chunks = range(tb // L)

    x = p_ref[...]
    rows = lax.broadcasted_iota(jnp.int32, x.shape, 0)
    shifted = jnp.where(rows == 0, prev[0:1, :], pltpu.roll(x, 1, 0))
    prev[0:1, :] = x[tb - 1:tb, :]
    x = x + mu_ref[...] * (shifted - x)
    r = x[:, 0:C]
    k = x[:, C:2 * C]
    v = x[:, 2 * C:3 * C]
    xw = x[:, 3 * C:3 * C + RANK_W]
    xa = x[:, 3 * C + RANK_W:3 * C + RANK_W + RANK_A]
    xg = x[:, 3 * C + RANK_W + RANK_A:]
    w = -jax.nn.softplus(-(w0_ref[...] + _dot3(jnp.tanh(xw), w2_ref[...]))) - 0.5
    logw = -jnp.exp(w)
    a = jax.nn.sigmoid(a0_ref[...] + _dot3(xa, a2_ref[...]))
    g = _dot(jax.nn.sigmoid(xg), g2_ref[...])
    kkf = k * kk_ref[...]
    kmod = k * (1.0 + (a - 1.0) * ka_ref[...])
    rk = r * kmod * rk_ref[...]

    tri_incl = jnp.where(_tri(L, strict=False), 1.0, 0.0)
    cum = jnp.concatenate([_dot_mask_lhs(tri_incl, logw[c * L:(c + 1) * L]) for c in chunks], axis=0)
    cum_last = [cum[(c + 1) * L - 1:(c + 1) * L, :] for c in chunks]
    cum_end = jnp.concatenate([jnp.broadcast_to(cl, (L, C)) for cl in cum_last], axis=0)
    e_pos = jnp.exp(cum)
    e_prev = jnp.exp(cum - logw)
    e_neg = jnp.exp(-cum)
    e_rem = jnp.exp(cum_end - cum)
    N = HEAD_DIM
    PW = 2 * N
    pairs = range(H_RWKV // 2)
    lane_l = lax.broadcasted_iota(jnp.int32, (L, PW), 1)
    row_l = lax.broadcasted_iota(jnp.int32, (L, PW), 0)
    col_l = jnp.bitwise_and(lane_l, N - 1)
    first = lane_l < N
    first2 = lax.broadcasted_iota(jnp.int32, (2 * L, PW), 1) < N
    strict2 = col_l < row_l
    incl2 = col_l <= row_l
    eye2 = jnp.where(col_l == row_l, 1.0, 0.0)

    def halves(x_):
        lane = lax.broadcasted_iota(jnp.int32, x_.shape, 1)
        sa = jnp.sum(jnp.where(lane < N, x_, 0.0), axis=-1, keepdims=True)
        sb = jnp.sum(jnp.where(lane < N, 0.0, x_), axis=-1, keepdims=True)
        return jnp.where(lane < N, sa, sb)

    ps = [slice(p * PW, (p + 1) * PW) for p in pairs]
    kk_p = [kkf[:, sl] for sl in ps]
    kk_p = [x_ / jnp.maximum(jnp.sqrt(halves(x_ * x_)), 1e-12) for x_ in kk_p]
    alpha_f = [-kk_p[p] * e_prev[:, ps[p]] for p in pairs]
    r_f = [r[:, ps[p]] * e_pos[:, ps[p]] for p in pairs]
    beta = [kk_p[p] * a[:, ps[p]] for p in pairs]
    beta_f = [beta[p] * e_neg[:, ps[p]] for p in pairs]
    k_f = [kmod[:, ps[p]] * e_neg[:, ps[p]] for p in pairs]
    beta_e = [beta[p] * e_rem[:, ps[p]] for p in pairs]
    k_e = [kmod[:, ps[p]] * e_rem[:, ps[p]] for p in pairs]
    items = [(c, p) for c in chunks for p in pairs]
    n_items = range(len(items))
    rs = [slice(c * L, (c + 1) * L) for c, _ in items]
    pi = [p for _, p in items]
    lhs = [jnp.concatenate([alpha_f[pi[i]][rs[i]], r_f[pi[i]][rs[i]]], axis=0).astype(BF16) for i in n_items]
    zero16 = jnp.zeros((2 * L, PW), BF16)
    lhs4 = [jnp.concatenate([jnp.where(first2, x_, zero16), jnp.where(first2, zero16, x_)], axis=0) for x_ in lhs]
    bk = [(beta_f[pi[i]][rs[i]].astype(BF16), k_f[pi[i]][rs[i]].astype(BF16)) for i in n_items]
    v_i = [v[rs[i], ps[pi[i]]] for i in n_items]
    out1 = [_dot_nt(lhs4[i], jnp.concatenate([bk[i][0], bk[i][1]], axis=0)) for i in n_items]
    out2 = [_dot_nt(lhs4[i], jnp.concatenate([bk[i][1], bk[i][0]], axis=0)) for i in n_items]
    m_ab = [jnp.where(strict2, jnp.where(first, out1[i][0:L], out2[i][2 * L:3 * L]), 0.0) for i in n_items]
    m_ak = [jnp.where(strict2, jnp.where(first, out2[i][0:L], out1[i][2 * L:3 * L]), 0.0) for i in n_items]
    m_rb = [jnp.where(incl2, jnp.where(first, out1[i][L:2 * L], out2[i][3 * L:4 * L]), 0.0) for i in n_items]
    m_rk = [jnp.where(incl2, jnp.where(first, out2[i][L:2 * L], out1[i][3 * L:4 * L]), 0.0) for i in n_items]
    t_inv = _unit_lower_inverse_pairs(m_ab, eye2)
    mv = [_dot(jnp.concatenate([m_ak[i], m_rk[i]], axis=0), _block_diag(v_i[i])) for i in n_items]
    pq = [_dot(t_inv[i], jnp.concatenate([_block_diag(alpha_f[pi[i]][rs[i]]), _block_diag(mv[i][0:L])], axis=1))
          for i in n_items]
    ry = [_dot(m_rb[i], jnp.concatenate([_block_diag(pq[i][:, 0:PW]), _block_diag(pq[i][:, PW:2 * PW])], axis=1))
          for i in n_items]
    r_eff = [r_f[pi[i]][rs[i]] + ry[i][:, 0:PW] for i in n_items]
    y0 = [ry[i][:, PW:2 * PW] + mv[i][L:2 * L] for i in n_items]
    zero_l = jnp.zeros((L, PW), F32)
    gh = [_dot_tn(jnp.concatenate([beta_e[pi[i]][rs[i]], k_e[pi[i]][rs[i]]], axis=0),
                  jnp.concatenate([pq[i], jnp.concatenate([zero_l, v_i[i]], axis=1)], axis=0)) for i in n_items]
    g_off = [jnp.where(first, gh[i][0:N, 0:PW], gh[i][N:2 * N, 0:PW]) for i in n_items]
    h_add = [jnp.where(first, gh[i][0:N, PW:2 * PW], gh[i][N:2 * N, PW:2 * PW]) for i in n_items]
    gam = [halves(eye2 * jnp.exp(cum_last[c][:, ps[p]])) for c, p in items]
    z = [state[p] for p in pairs]
    y_parts = [[] for _ in pairs]
    for c in chunks:
        idx = [c * len(pairs) + p for p in pairs]
        zb = [_block_diag(z[p]) for p in pairs]
        for p, i in enumerate(idx):
            y_parts[p].append(_dot(r_eff[i], zb[p]) + y0[i])
        z = [z[p] * gam[i] + _dot(g_off[i], zb[p]) + h_add[i] for p, i in enumerate(idx)]
    for p in pairs:
        state[p] = z[p]
    yn = []
    for p in pairs:
        y_p = jnp.concatenate(y_parts[p], axis=0)
        mu_y = halves(y_p) * (1.0 / N)
        dev = y_p - mu_y
        var = halves(dev * dev) * (1.0 / N)
        yn.append(dev * lax.rsqrt(var + RWKV_GN_EPS))
    bonus = [halves(rk[:, ps[p]]) * v[:, ps[p]] for p in pairs]
    yn = jnp.concatenate(yn, axis=-1)
    bonus = jnp.concatenate(bonus, axis=-1)
    o_ref[...] = (yn * lnw_ref[...] + lnb_ref[...] + bonus) * g


N_RWKV_ARGS = 11
N_ML_ARGS = 4


def _recurrent_body(*refs):
    p_rw, rw_args = refs[0], refs[1:1 + N_RWKV_ARGS]
    p_ml, ml_args = refs[1 + N_RWKV_ARGS], refs[2 + N_RWKV_ARGS:2 + N_RWKV_ARGS + N_ML_ARGS]
    o_rw, o_ml, state, prev_rw, cstate, mstate, prev_ml = refs[2 + N_RWKV_ARGS + N_ML_ARGS:]

    @pl.when(pl.program_id(1) == 0)
    def _():
        for ref in (state, prev_rw, cstate, mstate, prev_ml):
            ref[...] = jnp.zeros_like(ref)

    _rwkv_body(p_rw, *rw_args, o_rw, state, prev_rw)
    _mlstm_body(p_ml, *ml_args, o_ml, cstate, mstate, prev_ml)


def _recurrent(p_rw, rw_args, p_ml, ml_args, bsz, seq):
    t = p_rw.shape[0]
    tb = SEQ_BLOCK
    nb = seq // tb
    row = lambda b, c: (b * nb + c, 0)
    fix = lambda b, c: (0, 0)
    assert len(rw_args) == N_RWKV_ARGS and len(ml_args) == N_ML_ARGS
    return pl.pallas_call(
        _recurrent_body,
        out_shape=(jax.ShapeDtypeStruct((t, C_RWKV), F32), jax.ShapeDtypeStruct((t, C_MLSTM), F32)),
        grid=(bsz, nb),
        in_specs=([pl.BlockSpec((tb, RWKV_W), row)] + [pl.BlockSpec(a.shape, fix) for a in rw_args]
                  + [pl.BlockSpec((tb, ML_W), row)] + [pl.BlockSpec(a.shape, fix) for a in ml_args]),
        out_specs=(pl.BlockSpec((tb, C_RWKV), row), pl.BlockSpec((tb, C_MLSTM), row)),
        scratch_shapes=[pltpu.VMEM((H_RWKV // 2, HEAD_DIM, 2 * HEAD_DIM), F32), pltpu.VMEM((8, RWKV_W), F32),
                        pltpu.VMEM((H_MLSTM, 128, HEAD_DIM), F32), pltpu.VMEM((8, 128), F32),
                        pltpu.VMEM((8, 2 * C_MLSTM), F32)],
        compiler_params=_params("parallel", "arbitrary"),
        name="recurrent",
    )(p_rw, *rw_args, p_ml, *ml_args)


def _mlstm_body(p_ref, cw_ref, cb_ref, gb_ref, ng_ref, o_ref, cstate, mstate, prev):
    L = CHUNK
    C = C_MLSTM
    D = HEAD_DIM
    tb = p_ref.shape[0]
    chunks = range(tb // L)

    qk_in = p_ref[:, 0:2 * C]
    pv = prev[...]
    rows8 = lax.broadcasted_iota(jnp.int32, pv.shape, 0)
    conv = qk_in * cw_ref[CONV_WIDTH - 1:CONV_WIDTH, :] + cb_ref[...]
    for d in range(1, CONV_WIDTH):
        rolled = pltpu.roll(qk_in, d, 0)
        top = jnp.where(rows8 < d, pltpu.roll(pv, d, 0), rolled[0:8])
        sh = jnp.concatenate([top, rolled[8:]], axis=0)
        conv = conv + sh * cw_ref[CONV_WIDTH - 1 - d:CONV_WIDTH - d, :]
    prev[...] = qk_in[tb - 8:tb, :]
    qk = conv * jax.nn.sigmoid(conv)
    q = qk[:, 0:C]
    k = qk[:, C:2 * C] * (D ** -0.5)
    v = p_ref[:, 2 * C:3 * C]
    og = p_ref[:, 3 * C:4 * C]
    gates = p_ref[:, 4 * C:4 * C + 128] + gb_ref[...]
    lane = lax.broadcasted_iota(jnp.int32, gates.shape, 1)
    is_f = (lane >= H_MLSTM) & (lane < 2 * H_MLSTM)
    gl = jnp.where(is_f, jax.nn.log_sigmoid(gates), gates)
    tri_incl = jnp.where(_tri(L, strict=False), 1.0, 0.0)
    bcum = [_dot_mask_lhs(tri_incl, gl[c * L:(c + 1) * L]) for c in chunks]
    gl_t = [gl[c * L:(c + 1) * L].T for c in chunks]
    bcum_t = [_dot_nt_mask_rhs(g_, tri_incl) for g_ in gl_t]
    m_all = mstate[...]
    src = lax.broadcasted_iota(jnp.int32, (L, L), 0)
    qry = lax.broadcasted_iota(jnp.int32, (L, L), 1)
    causal_t = src <= qry
    ones_row = jnp.where(lax.broadcasted_iota(jnp.int32, (D, L), 0) == 0, 1.0, 0.0)

    heads = range(H_MLSTM)
    items = [(c, h) for c in chunks for h in heads]
    n_items = range(len(items))
    q_t = [q[c * L:(c + 1) * L].T for c in chunks]
    v_t = [v[c * L:(c + 1) * L].T for c in chunks]
    qt_i = [q_t[c][h * D:(h + 1) * D] for c, h in items]
    vt_ext = [jnp.concatenate([v_t[c][h * D:(h + 1) * D], ones_row], axis=0) for c, h in items]
    k_i = [k[c * L:(c + 1) * L, h * D:(h + 1) * D] for c, h in items]
    c_col = [gl[c * L:(c + 1) * L, h:h + 1] - bcum[c][:, H_MLSTM + h:H_MLSTM + h + 1] for c, h in items]
    b_row = [bcum_t[c][H_MLSTM + h:H_MLSTM + h + 1, :] for c, h in items]
    c_row = [gl_t[c][h:h + 1, :] - b_row[i] for i, (c, h) in enumerate(items)]
    g_tot = [bcum[c][L - 1:L, H_MLSTM + h:H_MLSTM + h + 1] for c, h in items]
    u_row = [g_tot[i] + c_row[i] for i in n_items]
    u_max = [jnp.max(x_, axis=1, keepdims=True) for x_ in u_row]
    m0 = []
    m_run = [m_all[0:1, h:h + 1] for h in heads]
    for c in chunks:
        for h in heads:
            i = c * H_MLSTM + h
            m0.append(m_run[h])
            m_run[h] = jnp.maximum(g_tot[i] + m_run[h], u_max[i])
    m_next = [m0[i + H_MLSTM] if i + H_MLSTM < len(items) else m_run[items[i][1]] for i in n_items]
    kq = [_dot(k_i[i], qt_i[i]) for i in n_items]
    kv = [_dot(vt_ext[i] * jnp.exp(u_row[i] - m_next[i]), k_i[i]) for i in n_items]
    cm = []
    cm_run = [cstate[h] for h in heads]
    for c in chunks:
        for h in heads:
            i = c * H_MLSTM + h
            cm.append(cm_run[h])
            cm_run[h] = jnp.exp(g_tot[i] + m0[i] - m_next[i]) * cm_run[h] + kv[i]
    inter = [_dot(cm[i], qt_i[i]) for i in n_items]
    c_b = [jnp.where(causal_t, jnp.broadcast_to(c_col[i], (L, L)), -jnp.inf) for i in n_items]
    m_row = [jnp.maximum(m0[i], jnp.max(c_b[i], axis=0, keepdims=True)) for i in n_items]
    s_t = [jnp.exp(c_b[i] - m_row[i]) * kq[i] for i in n_items]
    sv = [_dot(vt_ext[i], s_t[i]) for i in n_items]
    lane8 = lax.broadcasted_iota(jnp.int32, m_all.shape, 1)
    for h in heads:
        cstate[h] = cm_run[h]
        m_all = jnp.where(lane8 == h, m_run[h], m_all)
    mstate[...] = m_all
    outs = []
    for i in n_items:
        num = jnp.exp(m0[i] - m_row[i]) * inter[i] + sv[i]
        hh = num[0:D] / jnp.maximum(jnp.abs(num[D:D + 1]), jnp.exp(-(b_row[i] + m_row[i])))
        outs.append(hh * lax.rsqrt(jnp.mean(hh * hh, axis=0, keepdims=True) + NORM_EPS))
    hcat = jnp.concatenate([jnp.concatenate(outs[c * H_MLSTM:(c + 1) * H_MLSTM], axis=0).T for c in chunks], axis=0)
    o_ref[...] = jax.nn.sigmoid(og) * (hcat * ng_ref[...])


def _out_proj_body(a_ref, b_ref, c_ref, wa_ref, wb_ref, wc_ref, x_ref, gt_ref, g_ref, o_ref, *, sub):
    parts = [slice(s * sub, (s + 1) * sub) for s in range(x_ref.shape[0] // sub)]
    y = [jnp.dot(a_ref[p, :].astype(BF16), wa_ref[...], preferred_element_type=F32)
         + jnp.dot(b_ref[p, :].astype(BF16), wb_ref[...], preferred_element_type=F32)
         + jnp.dot(c_ref[p, :].astype(BF16), wc_ref[...], preferred_element_type=F32) for p in parts]
    for p, y_ in zip(parts, y):
        o_ref[p, :] = x_ref[p, :] + gt_ref[...] * _rms(y_, g_ref[...])


def _out_proj(o_nsa, o_rw, o_ml, wa, wb, wc, x2, gt, g, seq):
    t, d = x2.shape
    tm = 1024
    per_b = seq // tm
    row = lambda i: (i, 0)
    fix = lambda i: (0, 0)
    return pl.pallas_call(
        functools.partial(_out_proj_body, sub=tm // 2),
        out_shape=jax.ShapeDtypeStruct((t, d), F32),
        grid=(t // tm,),
        in_specs=[
            pl.BlockSpec((tm, C_ATT), row), pl.BlockSpec((tm, C_RWKV), row), pl.BlockSpec((tm, C_MLSTM), row),
            pl.BlockSpec(wa.shape, fix), pl.BlockSpec(wb.shape, fix), pl.BlockSpec(wc.shape, fix),
            pl.BlockSpec((tm, d), row),
            pl.BlockSpec((None, 1, d), lambda i: (i // per_b, 0, 0)),
            pl.BlockSpec((1, d), fix),
        ],
        out_specs=pl.BlockSpec((tm, d), row),
        compiler_params=_params("parallel"),
        name="out_proj",
    )(o_nsa, o_rw, o_ml, wa, wb, wc, x2, gt, g)


FFN_SLAB = 1024


def _ffn_body(x_ref, sc_ref, sh_ref, gt_ref, gpre_ref, gpost_ref, wg_ref, wu_ref, wd_ref, o_ref, *, sub):
    tm = x_ref.shape[0]
    dff = wg_ref.shape[1]
    parts = [slice(s * sub, (s + 1) * sub) for s in range(tm // sub)]
    slabs = [slice(a, min(a + FFN_SLAB, dff)) for a in range(0, dff, FFN_SLAB)]
    hb = [(_rms(x_ref[p, :], gpre_ref[...]) * (1.0 + sc_ref[...]) + sh_ref[...]).astype(BF16) for p in parts]
    y = [None for _ in parts]
    for sl in slabs:
        gate = [jnp.dot(h, wg_ref[:, sl], preferred_element_type=F32) for h in hb]
        up = [jnp.dot(h, wu_ref[:, sl], preferred_element_type=F32) for h in hb]
        act = [(g_ * jax.nn.sigmoid(g_) * u_).astype(BF16) for g_, u_ in zip(gate, up)]
        down = [jnp.dot(a_, wd_ref[sl, :], preferred_element_type=F32) for a_ in act]
        y = [d_ if y_ is None else y_ + d_ for y_, d_ in zip(y, down)]
    for p, y_ in zip(parts, y):
        o_ref[p, :] = x_ref[p, :] + gt_ref[...] * _rms(y_, gpost_ref[...])


def _ffn(x2, sc, sh, gt, g_pre, g_post, wg, wu, wd, seq):
    t, d = x2.shape
    tm = 1024
    per_b = seq // tm
    row = lambda i: (i, 0)
    bat = lambda i: (i // per_b, 0, 0)
    fix = lambda i: (0, 0)
    resident = lambda a: pl.BlockSpec(a.shape, fix, pipeline_mode=pl.Buffered(1))
    return pl.pallas_call(
        functools.partial(_ffn_body, sub=tm // 2),
        out_shape=jax.ShapeDtypeStruct((t, d), F32),
        grid=(t // tm,),
        in_specs=[
            pl.BlockSpec((tm, d), row),
            pl.BlockSpec((None, 1, d), bat), pl.BlockSpec((None, 1, d), bat), pl.BlockSpec((None, 1, d), bat),
            pl.BlockSpec((1, d), fix), pl.BlockSpec((1, d), fix),
            resident(wg), resident(wu), resident(wd),
        ],
        out_specs=pl.BlockSpec((tm, d), row),
        compiler_params=_params("parallel"),
        name="ffn",
    )(x2, sc, sh, gt, g_pre, g_post, wg, wu, wd)


def _pad_cols(a, width):
    return jnp.pad(a, ((0, 0), (0, width - a.shape[1])))


def _layout_w_in(w_in):
    d_in_rw = NSA_IN + RWKV_W
    w16 = w_in.astype(BF16)
    nsa = _pad_cols(w16[:, :NSA_IN], NSA_W)
    rw = w16[:, NSA_IN:d_in_rw]
    ml = _pad_cols(w16[:, d_in_rw:], ML_W)
    return jnp.concatenate([nsa, rw, ml], axis=1)


def _layout_cmp(ck_w1, cv_w1, ck_w2, cv_w2, pe_k, pe_v):
    half = CMP_BLOCK // 2
    ck = ck_w1.reshape(2, half, HEAD_DIM, HEAD_DIM)
    cv = cv_w1.reshape(2, half, HEAD_DIM, HEAD_DIM)
    z = jnp.zeros_like(ck[0])
    top = jnp.concatenate([ck[0], z, ck[1], z], axis=-1)
    bot = jnp.concatenate([z, cv[0], z, cv[1]], axis=-1)
    wc = jnp.concatenate([top, bot], axis=1).reshape(half * 2 * HEAD_DIM, 4 * HEAD_DIM)
    pe = jnp.concatenate([pe_k, pe_v], axis=-1).reshape(2, half * 2 * HEAD_DIM)
    pe8 = jnp.pad(pe, ((0, 6), (0, 0)))
    z2 = jnp.zeros_like(ck_w2)
    w2 = jnp.concatenate([jnp.concatenate([ck_w2, z2], axis=1), jnp.concatenate([z2, cv_w2], axis=1)], axis=0)
    return wc, pe8, w2


def kernel(x, c, w_mod, b_mod, g_pre_mix, g_post_mix, g_pre_ffn, g_post_ffn, w_in, w_out, nsa_pe_k, nsa_pe_v, nsa_ck_w1, nsa_ck_w2, nsa_cv_w1, nsa_cv_w2, nsa_gate_b, nsa_out_g, rw_mu, rw_w0, rw_w2, rw_a0, rw_a2, rw_g2, rw_kk, rw_ka, rw_rk, rw_ln_w, rw_ln_b, ml_conv_w, ml_conv_b, ml_ig_b, ml_fg_b, ml_norm_g, ffn_w_gate, ffn_w_up, ffn_w_down):
    bsz, seq, d = x.shape
    depth = w_mod.shape[0]
    t = bsz * seq
    mod = _mod(c, w_mod, b_mod)
    x2 = x.reshape(t, d)
    for l in range(depth):
        sh1, sc1, gt1, sh2, sc2, gt2 = [m.reshape(bsz, 1, d) for m in jnp.split(mod[l], 6, axis=-1)]
        p_nsa, p_rw, p_ml, kse, kw, vse, vwe = _in_proj(x2, sc1, sh1, g_pre_mix[l].reshape(1, d),
                                                        _layout_w_in(w_in[l]), bsz, seq)
        wc, pe8, w2 = _layout_cmp(nsa_ck_w1[l], nsa_cv_w1[l], nsa_ck_w2[l], nsa_cv_w2[l], nsa_pe_k[l], nsa_pe_v[l])
        kv_cmp = _nsa_cmp(p_nsa, wc, pe8, w2, bsz, seq)
        gate_b = jnp.pad(nsa_gate_b[l], (0, 128 - 3 * H_ATT)).reshape(1, 128)
        o_nsa = _nsa(p_nsa, kv_cmp, kse.reshape(bsz, seq, -1), vse, kw.reshape(bsz, seq, -1), vwe, gate_b,
                     nsa_out_g[l].reshape(1, C_ATT), bsz, seq)
        vec = lambda a: a.reshape(1, -1)
        rw_args = (vec(rw_mu[l]), vec(rw_w0[l]), rw_w2[l], vec(rw_a0[l]), rw_a2[l], rw_g2[l], vec(rw_kk[l]),
                   vec(rw_ka[l]), vec(rw_rk[l]), vec(rw_ln_w[l]), vec(rw_ln_b[l]))
        ml_gate_b = jnp.pad(jnp.concatenate([ml_ig_b[l], ml_fg_b[l]]), (0, 128 - 2 * H_MLSTM)).reshape(1, 128)
        ml_args = (ml_conv_w[l], vec(ml_conv_b[l]), ml_gate_b, vec(ml_norm_g[l]))
        o_rw, o_ml = _recurrent(p_rw, rw_args, p_ml, ml_args, bsz, seq)
        wo = w_out[l].astype(BF16)
        x2 = _out_proj(o_nsa, o_rw, o_ml, wo[:C_ATT], wo[C_ATT:C_ATT + C_RWKV], wo[C_ATT + C_RWKV:], x2, gt1,
                       g_post_mix[l].reshape(1, d), seq)
        x2 = _ffn(x2, sc2, sh2, gt2, g_pre_ffn[l].reshape(1, d), g_post_ffn[l].reshape(1, d),
                  ffn_w_gate[l].astype(BF16), ffn_w_up[l].astype(BF16), ffn_w_down[l].astype(BF16), seq)
    return x2.reshape(bsz, seq, d)
```

```python
import functools

import jax
import jax.numpy as jnp
from jax import lax
from jax.experimental import pallas as pl
from jax.experimental.pallas import tpu as pltpu

F32 = jnp.float32
BF16 = jnp.bfloat16
HIGHEST = lax.Precision.HIGHEST

HEAD_DIM = 64
H_ATT = 4
C_ATT = H_ATT * HEAD_DIM
H_RWKV = 6
C_RWKV = H_RWKV * HEAD_DIM
H_MLSTM = 6
C_MLSTM = H_MLSTM * HEAD_DIM
CMP_BLOCK = 32
CMP_STRIDE = 16
SLC_BLOCK = 64
SLC_SHIFT = 6
N_SELECT = 16
WINDOW = 512
NEG = -1e30
FORCE = 1e9
MASK_BIG = 2.0 ** 100
LOG2E = 1.4426950408889634
RANK_W = 64
RANK_A = 64
RANK_G = 128
RWKV_GN_EPS = 64e-5
CHUNK = 64
SEQ_BLOCK = 512
CONV_WIDTH = 4
NORM_EPS = 1e-6

NSA_W = 768
RWKV_W = 3 * C_RWKV + RANK_W + RANK_A + RANK_G
ML_W = 4 * C_MLSTM + 128
NSA_IN = C_ATT + 6 * HEAD_DIM + 3 * H_ATT
ML_IN = 4 * C_MLSTM + 2 * H_MLSTM

VMEM_LIMIT = 56 * 1024 * 1024

TQ = 256
TK = 256


def _params(*sem):
    return pltpu.CompilerParams(dimension_semantics=sem, vmem_limit_bytes=VMEM_LIMIT)


def _dot(a, b, precise=False):
    if precise:
        return jnp.dot(a.astype(F32), b.astype(F32), preferred_element_type=F32, precision=HIGHEST)
    return jnp.dot(a.astype(BF16), b.astype(BF16), preferred_element_type=F32)


def _dot_nt(a, b, precise=False):
    dn = (((1,), (1,)), ((), ()))
    if precise:
        return lax.dot_general(a.astype(F32), b.astype(F32), dn, preferred_element_type=F32, precision=HIGHEST)
    return lax.dot_general(a.astype(BF16), b.astype(BF16), dn, preferred_element_type=F32)


def _dot_tn(a, b, precise=False):
    dn = (((0,), (0,)), ((), ()))
    if precise:
        return lax.dot_general(a.astype(F32), b.astype(F32), dn, preferred_element_type=F32, precision=HIGHEST)
    return lax.dot_general(a.astype(BF16), b.astype(BF16), dn, preferred_element_type=F32)


def _split3(x):
    hi = x.astype(BF16)
    rest = x - hi.astype(F32)
    mid = rest.astype(BF16)
    lo = (rest - mid.astype(F32)).astype(BF16)
    return hi, mid, lo


def _dot_mask_lhs(mask01, x):
    mb = mask01.astype(BF16)
    hi, mid, lo = _split3(x)
    dot = lambda p: jnp.dot(mb, p, preferred_element_type=F32)
    return (dot(lo) + dot(mid)) + dot(hi)


def _dot_nt_mask_rhs(x, mask01):
    mb = mask01.astype(BF16)
    hi, mid, lo = _split3(x)
    dot = lambda p: lax.dot_general(p, mb, (((1,), (1,)), ((), ())), preferred_element_type=F32)
    return (dot(lo) + dot(mid)) + dot(hi)


def _dot3(a, b):
    a_hi, a_lo, _ = _split3(a)
    b_hi, b_lo, _ = _split3(b)
    dot = lambda p, q: jnp.dot(p, q, preferred_element_type=F32)
    return (dot(a_lo, b_hi) + dot(a_hi, b_lo)) + dot(a_hi, b_hi)


def _rms(x, g):
    return x * lax.rsqrt(jnp.mean(x * x, axis=-1, keepdims=True) + NORM_EPS) * g


def _mod_body(c_ref, w_ref, b_ref, o_ref):
    c = c_ref[...]
    cs = c * jax.nn.sigmoid(c)
    o_ref[...] = _dot(cs, w_ref[...], precise=True) + b_ref[...]


def _mod(c, w_mod, b_mod):
    depth, d, n = w_mod.shape
    bsz = c.shape[0]
    tn = 1536
    return pl.pallas_call(
        _mod_body,
        out_shape=jax.ShapeDtypeStruct((depth, bsz, n), F32),
        grid=(depth, n // tn),
        in_specs=[
            pl.BlockSpec((bsz, d), lambda l, j: (0, 0)),
            pl.BlockSpec((None, d, tn), lambda l, j: (l, 0, j)),
            pl.BlockSpec((None, 1, tn), lambda l, j: (l, 0, j)),
        ],
        out_specs=pl.BlockSpec((None, bsz, tn), lambda l, j: (l, 0, j)),
        compiler_params=_params("parallel", "parallel"),
        name="mod",
    )(c, w_mod, b_mod.reshape(depth, 1, n))


def _in_proj_body(x_ref, sc_ref, sh_ref, g_ref, w_ref, nsa_ref, rw_ref, ml_ref, kse_ref, kw_ref, vse_ref,
                  vwe_ref, *, per_b):
    tm = x_ref.shape[0]
    h = _rms(x_ref[...], g_ref[...]) * (1.0 + sc_ref[...]) + sh_ref[...]
    hb = h.astype(BF16)
    nsa = jnp.dot(hb, w_ref[:, 0:NSA_W], preferred_element_type=F32)
    nsa_ref[...] = nsa
    rw_ref[...] = jnp.dot(hb, w_ref[:, NSA_W:NSA_W + RWKV_W], preferred_element_type=F32)
    ml_ref[...] = jnp.dot(hb, w_ref[:, NSA_W + RWKV_W:], preferred_element_type=F32)
    D = HEAD_DIM
    off = C_ATT + 2 * D
    lane = lax.broadcasted_iota(jnp.int32, (tm, 2 * D), 1)
    pos = (pl.program_id(0) % per_b) * tm + lax.broadcasted_iota(jnp.int32, (tm, 2 * D), 0)
    expand = jnp.where(lane - D == jnp.right_shift(pos, SLC_SHIFT), MASK_BIG, 0.0)
    kse_ref[...] = jnp.where(lane < D, nsa[:, off:off + 2 * D], expand).astype(BF16)
    kw_ref[...] = nsa[:, off + 2 * D:off + 3 * D].astype(BF16)
    tail = jnp.where(lax.broadcasted_iota(jnp.int32, (D, tm), 0) == 0, 1.0, 0.0)
    vse_ref[...] = jnp.concatenate([nsa[:, off + D:off + 2 * D].T, tail], axis=0).astype(BF16)
    vwe_ref[...] = jnp.concatenate([nsa[:, off + 3 * D:off + 4 * D].T, tail], axis=0).astype(BF16)


def _in_proj(x2, sc, sh, g, w_p, bsz, seq):
    t, d = x2.shape
    tm = 512
    per_b = seq // tm
    n_slc = seq // SLC_BLOCK
    assert n_slc == HEAD_DIM, "kse packs the key and one column per selection block into 128 lanes"
    row = lambda i: (i, 0)
    bat = lambda i: (i // per_b, 0, 0)
    fix = lambda i: (0, 0)
    col = lambda i: (i // per_b, 0, i % per_b)
    return pl.pallas_call(
        functools.partial(_in_proj_body, per_b=per_b),
        out_shape=(jax.ShapeDtypeStruct((t, NSA_W), F32),
                   jax.ShapeDtypeStruct((t, RWKV_W), F32),
                   jax.ShapeDtypeStruct((t, ML_W), F32),
                   jax.ShapeDtypeStruct((t, 2 * HEAD_DIM), BF16),
                   jax.ShapeDtypeStruct((t, HEAD_DIM), BF16),
                   jax.ShapeDtypeStruct((bsz, 128, seq), BF16),
                   jax.ShapeDtypeStruct((bsz, 128, seq), BF16)),
        grid=(t // tm,),
        in_specs=[
            pl.BlockSpec((tm, d), row),
            pl.BlockSpec((None, 1, d), bat),
            pl.BlockSpec((None, 1, d), bat),
            pl.BlockSpec((1, d), fix),
            pl.BlockSpec(w_p.shape, fix),
        ],
        out_specs=(pl.BlockSpec((tm, NSA_W), row), pl.BlockSpec((tm, RWKV_W), row),
                   pl.BlockSpec((tm, ML_W), row), pl.BlockSpec((tm, 2 * HEAD_DIM), row),
                   pl.BlockSpec((tm, HEAD_DIM), row), pl.BlockSpec((None, 128, tm), col),
                   pl.BlockSpec((None, 128, tm), col)),
        compiler_params=_params("parallel"),
        name="in_proj",
    )(x2, sc, sh, g, w_p)


def _nsa_cmp_body(x_ref, wc_ref, pe_ref, w2_ref, o_ref):
    half = CMP_BLOCK // 2
    width = x_ref.shape[1]
    g = x_ref.shape[0] // half
    f = None
    for tl in range(half):
        part = _dot3(x_ref[pl.ds(tl, g, stride=half), :], wc_ref[tl * width:(tl + 1) * width, :])
        f = part if f is None else f + part
    c = _dot(pe_ref[...], wc_ref[...], precise=True)
    second = pltpu.roll(f[:, 128:256], g - 1, 0)
    pre = f[:, 0:128] + second + c[0:1, 0:128] + c[1:2, 128:256]
    act = jax.nn.gelu(pre, approximate=True)
    out = _dot(act, w2_ref[...], precise=True)
    rows = lax.broadcasted_iota(jnp.int32, out.shape, 0)
    o_ref[...] = jnp.where(rows < g - 1, out, 0.0)


def _nsa_cmp(p_nsa, wc, pe8, w2, bsz, seq):
    g = seq // (CMP_BLOCK // 2)
    kc_block = C_ATT // (2 * HEAD_DIM)
    return pl.pallas_call(
        _nsa_cmp_body,
        out_shape=jax.ShapeDtypeStruct((bsz, g, 128), F32),
        grid=(bsz,),
        in_specs=[
            pl.BlockSpec((seq, 2 * HEAD_DIM), lambda b: (b, kc_block)),
            pl.BlockSpec(wc.shape, lambda b: (0, 0)),
            pl.BlockSpec(pe8.shape, lambda b: (0, 0)),
            pl.BlockSpec(w2.shape, lambda b: (0, 0)),
        ],
        out_specs=pl.BlockSpec((None, g, 128), lambda b: (b, 0, 0)),
        compiler_params=_params("parallel"),
        name="nsa_cmp",
    )(p_nsa, wc, pe8, w2)


def _nsa_body(q_ref, gts_ref, kvc_ref, kse_ref, vse_ref, kw_ref, vwe_ref, gb_ref, g_ref, o_ref,
              m_s, acc_s, s_even, s_odd, ocmp, cnt_ref, *, n_slc):
    tq = q_ref.shape[0]
    n_cmp = kvc_ref.shape[0]
    qi = pl.program_id(1)
    q0 = qi * tq
    scale = HEAD_DIM ** -0.5
    D = HEAD_DIM
    qt = q_ref[...].T

    kc = kvc_ref[:, 0:D]
    vc = kvc_ref[:, D:2 * D]
    nidx = lax.broadcasted_iota(jnp.int32, (n_cmp, tq), 0)
    pos = q0 + lax.broadcasted_iota(jnp.int32, (n_cmp, tq), 1)
    cmask = (nidx * CMP_STRIDE + (CMP_BLOCK - 1)) <= pos
    heads = range(H_ATT)
    sc = [_dot3(kc, qt[h * D:(h + 1) * D]) for h in heads]
    sc = [jnp.where(cmask, x * scale, NEG) for x in sc]
    ec = [jnp.where(cmask, jnp.exp(x - jnp.max(x, axis=0, keepdims=True)), 0.0) for x in sc]
    pc = [e / jnp.maximum(jnp.sum(e, axis=0, keepdims=True), 1e-30) for e in ec]
    oc = [_dot_tn(vc, p) for p in pc]
    for h in heads:
        ocmp[h * D:(h + 1) * D, :] = oc[h]
    psum = sum(pc[1:], pc[0])

    jrow = lax.broadcasted_iota(jnp.int32, (n_slc, n_cmp), 0) * SLC_BLOCK
    ncol = lax.broadcasted_iota(jnp.int32, (n_slc, n_cmp), 1) * CMP_STRIDE
    overlap_t = jnp.where((ncol < jrow + SLC_BLOCK) & (ncol + CMP_BLOCK > jrow), 1.0, 0.0)
    imp_t = _dot_mask_lhs(overlap_t, psum)
    jj = lax.broadcasted_iota(jnp.int32, (n_slc, tq), 0)
    cur = jnp.right_shift(q0 + lax.broadcasted_iota(jnp.int32, (n_slc, tq), 1), SLC_SHIFT)
    forced = (jj == 0) | (jj == cur) | (jj == cur - 1)
    score = jnp.where(forced, FORCE, jnp.where(jj <= cur, imp_t, NEG))
    group = 8
    batch = 16
    per_tile = tq // SLC_BLOCK
    cnt_ref[...] = jnp.zeros_like(cnt_ref)

    def count_batch(first_j2):
        for gi in range(n_slc // group):
            sg = score[gi * group:(gi + 1) * group]
            acc = None
            for j2 in range(first_j2, first_j2 + batch):
                row = score[j2:j2 + 1, :]
                if gi * group > j2:
                    hit = jnp.where(row >= sg, 1.0, 0.0)
                elif (gi + 1) * group - 1 <= j2:
                    hit = jnp.where(row > sg, 1.0, 0.0)
                else:
                    hit = jnp.where(lax.broadcasted_iota(jnp.int32, (group, tq), 0) + gi * group > j2,
                                    jnp.where(row >= sg, 1.0, 0.0), jnp.where(row > sg, 1.0, 0.0))
                acc = hit if acc is None else acc + hit
            cnt_ref[gi * group:(gi + 1) * group, :] += acc

    count_batch(0)
    for first_j2 in range(batch, n_slc, batch):
        pl.when(first_j2 < (qi + 1) * per_tile)(functools.partial(count_batch, first_j2))
    cnt = cnt_ref[...]
    selm = jnp.where(cnt < min(N_SELECT, n_slc), 0.0, -1.0).astype(BF16)

    qs = (qt * (scale * LOG2E)).astype(BF16)
    rhs_w = [qs[h * D:(h + 1) * D] for h in heads]
    rhs_s = [jnp.concatenate([rhs_w[h], selm], axis=0) for h in heads]
    m_s[...] = jnp.full(m_s.shape, NEG, F32)
    acc_s[...] = jnp.zeros_like(acc_s)

    def scores_into(buf, j):
        k = kse_ref[pl.ds(pl.multiple_of(j * TK, TK), TK), :]
        for h in heads:
            buf[h] = jnp.dot(k, rhs_s[h], preferred_element_type=F32)

    def consume(buf, j, mask):
        vt = vse_ref[:, pl.ds(pl.multiple_of(j * TK, TK), TK)]
        for h in heads:
            s = buf[h]
            if mask is not None:
                s = jnp.where(mask, s, NEG)
            m_old = m_s[h:h + 1, :]
            m_new = jnp.maximum(m_old, jnp.max(s, axis=0, keepdims=True))
            p = jnp.exp2(s - m_new).astype(BF16)
            acc_s[h] = jnp.exp2(m_old - m_new) * acc_s[h] + jnp.dot(vt, p, preferred_element_type=F32)
            m_s[h:h + 1, :] = m_new

    kpos = lax.broadcasted_iota(jnp.int32, (TK, tq), 0)
    qpos = lax.broadcasted_iota(jnp.int32, (TK, tq), 1)
    last = jnp.maximum(qi - 1, 0)
    scores_into(s_even, qi)
    scores_into(s_odd, 0)
    consume(s_even, qi, kpos <= qpos)

    def body(jp, carry):
        j = 2 * jp
        scores_into(s_even, jnp.minimum(j + 1, last))
        consume(s_odd, j, None)
        scores_into(s_odd, jnp.minimum(j + 2, last))
        consume(s_even, j + 1, None)
        return carry

    lax.fori_loop(0, qi // 2, body, 0)

    @pl.when(qi % 2 == 1)
    def _():
        consume(s_odd, qi - 1, None)

    span = WINDOW + tq
    w0 = pl.multiple_of(jnp.maximum(q0 - WINDOW, 0), TK)
    kw = kw_ref[pl.ds(w0, span), :]
    vwt = vwe_ref[:, pl.ds(w0, span)]
    kabs = w0 + lax.broadcasted_iota(jnp.int32, (span, tq), 0)
    qabs = q0 + lax.broadcasted_iota(jnp.int32, (span, tq), 1)
    wmask = (kabs <= qabs) & (kabs > qabs - WINDOW)
    sw = [jnp.where(wmask, jnp.dot(kw, rhs_w[h], preferred_element_type=F32), NEG) for h in heads]
    pw = [jnp.exp2(x - jnp.max(x, axis=0, keepdims=True)).astype(BF16) for x in sw]
    acc_w = [jnp.dot(vwt, p, preferred_element_type=F32) for p in pw]

    gate = jax.nn.sigmoid((gts_ref[...] + gb_ref[...]).T)
    outs = []
    for h in heads:
        a_s = acc_s[h]
        a_w = acc_w[h]
        outs.append(gate[h:h + 1] * ocmp[h * D:(h + 1) * D, :]
                    + gate[H_ATT + h:H_ATT + h + 1] * (a_s[0:D] / a_s[D:D + 1])
                    + gate[2 * H_ATT + h:2 * H_ATT + h + 1] * (a_w[0:D] / a_w[D:D + 1]))
    ot = jnp.concatenate(outs, axis=0)
    ot = ot * lax.rsqrt(jnp.mean(ot * ot, axis=0, keepdims=True) + NORM_EPS)
    o_ref[...] = ot.T * g_ref[...]


def _nsa(p_nsa, kv_cmp, kse, vse, kw, vwe, gate_b, out_g, bsz, seq):
    t = p_nsa.shape[0]
    nq = seq // TQ
    n_slc = seq // SLC_BLOCK
    g = kv_cmp.shape[1]
    row = lambda b, i: (b * nq + i, 0)
    per_b = lambda b, i: (b, 0, 0)
    fix = lambda b, i: (0, 0)
    return pl.pallas_call(
        functools.partial(_nsa_body, n_slc=n_slc),
        out_shape=jax.ShapeDtypeStruct((t, C_ATT), F32),
        grid=(bsz, nq),
        in_specs=[
            pl.BlockSpec((TQ, C_ATT), row),
            pl.BlockSpec((TQ, 128), lambda b, i: (b * nq + i, NSA_W // 128 - 1)),
            pl.BlockSpec((None, g, 128), per_b),
            pl.BlockSpec((None, seq, HEAD_DIM + n_slc), per_b),
            pl.BlockSpec((None, 128, seq), per_b),
            pl.BlockSpec((None, seq, HEAD_DIM), per_b),
            pl.BlockSpec((None, 128, seq), per_b),
            pl.BlockSpec((1, 128), fix),
            pl.BlockSpec((1, C_ATT), fix),
        ],
        out_specs=pl.BlockSpec((TQ, C_ATT), row),
        scratch_shapes=[pltpu.VMEM((8, TQ), F32), pltpu.VMEM((H_ATT, 128, TQ), F32),
                        pltpu.VMEM((H_ATT, TK, TQ), F32), pltpu.VMEM((H_ATT, TK, TQ), F32),
                        pltpu.VMEM((C_ATT, TQ), F32), pltpu.VMEM((n_slc, TQ), F32)],
        compiler_params=_params("parallel", "arbitrary"),
        name="nsa",
    )(p_nsa, p_nsa, kv_cmp, kse, vse, kw, vwe, gate_b, out_g)


def _tri(n, strict):
    r = lax.broadcasted_iota(jnp.int32, (n, n), 0)
    c = lax.broadcasted_iota(jnp.int32, (n, n), 1)
    return (c < r) if strict else (c <= r)


def _block_diag(x):
    xb = x.astype(BF16)
    first = lax.broadcasted_iota(jnp.int32, xb.shape, 1) < xb.shape[1] // 2
    zero = jnp.zeros_like(xb)
    return jnp.concatenate([jnp.where(first, xb, zero), jnp.where(first, zero, xb)], axis=0)


def _unit_lower_inverse_pairs(ms, eye2):
    n = eye2.shape[0]
    idx = range(len(ms))
    r = lax.broadcasted_iota(jnp.int32, (n, 2 * n), 0)
    c = jnp.bitwise_and(lax.broadcasted_iota(jnp.int32, (n, 2 * n), 1), n - 1)
    base = 8
    diag = jnp.right_shift(r, 3) == jnp.right_shift(c, 3)
    m8 = [jnp.where(diag, m, 0.0) for m in ms]
    t = [eye2 + m for m in m8]
    p2 = [_dot(m, _block_diag(m)) for m in m8]
    t = [t[i] + _dot(p2[i], _block_diag(t[i])) for i in idx]
    p4 = [_dot(p, _block_diag(p)) for p in p2]
    t = [t[i] + _dot(p4[i], _block_diag(t[i])) for i in idx]
    b = base
    while b < n:
        sh = b.bit_length() - 1
        pair = jnp.right_shift(r, sh + 1) == jnp.right_shift(c, sh + 1)
        lower_left = pair & (jnp.right_shift(r, sh) != jnp.right_shift(c, sh))
        left = [_dot(t[i], _block_diag(jnp.where(lower_left, ms[i], 0.0))) for i in idx]
        t = [t[i] + _dot(left[i], _block_diag(t[i])) for i in idx]
        b *= 2
    return t


def _rwkv_body(p_ref, mu_ref, w0_ref, w2_ref, a0_ref, a2_ref, g2_ref, kk_ref, ka_ref, rk_ref, lnw_ref,
               lnb_ref, o_ref, state, prev):
    L = CHUNK
    C = C_RWKV
    tb = p_ref.shape[0]
    chunks = range(tb // L)

    x = p_ref[...]
    rows = lax.broadcasted_iota(jnp.int32, x.shape, 0)
    shifted = jnp.where(rows == 0, prev[0:1, :], pltpu.roll(x, 1, 0))
    prev[0:1, :] = x[tb - 1:tb, :]
    x = x + mu_ref[...] * (shifted - x)
    r = x[:, 0:C]
    k = x[:, C:2 * C]
    v = x[:, 2 * C:3 * C]
    xw = x[:, 3 * C:3 * C + RANK_W]
    xa = x[:, 3 * C + RANK_W:3 * C + RANK_W + RANK_A]
    xg = x[:, 3 * C + RANK_W + RANK_A:]
    w = -jax.nn.softplus(-(w0_ref[...] + _dot3(jnp.tanh(xw), w2_ref[...]))) - 0.5
    logw = -jnp.exp(w)
    a = jax.nn.sigmoid(a0_ref[...] + _dot3(xa, a2_ref[...]))
    g = _dot(jax.nn.sigmoid(xg), g2_ref[...])
    kkf = k * kk_ref[...]
    kmod = k * (1.0 + (a - 1.0) * ka_ref[...])
    rk = r * kmod * rk_ref[...]

    tri_incl = jnp.where(_tri(L, strict=False), 1.0, 0.0)
    cum = jnp.concatenate([_dot_mask_lhs(tri_incl, logw[c * L:(c + 1) * L]) for c in chunks], axis=0)
    cum_last = [cum[(c + 1) * L - 1:(c + 1) * L, :] for c in chunks]
    cum_end = jnp.concatenate([jnp.broadcast_to(cl, (L, C)) for cl in cum_last], axis=0)
    e_pos = jnp.exp(cum)
    e_prev = jnp.exp(cum - logw)
    e_neg = jnp.exp(-cum)
    e_rem = jnp.exp(cum_end - cum)
    N = HEAD_DIM
    PW = 2 * N
    pairs = range(H_RWKV // 2)
    lane_l = lax.broadcasted_iota(jnp.int32, (L, PW), 1)
    row_l = lax.broadcasted_iota(jnp.int32, (L, PW), 0)
    col_l = jnp.bitwise_and(lane_l, N - 1)
    first = lane_l < N
    first2 = lax.broadcasted_iota(jnp.int32, (2 * L, PW), 1) < N
    strict2 = col_l < row_l
    incl2 = col_l <= row_l
    eye2 = jnp.where(col_l == row_l, 1.0, 0.0)

    def halves(x_):
        lane = lax.broadcasted_iota(jnp.int32, x_.shape, 1)
        sa = jnp.sum(jnp.where(lane < N, x_, 0.0), axis=-1, keepdims=True)
        sb = jnp.sum(jnp.where(lane < N, 0.0, x_), axis=-1, keepdims=True)
        return jnp.where(lane < N, sa, sb)

    ps = [slice(p * PW, (p + 1) * PW) for p in pairs]
    kk_p = [kkf[:, sl] for sl in ps]
    kk_p = [x_ / jnp.maximum(jnp.sqrt(halves(x_ * x_)), 1e-12) for x_ in kk_p]
    alpha_f = [-kk_p[p] * e_prev[:, ps[p]] for p in pairs]
    r_f = [r[:, ps[p]] * e_pos[:, ps[p]] for p in pairs]
    beta = [kk_p[p] * a[:, ps[p]] for p in pairs]
    beta_f = [beta[p] * e_neg[:, ps[p]] for p in pairs]
    k_f = [kmod[:, ps[p]] * e_neg[:, ps[p]] for p in pairs]
    beta_e = [beta[p] * e_rem[:, ps[p]] for p in pairs]
    k_e = [kmod[:, ps[p]] * e_rem[:, ps[p]] for p in pairs]
    items = [(c, p) for c in chunks for p in pairs]
    n_items = range(len(items))
    rs = [slice(c * L, (c + 1) * L) for c, _ in items]
    pi = [p for _, p in items]
    lhs = [jnp.concatenate([alpha_f[pi[i]][rs[i]], r_f[pi[i]][rs[i]]], axis=0).astype(BF16) for i in n_items]
    zero16 = jnp.zeros((2 * L, PW), BF16)
    lhs4 = [jnp.concatenate([jnp.where(first2, x_, zero16), jnp.where(first2, zero16, x_)], axis=0) for x_ in lhs]
    bk = [(beta_f[pi[i]][rs[i]].astype(BF16), k_f[pi[i]][rs[i]].astype(BF16)) for i in n_items]
    v_i = [v[rs[i], ps[pi[i]]] for i in n_items]
    out1 = [_dot_nt(lhs4[i], jnp.concatenate([bk[i][0], bk[i][1]], axis=0)) for i in n_items]
    out2 = [_dot_nt(lhs4[i], jnp.concatenate([bk[i][1], bk[i][0]], axis=0)) for i in n_items]
    m_ab = [jnp.where(strict2, jnp.where(first, out1[i][0:L], out2[i][2 * L:3 * L]), 0.0) for i in n_items]
    m_ak = [jnp.where(strict2, jnp.where(first, out2[i][0:L], out1[i][2 * L:3 * L]), 0.0) for i in n_items]
    m_rb = [jnp.where(incl2, jnp.where(first, out1[i][L:2 * L], out2[i][3 * L:4 * L]), 0.0) for i in n_items]
    m_rk = [jnp.where(incl2, jnp.where(first, out2[i][L:2 * L], out1[i][3 * L:4 * L]), 0.0) for i in n_items]
    t_inv = _unit_lower_inverse_pairs(m_ab, eye2)
    mv = [_dot(jnp.concatenate([m_ak[i], m_rk[i]], axis=0), _block_diag(v_i[i])) for i in n_items]
    pq = [_dot(t_inv[i], jnp.concatenate([_block_diag(alpha_f[pi[i]][rs[i]]), _block_diag(mv[i][0:L])], axis=1))
          for i in n_items]
    ry = [_dot(m_rb[i], jnp.concatenate([_block_diag(pq[i][:, 0:PW]), _block_diag(pq[i][:, PW:2 * PW])], axis=1))
          for i in n_items]
    r_eff = [r_f[pi[i]][rs[i]] + ry[i][:, 0:PW] for i in n_items]
    y0 = [ry[i][:, PW:2 * PW] + mv[i][L:2 * L] for i in n_items]
    zero_l = jnp.zeros((L, PW), F32)
    gh = [_dot_tn(jnp.concatenate([beta_e[pi[i]][rs[i]], k_e[pi[i]][rs[i]]], axis=0),
                  jnp.concatenate([pq[i], jnp.concatenate([zero_l, v_i[i]], axis=1)], axis=0)) for i in n_items]
    g_off = [jnp.where(first, gh[i][0:N, 0:PW], gh[i][N:2 * N, 0:PW]) for i in n_items]
    h_add = [jnp.where(first, gh[i][0:N, PW:2 * PW], gh[i][N:2 * N, PW:2 * PW]) for i in n_items]
    gam = [halves(eye2 * jnp.exp(cum_last[c][:, ps[p]])) for c, p in items]
    z = [state[p] for p in pairs]
    y_parts = [[] for _ in pairs]
    for c in chunks:
        idx = [c * len(pairs) + p for p in pairs]
        zb = [_block_diag(z[p]) for p in pairs]
        for p, i in enumerate(idx):
            y_parts[p].append(_dot(r_eff[i], zb[p]) + y0[i])
        z = [z[p] * gam[i] + _dot(g_off[i], zb[p]) + h_add[i] for p, i in enumerate(idx)]
    for p in pairs:
        state[p] = z[p]
    yn = []
    for p in pairs:
        y_p = jnp.concatenate(y_parts[p], axis=0)
        mu_y = halves(y_p) * (1.0 / N)
        dev = y_p - mu_y
        var = halves(dev * dev) * (1.0 / N)
        yn.append(dev * lax.rsqrt(var + RWKV_GN_EPS))
    bonus = [halves(rk[:, ps[p]]) * v[:, ps[p]] for p in pairs]
    yn = jnp.concatenate(yn, axis=-1)
    bonus = jnp.concatenate(bonus, axis=-1)
    o_ref[...] = (yn * lnw_ref[...] + lnb_ref[...] + bonus) * g


N_RWKV_ARGS = 11
N_ML_ARGS = 4


def _recurrent_body(*refs):
    p_rw, rw_args = refs[0], refs[1:1 + N_RWKV_ARGS]
    p_ml, ml_args = refs[1 + N_RWKV_ARGS], refs[2 + N_RWKV_ARGS:2 + N_RWKV_ARGS + N_ML_ARGS]
    o_rw, o_ml, state, prev_rw, cstate, mstate, prev_ml = refs[2 + N_RWKV_ARGS + N_ML_ARGS:]

    @pl.when(pl.program_id(1) == 0)
    def _():
        for ref in (state, prev_rw, cstate, mstate, prev_ml):
            ref[...] = jnp.zeros_like(ref)

    _rwkv_body(p_rw, *rw_args, o_rw, state, prev_rw)
    _mlstm_body(p_ml, *ml_args, o_ml, cstate, mstate, prev_ml)


def _recurrent(p_rw, rw_args, p_ml, ml_args, bsz, seq):
    t = p_rw.shape[0]
    tb = SEQ_BLOCK
    nb = seq // tb
    row = lambda b, c: (b * nb + c, 0)
    fix = lambda b, c: (0, 0)
    assert len(rw_args) == N_RWKV_ARGS and len(ml_args) == N_ML_ARGS
    return pl.pallas_call(
        _recurrent_body,
        out_shape=(jax.ShapeDtypeStruct((t, C_RWKV), F32), jax.ShapeDtypeStruct((t, C_MLSTM), F32)),
        grid=(bsz, nb),
        in_specs=([pl.BlockSpec((tb, RWKV_W), row)] + [pl.BlockSpec(a.shape, fix) for a in rw_args]
                  + [pl.BlockSpec((tb, ML_W), row)] + [pl.BlockSpec(a.shape, fix) for a in ml_args]),
        out_specs=(pl.BlockSpec((tb, C_RWKV), row), pl.BlockSpec((tb, C_MLSTM), row)),
        scratch_shapes=[pltpu.VMEM((H_RWKV // 2, HEAD_DIM, 2 * HEAD_DIM), F32), pltpu.VMEM((8, RWKV_W), F32),
                        pltpu.VMEM((H_MLSTM, 128, HEAD_DIM), F32), pltpu.VMEM((8, 128), F32),
                        pltpu.VMEM((8, 2 * C_MLSTM), F32)],
        compiler_params=_params("parallel", "arbitrary"),
        name="recurrent",
    )(p_rw, *rw_args, p_ml, *ml_args)


def _mlstm_body(p_ref, cw_ref, cb_ref, gb_ref, ng_ref, o_ref, cstate, mstate, prev):
    L = CHUNK
    C = C_MLSTM
    D = HEAD_DIM
    tb = p_ref.shape[0]
    chunks = range(tb // L)

    qk_in = p_ref[:, 0:2 * C]
    pv = prev[...]
    rows8 = lax.broadcasted_iota(jnp.int32, pv.shape, 0)
    conv = qk_in * cw_ref[CONV_WIDTH - 1:CONV_WIDTH, :] + cb_ref[...]
    for d in range(1, CONV_WIDTH):
        rolled = pltpu.roll(qk_in, d, 0)
        top = jnp.where(rows8 < d, pltpu.roll(pv, d, 0), rolled[0:8])
        sh = jnp.concatenate([top, rolled[8:]], axis=0)
        conv = conv + sh * cw_ref[CONV_WIDTH - 1 - d:CONV_WIDTH - d, :]
    prev[...] = qk_in[tb - 8:tb, :]
    qk = conv * jax.nn.sigmoid(conv)
    q = qk[:, 0:C]
    k = qk[:, C:2 * C] * (D ** -0.5)
    v = p_ref[:, 2 * C:3 * C]
    og = p_ref[:, 3 * C:4 * C]
    gates = p_ref[:, 4 * C:4 * C + 128] + gb_ref[...]
    lane = lax.broadcasted_iota(jnp.int32, gates.shape, 1)
    is_f = (lane >= H_MLSTM) & (lane < 2 * H_MLSTM)
    gl = jnp.where(is_f, jax.nn.log_sigmoid(gates), gates)
    tri_incl = jnp.where(_tri(L, strict=False), 1.0, 0.0)
    bcum = [_dot_mask_lhs(tri_incl, gl[c * L:(c + 1) * L]) for c in chunks]
    gl_t = [gl[c * L:(c + 1) * L].T for c in chunks]
    bcum_t = [_dot_nt_mask_rhs(g_, tri_incl) for g_ in gl_t]
    m_all = mstate[...]
    src = lax.broadcasted_iota(jnp.int32, (L, L), 0)
    qry = lax.broadcasted_iota(jnp.int32, (L, L), 1)
    causal_t = src <= qry
    ones_row = jnp.where(lax.broadcasted_iota(jnp.int32, (D, L), 0) == 0, 1.0, 0.0)

    heads = range(H_MLSTM)
    items = [(c, h) for c in chunks for h in heads]
    n_items = range(len(items))
    q_t = [q[c * L:(c + 1) * L].T for c in chunks]
    v_t = [v[c * L:(c + 1) * L].T for c in chunks]
    qt_i = [q_t[c][h * D:(h + 1) * D] for c, h in items]
    vt_ext = [jnp.concatenate([v_t[c][h * D:(h + 1) * D], ones_row], axis=0) for c, h in items]
    k_i = [k[c * L:(c + 1) * L, h * D:(h + 1) * D] for c, h in items]
    c_col = [gl[c * L:(c + 1) * L, h:h + 1] - bcum[c][:, H_MLSTM + h:H_MLSTM + h + 1] for c, h in items]
    b_row = [bcum_t[c][H_MLSTM + h:H_MLSTM + h + 1, :] for c, h in items]
    c_row = [gl_t[c][h:h + 1, :] - b_row[i] for i, (c, h) in enumerate(items)]
    g_tot = [bcum[c][L - 1:L, H_MLSTM + h:H_MLSTM + h + 1] for c, h in items]
    u_row = [g_tot[i] + c_row[i] for i in n_items]
    u_max = [jnp.max(x_, axis=1, keepdims=True) for x_ in u_row]
    m0 = []
    m_run = [m_all[0:1, h:h + 1] for h in heads]
    for c in chunks:
        for h in heads:
            i = c * H_MLSTM + h
            m0.append(m_run[h])
            m_run[h] = jnp.maximum(g_tot[i] + m_run[h], u_max[i])
    m_next = [m0[i + H_MLSTM] if i + H_MLSTM < len(items) else m_run[items[i][1]] for i in n_items]
    kq = [_dot(k_i[i], qt_i[i]) for i in n_items]
    kv = [_dot(vt_ext[i] * jnp.exp(u_row[i] - m_next[i]), k_i[i]) for i in n_items]
    cm = []
    cm_run = [cstate[h] for h in heads]
    for c in chunks:
        for h in heads:
            i = c * H_MLSTM + h
            cm.append(cm_run[h])
            cm_run[h] = jnp.exp(g_tot[i] + m0[i] - m_next[i]) * cm_run[h] + kv[i]
    inter = [_dot(cm[i], qt_i[i]) for i in n_items]
    c_b = [jnp.where(causal_t, jnp.broadcast_to(c_col[i], (L, L)), -jnp.inf) for i in n_items]
    m_row = [jnp.maximum(m0[i], jnp.max(c_b[i], axis=0, keepdims=True)) for i in n_items]
    s_t = [jnp.exp(c_b[i] - m_row[i]) * kq[i] for i in n_items]
    sv = [_dot(vt_ext[i], s_t[i]) for i in n_items]
    lane8 = lax.broadcasted_iota(jnp.int32, m_all.shape, 1)
    for h in heads:
        cstate[h] = cm_run[h]
        m_all = jnp.where(lane8 == h, m_run[h], m_all)
    mstate[...] = m_all
    outs = []
    for i in n_items:
        num = jnp.exp(m0[i] - m_row[i]) * inter[i] + sv[i]
        hh = num[0:D] / jnp.maximum(jnp.abs(num[D:D + 1]), jnp.exp(-(b_row[i] + m_row[i])))
        outs.append(hh * lax.rsqrt(jnp.mean(hh * hh, axis=0, keepdims=True) + NORM_EPS))
    hcat = jnp.concatenate([jnp.concatenate(outs[c * H_MLSTM:(c + 1) * H_MLSTM], axis=0).T for c in chunks], axis=0)
    o_ref[...] = jax.nn.sigmoid(og) * (hcat * ng_ref[...])


def _out_proj_body(a_ref, b_ref, c_ref, wa_ref, wb_ref, wc_ref, x_ref, gt_ref, g_ref, o_ref, *, sub):
    parts = [slice(s * sub, (s + 1) * sub) for s in range(x_ref.shape[0] // sub)]
    y = [jnp.dot(a_ref[p, :].astype(BF16), wa_ref[...], preferred_element_type=F32)
         + jnp.dot(b_ref[p, :].astype(BF16), wb_ref[...], preferred_element_type=F32)
         + jnp.dot(c_ref[p, :].astype(BF16), wc_ref[...], preferred_element_type=F32) for p in parts]
    for p, y_ in zip(parts, y):
        o_ref[p, :] = x_ref[p, :] + gt_ref[...] * _rms(y_, g_ref[...])


def _out_proj(o_nsa, o_rw, o_ml, wa, wb, wc, x2, gt, g, seq):
    t, d = x2.shape
    tm = 1024
    per_b = seq // tm
    row = lambda i: (i, 0)
    fix = lambda i: (0, 0)
    return pl.pallas_call(
        functools.partial(_out_proj_body, sub=tm // 2),
        out_shape=jax.ShapeDtypeStruct((t, d), F32),
        grid=(t // tm,),
        in_specs=[
            pl.BlockSpec((tm, C_ATT), row), pl.BlockSpec((tm, C_RWKV), row), pl.BlockSpec((tm, C_MLSTM), row),
            pl.BlockSpec(wa.shape, fix), pl.BlockSpec(wb.shape, fix), pl.BlockSpec(wc.shape, fix),
            pl.BlockSpec((tm, d), row),
            pl.BlockSpec((None, 1, d), lambda i: (i // per_b, 0, 0)),
            pl.BlockSpec((1, d), fix),
        ],
        out_specs=pl.BlockSpec((tm, d), row),
        compiler_params=_params("parallel"),
        name="out_proj",
    )(o_nsa, o_rw, o_ml, wa, wb, wc, x2, gt, g)


FFN_SLAB = 1024


def _ffn_body(x_ref, sc_ref, sh_ref, gt_ref, gpre_ref, gpost_ref, wg_ref, wu_ref, wd_ref, o_ref, *, sub):
    tm = x_ref.shape[0]
    dff = wg_ref.shape[1]
    parts = [slice(s * sub, (s + 1) * sub) for s in range(tm // sub)]
    slabs = [slice(a, min(a + FFN_SLAB, dff)) for a in range(0, dff, FFN_SLAB)]
    hb = [(_rms(x_ref[p, :], gpre_ref[...]) * (1.0 + sc_ref[...]) + sh_ref[...]).astype(BF16) for p in parts]
    y = [None for _ in parts]
    for sl in slabs:
        gate = [jnp.dot(h, wg_ref[:, sl], preferred_element_type=F32) for h in hb]
        up = [jnp.dot(h, wu_ref[:, sl], preferred_element_type=F32) for h in hb]
        act = [(g_ * jax.nn.sigmoid(g_) * u_).astype(BF16) for g_, u_ in zip(gate, up)]
        down = [jnp.dot(a_, wd_ref[sl, :], preferred_element_type=F32) for a_ in act]
        y = [d_ if y_ is None else y_ + d_ for y_, d_ in zip(y, down)]
    for p, y_ in zip(parts, y):
        o_ref[p, :] = x_ref[p, :] + gt_ref[...] * _rms(y_, gpost_ref[...])


def _ffn(x2, sc, sh, gt, g_pre, g_post, wg, wu, wd, seq):
    t, d = x2.shape
    tm = 1024
    per_b = seq // tm
    row = lambda i: (i, 0)
    bat = lambda i: (i // per_b, 0, 0)
    fix = lambda i: (0, 0)
    resident = lambda a: pl.BlockSpec(a.shape, fix, pipeline_mode=pl.Buffered(1))
    return pl.pallas_call(
        functools.partial(_ffn_body, sub=tm // 2),
        out_shape=jax.ShapeDtypeStruct((t, d), F32),
        grid=(t // tm,),
        in_specs=[
            pl.BlockSpec((tm, d), row),
            pl.BlockSpec((None, 1, d), bat), pl.BlockSpec((None, 1, d), bat), pl.BlockSpec((None, 1, d), bat),
            pl.BlockSpec((1, d), fix), pl.BlockSpec((1, d), fix),
            resident(wg), resident(wu), resident(wd),
        ],
        out_specs=pl.BlockSpec((tm, d), row),
        compiler_params=_params("parallel"),
        name="ffn",
    )(x2, sc, sh, gt, g_pre, g_post, wg, wu, wd)


def _pad_cols(a, width):
    return jnp.pad(a, ((0, 0), (0, width - a.shape[1])))


def _layout_w_in(w_in):
    d_in_rw = NSA_IN + RWKV_W
    w16 = w_in.astype(BF16)
    nsa = _pad_cols(w16[:, :NSA_IN], NSA_W)
    rw = w16[:, NSA_IN:d_in_rw]
    ml = _pad_cols(w16[:, d_in_rw:], ML_W)
    return jnp.concatenate([nsa, rw, ml], axis=1)


def _layout_cmp(ck_w1, cv_w1, ck_w2, cv_w2, pe_k, pe_v):
    half = CMP_BLOCK // 2
    ck = ck_w1.reshape(2, half, HEAD_DIM, HEAD_DIM)
    cv = cv_w1.reshape(2, half, HEAD_DIM, HEAD_DIM)
    z = jnp.zeros_like(ck[0])
    top = jnp.concatenate([ck[0], z, ck[1], z], axis=-1)
    bot = jnp.concatenate([z, cv[0], z, cv[1]], axis=-1)
    wc = jnp.concatenate([top, bot], axis=1).reshape(half * 2 * HEAD_DIM, 4 * HEAD_DIM)
    pe = jnp.concatenate([pe_k, pe_v], axis=-1).reshape(2, half * 2 * HEAD_DIM)
    pe8 = jnp.pad(pe, ((0, 6), (0, 0)))
    z2 = jnp.zeros_like(ck_w2)
    w2 = jnp.concatenate([jnp.concatenate([ck_w2, z2], axis=1), jnp.concatenate([z2, cv_w2], axis=1)], axis=0)
    return wc, pe8, w2


def kernel(x, c, w_mod, b_mod, g_pre_mix, g_post_mix, g_pre_ffn, g_post_ffn, w_in, w_out, nsa_pe_k, nsa_pe_v, nsa_ck_w1, nsa_ck_w2, nsa_cv_w1, nsa_cv_w2, nsa_gate_b, nsa_out_g, rw_mu, rw_w0, rw_w2, rw_a0, rw_a2, rw_g2, rw_kk, rw_ka, rw_rk, rw_ln_w, rw_ln_b, ml_conv_w, ml_conv_b, ml_ig_b, ml_fg_b, ml_norm_g, ffn_w_gate, ffn_w_up, ffn_w_down):
    bsz, seq, d = x.shape
    depth = w_mod.shape[0]
    t = bsz * seq
    mod = _mod(c, w_mod, b_mod)
    x2 = x.reshape(t, d)
    for l in range(depth):
        sh1, sc1, gt1, sh2, sc2, gt2 = [m.reshape(bsz, 1, d) for m in jnp.split(mod[l], 6, axis=-1)]
        p_nsa, p_rw, p_ml, kse, kw, vse, vwe = _in_proj(x2, sc1, sh1, g_pre_mix[l].reshape(1, d),
                                                        _layout_w_in(w_in[l]), bsz, seq)
        wc, pe8, w2 = _layout_cmp(nsa_ck_w1[l], nsa_cv_w1[l], nsa_ck_w2[l], nsa_cv_w2[l], nsa_pe_k[l], nsa_pe_v[l])
        kv_cmp = _nsa_cmp(p_nsa, wc, pe8, w2, bsz, seq)
        gate_b = jnp.pad(nsa_gate_b[l], (0, 128 - 3 * H_ATT)).reshape(1, 128)
        o_nsa = _nsa(p_nsa, kv_cmp, kse.reshape(bsz, seq, -1), vse, kw.reshape(bsz, seq, -1), vwe, gate_b,
                     nsa_out_g[l].reshape(1, C_ATT), bsz, seq)
        vec = lambda a: a.reshape(1, -1)
        rw_args = (vec(rw_mu[l]), vec(rw_w0[l]), rw_w2[l], vec(rw_a0[l]), rw_a2[l], rw_g2[l], vec(rw_kk[l]),
                   vec(rw_ka[l]), vec(rw_rk[l]), vec(rw_ln_w[l]), vec(rw_ln_b[l]))
        ml_gate_b = jnp.pad(jnp.concatenate([ml_ig_b[l], ml_fg_b[l]]), (0, 128 - 2 * H_MLSTM)).reshape(1, 128)
        ml_args = (ml_conv_w[l], vec(ml_conv_b[l]), ml_gate_b, vec(ml_norm_g[l]))
        o_rw, o_ml = _recurrent(p_rw, rw_args, p_ml, ml_args, bsz, seq)
        wo = w_out[l].astype(BF16)
        x2 = _out_proj(o_nsa, o_rw, o_ml, wo[:C_ATT], wo[C_ATT:C_ATT + C_RWKV], wo[C_ATT + C_RWKV:], x2, gt1,
                       g_post_mix[l].reshape(1, d), seq)
        x2 = _ffn(x2, sc2, sh2, gt2, g_pre_ffn[l].reshape(1, d), g_post_ffn[l].reshape(1, d),
                  ffn_w_gate[l].astype(BF16), ffn_w_up[l].astype(BF16), ffn_w_down[l].astype(BF16), seq)
    return x2.reshape(bsz, seq, d)
```

```python
import functools

import jax
import jax.numpy as jnp
from jax import lax
from jax.experimental import pallas as pl
from jax.experimental.pallas import tpu as pltpu

F32 = jnp.float32
BF16 = jnp.bfloat16
HIGHEST = lax.Precision.HIGHEST

HEAD_DIM = 64
H_ATT = 4
C_ATT = H_ATT * HEAD_DIM
H_RWKV = 6
C_RWKV = H_RWKV * HEAD_DIM
H_MLSTM = 6
C_MLSTM = H_MLSTM * HEAD_DIM
CMP_BLOCK = 32
CMP_STRIDE = 16
SLC_BLOCK = 64
SLC_SHIFT = 6
N_SELECT = 16
WINDOW = 512
NEG = -1e30
FORCE = 1e9
MASK_BIG = 2.0 ** 100
LOG2E = 1.4426950408889634
RANK_W = 64
RANK_A = 64
RANK_G = 128
RWKV_GN_EPS = 64e-5
CHUNK = 64
SEQ_BLOCK = 512
CONV_WIDTH = 4
NORM_EPS = 1e-6

NSA_W = 768
RWKV_W = 3 * C_RWKV + RANK_W + RANK_A + RANK_G
ML_W = 4 * C_MLSTM + 128
NSA_IN = C_ATT + 6 * HEAD_DIM + 3 * H_ATT
ML_IN = 4 * C_MLSTM + 2 * H_MLSTM

VMEM_LIMIT = 56 * 1024 * 1024

TQ = 256
TK = 256


def _params(*sem):
    return pltpu.CompilerParams(dimension_semantics=sem, vmem_limit_bytes=VMEM_LIMIT)


def _dot(a, b, precise=False):
    if precise:
        return jnp.dot(a.astype(F32), b.astype(F32), preferred_element_type=F32, precision=HIGHEST)
    return jnp.dot(a.astype(BF16), b.astype(BF16), preferred_element_type=F32)


def _dot_nt(a, b, precise=False):
    dn = (((1,), (1,)), ((), ()))
    if precise:
        return lax.dot_general(a.astype(F32), b.astype(F32), dn, preferred_element_type=F32, precision=HIGHEST)
    return lax.dot_general(a.astype(BF16), b.astype(BF16), dn, preferred_element_type=F32)


def _dot_tn(a, b, precise=False):
    dn = (((0,), (0,)), ((), ()))
    if precise:
        return lax.dot_general(a.astype(F32), b.astype(F32), dn, preferred_element_type=F32, precision=HIGHEST)
    return lax.dot_general(a.astype(BF16), b.astype(BF16), dn, preferred_element_type=F32)


def _split3(x):
    hi = x.astype(BF16)
    rest = x - hi.astype(F32)
    mid = rest.astype(BF16)
    lo = (rest - mid.astype(F32)).astype(BF16)
    return hi, mid, lo


def _dot_mask_lhs(mask01, x):
    mb = mask01.astype(BF16)
    hi, mid, lo = _split3(x)
    dot = lambda p: jnp.dot(mb, p, preferred_element_type=F32)
    return (dot(lo) + dot(mid)) + dot(hi)


def _dot_nt_mask_rhs(x, mask01):
    mb = mask01.astype(BF16)
    hi, mid, lo = _split3(x)
    dot = lambda p: lax.dot_general(p, mb, (((1,), (1,)), ((), ())), preferred_element_type=F32)
    return (dot(lo) + dot(mid)) + dot(hi)


def _dot3(a, b):
    a_hi, a_lo, _ = _split3(a)
    b_hi, b_lo, _ = _split3(b)
    dot = lambda p, q: jnp.dot(p, q, preferred_element_type=F32)
    return (dot(a_lo, b_hi) + dot(a_hi, b_lo)) + dot(a_hi, b_hi)


def _rms(x, g):
    return x * lax.rsqrt(jnp.mean(x * x, axis=-1, keepdims=True) + NORM_EPS) * g


def _mod_body(c_ref, w_ref, b_ref, o_ref):
    c = c_ref[...]
    cs = c * jax.nn.sigmoid(c)
    o_ref[...] = _dot(cs, w_ref[...], precise=True) + b_ref[...]


def _mod(c, w_mod, b_mod):
    depth, d, n = w_mod.shape
    bsz = c.shape[0]
    tn = 1536
    return pl.pallas_call(
        _mod_body,
        out_shape=jax.ShapeDtypeStruct((depth, bsz, n), F32),
        grid=(depth, n // tn),
        in_specs=[
            pl.BlockSpec((bsz, d), lambda l, j: (0, 0)),
            pl.BlockSpec((None, d, tn), lambda l, j: (l, 0, j)),
            pl.BlockSpec((None, 1, tn), lambda l, j: (l, 0, j)),
        ],
        out_specs=pl.BlockSpec((None, bsz, tn), lambda l, j: (l, 0, j)),
        compiler_params=_params("parallel", "parallel"),
        name="mod",
    )(c, w_mod, b_mod.reshape(depth, 1, n))


def _in_proj_body(x_ref, sc_ref, sh_ref, g_ref, w_ref, nsa_ref, rw_ref, ml_ref, kse_ref, kw_ref, vse_ref,
                  vwe_ref, *, per_b):
    tm = x_ref.shape[0]
    h = _rms(x_ref[...], g_ref[...]) * (1.0 + sc_ref[...]) + sh_ref[...]
    hb = h.astype(BF16)
    nsa = jnp.dot(hb, w_ref[:, 0:NSA_W], preferred_element_type=F32)
    nsa_ref[...] = nsa
    rw_ref[...] = jnp.dot(hb, w_ref[:, NSA_W:NSA_W + RWKV_W], preferred_element_type=F32)
    ml_ref[...] = jnp.dot(hb, w_ref[:, NSA_W + RWKV_W:], preferred_element_type=F32)
    D = HEAD_DIM
    off = C_ATT + 2 * D
    lane = lax.broadcasted_iota(jnp.int32, (tm, 2 * D), 1)
    pos = (pl.program_id(0) % per_b) * tm + lax.broadcasted_iota(jnp.int32, (tm, 2 * D), 0)
    expand = jnp.where(lane - D == jnp.right_shift(pos, SLC_SHIFT), MASK_BIG, 0.0)
    kse_ref[...] = jnp.where(lane < D, nsa[:, off:off + 2 * D], expand).astype(BF16)
    kw_ref[...] = nsa[:, off + 2 * D:off + 3 * D].astype(BF16)
    tail = jnp.where(lax.broadcasted_iota(jnp.int32, (D, tm), 0) == 0, 1.0, 0.0)
    vse_ref[...] = jnp.concatenate([nsa[:, off + D:off + 2 * D].T, tail], axis=0).astype(BF16)
    vwe_ref[...] = jnp.concatenate([nsa[:, off + 3 * D:off + 4 * D].T, tail], axis=0).astype(BF16)


def _in_proj(x2, sc, sh, g, w_p, bsz, seq):
    t, d = x2.shape
    tm = 512
    per_b = seq // tm
    n_slc = seq // SLC_BLOCK
    assert n_slc == HEAD_DIM, "kse packs the key and one column per selection block into 128 lanes"
    row = lambda i: (i, 0)
    bat = lambda i: (i // per_b, 0, 0)
    fix = lambda i: (0, 0)
    col = lambda i: (i // per_b, 0, i % per_b)
    return pl.pallas_call(
        functools.partial(_in_proj_body, per_b=per_b),
        out_shape=(jax.ShapeDtypeStruct((t, NSA_W), F32),
                   jax.ShapeDtypeStruct((t, RWKV_W), F32),
                   jax.ShapeDtypeStruct((t, ML_W), F32),
                   jax.ShapeDtypeStruct((t, 2 * HEAD_DIM), BF16),
                   jax.ShapeDtypeStruct((t, HEAD_DIM), BF16),
                   jax.ShapeDtypeStruct((bsz, 128, seq), BF16),
                   jax.ShapeDtypeStruct((bsz, 128, seq), BF16)),
        grid=(t // tm,),
        in_specs=[
            pl.BlockSpec((tm, d), row),
            pl.BlockSpec((None, 1, d), bat),
            pl.BlockSpec((None, 1, d), bat),
            pl.BlockSpec((1, d), fix),
            pl.BlockSpec(w_p.shape, fix),
        ],
        out_specs=(pl.BlockSpec((tm, NSA_W), row), pl.BlockSpec((tm, RWKV_W), row),
                   pl.BlockSpec((tm, ML_W), row), pl.BlockSpec((tm, 2 * HEAD_DIM), row),
                   pl.BlockSpec((tm, HEAD_DIM), row), pl.BlockSpec((None, 128, tm), col),
                   pl.BlockSpec((None, 128, tm), col)),
        compiler_params=_params("parallel"),
        name="in_proj",
    )(x2, sc, sh, g, w_p)


def _nsa_cmp_body(x_ref, wc_ref, pe_ref, w2_ref, o_ref):
    half = CMP_BLOCK // 2
    width = x_ref.shape[1]
    g = x_ref.shape[0] // half
    f = None
    for tl in range(half):
        part = _dot3(x_ref[pl.ds(tl, g, stride=half), :], wc_ref[tl * width:(tl + 1) * width, :])
        f = part if f is None else f + part
    c = _dot(pe_ref[...], wc_ref[...], precise=True)
    second = pltpu.roll(f[:, 128:256], g - 1, 0)
    pre = f[:, 0:128] + second + c[0:1, 0:128] + c[1:2, 128:256]
    act = jax.nn.gelu(pre, approximate=True)
    out = _dot(act, w2_ref[...], precise=True)
    rows = lax.broadcasted_iota(jnp.int32, out.shape, 0)
    o_ref[...] = jnp.where(rows < g - 1, out, 0.0)


def _nsa_cmp(p_nsa, wc, pe8, w2, bsz, seq):
    g = seq // (CMP_BLOCK // 2)
    kc_block = C_ATT // (2 * HEAD_DIM)
    return pl.pallas_call(
        _nsa_cmp_body,
        out_shape=jax.ShapeDtypeStruct((bsz, g, 128), F32),
        grid=(bsz,),
        in_specs=[
            pl.BlockSpec((seq, 2 * HEAD_DIM), lambda b: (b, kc_block)),
            pl.BlockSpec(wc.shape, lambda b: (0, 0)),
            pl.BlockSpec(pe8.shape, lambda b: (0, 0)),
            pl.BlockSpec(w2.shape, lambda b: (0, 0)),
        ],
        out_specs=pl.BlockSpec((None, g, 128), lambda b: (b, 0, 0)),
        compiler_params=_params("parallel"),
        name="nsa_cmp",
    )(p_nsa, wc, pe8, w2)


def _nsa_body(q_ref, gts_ref, kvc_ref, kse_ref, vse_ref, kw_ref, vwe_ref, gb_ref, g_ref, o_ref,
              m_s, acc_s, s_even, s_odd, ocmp, cnt_ref, imp_ref, *, n_slc):
    tq = q_ref.shape[0]
    n_cmp = kvc_ref.shape[0]
    qi = pl.program_id(1)
    q0 = qi * tq
    scale = HEAD_DIM ** -0.5
    D = HEAD_DIM
    qt = q_ref[...].T

    heads = range(H_ATT)

    def compressed(rows):
        kc = kvc_ref[0:rows, 0:D]
        vc = kvc_ref[0:rows, D:2 * D]
        nidx = lax.broadcasted_iota(jnp.int32, (rows, tq), 0)
        pos = q0 + lax.broadcasted_iota(jnp.int32, (rows, tq), 1)
        cmask = (nidx * CMP_STRIDE + (CMP_BLOCK - 1)) <= pos
        sc = [_dot3(kc, qt[h * D:(h + 1) * D]) for h in heads]
        sc = [jnp.where(cmask, x * scale, NEG) for x in sc]
        ec = [jnp.where(cmask, jnp.exp(x - jnp.max(x, axis=0, keepdims=True)), 0.0) for x in sc]
        pc = [e / jnp.maximum(jnp.sum(e, axis=0, keepdims=True), 1e-30) for e in ec]
        oc = [_dot_tn(vc, p) for p in pc]
        for h in heads:
            ocmp[h * D:(h + 1) * D, :] = oc[h]
        psum = sum(pc[1:], pc[0])
        jrow = lax.broadcasted_iota(jnp.int32, (n_slc, rows), 0) * SLC_BLOCK
        ncol = lax.broadcasted_iota(jnp.int32, (n_slc, rows), 1) * CMP_STRIDE
        overlap_t = jnp.where((ncol < jrow + SLC_BLOCK) & (ncol + CMP_BLOCK > jrow), 1.0, 0.0)
        imp_ref[...] = _dot_mask_lhs(overlap_t, psum)

    step = tq // CMP_STRIDE
    tiles_per_part = max(1, (n_cmp // 4) // step)
    bounds = sorted(set(min(n_cmp, step * tiles_per_part * (i + 1)) for i in range(4)))
    for i, rows in enumerate(bounds):
        lo = 0 if i == 0 else bounds[i - 1] // step
        pl.when((qi >= lo) & (qi < rows // step))(functools.partial(compressed, rows))

    imp_t = imp_ref[...]
    jj = lax.broadcasted_iota(jnp.int32, (n_slc, tq), 0)
    cur = jnp.right_shift(q0 + lax.broadcasted_iota(jnp.int32, (n_slc, tq), 1), SLC_SHIFT)
    forced = (jj == 0) | (jj == cur) | (jj == cur - 1)
    score = jnp.where(forced, FORCE, jnp.where(jj <= cur, imp_t, NEG))
    group = 8
    batch = 16
    per_tile = tq // SLC_BLOCK
    cnt_ref[...] = jnp.zeros_like(cnt_ref)

    def count_batch(first_j2):
        for gi in range(n_slc // group):
            sg = score[gi * group:(gi + 1) * group]
            acc = None
            for j2 in range(first_j2, first_j2 + batch):
                row = score[j2:j2 + 1, :]
                if gi * group > j2:
                    hit = jnp.where(row >= sg, 1.0, 0.0)
                elif (gi + 1) * group - 1 <= j2:
                    hit = jnp.where(row > sg, 1.0, 0.0)
                else:
                    hit = jnp.where(lax.broadcasted_iota(jnp.int32, (group, tq), 0) + gi * group > j2,
                                    jnp.where(row >= sg, 1.0, 0.0), jnp.where(row > sg, 1.0, 0.0))
                acc = hit if acc is None else acc + hit
            cnt_ref[gi * group:(gi + 1) * group, :] += acc

    count_batch(0)
    for first_j2 in range(batch, n_slc, batch):
        pl.when(first_j2 < (qi + 1) * per_tile)(functools.partial(count_batch, first_j2))
    cnt = cnt_ref[...]
    selm = jnp.where(cnt < min(N_SELECT, n_slc), 0.0, -1.0).astype(BF16)

    qs = (qt * (scale * LOG2E)).astype(BF16)
    rhs_w = [qs[h * D:(h + 1) * D] for h in heads]
    rhs_s = [jnp.concatenate([rhs_w[h], selm], axis=0) for h in heads]
    m_s[...] = jnp.full(m_s.shape, NEG, F32)
    acc_s[...] = jnp.zeros_like(acc_s)

    def scores_into(buf, j):
        k = kse_ref[pl.ds(pl.multiple_of(j * TK, TK), TK), :]
        for h in heads:
            buf[h] = jnp.dot(k, rhs_s[h], preferred_element_type=F32)

    def consume(buf, j, mask):
        vt = vse_ref[:, pl.ds(pl.multiple_of(j * TK, TK), TK)]
        for h in heads:
            s = buf[h]
            if mask is not None:
                s = jnp.where(mask, s, NEG)
            m_old = m_s[h:h + 1, :]
            m_new = jnp.maximum(m_old, jnp.max(s, axis=0, keepdims=True))
            p = jnp.exp2(s - m_new).astype(BF16)
            acc_s[h] = jnp.exp2(m_old - m_new) * acc_s[h] + jnp.dot(vt, p, preferred_element_type=F32)
            m_s[h:h + 1, :] = m_new

    kpos = lax.broadcasted_iota(jnp.int32, (TK, tq), 0)
    qpos = lax.broadcasted_iota(jnp.int32, (TK, tq), 1)
    last = jnp.maximum(qi - 1, 0)
    scores_into(s_even, qi)
    scores_into(s_odd, 0)
    consume(s_even, qi, kpos <= qpos)

    def body(jp, carry):
        j = 2 * jp
        scores_into(s_even, jnp.minimum(j + 1, last))
        consume(s_odd, j, None)
        scores_into(s_odd, jnp.minimum(j + 2, last))
        consume(s_even, j + 1, None)
        return carry

    lax.fori_loop(0, qi // 2, body, 0)

    @pl.when(qi % 2 == 1)
    def _():
        consume(s_odd, qi - 1, None)

    span = WINDOW + tq
    w0 = pl.multiple_of(jnp.maximum(q0 - WINDOW, 0), TK)
    kw = kw_ref[pl.ds(w0, span), :]
    vwt = vwe_ref[:, pl.ds(w0, span)]
    kabs = w0 + lax.broadcasted_iota(jnp.int32, (span, tq), 0)
    qabs = q0 + lax.broadcasted_iota(jnp.int32, (span, tq), 1)
    wmask = (kabs <= qabs) & (kabs > qabs - WINDOW)
    sw = [jnp.where(wmask, jnp.dot(kw, rhs_w[h], preferred_element_type=F32), NEG) for h in heads]
    pw = [jnp.exp2(x - jnp.max(x, axis=0, keepdims=True)).astype(BF16) for x in sw]
    acc_w = [jnp.dot(vwt, p, preferred_element_type=F32) for p in pw]

    gate = jax.nn.sigmoid((gts_ref[...] + gb_ref[...]).T)
    outs = []
    for h in heads:
        a_s = acc_s[h]
        a_w = acc_w[h]
        outs.append(gate[h:h + 1] * ocmp[h * D:(h + 1) * D, :]
                    + gate[H_ATT + h:H_ATT + h + 1] * (a_s[0:D] / a_s[D:D + 1])
                    + gate[2 * H_ATT + h:2 * H_ATT + h + 1] * (a_w[0:D] / a_w[D:D + 1]))
    ot = jnp.concatenate(outs, axis=0)
    ot = ot * lax.rsqrt(jnp.mean(ot * ot, axis=0, keepdims=True) + NORM_EPS)
    o_ref[...] = ot.T * g_ref[...]


def _nsa(p_nsa, kv_cmp, kse, vse, kw, vwe, gate_b, out_g, bsz, seq):
    t = p_nsa.shape[0]
    nq = seq // TQ
    n_slc = seq // SLC_BLOCK
    g = kv_cmp.shape[1]
    row = lambda b, i: (b * nq + i, 0)
    per_b = lambda b, i: (b, 0, 0)
    fix = lambda b, i: (0, 0)
    return pl.pallas_call(
        functools.partial(_nsa_body, n_slc=n_slc),
        out_shape=jax.ShapeDtypeStruct((t, C_ATT), F32),
        grid=(bsz, nq),
        in_specs=[
            pl.BlockSpec((TQ, C_ATT), row),
            pl.BlockSpec((TQ, 128), lambda b, i: (b * nq + i, NSA_W // 128 - 1)),
            pl.BlockSpec((None, g, 128), per_b),
            pl.BlockSpec((None, seq, HEAD_DIM + n_slc), per_b),
            pl.BlockSpec((None, 128, seq), per_b),
            pl.BlockSpec((None, seq, HEAD_DIM), per_b),
            pl.BlockSpec((None, 128, seq), per_b),
            pl.BlockSpec((1, 128), fix),
            pl.BlockSpec((1, C_ATT), fix),
        ],
        out_specs=pl.BlockSpec((TQ, C_ATT), row),
        scratch_shapes=[pltpu.VMEM((8, TQ), F32), pltpu.VMEM((H_ATT, 128, TQ), F32),
                        pltpu.VMEM((H_ATT, TK, TQ), F32), pltpu.VMEM((H_ATT, TK, TQ), F32),
                        pltpu.VMEM((C_ATT, TQ), F32), pltpu.VMEM((n_slc, TQ), F32),
                        pltpu.VMEM((n_slc, TQ), F32)],
        compiler_params=_params("parallel", "arbitrary"),
        name="nsa",
    )(p_nsa, p_nsa, kv_cmp, kse, vse, kw, vwe, gate_b, out_g)


def _tri(n, strict):
    r = lax.broadcasted_iota(jnp.int32, (n, n), 0)
    c = lax.broadcasted_iota(jnp.int32, (n, n), 1)
    return (c < r) if strict else (c <= r)


def _block_diag(x):
    xb = x.astype(BF16)
    first = lax.broadcasted_iota(jnp.int32, xb.shape, 1) < xb.shape[1] // 2
    zero = jnp.zeros_like(xb)
    return jnp.concatenate([jnp.where(first, xb, zero), jnp.where(first, zero, xb)], axis=0)


def _unit_lower_inverse_pairs(ms, eye2):
    n = eye2.shape[0]
    idx = range(len(ms))
    r = lax.broadcasted_iota(jnp.int32, (n, 2 * n), 0)
    c = jnp.bitwise_and(lax.broadcasted_iota(jnp.int32, (n, 2 * n), 1), n - 1)
    base = 8
    diag = jnp.right_shift(r, 3) == jnp.right_shift(c, 3)
    m8 = [jnp.where(diag, m, 0.0) for m in ms]
    t = [eye2 + m for m in m8]
    p2 = [_dot(m, _block_diag(m)) for m in m8]
    t = [t[i] + _dot(p2[i], _block_diag(t[i])) for i in idx]
    p4 = [_dot(p, _block_diag(p)) for p in p2]
    t = [t[i] + _dot(p4[i], _block_diag(t[i])) for i in idx]
    b = base
    while b < n:
        sh = b.bit_length() - 1
        pair = jnp.right_shift(r, sh + 1) == jnp.right_shift(c, sh + 1)
        lower_left = pair & (jnp.right_shift(r, sh) != jnp.right_shift(c, sh))
        left = [_dot(t[i], _block_diag(jnp.where(lower_left, ms[i], 0.0))) for i in idx]
        t = [t[i] + _dot(left[i], _block_diag(t[i])) for i in idx]
        b *= 2
    return t


def _rwkv_body(p_ref, mu_ref, w0_ref, w2_ref, a0_ref, a2_ref, g2_ref, kk_ref, ka_ref, rk_ref, lnw_ref,
               lnb_ref, o_ref, state, prev):
    L = CHUNK
    C = C_RWKV
    tb = p_ref.shape[0]
    chunks = range(tb // L)

    x = p_ref[...]
    rows = lax.broadcasted_iota(jnp.int32, x.shape, 0)
    shifted = jnp.where(rows == 0, prev[0:1, :], pltpu.roll(x, 1, 0))
    prev[0:1, :] = x[tb - 1:tb, :]
    x = x + mu_ref[...] * (shifted - x)
    r = x[:, 0:C]
    k = x[:, C:2 * C]
    v = x[:, 2 * C:3 * C]
    xw = x[:, 3 * C:3 * C + RANK_W]
    xa = x[:, 3 * C + RANK_W:3 * C + RANK_W + RANK_A]
    xg = x[:, 3 * C + RANK_W + RANK_A:]
    w = -jax.nn.softplus(-(w0_ref[...] + _dot3(jnp.tanh(xw), w2_ref[...]))) - 0.5
    logw = -jnp.exp(w)
    a = jax.nn.sigmoid(a0_ref[...] + _dot3(xa, a2_ref[...]))
    g = _dot(jax.nn.sigmoid(xg), g2_ref[...])
    kkf = k * kk_ref[...]
    kmod = k * (1.0 + (a - 1.0) * ka_ref[...])
    rk = r * kmod * rk_ref[...]

    tri_incl = jnp.where(_tri(L, strict=False), 1.0, 0.0)
    cum = jnp.concatenate([_dot_mask_lhs(tri_incl, logw[c * L:(c + 1) * L]) for c in chunks], axis=0)
    cum_last = [cum[(c + 1) * L - 1:(c + 1) * L, :] for c in chunks]
    cum_end = jnp.concatenate([jnp.broadcast_to(cl, (L, C)) for cl in cum_last], axis=0)
    e_pos = jnp.exp(cum)
    e_prev = jnp.exp(cum - logw)
    e_neg = jnp.exp(-cum)
    e_rem = jnp.exp(cum_end - cum)
    N = HEAD_DIM
    PW = 2 * N
    pairs = range(H_RWKV // 2)
    lane_l = lax.broadcasted_iota(jnp.int32, (L, PW), 1)
    row_l = lax.broadcasted_iota(jnp.int32, (L, PW), 0)
    col_l = jnp.bitwise_and(lane_l, N - 1)
    first = lane_l < N
    first2 = lax.broadcasted_iota(jnp.int32, (2 * L, PW), 1) < N
    strict2 = col_l < row_l
    incl2 = col_l <= row_l
    eye2 = jnp.where(col_l == row_l, 1.0, 0.0)

    def halves(x_):
        lane = lax.broadcasted_iota(jnp.int32, x_.shape, 1)
        sa = jnp.sum(jnp.where(lane < N, x_, 0.0), axis=-1, keepdims=True)
        sb = jnp.sum(jnp.where(lane < N, 0.0, x_), axis=-1, keepdims=True)
        return jnp.where(lane < N, sa, sb)

    ps = [slice(p * PW, (p + 1) * PW) for p in pairs]
    kk_p = [kkf[:, sl] for sl in ps]
    kk_p = [x_ / jnp.maximum(jnp.sqrt(halves(x_ * x_)), 1e-12) for x_ in kk_p]
    alpha_f = [-kk_p[p] * e_prev[:, ps[p]] for p in pairs]
    r_f = [r[:, ps[p]] * e_pos[:, ps[p]] for p in pairs]
    beta = [kk_p[p] * a[:, ps[p]] for p in pairs]
    beta_f = [beta[p] * e_neg[:, ps[p]] for p in pairs]
    k_f = [kmod[:, ps[p]] * e_neg[:, ps[p]] for p in pairs]
    beta_e = [beta[p] * e_rem[:, ps[p]] for p in pairs]
    k_e = [kmod[:, ps[p]] * e_rem[:, ps[p]] for p in pairs]
    items = [(c, p) for c in chunks for p in pairs]
    n_items = range(len(items))
    rs = [slice(c * L, (c + 1) * L) for c, _ in items]
    pi = [p for _, p in items]
    lhs = [jnp.concatenate([alpha_f[pi[i]][rs[i]], r_f[pi[i]][rs[i]]], axis=0).astype(BF16) for i in n_items]
    zero16 = jnp.zeros((2 * L, PW), BF16)
    lhs4 = [jnp.concatenate([jnp.where(first2, x_, zero16), jnp.where(first2, zero16, x_)], axis=0) for x_ in lhs]
    bk = [(beta_f[pi[i]][rs[i]].astype(BF16), k_f[pi[i]][rs[i]].astype(BF16)) for i in n_items]
    v_i = [v[rs[i], ps[pi[i]]] for i in n_items]
    out1 = [_dot_nt(lhs4[i], jnp.concatenate([bk[i][0], bk[i][1]], axis=0)) for i in n_items]
    out2 = [_dot_nt(lhs4[i], jnp.concatenate([bk[i][1], bk[i][0]], axis=0)) for i in n_items]
    m_ab = [jnp.where(strict2, jnp.where(first, out1[i][0:L], out2[i][2 * L:3 * L]), 0.0) for i in n_items]
    m_ak = [jnp.where(strict2, jnp.where(first, out2[i][0:L], out1[i][2 * L:3 * L]), 0.0) for i in n_items]
    m_rb = [jnp.where(incl2, jnp.where(first, out1[i][L:2 * L], out2[i][3 * L:4 * L]), 0.0) for i in n_items]
    m_rk = [jnp.where(incl2, jnp.where(first, out2[i][L:2 * L], out1[i][3 * L:4 * L]), 0.0) for i in n_items]
    t_inv = _unit_lower_inverse_pairs(m_ab, eye2)
    mv = [_dot(jnp.concatenate([m_ak[i], m_rk[i]], axis=0), _block_diag(v_i[i])) for i in n_items]
    pq = [_dot(t_inv[i], jnp.concatenate([_block_diag(alpha_f[pi[i]][rs[i]]), _block_diag(mv[i][0:L])], axis=1))
          for i in n_items]
    ry = [_dot(m_rb[i], jnp.concatenate([_block_diag(pq[i][:, 0:PW]), _block_diag(pq[i][:, PW:2 * PW])], axis=1))
          for i in n_items]
    r_eff = [r_f[pi[i]][rs[i]] + ry[i][:, 0:PW] for i in n_items]
    y0 = [ry[i][:, PW:2 * PW] + mv[i][L:2 * L] for i in n_items]
    zero_l = jnp.zeros((L, PW), F32)
    gh = [_dot_tn(jnp.concatenate([beta_e[pi[i]][rs[i]], k_e[pi[i]][rs[i]]], axis=0),
                  jnp.concatenate([pq[i], jnp.concatenate([zero_l, v_i[i]], axis=1)], axis=0)) for i in n_items]
    g_off = [jnp.where(first, gh[i][0:N, 0:PW], gh[i][N:2 * N, 0:PW]) for i in n_items]
    h_add = [jnp.where(first, gh[i][0:N, PW:2 * PW], gh[i][N:2 * N, PW:2 * PW]) for i in n_items]
    gam = [halves(eye2 * jnp.exp(cum_last[c][:, ps[p]])) for c, p in items]
    z = [state[p] for p in pairs]
    y_parts = [[] for _ in pairs]
    for c in chunks:
        idx = [c * len(pairs) + p for p in pairs]
        zb = [_block_diag(z[p]) for p in pairs]
        for p, i in enumerate(idx):
            y_parts[p].append(_dot(r_eff[i], zb[p]) + y0[i])
        z = [z[p] * gam[i] + _dot(g_off[i], zb[p]) + h_add[i] for p, i in enumerate(idx)]
    for p in pairs:
        state[p] = z[p]
    yn = []
    for p in pairs:
        y_p = jnp.concatenate(y_parts[p], axis=0)
        mu_y = halves(y_p) * (1.0 / N)
        dev = y_p - mu_y
        var = halves(dev * dev) * (1.0 / N)
        yn.append(dev * lax.rsqrt(var + RWKV_GN_EPS))
    bonus = [halves(rk[:, ps[p]]) * v[:, ps[p]] for p in pairs]
    yn = jnp.concatenate(yn, axis=-1)
    bonus = jnp.concatenate(bonus, axis=-1)
    o_ref[...] = (yn * lnw_ref[...] + lnb_ref[...] + bonus) * g


N_RWKV_ARGS = 11
N_ML_ARGS = 4


def _recurrent_body(*refs):
    p_rw, rw_args = refs[0], refs[1:1 + N_RWKV_ARGS]
    p_ml, ml_args = refs[1 + N_RWKV_ARGS], refs[2 + N_RWKV_ARGS:2 + N_RWKV_ARGS + N_ML_ARGS]
    o_rw, o_ml, state, prev_rw, cstate, mstate, prev_ml = refs[2 + N_RWKV_ARGS + N_ML_ARGS:]

    @pl.when(pl.program_id(1) == 0)
    def _():
        for ref in (state, prev_rw, cstate, mstate, prev_ml):
            ref[...] = jnp.zeros_like(ref)

    _rwkv_body(p_rw, *rw_args, o_rw, state, prev_rw)
    _mlstm_body(p_ml, *ml_args, o_ml, cstate, mstate, prev_ml)


def _recurrent(p_rw, rw_args, p_ml, ml_args, bsz, seq):
    t = p_rw.shape[0]
    tb = SEQ_BLOCK
    nb = seq // tb
    row = lambda b, c: (b * nb + c, 0)
    fix = lambda b, c: (0, 0)
    assert len(rw_args) == N_RWKV_ARGS and len(ml_args) == N_ML_ARGS
    return pl.pallas_call(
        _recurrent_body,
        out_shape=(jax.ShapeDtypeStruct((t, C_RWKV), F32), jax.ShapeDtypeStruct((t, C_MLSTM), F32)),
        grid=(bsz, nb),
        in_specs=([pl.BlockSpec((tb, RWKV_W), row)] + [pl.BlockSpec(a.shape, fix) for a in rw_args]
                  + [pl.BlockSpec((tb, ML_W), row)] + [pl.BlockSpec(a.shape, fix) for a in ml_args]),
        out_specs=(pl.BlockSpec((tb, C_RWKV), row), pl.BlockSpec((tb, C_MLSTM), row)),
        scratch_shapes=[pltpu.VMEM((H_RWKV // 2, HEAD_DIM, 2 * HEAD_DIM), F32), pltpu.VMEM((8, RWKV_W), F32),
                        pltpu.VMEM((H_MLSTM, 128, HEAD_DIM), F32), pltpu.VMEM((8, 128), F32),
                        pltpu.VMEM((8, 2 * C_MLSTM), F32)],
        compiler_params=_params("parallel", "arbitrary"),
        name="recurrent",
    )(p_rw, *rw_args, p_ml, *ml_args)


def _mlstm_body(p_ref, cw_ref, cb_ref, gb_ref, ng_ref, o_ref, cstate, mstate, prev):
    L = CHUNK
    C = C_MLSTM
    D = HEAD_DIM
    tb = p_ref.shape[0]
    chunks = range(tb // L)

    qk_in = p_ref[:, 0:2 * C]
    pv = prev[...]
    rows8 = lax.broadcasted_iota(jnp.int32, pv.shape, 0)
    conv = qk_in * cw_ref[CONV_WIDTH - 1:CONV_WIDTH, :] + cb_ref[...]
    for d in range(1, CONV_WIDTH):
        rolled = pltpu.roll(qk_in, d, 0)
        top = jnp.where(rows8 < d, pltpu.roll(pv, d, 0), rolled[0:8])
        sh = jnp.concatenate([top, rolled[8:]], axis=0)
        conv = conv + sh * cw_ref[CONV_WIDTH - 1 - d:CONV_WIDTH - d, :]
    prev[...] = qk_in[tb - 8:tb, :]
    qk = conv * jax.nn.sigmoid(conv)
    q = qk[:, 0:C]
    k = qk[:, C:2 * C] * (D ** -0.5)
    v = p_ref[:, 2 * C:3 * C]
    og = p_ref[:, 3 * C:4 * C]
    gates = p_ref[:, 4 * C:4 * C + 128] + gb_ref[...]
    lane = lax.broadcasted_iota(jnp.int32, gates.shape, 1)
    is_f = (lane >= H_MLSTM) & (lane < 2 * H_MLSTM)
    gl = jnp.where(is_f, jax.nn.log_sigmoid(gates), gates)
    tri_incl = jnp.where(_tri(L, strict=False), 1.0, 0.0)
    bcum = [_dot_mask_lhs(tri_incl, gl[c * L:(c + 1) * L]) for c in chunks]
    gl_t = [gl[c * L:(c + 1) * L].T for c in chunks]
    bcum_t = [_dot_nt_mask_rhs(g_, tri_incl) for g_ in gl_t]
    m_all = mstate[...]
    src = lax.broadcasted_iota(jnp.int32, (L, L), 0)
    qry = lax.broadcasted_iota(jnp.int32, (L, L), 1)
    causal_t = src <= qry
    ones_row = jnp.where(lax.broadcasted_iota(jnp.int32, (D, L), 0) == 0, 1.0, 0.0)

    heads = range(H_MLSTM)
    items = [(c, h) for c in chunks for h in heads]
    n_items = range(len(items))
    q_t = [q[c * L:(c + 1) * L].T for c in chunks]
    v_t = [v[c * L:(c + 1) * L].T for c in chunks]
    qt_i = [q_t[c][h * D:(h + 1) * D] for c, h in items]
    vt_ext = [jnp.concatenate([v_t[c][h * D:(h + 1) * D], ones_row], axis=0) for c, h in items]
    k_i = [k[c * L:(c + 1) * L, h * D:(h + 1) * D] for c, h in items]
    c_col = [gl[c * L:(c + 1) * L, h:h + 1] - bcum[c][:, H_MLSTM + h:H_MLSTM + h + 1] for c, h in items]
    b_row = [bcum_t[c][H_MLSTM + h:H_MLSTM + h + 1, :] for c, h in items]
    c_row = [gl_t[c][h:h + 1, :] - b_row[i] for i, (c, h) in enumerate(items)]
    g_tot = [bcum[c][L - 1:L, H_MLSTM + h:H_MLSTM + h + 1] for c, h in items]
    u_row = [g_tot[i] + c_row[i] for i in n_items]
    u_max = [jnp.max(x_, axis=1, keepdims=True) for x_ in u_row]
    m0 = []
    m_run = [m_all[0:1, h:h + 1] for h in heads]
    for c in chunks:
        for h in heads:
            i = c * H_MLSTM + h
            m0.append(m_run[h])
            m_run[h] = jnp.maximum(g_tot[i] + m_run[h], u_max[i])
    m_next = [m0[i + H_MLSTM] if i + H_MLSTM < len(items) else m_run[items[i][1]] for i in n_items]
    kq = [_dot(k_i[i], qt_i[i]) for i in n_items]
    kv = [_dot(vt_ext[i] * jnp.exp(u_row[i] - m_next[i]), k_i[i]) for i in n_items]
    cm = []
    cm_run = [cstate[h] for h in heads]
    for c in chunks:
        for h in heads:
            i = c * H_MLSTM + h
            cm.append(cm_run[h])
            cm_run[h] = jnp.exp(g_tot[i] + m0[i] - m_next[i]) * cm_run[h] + kv[i]
    inter = [_dot(cm[i], qt_i[i]) for i in n_items]
    c_b = [jnp.where(causal_t, jnp.broadcast_to(c_col[i], (L, L)), -jnp.inf) for i in n_items]
    m_row = [jnp.maximum(m0[i], jnp.max(c_b[i], axis=0, keepdims=True)) for i in n_items]
    s_t = [jnp.exp(c_b[i] - m_row[i]) * kq[i] for i in n_items]
    sv = [_dot(vt_ext[i], s_t[i]) for i in n_items]
    lane8 = lax.broadcasted_iota(jnp.int32, m_all.shape, 1)
    for h in heads:
        cstate[h] = cm_run[h]
        m_all = jnp.where(lane8 == h, m_run[h], m_all)
    mstate[...] = m_all
    outs = []
    for i in n_items:
        num = jnp.exp(m0[i] - m_row[i]) * inter[i] + sv[i]
        hh = num[0:D] / jnp.maximum(jnp.abs(num[D:D + 1]), jnp.exp(-(b_row[i] + m_row[i])))
        outs.append(hh * lax.rsqrt(jnp.mean(hh * hh, axis=0, keepdims=True) + NORM_EPS))
    hcat = jnp.concatenate([jnp.concatenate(outs[c * H_MLSTM:(c + 1) * H_MLSTM], axis=0).T for c in chunks], axis=0)
    o_ref[...] = jax.nn.sigmoid(og) * (hcat * ng_ref[...])


def _out_proj_body(a_ref, b_ref, c_ref, wa_ref, wb_ref, wc_ref, x_ref, gt_ref, g_ref, o_ref, *, sub):
    parts = [slice(s * sub, (s + 1) * sub) for s in range(x_ref.shape[0] // sub)]
    y = [jnp.dot(a_ref[p, :].astype(BF16), wa_ref[...], preferred_element_type=F32)
         + jnp.dot(b_ref[p, :].astype(BF16), wb_ref[...], preferred_element_type=F32)
         + jnp.dot(c_ref[p, :].astype(BF16), wc_ref[...], preferred_element_type=F32) for p in parts]
    for p, y_ in zip(parts, y):
        o_ref[p, :] = x_ref[p, :] + gt_ref[...] * _rms(y_, g_ref[...])


def _out_proj(o_nsa, o_rw, o_ml, wa, wb, wc, x2, gt, g, seq):
    t, d = x2.shape
    tm = 1024
    per_b = seq // tm
    row = lambda i: (i, 0)
    fix = lambda i: (0, 0)
    return pl.pallas_call(
        functools.partial(_out_proj_body, sub=tm // 2),
        out_shape=jax.ShapeDtypeStruct((t, d), F32),
        grid=(t // tm,),
        in_specs=[
            pl.BlockSpec((tm, C_ATT), row), pl.BlockSpec((tm, C_RWKV), row), pl.BlockSpec((tm, C_MLSTM), row),
            pl.BlockSpec(wa.shape, fix), pl.BlockSpec(wb.shape, fix), pl.BlockSpec(wc.shape, fix),
            pl.BlockSpec((tm, d), row),
            pl.BlockSpec((None, 1, d), lambda i: (i // per_b, 0, 0)),
            pl.BlockSpec((1, d), fix),
        ],
        out_specs=pl.BlockSpec((tm, d), row),
        compiler_params=_params("parallel"),
        name="out_proj",
    )(o_nsa, o_rw, o_ml, wa, wb, wc, x2, gt, g)


FFN_SLAB = 1024


def _ffn_body(x_ref, sc_ref, sh_ref, gt_ref, gpre_ref, gpost_ref, wg_ref, wu_ref, wd_ref, o_ref, *, sub):
    tm = x_ref.shape[0]
    dff = wg_ref.shape[1]
    parts = [slice(s * sub, (s + 1) * sub) for s in range(tm // sub)]
    slabs = [slice(a, min(a + FFN_SLAB, dff)) for a in range(0, dff, FFN_SLAB)]
    hb = [(_rms(x_ref[p, :], gpre_ref[...]) * (1.0 + sc_ref[...]) + sh_ref[...]).astype(BF16) for p in parts]
    y = [None for _ in parts]
    for sl in slabs:
        gate = [jnp.dot(h, wg_ref[:, sl], preferred_element_type=F32) for h in hb]
        up = [jnp.dot(h, wu_ref[:, sl], preferred_element_type=F32) for h in hb]
        act = [(g_ * jax.nn.sigmoid(g_) * u_).astype(BF16) for g_, u_ in zip(gate, up)]
        down = [jnp.dot(a_, wd_ref[sl, :], preferred_element_type=F32) for a_ in act]
        y = [d_ if y_ is None else y_ + d_ for y_, d_ in zip(y, down)]
    for p, y_ in zip(parts, y):
        o_ref[p, :] = x_ref[p, :] + gt_ref[...] * _rms(y_, gpost_ref[...])


def _ffn(x2, sc, sh, gt, g_pre, g_post, wg, wu, wd, seq):
    t, d = x2.shape
    tm = 1024
    per_b = seq // tm
    row = lambda i: (i, 0)
    bat = lambda i: (i // per_b, 0, 0)
    fix = lambda i: (0, 0)
    resident = lambda a: pl.BlockSpec(a.shape, fix, pipeline_mode=pl.Buffered(1))
    return pl.pallas_call(
        functools.partial(_ffn_body, sub=tm // 2),
        out_shape=jax.ShapeDtypeStruct((t, d), F32),
        grid=(t // tm,),
        in_specs=[
            pl.BlockSpec((tm, d), row),
            pl.BlockSpec((None, 1, d), bat), pl.BlockSpec((None, 1, d), bat), pl.BlockSpec((None, 1, d), bat),
            pl.BlockSpec((1, d), fix), pl.BlockSpec((1, d), fix),
            resident(wg), resident(wu), resident(wd),
        ],
        out_specs=pl.BlockSpec((tm, d), row),
        compiler_params=_params("parallel"),
        name="ffn",
    )(x2, sc, sh, gt, g_pre, g_post, wg, wu, wd)


def _pad_cols(a, width):
    return jnp.pad(a, ((0, 0), (0, width - a.shape[1])))


def _layout_w_in(w_in):
    d_in_rw = NSA_IN + RWKV_W
    w16 = w_in.astype(BF16)
    nsa = _pad_cols(w16[:, :NSA_IN], NSA_W)
    rw = w16[:, NSA_IN:d_in_rw]
    ml = _pad_cols(w16[:, d_in_rw:], ML_W)
    return jnp.concatenate([nsa, rw, ml], axis=1)


def _layout_cmp(ck_w1, cv_w1, ck_w2, cv_w2, pe_k, pe_v):
    half = CMP_BLOCK // 2
    ck = ck_w1.reshape(2, half, HEAD_DIM, HEAD_DIM)
    cv = cv_w1.reshape(2, half, HEAD_DIM, HEAD_DIM)
    z = jnp.zeros_like(ck[0])
    top = jnp.concatenate([ck[0], z, ck[1], z], axis=-1)
    bot = jnp.concatenate([z, cv[0], z, cv[1]], axis=-1)
    wc = jnp.concatenate([top, bot], axis=1).reshape(half * 2 * HEAD_DIM, 4 * HEAD_DIM)
    pe = jnp.concatenate([pe_k, pe_v], axis=-1).reshape(2, half * 2 * HEAD_DIM)
    pe8 = jnp.pad(pe, ((0, 6), (0, 0)))
    z2 = jnp.zeros_like(ck_w2)
    w2 = jnp.concatenate([jnp.concatenate([ck_w2, z2], axis=1), jnp.concatenate([z2, cv_w2], axis=1)], axis=0)
    return wc, pe8, w2


def kernel(x, c, w_mod, b_mod, g_pre_mix, g_post_mix, g_pre_ffn, g_post_ffn, w_in, w_out, nsa_pe_k, nsa_pe_v, nsa_ck_w1, nsa_ck_w2, nsa_cv_w1, nsa_cv_w2, nsa_gate_b, nsa_out_g, rw_mu, rw_w0, rw_w2, rw_a0, rw_a2, rw_g2, rw_kk, rw_ka, rw_rk, rw_ln_w, rw_ln_b, ml_conv_w, ml_conv_b, ml_ig_b, ml_fg_b, ml_norm_g, ffn_w_gate, ffn_w_up, ffn_w_down):
    bsz, seq, d = x.shape
    depth = w_mod.shape[0]
    t = bsz * seq
    mod = _mod(c, w_mod, b_mod)
    x2 = x.reshape(t, d)
    for l in range(depth):
        sh1, sc1, gt1, sh2, sc2, gt2 = [m.reshape(bsz, 1, d) for m in jnp.split(mod[l], 6, axis=-1)]
        p_nsa, p_rw, p_ml, kse, kw, vse, vwe = _in_proj(x2, sc1, sh1, g_pre_mix[l].reshape(1, d),
                                                        _layout_w_in(w_in[l]), bsz, seq)
        wc, pe8, w2 = _layout_cmp(nsa_ck_w1[l], nsa_cv_w1[l], nsa_ck_w2[l], nsa_cv_w2[l], nsa_pe_k[l], nsa_pe_v[l])
        kv_cmp = _nsa_cmp(p_nsa, wc, pe8, w2, bsz, seq)
        gate_b = jnp.pad(nsa_gate_b[l], (0, 128 - 3 * H_ATT)).reshape(1, 128)
        o_nsa = _nsa(p_nsa, kv_cmp, kse.reshape(bsz, seq, -1), vse, kw.reshape(bsz, seq, -1), vwe, gate_b,
                     nsa_out_g[l].reshape(1, C_ATT), bsz, seq)
        vec = lambda a: a.reshape(1, -1)
        rw_args = (vec(rw_mu[l]), vec(rw_w0[l]), rw_w2[l], vec(rw_a0[l]), rw_a2[l], rw_g2[l], vec(rw_kk[l]),
                   vec(rw_ka[l]), vec(rw_rk[l]), vec(rw_ln_w[l]), vec(rw_ln_b[l]))
        ml_gate_b = jnp.pad(jnp.concatenate([ml_ig_b[l], ml_fg_b[l]]), (0, 128 - 2 * H_MLSTM)).reshape(1, 128)
        ml_args = (ml_conv_w[l], vec(ml_conv_b[l]), ml_gate_b, vec(ml_norm_g[l]))
        o_rw, o_ml = _recurrent(p_rw, rw_args, p_ml, ml_args, bsz, seq)
        wo = w_out[l].astype(BF16)
        x2 = _out_proj(o_nsa, o_rw, o_ml, wo[:C_ATT], wo[C_ATT:C_ATT + C_RWKV], wo[C_ATT + C_RWKV:], x2, gt1,
                       g_post_mix[l].reshape(1, d), seq)
        x2 = _ffn(x2, sc2, sh2, gt2, g_pre_ffn[l].reshape(1, d), g_post_ffn[l].reshape(1, d),
                  ffn_w_gate[l].astype(BF16), ffn_w_up[l].astype(BF16), ffn_w_down[l].astype(BF16), seq)
    return x2.reshape(bsz, seq, d)
```

```python
import functools

import jax
import jax.numpy as jnp
from jax import lax
from jax.experimental import pallas as pl
from jax.experimental.pallas import tpu as pltpu

F32 = jnp.float32
BF16 = jnp.bfloat16
HIGHEST = lax.Precision.HIGHEST

HEAD_DIM = 64
H_ATT = 4
C_ATT = H_ATT * HEAD_DIM
H_RWKV = 6
C_RWKV = H_RWKV * HEAD_DIM
H_MLSTM = 6
C_MLSTM = H_MLSTM * HEAD_DIM
CMP_BLOCK = 32
CMP_STRIDE = 16
SLC_BLOCK = 64
SLC_SHIFT = 6
N_SELECT = 16
WINDOW = 512
NEG = -1e30
FORCE = 1e9
MASK_BIG = 2.0 ** 100
LOG2E = 1.4426950408889634
RANK_W = 64
RANK_A = 64
RANK_G = 128
RWKV_GN_EPS = 64e-5
CHUNK = 64
SEQ_BLOCK = 512
CONV_WIDTH = 4
NORM_EPS = 1e-6

NSA_W = 768
RWKV_W = 3 * C_RWKV + RANK_W + RANK_A + RANK_G
ML_W = 4 * C_MLSTM + 128
NSA_IN = C_ATT + 6 * HEAD_DIM + 3 * H_ATT
ML_IN = 4 * C_MLSTM + 2 * H_MLSTM

VMEM_LIMIT = 56 * 1024 * 1024

TQ = 256
TK = 256
PV_ROWS = HEAD_DIM + 16


def _params(*sem):
    return pltpu.CompilerParams(dimension_semantics=sem, vmem_limit_bytes=VMEM_LIMIT)


def _dot(a, b, precise=False):
    if precise:
        return jnp.dot(a.astype(F32), b.astype(F32), preferred_element_type=F32, precision=HIGHEST)
    return jnp.dot(a.astype(BF16), b.astype(BF16), preferred_element_type=F32)


def _dot_nt(a, b, precise=False):
    dn = (((1,), (1,)), ((), ()))
    if precise:
        return lax.dot_general(a.astype(F32), b.astype(F32), dn, preferred_element_type=F32, precision=HIGHEST)
    return lax.dot_general(a.astype(BF16), b.astype(BF16), dn, preferred_element_type=F32)


def _dot_tn(a, b, precise=False):
    dn = (((0,), (0,)), ((), ()))
    if precise:
        return lax.dot_general(a.astype(F32), b.astype(F32), dn, preferred_element_type=F32, precision=HIGHEST)
    return lax.dot_general(a.astype(BF16), b.astype(BF16), dn, preferred_element_type=F32)


def _split3(x):
    hi = x.astype(BF16)
    rest = x - hi.astype(F32)
    mid = rest.astype(BF16)
    lo = (rest - mid.astype(F32)).astype(BF16)
    return hi, mid, lo


def _dot_mask_lhs(mask01, x):
    mb = mask01.astype(BF16)
    hi, mid, lo = _split3(x)
    dot = lambda p: jnp.dot(mb, p, preferred_element_type=F32)
    return (dot(lo) + dot(mid)) + dot(hi)


def _dot_nt_mask_rhs(x, mask01):
    mb = mask01.astype(BF16)
    hi, mid, lo = _split3(x)
    dot = lambda p: lax.dot_general(p, mb, (((1,), (1,)), ((), ())), preferred_element_type=F32)
    return (dot(lo) + dot(mid)) + dot(hi)


def _dot3(a, b):
    a_hi, a_lo, _ = _split3(a)
    b_hi, b_lo, _ = _split3(b)
    dot = lambda p, q: jnp.dot(p, q, preferred_element_type=F32)
    return (dot(a_lo, b_hi) + dot(a_hi, b_lo)) + dot(a_hi, b_hi)


def _rms(x, g):
    return x * lax.rsqrt(jnp.mean(x * x, axis=-1, keepdims=True) + NORM_EPS) * g


def _mod_body(c_ref, w_ref, b_ref, o_ref):
    c = c_ref[...]
    cs = c * jax.nn.sigmoid(c)
    o_ref[...] = _dot(cs, w_ref[...], precise=True) + b_ref[...]


def _mod(c, w_mod, b_mod):
    depth, d, n = w_mod.shape
    bsz = c.shape[0]
    tn = 1536
    return pl.pallas_call(
        _mod_body,
        out_shape=jax.ShapeDtypeStruct((depth, bsz, n), F32),
        grid=(depth, n // tn),
        in_specs=[
            pl.BlockSpec((bsz, d), lambda l, j: (0, 0)),
            pl.BlockSpec((None, d, tn), lambda l, j: (l, 0, j)),
            pl.BlockSpec((None, 1, tn), lambda l, j: (l, 0, j)),
        ],
        out_specs=pl.BlockSpec((None, bsz, tn), lambda l, j: (l, 0, j)),
        compiler_params=_params("parallel", "parallel"),
        name="mod",
    )(c, w_mod, b_mod.reshape(depth, 1, n))


def _in_proj_body(x_ref, sc_ref, sh_ref, g_ref, w_ref, nsa_ref, rw_ref, ml_ref, kse_ref, kw_ref, vse_ref,
                  vwe_ref, *, per_b):
    tm = x_ref.shape[0]
    h = _rms(x_ref[...], g_ref[...]) * (1.0 + sc_ref[...]) + sh_ref[...]
    hb = h.astype(BF16)
    nsa = jnp.dot(hb, w_ref[:, 0:NSA_W], preferred_element_type=F32)
    nsa_ref[...] = nsa
    rw_ref[...] = jnp.dot(hb, w_ref[:, NSA_W:NSA_W + RWKV_W], preferred_element_type=F32)
    ml_ref[...] = jnp.dot(hb, w_ref[:, NSA_W + RWKV_W:], preferred_element_type=F32)
    D = HEAD_DIM
    off = C_ATT + 2 * D
    lane = lax.broadcasted_iota(jnp.int32, (tm, 2 * D), 1)
    pos = (pl.program_id(0) % per_b) * tm + lax.broadcasted_iota(jnp.int32, (tm, 2 * D), 0)
    expand = jnp.where(lane - D == jnp.right_shift(pos, SLC_SHIFT), MASK_BIG, 0.0)
    kse_ref[...] = jnp.where(lane < D, nsa[:, off:off + 2 * D], expand).astype(BF16)
    kw_ref[...] = nsa[:, off + 2 * D:off + 3 * D].astype(BF16)
    tail = jnp.where(lax.broadcasted_iota(jnp.int32, (D, tm), 0) == 0, 1.0, 0.0)
    vse_ref[...] = jnp.concatenate([nsa[:, off + D:off + 2 * D].T, tail], axis=0).astype(BF16)
    vwe_ref[...] = jnp.concatenate([nsa[:, off + 3 * D:off + 4 * D].T, tail], axis=0).astype(BF16)


def _in_proj(x2, sc, sh, g, w_p, bsz, seq):
    t, d = x2.shape
    tm = 512
    per_b = seq // tm
    n_slc = seq // SLC_BLOCK
    assert n_slc == HEAD_DIM, "kse packs the key and one column per selection block into 128 lanes"
    row = lambda i: (i, 0)
    bat = lambda i: (i // per_b, 0, 0)
    fix = lambda i: (0, 0)
    col = lambda i: (i // per_b, 0, i % per_b)
    return pl.pallas_call(
        functools.partial(_in_proj_body, per_b=per_b),
        out_shape=(jax.ShapeDtypeStruct((t, NSA_W), F32),
                   jax.ShapeDtypeStruct((t, RWKV_W), F32),
                   jax.ShapeDtypeStruct((t, ML_W), F32),
                   jax.ShapeDtypeStruct((t, 2 * HEAD_DIM), BF16),
                   jax.ShapeDtypeStruct((t, HEAD_DIM), BF16),
                   jax.ShapeDtypeStruct((bsz, 128, seq), BF16),
                   jax.ShapeDtypeStruct((bsz, 128, seq), BF16)),
        grid=(t // tm,),
        in_specs=[
            pl.BlockSpec((tm, d), row),
            pl.BlockSpec((None, 1, d), bat),
            pl.BlockSpec((None, 1, d), bat),
            pl.BlockSpec((1, d), fix),
            pl.BlockSpec(w_p.shape, fix),
        ],
        out_specs=(pl.BlockSpec((tm, NSA_W), row), pl.BlockSpec((tm, RWKV_W), row),
                   pl.BlockSpec((tm, ML_W), row), pl.BlockSpec((tm, 2 * HEAD_DIM), row),
                   pl.BlockSpec((tm, HEAD_DIM), row), pl.BlockSpec((None, 128, tm), col),
                   pl.BlockSpec((None, 128, tm), col)),
        compiler_params=_params("parallel"),
        name="in_proj",
    )(x2, sc, sh, g, w_p)


def _nsa_cmp_body(x_ref, wc_ref, pe_ref, w2_ref, o_ref):
    half = CMP_BLOCK // 2
    width = x_ref.shape[1]
    g = x_ref.shape[0] // half
    f = None
    for tl in range(half):
        part = _dot3(x_ref[pl.ds(tl, g, stride=half), :], wc_ref[tl * width:(tl + 1) * width, :])
        f = part if f is None else f + part
    c = _dot(pe_ref[...], wc_ref[...], precise=True)
    second = pltpu.roll(f[:, 128:256], g - 1, 0)
    pre = f[:, 0:128] + second + c[0:1, 0:128] + c[1:2, 128:256]
    act = jax.nn.gelu(pre, approximate=True)
    out = _dot(act, w2_ref[...], precise=True)
    rows = lax.broadcasted_iota(jnp.int32, out.shape, 0)
    o_ref[...] = jnp.where(rows < g - 1, out, 0.0)


def _nsa_cmp(p_nsa, wc, pe8, w2, bsz, seq):
    g = seq // (CMP_BLOCK // 2)
    kc_block = C_ATT // (2 * HEAD_DIM)
    return pl.pallas_call(
        _nsa_cmp_body,
        out_shape=jax.ShapeDtypeStruct((bsz, g, 128), F32),
        grid=(bsz,),
        in_specs=[
            pl.BlockSpec((seq, 2 * HEAD_DIM), lambda b: (b, kc_block)),
            pl.BlockSpec(wc.shape, lambda b: (0, 0)),
            pl.BlockSpec(pe8.shape, lambda b: (0, 0)),
            pl.BlockSpec(w2.shape, lambda b: (0, 0)),
        ],
        out_specs=pl.BlockSpec((None, g, 128), lambda b: (b, 0, 0)),
        compiler_params=_params("parallel"),
        name="nsa_cmp",
    )(p_nsa, wc, pe8, w2)


def _nsa_body(q_ref, gts_ref, kvc_ref, kse_ref, vse_ref, kw_ref, vwe_ref, gb_ref, g_ref, o_ref,
              m_s, acc_s, s_even, s_odd, ocmp, cnt_ref, imp_ref, *, n_slc):
    tq = q_ref.shape[0]
    n_cmp = kvc_ref.shape[0]
    qi = pl.program_id(1)
    q0 = qi * tq
    scale = HEAD_DIM ** -0.5
    D = HEAD_DIM
    qt = q_ref[...].T

    heads = range(H_ATT)

    def compressed(rows):
        kc = kvc_ref[0:rows, 0:D]
        vc = kvc_ref[0:rows, D:2 * D]
        nidx = lax.broadcasted_iota(jnp.int32, (rows, tq), 0)
        pos = q0 + lax.broadcasted_iota(jnp.int32, (rows, tq), 1)
        cmask = (nidx * CMP_STRIDE + (CMP_BLOCK - 1)) <= pos
        sc = [_dot3(kc, qt[h * D:(h + 1) * D]) for h in heads]
        sc = [jnp.where(cmask, x * scale, NEG) for x in sc]
        ec = [jnp.where(cmask, jnp.exp(x - jnp.max(x, axis=0, keepdims=True)), 0.0) for x in sc]
        pc = [e / jnp.maximum(jnp.sum(e, axis=0, keepdims=True), 1e-30) for e in ec]
        oc = [_dot_tn(vc, p) for p in pc]
        for h in heads:
            ocmp[h * D:(h + 1) * D, :] = oc[h]
        psum = sum(pc[1:], pc[0])
        jrow = lax.broadcasted_iota(jnp.int32, (n_slc, rows), 0) * SLC_BLOCK
        ncol = lax.broadcasted_iota(jnp.int32, (n_slc, rows), 1) * CMP_STRIDE
        overlap_t = jnp.where((ncol < jrow + SLC_BLOCK) & (ncol + CMP_BLOCK > jrow), 1.0, 0.0)
        imp_ref[...] = _dot_mask_lhs(overlap_t, psum)

    step = tq // CMP_STRIDE
    tiles_per_part = max(1, (n_cmp // 4) // step)
    bounds = sorted(set(min(n_cmp, step * tiles_per_part * (i + 1)) for i in range(4)))
    for i, rows in enumerate(bounds):
        lo = 0 if i == 0 else bounds[i - 1] // step
        pl.when((qi >= lo) & (qi < rows // step))(functools.partial(compressed, rows))

    imp_t = imp_ref[...]
    jj = lax.broadcasted_iota(jnp.int32, (n_slc, tq), 0)
    cur = jnp.right_shift(q0 + lax.broadcasted_iota(jnp.int32, (n_slc, tq), 1), SLC_SHIFT)
    forced = (jj == 0) | (jj == cur) | (jj == cur - 1)
    score = jnp.where(forced, FORCE, jnp.where(jj <= cur, imp_t, NEG))
    group = 8
    batch = 16
    per_tile = tq // SLC_BLOCK
    cnt_ref[...] = jnp.zeros_like(cnt_ref)

    def count_batch(first_j2):
        for gi in range(n_slc // group):
            sg = score[gi * group:(gi + 1) * group]
            acc = None
            for j2 in range(first_j2, first_j2 + batch):
                row = score[j2:j2 + 1, :]
                if gi * group > j2:
                    hit = jnp.where(row >= sg, 1.0, 0.0)
                elif (gi + 1) * group - 1 <= j2:
                    hit = jnp.where(row > sg, 1.0, 0.0)
                else:
                    hit = jnp.where(lax.broadcasted_iota(jnp.int32, (group, tq), 0) + gi * group > j2,
                                    jnp.where(row >= sg, 1.0, 0.0), jnp.where(row > sg, 1.0, 0.0))
                acc = hit if acc is None else acc + hit
            cnt_ref[gi * group:(gi + 1) * group, :] += acc

    count_batch(0)
    for first_j2 in range(batch, n_slc, batch):
        pl.when(first_j2 < (qi + 1) * per_tile)(functools.partial(count_batch, first_j2))
    cnt = cnt_ref[...]
    selm = jnp.where(cnt < min(N_SELECT, n_slc), 0.0, -1.0).astype(BF16)

    qs = (qt * (scale * LOG2E)).astype(BF16)
    rhs_w = [qs[h * D:(h + 1) * D] for h in heads]
    rhs_s = [jnp.concatenate([rhs_w[h], selm], axis=0) for h in heads]
    m_s[...] = jnp.full(m_s.shape, NEG, F32)
    acc_s[...] = jnp.zeros_like(acc_s)

    def scores_into(buf, j):
        k = kse_ref[pl.ds(pl.multiple_of(j * TK, TK), TK), :]
        for h in heads:
            buf[h] = jnp.dot(k, rhs_s[h], preferred_element_type=F32)

    def consume(buf, j, mask):
        vt = vse_ref[0:PV_ROWS, pl.ds(pl.multiple_of(j * TK, TK), TK)]
        for h in heads:
            s = buf[h]
            if mask is not None:
                s = jnp.where(mask, s, NEG)
            m_old = m_s[h:h + 1, :]
            m_new = jnp.maximum(m_old, jnp.max(s, axis=0, keepdims=True))
            p = jnp.exp2(s - m_new).astype(BF16)
            acc_s[h] = jnp.exp2(m_old - m_new) * acc_s[h] + jnp.dot(vt, p, preferred_element_type=F32)
            m_s[h:h + 1, :] = m_new

    kpos = lax.broadcasted_iota(jnp.int32, (TK, tq), 0)
    qpos = lax.broadcasted_iota(jnp.int32, (TK, tq), 1)
    last = jnp.maximum(qi - 1, 0)
    scores_into(s_even, qi)
    scores_into(s_odd, 0)
    consume(s_even, qi, kpos <= qpos)

    def body(jp, carry):
        j = 2 * jp
        scores_into(s_even, jnp.minimum(j + 1, last))
        consume(s_odd, j, None)
        scores_into(s_odd, jnp.minimum(j + 2, last))
        consume(s_even, j + 1, None)
        return carry

    lax.fori_loop(0, qi // 2, body, 0)

    @pl.when(qi % 2 == 1)
    def _():
        consume(s_odd, qi - 1, None)

    span = WINDOW + tq
    w0 = pl.multiple_of(jnp.maximum(q0 - WINDOW, 0), TK)
    kw = kw_ref[pl.ds(w0, span), :]
    vwt = vwe_ref[0:PV_ROWS, pl.ds(w0, span)]
    kabs = w0 + lax.broadcasted_iota(jnp.int32, (span, tq), 0)
    qabs = q0 + lax.broadcasted_iota(jnp.int32, (span, tq), 1)
    wmask = (kabs <= qabs) & (kabs > qabs - WINDOW)
    sw = [jnp.where(wmask, jnp.dot(kw, rhs_w[h], preferred_element_type=F32), NEG) for h in heads]
    pw = [jnp.exp2(x - jnp.max(x, axis=0, keepdims=True)).astype(BF16) for x in sw]
    acc_w = [jnp.dot(vwt, p, preferred_element_type=F32) for p in pw]

    gate = jax.nn.sigmoid((gts_ref[...] + gb_ref[...]).T)
    outs = []
    for h in heads:
        a_s = acc_s[h]
        a_w = acc_w[h]
        outs.append(gate[h:h + 1] * ocmp[h * D:(h + 1) * D, :]
                    + gate[H_ATT + h:H_ATT + h + 1] * (a_s[0:D] / a_s[D:D + 1])
                    + gate[2 * H_ATT + h:2 * H_ATT + h + 1] * (a_w[0:D] / a_w[D:D + 1]))
    ot = jnp.concatenate(outs, axis=0)
    ot = ot * lax.rsqrt(jnp.mean(ot * ot, axis=0, keepdims=True) + NORM_EPS)
    o_ref[...] = ot.T * g_ref[...]


def _nsa(p_nsa, kv_cmp, kse, vse, kw, vwe, gate_b, out_g, bsz, seq):
    t = p_nsa.shape[0]
    nq = seq // TQ
    n_slc = seq // SLC_BLOCK
    g = kv_cmp.shape[1]
    row = lambda b, i: (b * nq + i, 0)
    per_b = lambda b, i: (b, 0, 0)
    fix = lambda b, i: (0, 0)
    return pl.pallas_call(
        functools.partial(_nsa_body, n_slc=n_slc),
        out_shape=jax.ShapeDtypeStruct((t, C_ATT), F32),
        grid=(bsz, nq),
        in_specs=[
            pl.BlockSpec((TQ, C_ATT), row),
            pl.BlockSpec((TQ, 128), lambda b, i: (b * nq + i, NSA_W // 128 - 1)),
            pl.BlockSpec((None, g, 128), per_b),
            pl.BlockSpec((None, seq, HEAD_DIM + n_slc), per_b),
            pl.BlockSpec((None, 128, seq), per_b),
            pl.BlockSpec((None, seq, HEAD_DIM), per_b),
            pl.BlockSpec((None, 128, seq), per_b),
            pl.BlockSpec((1, 128), fix),
            pl.BlockSpec((1, C_ATT), fix),
        ],
        out_specs=pl.BlockSpec((TQ, C_ATT), row),
        scratch_shapes=[pltpu.VMEM((8, TQ), F32), pltpu.VMEM((H_ATT, PV_ROWS, TQ), F32),
                        pltpu.VMEM((H_ATT, TK, TQ), F32), pltpu.VMEM((H_ATT, TK, TQ), F32),
                        pltpu.VMEM((C_ATT, TQ), F32), pltpu.VMEM((n_slc, TQ), F32),
                        pltpu.VMEM((n_slc, TQ), F32)],
        compiler_params=_params("parallel", "arbitrary"),
        name="nsa",
    )(p_nsa, p_nsa, kv_cmp, kse, vse, kw, vwe, gate_b, out_g)


def _tri(n, strict):
    r = lax.broadcasted_iota(jnp.int32, (n, n), 0)
    c = lax.broadcasted_iota(jnp.int32, (n, n), 1)
    return (c < r) if strict else (c <= r)


def _block_diag(x):
    xb = x.astype(BF16)
    first = lax.broadcasted_iota(jnp.int32, xb.shape, 1) < xb.shape[1] // 2
    zero = jnp.zeros_like(xb)
    return jnp.concatenate([jnp.where(first, xb, zero), jnp.where(first, zero, xb)], axis=0)


def _unit_lower_inverse_pairs(ms, eye2):
    n = eye2.shape[0]
    idx = range(len(ms))
    r = lax.broadcasted_iota(jnp.int32, (n, 2 * n), 0)
    c = jnp.bitwise_and(lax.broadcasted_iota(jnp.int32, (n, 2 * n), 1), n - 1)
    base = 8
    diag = jnp.right_shift(r, 3) == jnp.right_shift(c, 3)
    m8 = [jnp.where(diag, m, 0.0) for m in ms]
    t = [eye2 + m for m in m8]
    p2 = [_dot(m, _block_diag(m)) for m in m8]
    t = [t[i] + _dot(p2[i], _block_diag(t[i])) for i in idx]
    p4 = [_dot(p, _block_diag(p)) for p in p2]
    t = [t[i] + _dot(p4[i], _block_diag(t[i])) for i in idx]
    b = base
    while b < n:
        sh = b.bit_length() - 1
        pair = jnp.right_shift(r, sh + 1) == jnp.right_shift(c, sh + 1)
        lower_left = pair & (jnp.right_shift(r, sh) != jnp.right_shift(c, sh))
        left = [_dot(t[i], _block_diag(jnp.where(lower_left, ms[i], 0.0))) for i in idx]
        t = [t[i] + _dot(left[i], _block_diag(t[i])) for i in idx]
        b *= 2
    return t


def _rwkv_body(p_ref, mu_ref, w0_ref, w2_ref, a0_ref, a2_ref, g2_ref, kk_ref, ka_ref, rk_ref, lnw_ref,
               lnb_ref, o_ref, state, prev):
    L = CHUNK
    C = C_RWKV
    tb = p_ref.shape[0]
    chunks = range(tb // L)

    x = p_ref[...]
    rows = lax.broadcasted_iota(jnp.int32, x.shape, 0)
    shifted = jnp.where(rows == 0, prev[0:1, :], pltpu.roll(x, 1, 0))
    prev[0:1, :] = x[tb - 1:tb, :]
    x = x + mu_ref[...] * (shifted - x)
    r = x[:, 0:C]
    k = x[:, C:2 * C]
    v = x[:, 2 * C:3 * C]
    xw = x[:, 3 * C:3 * C + RANK_W]
    xa = x[:, 3 * C + RANK_W:3 * C + RANK_W + RANK_A]
    xg = x[:, 3 * C + RANK_W + RANK_A:]
    w = -jax.nn.softplus(-(w0_ref[...] + _dot3(jnp.tanh(xw), w2_ref[...]))) - 0.5
    logw = -jnp.exp(w)
    a = jax.nn.sigmoid(a0_ref[...] + _dot3(xa, a2_ref[...]))
    g = _dot(jax.nn.sigmoid(xg), g2_ref[...])
    kkf = k * kk_ref[...]
    kmod = k * (1.0 + (a - 1.0) * ka_ref[...])
    rk = r * kmod * rk_ref[...]

    tri_incl = jnp.where(_tri(L, strict=False), 1.0, 0.0)
    cum = jnp.concatenate([_dot_mask_lhs(tri_incl, logw[c * L:(c + 1) * L]) for c in chunks], axis=0)
    cum_last = [cum[(c + 1) * L - 1:(c + 1) * L, :] for c in chunks]
    cum_end = jnp.concatenate([jnp.broadcast_to(cl, (L, C)) for cl in cum_last], axis=0)
    e_pos = jnp.exp(cum)
    e_prev = jnp.exp(cum - logw)
    e_neg = jnp.exp(-cum)
    e_rem = jnp.exp(cum_end - cum)
    N = HEAD_DIM
    PW = 2 * N
    pairs = range(H_RWKV // 2)
    lane_l = lax.broadcasted_iota(jnp.int32, (L, PW), 1)
    row_l = lax.broadcasted_iota(jnp.int32, (L, PW), 0)
    col_l = jnp.bitwise_and(lane_l, N - 1)
    first = lane_l < N
    first2 = lax.broadcasted_iota(jnp.int32, (2 * L, PW), 1) < N
    strict2 = col_l < row_l
    incl2 = col_l <= row_l
    eye2 = jnp.where(col_l == row_l, 1.0, 0.0)

    def halves(x_):
        lane = lax.broadcasted_iota(jnp.int32, x_.shape, 1)
        sa = jnp.sum(jnp.where(lane < N, x_, 0.0), axis=-1, keepdims=True)
        sb = jnp.sum(jnp.where(lane < N, 0.0, x_), axis=-1, keepdims=True)
        return jnp.where(lane < N, sa, sb)

    ps = [slice(p * PW, (p + 1) * PW) for p in pairs]
    kk_p = [kkf[:, sl] for sl in ps]
    kk_p = [x_ / jnp.maximum(jnp.sqrt(halves(x_ * x_)), 1e-12) for x_ in kk_p]
    alpha_f = [-kk_p[p] * e_prev[:, ps[p]] for p in pairs]
    r_f = [r[:, ps[p]] * e_pos[:, ps[p]] for p in pairs]
    beta = [kk_p[p] * a[:, ps[p]] for p in pairs]
    beta_f = [beta[p] * e_neg[:, ps[p]] for p in pairs]
    k_f = [kmod[:, ps[p]] * e_neg[:, ps[p]] for p in pairs]
    beta_e = [beta[p] * e_rem[:, ps[p]] for p in pairs]
    k_e = [kmod[:, ps[p]] * e_rem[:, ps[p]] for p in pairs]
    items = [(c, p) for c in chunks for p in pairs]
    n_items = range(len(items))
    rs = [slice(c * L, (c + 1) * L) for c, _ in items]
    pi = [p for _, p in items]
    lhs = [jnp.concatenate([alpha_f[pi[i]][rs[i]], r_f[pi[i]][rs[i]]], axis=0).astype(BF16) for i in n_items]
    zero16 = jnp.zeros((2 * L, PW), BF16)
    lhs4 = [jnp.concatenate([jnp.where(first2, x_, zero16), jnp.where(first2, zero16, x_)], axis=0) for x_ in lhs]
    bk = [(beta_f[pi[i]][rs[i]].astype(BF16), k_f[pi[i]][rs[i]].astype(BF16)) for i in n_items]
    v_i = [v[rs[i], ps[pi[i]]] for i in n_items]
    out1 = [_dot_nt(lhs4[i], jnp.concatenate([bk[i][0], bk[i][1]], axis=0)) for i in n_items]
    out2 = [_dot_nt(lhs4[i], jnp.concatenate([bk[i][1], bk[i][0]], axis=0)) for i in n_items]
    m_ab = [jnp.where(strict2, jnp.where(first, out1[i][0:L], out2[i][2 * L:3 * L]), 0.0) for i in n_items]
    m_ak = [jnp.where(strict2, jnp.where(first, out2[i][0:L], out1[i][2 * L:3 * L]), 0.0) for i in n_items]
    m_rb = [jnp.where(incl2, jnp.where(first, out1[i][L:2 * L], out2[i][3 * L:4 * L]), 0.0) for i in n_items]
    m_rk = [jnp.where(incl2, jnp.where(first, out2[i][L:2 * L], out1[i][3 * L:4 * L]), 0.0) for i in n_items]
    t_inv = _unit_lower_inverse_pairs(m_ab, eye2)
    mv = [_dot(jnp.concatenate([m_ak[i], m_rk[i]], axis=0), _block_diag(v_i[i])) for i in n_items]
    pq = [_dot(t_inv[i], jnp.concatenate([_block_diag(alpha_f[pi[i]][rs[i]]), _block_diag(mv[i][0:L])], axis=1))
          for i in n_items]
    ry = [_dot(m_rb[i], jnp.concatenate([_block_diag(pq[i][:, 0:PW]), _block_diag(pq[i][:, PW:2 * PW])], axis=1))
          for i in n_items]
    r_eff = [r_f[pi[i]][rs[i]] + ry[i][:, 0:PW] for i in n_items]
    y0 = [ry[i][:, PW:2 * PW] + mv[i][L:2 * L] for i in n_items]
    zero_l = jnp.zeros((L, PW), F32)
    gh = [_dot_tn(jnp.concatenate([beta_e[pi[i]][rs[i]], k_e[pi[i]][rs[i]]], axis=0),
                  jnp.concatenate([pq[i], jnp.concatenate([zero_l, v_i[i]], axis=1)], axis=0)) for i in n_items]
    g_off = [jnp.where(first, gh[i][0:N, 0:PW], gh[i][N:2 * N, 0:PW]) for i in n_items]
    h_add = [jnp.where(first, gh[i][0:N, PW:2 * PW], gh[i][N:2 * N, PW:2 * PW]) for i in n_items]
    gam = [halves(eye2 * jnp.exp(cum_last[c][:, ps[p]])) for c, p in items]
    z = [state[p] for p in pairs]
    y_parts = [[] for _ in pairs]
    for c in chunks:
        idx = [c * len(pairs) + p for p in pairs]
        zb = [_block_diag(z[p]) for p in pairs]
        for p, i in enumerate(idx):
            y_parts[p].append(_dot(r_eff[i], zb[p]) + y0[i])
        z = [z[p] * gam[i] + _dot(g_off[i], zb[p]) + h_add[i] for p, i in enumerate(idx)]
    for p in pairs:
        state[p] = z[p]
    yn = []
    for p in pairs:
        y_p = jnp.concatenate(y_parts[p], axis=0)
        mu_y = halves(y_p) * (1.0 / N)
        dev = y_p - mu_y
        var = halves(dev * dev) * (1.0 / N)
        yn.append(dev * lax.rsqrt(var + RWKV_GN_EPS))
    bonus = [halves(rk[:, ps[p]]) * v[:, ps[p]] for p in pairs]
    yn = jnp.concatenate(yn, axis=-1)
    bonus = jnp.concatenate(bonus, axis=-1)
    o_ref[...] = (yn * lnw_ref[...] + lnb_ref[...] + bonus) * g


N_RWKV_ARGS = 11
N_ML_ARGS = 4


def _recurrent_body(*refs):
    p_rw, rw_args = refs[0], refs[1:1 + N_RWKV_ARGS]
    p_ml, ml_args = refs[1 + N_RWKV_ARGS], refs[2 + N_RWKV_ARGS:2 + N_RWKV_ARGS + N_ML_ARGS]
    o_rw, o_ml, state, prev_rw, cstate, mstate, prev_ml = refs[2 + N_RWKV_ARGS + N_ML_ARGS:]

    @pl.when(pl.program_id(1) == 0)
    def _():
        for ref in (state, prev_rw, cstate, mstate, prev_ml):
            ref[...] = jnp.zeros_like(ref)

    _rwkv_body(p_rw, *rw_args, o_rw, state, prev_rw)
    _mlstm_body(p_ml, *ml_args, o_ml, cstate, mstate, prev_ml)


def _recurrent(p_rw, rw_args, p_ml, ml_args, bsz, seq):
    t = p_rw.shape[0]
    tb = SEQ_BLOCK
    nb = seq // tb
    row = lambda b, c: (b * nb + c, 0)
    fix = lambda b, c: (0, 0)
    assert len(rw_args) == N_RWKV_ARGS and len(ml_args) == N_ML_ARGS
    return pl.pallas_call(
        _recurrent_body,
        out_shape=(jax.ShapeDtypeStruct((t, C_RWKV), F32), jax.ShapeDtypeStruct((t, C_MLSTM), F32)),
        grid=(bsz, nb),
        in_specs=([pl.BlockSpec((tb, RWKV_W), row)] + [pl.BlockSpec(a.shape, fix) for a in rw_args]
                  + [pl.BlockSpec((tb, ML_W), row)] + [pl.BlockSpec(a.shape, fix) for a in ml_args]),
        out_specs=(pl.BlockSpec((tb, C_RWKV), row), pl.BlockSpec((tb, C_MLSTM), row)),
        scratch_shapes=[pltpu.VMEM((H_RWKV // 2, HEAD_DIM, 2 * HEAD_DIM), F32), pltpu.VMEM((8, RWKV_W), F32),
                        pltpu.VMEM((H_MLSTM, 128, HEAD_DIM), F32), pltpu.VMEM((8, 128), F32),
                        pltpu.VMEM((8, 2 * C_MLSTM), F32)],
        compiler_params=_params("parallel", "arbitrary"),
        name="recurrent",
    )(p_rw, *rw_args, p_ml, *ml_args)


def _mlstm_body(p_ref, cw_ref, cb_ref, gb_ref, ng_ref, o_ref, cstate, mstate, prev):
    L = CHUNK
    C = C_MLSTM
    D = HEAD_DIM
    tb = p_ref.shape[0]
    chunks = range(tb // L)

    qk_in = p_ref[:, 0:2 * C]
    pv = prev[...]
    rows8 = lax.broadcasted_iota(jnp.int32, pv.shape, 0)
    conv = qk_in * cw_ref[CONV_WIDTH - 1:CONV_WIDTH, :] + cb_ref[...]
    for d in range(1, CONV_WIDTH):
        rolled = pltpu.roll(qk_in, d, 0)
        top = jnp.where(rows8 < d, pltpu.roll(pv, d, 0), rolled[0:8])
        sh = jnp.concatenate([top, rolled[8:]], axis=0)
        conv = conv + sh * cw_ref[CONV_WIDTH - 1 - d:CONV_WIDTH - d, :]
    prev[...] = qk_in[tb - 8:tb, :]
    qk = conv * jax.nn.sigmoid(conv)
    q = qk[:, 0:C]
    k = qk[:, C:2 * C] * (D ** -0.5)
    v = p_ref[:, 2 * C:3 * C]
    og = p_ref[:, 3 * C:4 * C]
    gates = p_ref[:, 4 * C:4 * C + 128] + gb_ref[...]
    lane = lax.broadcasted_iota(jnp.int32, gates.shape, 1)
    is_f = (lane >= H_MLSTM) & (lane < 2 * H_MLSTM)
    gl = jnp.where(is_f, jax.nn.log_sigmoid(gates), gates)
    tri_incl = jnp.where(_tri(L, strict=False), 1.0, 0.0)
    bcum = [_dot_mask_lhs(tri_incl, gl[c * L:(c + 1) * L]) for c in chunks]
    gl_t = [gl[c * L:(c + 1) * L].T for c in chunks]
    bcum_t = [_dot_nt_mask_rhs(g_, tri_incl) for g_ in gl_t]
    m_all = mstate[...]
    src = lax.broadcasted_iota(jnp.int32, (L, L), 0)
    qry = lax.broadcasted_iota(jnp.int32, (L, L), 1)
    causal_t = src <= qry
    ones_row = jnp.where(lax.broadcasted_iota(jnp.int32, (D, L), 0) == 0, 1.0, 0.0)

    heads = range(H_MLSTM)
    items = [(c, h) for c in chunks for h in heads]
    n_items = range(len(items))
    q_t = [q[c * L:(c + 1) * L].T for c in chunks]
    v_t = [v[c * L:(c + 1) * L].T for c in chunks]
    qt_i = [q_t[c][h * D:(h + 1) * D] for c, h in items]
    vt_ext = [jnp.concatenate([v_t[c][h * D:(h + 1) * D], ones_row], axis=0) for c, h in items]
    k_i = [k[c * L:(c + 1) * L, h * D:(h + 1) * D] for c, h in items]
    c_col = [gl[c * L:(c + 1) * L, h:h + 1] - bcum[c][:, H_MLSTM + h:H_MLSTM + h + 1] for c, h in items]
    b_row = [bcum_t[c][H_MLSTM + h:H_MLSTM + h + 1, :] for c, h in items]
    c_row = [gl_t[c][h:h + 1, :] - b_row[i] for i, (c, h) in enumerate(items)]
    g_tot = [bcum[c][L - 1:L, H_MLSTM + h:H_MLSTM + h + 1] for c, h in items]
    u_row = [g_tot[i] + c_row[i] for i in n_items]
    u_max = [jnp.max(x_, axis=1, keepdims=True) for x_ in u_row]
    m0 = []
    m_run = [m_all[0:1, h:h + 1] for h in heads]
    for c in chunks:
        for h in heads:
            i = c * H_MLSTM + h
            m0.append(m_run[h])
            m_run[h] = jnp.maximum(g_tot[i] + m_run[h], u_max[i])
    m_next = [m0[i + H_MLSTM] if i + H_MLSTM < len(items) else m_run[items[i][1]] for i in n_items]
    kq = [_dot(k_i[i], qt_i[i]) for i in n_items]
    kv = [_dot(vt_ext[i] * jnp.exp(u_row[i] - m_next[i]), k_i[i]) for i in n_items]
    cm = []
    cm_run = [cstate[h] for h in heads]
    for c in chunks:
        for h in heads:
            i = c * H_MLSTM + h
            cm.append(cm_run[h])
            cm_run[h] = jnp.exp(g_tot[i] + m0[i] - m_next[i]) * cm_run[h] + kv[i]
    inter = [_dot(cm[i], qt_i[i]) for i in n_items]
    c_b = [jnp.where(causal_t, jnp.broadcast_to(c_col[i], (L, L)), -jnp.inf) for i in n_items]
    m_row = [jnp.maximum(m0[i], jnp.max(c_b[i], axis=0, keepdims=True)) for i in n_items]
    s_t = [jnp.exp(c_b[i] - m_row[i]) * kq[i] for i in n_items]
    sv = [_dot(vt_ext[i], s_t[i]) for i in n_items]
    lane8 = lax.broadcasted_iota(jnp.int32, m_all.shape, 1)
    for h in heads:
        cstate[h] = cm_run[h]
        m_all = jnp.where(lane8 == h, m_run[h], m_all)
    mstate[...] = m_all
    outs = []
    for i in n_items:
        num = jnp.exp(m0[i] - m_row[i]) * inter[i] + sv[i]
        hh = num[0:D] / jnp.maximum(jnp.abs(num[D:D + 1]), jnp.exp(-(b_row[i] + m_row[i])))
        outs.append(hh * lax.rsqrt(jnp.mean(hh * hh, axis=0, keepdims=True) + NORM_EPS))
    hcat = jnp.concatenate([jnp.concatenate(outs[c * H_MLSTM:(c + 1) * H_MLSTM], axis=0).T for c in chunks], axis=0)
    o_ref[...] = jax.nn.sigmoid(og) * (hcat * ng_ref[...])


def _out_proj_body(a_ref, b_ref, c_ref, wa_ref, wb_ref, wc_ref, x_ref, gt_ref, g_ref, o_ref, *, sub):
    parts = [slice(s * sub, (s + 1) * sub) for s in range(x_ref.shape[0] // sub)]
    y = [jnp.dot(a_ref[p, :].astype(BF16), wa_ref[...], preferred_element_type=F32)
         + jnp.dot(b_ref[p, :].astype(BF16), wb_ref[...], preferred_element_type=F32)
         + jnp.dot(c_ref[p, :].astype(BF16), wc_ref[...], preferred_element_type=F32) for p in parts]
    for p, y_ in zip(parts, y):
        o_ref[p, :] = x_ref[p, :] + gt_ref[...] * _rms(y_, g_ref[...])


def _out_proj(o_nsa, o_rw, o_ml, wa, wb, wc, x2, gt, g, seq):
    t, d = x2.shape
    tm = 1024
    per_b = seq // tm
    row = lambda i: (i, 0)
    fix = lambda i: (0, 0)
    return pl.pallas_call(
        functools.partial(_out_proj_body, sub=tm // 2),
        out_shape=jax.ShapeDtypeStruct((t, d), F32),
        grid=(t // tm,),
        in_specs=[
            pl.BlockSpec((tm, C_ATT), row), pl.BlockSpec((tm, C_RWKV), row), pl.BlockSpec((tm, C_MLSTM), row),
            pl.BlockSpec(wa.shape, fix), pl.BlockSpec(wb.shape, fix), pl.BlockSpec(wc.shape, fix),
            pl.BlockSpec((tm, d), row),
            pl.BlockSpec((None, 1, d), lambda i: (i // per_b, 0, 0)),
            pl.BlockSpec((1, d), fix),
        ],
        out_specs=pl.BlockSpec((tm, d), row),
        compiler_params=_params("parallel"),
        name="out_proj",
    )(o_nsa, o_rw, o_ml, wa, wb, wc, x2, gt, g)


FFN_SLAB = 1024


def _ffn_body(x_ref, sc_ref, sh_ref, gt_ref, gpre_ref, gpost_ref, wg_ref, wu_ref, wd_ref, o_ref, *, sub):
    tm = x_ref.shape[0]
    dff = wg_ref.shape[1]
    parts = [slice(s * sub, (s + 1) * sub) for s in range(tm // sub)]
    slabs = [slice(a, min(a + FFN_SLAB, dff)) for a in range(0, dff, FFN_SLAB)]
    hb = [(_rms(x_ref[p, :], gpre_ref[...]) * (1.0 + sc_ref[...]) + sh_ref[...]).astype(BF16) for p in parts]
    y = [None for _ in parts]
    for sl in slabs:
        gate = [jnp.dot(h, wg_ref[:, sl], preferred_element_type=F32) for h in hb]
        up = [jnp.dot(h, wu_ref[:, sl], preferred_element_type=F32) for h in hb]
        act = [(g_ * jax.nn.sigmoid(g_) * u_).astype(BF16) for g_, u_ in zip(gate, up)]
        down = [jnp.dot(a_, wd_ref[sl, :], preferred_element_type=F32) for a_ in act]
        y = [d_ if y_ is None else y_ + d_ for y_, d_ in zip(y, down)]
    for p, y_ in zip(parts, y):
        o_ref[p, :] = x_ref[p, :] + gt_ref[...] * _rms(y_, gpost_ref[...])


def _ffn(x2, sc, sh, gt, g_pre, g_post, wg, wu, wd, seq):
    t, d = x2.shape
    tm = 1024
    per_b = seq // tm
    row = lambda i: (i, 0)
    bat = lambda i: (i // per_b, 0, 0)
    fix = lambda i: (0, 0)
    resident = lambda a: pl.BlockSpec(a.shape, fix, pipeline_mode=pl.Buffered(1))
    return pl.pallas_call(
        functools.partial(_ffn_body, sub=tm // 2),
        out_shape=jax.ShapeDtypeStruct((t, d), F32),
        grid=(t // tm,),
        in_specs=[
            pl.BlockSpec((tm, d), row),
            pl.BlockSpec((None, 1, d), bat), pl.BlockSpec((None, 1, d), bat), pl.BlockSpec((None, 1, d), bat),
            pl.BlockSpec((1, d), fix), pl.BlockSpec((1, d), fix),
            resident(wg), resident(wu), resident(wd),
        ],
        out_specs=pl.BlockSpec((tm, d), row),
        compiler_params=_params("parallel"),
        name="ffn",
    )(x2, sc, sh, gt, g_pre, g_post, wg, wu, wd)


def _pad_cols(a, width):
    return jnp.pad(a, ((0, 0), (0, width - a.shape[1])))


def _layout_w_in(w_in):
    d_in_rw = NSA_IN + RWKV_W
    w16 = w_in.astype(BF16)
    nsa = _pad_cols(w16[:, :NSA_IN], NSA_W)
    rw = w16[:, NSA_IN:d_in_rw]
    ml = _pad_cols(w16[:, d_in_rw:], ML_W)
    return jnp.concatenate([nsa, rw, ml], axis=1)


def _layout_cmp(ck_w1, cv_w1, ck_w2, cv_w2, pe_k, pe_v):
    half = CMP_BLOCK // 2
    ck = ck_w1.reshape(2, half, HEAD_DIM, HEAD_DIM)
    cv = cv_w1.reshape(2, half, HEAD_DIM, HEAD_DIM)
    z = jnp.zeros_like(ck[0])
    top = jnp.concatenate([ck[0], z, ck[1], z], axis=-1)
    bot = jnp.concatenate([z, cv[0], z, cv[1]], axis=-1)
    wc = jnp.concatenate([top, bot], axis=1).reshape(half * 2 * HEAD_DIM, 4 * HEAD_DIM)
    pe = jnp.concatenate([pe_k, pe_v], axis=-1).reshape(2, half * 2 * HEAD_DIM)
    pe8 = jnp.pad(pe, ((0, 6), (0, 0)))
    z2 = jnp.zeros_like(ck_w2)
    w2 = jnp.concatenate([jnp.concatenate([ck_w2, z2], axis=1), jnp.concatenate([z2, cv_w2], axis=1)], axis=0)
    return wc, pe8, w2


def kernel(x, c, w_mod, b_mod, g_pre_mix, g_post_mix, g_pre_ffn, g_post_ffn, w_in, w_out, nsa_pe_k, nsa_pe_v, nsa_ck_w1, nsa_ck_w2, nsa_cv_w1, nsa_cv_w2, nsa_gate_b, nsa_out_g, rw_mu, rw_w0, rw_w2, rw_a0, rw_a2, rw_g2, rw_kk, rw_ka, rw_rk, rw_ln_w, rw_ln_b, ml_conv_w, ml_conv_b, ml_ig_b, ml_fg_b, ml_norm_g, ffn_w_gate, ffn_w_up, ffn_w_down):
    bsz, seq, d = x.shape
    depth = w_mod.shape[0]
    t = bsz * seq
    mod = _mod(c, w_mod, b_mod)
    x2 = x.reshape(t, d)
    for l in range(depth):
        sh1, sc1, gt1, sh2, sc2, gt2 = [m.reshape(bsz, 1, d) for m in jnp.split(mod[l], 6, axis=-1)]
        p_nsa, p_rw, p_ml, kse, kw, vse, vwe = _in_proj(x2, sc1, sh1, g_pre_mix[l].reshape(1, d),
                                                        _layout_w_in(w_in[l]), bsz, seq)
        wc, pe8, w2 = _layout_cmp(nsa_ck_w1[l], nsa_cv_w1[l], nsa_ck_w2[l], nsa_cv_w2[l], nsa_pe_k[l], nsa_pe_v[l])
        kv_cmp = _nsa_cmp(p_nsa, wc, pe8, w2, bsz, seq)
        gate_b = jnp.pad(nsa_gate_b[l], (0, 128 - 3 * H_ATT)).reshape(1, 128)
        o_nsa = _nsa(p_nsa, kv_cmp, kse.reshape(bsz, seq, -1), vse, kw.reshape(bsz, seq, -1), vwe, gate_b,
                     nsa_out_g[l].reshape(1, C_ATT), bsz, seq)
        vec = lambda a: a.reshape(1, -1)
        rw_args = (vec(rw_mu[l]), vec(rw_w0[l]), rw_w2[l], vec(rw_a0[l]), rw_a2[l], rw_g2[l], vec(rw_kk[l]),
                   vec(rw_ka[l]), vec(rw_rk[l]), vec(rw_ln_w[l]), vec(rw_ln_b[l]))
        ml_gate_b = jnp.pad(jnp.concatenate([ml_ig_b[l], ml_fg_b[l]]), (0, 128 - 2 * H_MLSTM)).reshape(1, 128)
        ml_args = (ml_conv_w[l], vec(ml_conv_b[l]), ml_gate_b, vec(ml_norm_g[l]))
        o_rw, o_ml = _recurrent(p_rw, rw_args, p_ml, ml_args, bsz, seq)
        wo = w_out[l].astype(BF16)
        x2 = _out_proj(o_nsa, o_rw, o_ml, wo[:C_ATT], wo[C_ATT:C_ATT + C_RWKV], wo[C_ATT + C_RWKV:], x2, gt1,
                       g_post_mix[l].reshape(1, d), seq)
        x2 = _ffn(x2, sc2, sh2, gt2, g_pre_ffn[l].reshape(1, d), g_post_ffn[l].reshape(1, d),
                  ffn_w_gate[l].astype(BF16), ffn_w_up[l].astype(BF16), ffn_w_down[l].astype(BF16), seq)
    return x2.reshape(bsz, seq, d)
```
